```python
import functools
import jax
import jax.numpy as jnp
from jax import lax
import numpy as np

D_MODEL = 1024
BATCH = 4
SEQ = 4096
DEPTH = 2
DEC_BATCH = 128
DEC_SEQ = 4
PAST_LEN = 2048
PAGE_SIZE = 128

M_HEADS = 4
M_WIDTH = D_MODEL // 2
M_HEAD_DIM = M_WIDTH // M_HEADS
M_CONV = 4
M_CHUNK = 64
HEAD_DIM = 64
N_WIDTH = D_MODEL // 2
N_HEADS = N_WIDTH // HEAD_DIM
N_KV = 2
KV_W = N_KV * HEAD_DIM
CMP_STRIDE = 16
CMP_LEN = 2 * CMP_STRIDE
CMP_HID = 128
SEL_BLOCK = 64
SEL_TOP = 16
WINDOW = 512
Q_BLOCK = 128
ROT_DIM = HEAD_DIM // 4
ROPE_THETA = 500000.0
D_FF = -(-8 * D_MODEL // (3 * 256)) * 256
EPS = 1e-6
NEG = -1e30
TINY = 1e-30
FORCE_SCORE = 1e9
SPLIT_SIZES = (M_WIDTH, M_WIDTH, M_WIDTH, M_HEADS, M_HEADS, N_WIDTH, 6 * KV_W, 3 * N_HEADS, D_MODEL, D_MODEL)
IN_WIDTH = sum(SPLIT_SIZES)

kernel_name = 'hybrid_mlstm_nsa_decoder_step'


def rmsnorm(x, g):
    xf = x.astype(jnp.float32)
    y = xf * lax.rsqrt(jnp.mean(xf * xf, axis=-1, keepdims=True) + EPS) * g.astype(jnp.float32)
    return y.astype(x.dtype)


def split_cols(z):
    pts = [int(p) for p in np.cumsum(SPLIT_SIZES)[:-1]]
    return jnp.split(z, pts, axis=-1)


def rope_partial(x, pos):
    half = ROT_DIM // 2
    inv = ROPE_THETA ** (-jnp.arange(half, dtype=jnp.float32) / half)
    ang = pos.astype(jnp.float32)[:, None] * inv[None, :]
    cos = jnp.cos(ang)[None, :, None, :]
    sin = jnp.sin(ang)[None, :, None, :]
    xf = x.astype(jnp.float32)
    x1 = xf[..., :half]
    x2 = xf[..., half:ROT_DIM]
    out = jnp.concatenate([x1 * cos - x2 * sin, x2 * cos + x1 * sin, xf[..., ROT_DIM:]], axis=-1)
    return out.astype(x.dtype)


def masked_softmax(s, mask):
    s = jnp.where(mask, s.astype(jnp.float32), NEG)
    p = jnp.exp(s - jnp.max(s, axis=-1, keepdims=True)) * mask
    return p / jnp.maximum(jnp.sum(p, axis=-1, keepdims=True), TINY)


def causal_conv(u, buf, w, b):
    T = u.shape[1]
    xp = jnp.concatenate([buf.astype(u.dtype), u], axis=1)
    y = b + sum(xp[:, j:j + T] * w[j] for j in range(M_CONV))
    return jax.nn.silu(y), xp[:, T:]


def mlstm_chunkwise(q, k, v, i_pre, log_f, C0, n0, m0):
    B, T, H, _ = q.shape
    cs = T if T <= M_CHUNK else M_CHUNK
    nc = T // cs

    def to_chunks(a):
        a = a.astype(jnp.float32).reshape((B, nc, cs, H) + a.shape[3:])
        return jnp.moveaxis(a, (1, 3), (0, 2))

    tril = jnp.tril(jnp.ones((cs, cs), dtype=bool))

    def step(carry, inp):
        C, n, m = carry
        qc, kc, vc, ic, fc = inp
        F = jnp.cumsum(fc, axis=-1)
        a = ic - F
        mt = F + jnp.maximum(m[..., None], lax.cummax(a, axis=a.ndim - 1))
        logD = jnp.where(tril, a[..., None, :] + (F - mt)[..., :, None], -jnp.inf)
        S = jnp.einsum('bhtd,bhsd->bhts', qc, kc) * jnp.exp(logD)
        inter = jnp.exp(m[..., None] + F - mt)
        num = jnp.einsum('bhts,bhse->bhte', S, vc) + inter[..., None] * jnp.einsum('bhtd,bhde->bhte', qc, C)
        den = jnp.sum(S, axis=-1) + inter * jnp.einsum('bhtd,bhd->bht', qc, n)
        h = num / jnp.maximum(jnp.abs(den), jnp.exp(-mt))[..., None]
        m_new = mt[..., -1]
        w = jnp.exp(a + (F[..., -1] - m_new)[..., None])
        decay = jnp.exp(m + F[..., -1] - m_new)
        C_new = decay[..., None, None] * C + jnp.einsum('bhs,bhsd,bhse->bhde', w, kc, vc)
        n_new = decay[..., None] * n + jnp.einsum('bhs,bhsd->bhd', w, kc)
        return (C_new, n_new, m_new), h

    init = (C0.astype(jnp.float32), n0.astype(jnp.float32), m0.astype(jnp.float32))
    (C, n, m), h = lax.scan(step, init, (to_chunks(q), to_chunks(k), to_chunks(v), to_chunks(i_pre), to_chunks(log_f)))
    h = jnp.moveaxis(h, (0, 2), (1, 3)).reshape(B, T, H, v.shape[-1])
    return h, C, n, m


def mlstm_branch(u, v, o, ig, fg, conv_buf, C0, n0, m0, conv_w, conv_b, w_mq, w_mk, b_gates, norm_g):
    B, T, _ = u.shape
    c, conv_new = causal_conv(u, conv_buf, conv_w, conv_b)
    c = c.reshape(B, T, M_HEADS, M_HEAD_DIM)
    q = jnp.einsum('bthd,hde->bthe', c, w_mq)
    k = jnp.einsum('bthd,hde->bthe', c, w_mk) * (M_HEAD_DIM ** -0.5)
    v = v.reshape(B, T, M_HEADS, M_HEAD_DIM)
    pre = jnp.concatenate([ig, fg], axis=-1).astype(jnp.float32) + b_gates.astype(jnp.float32)
    i_pre = pre[..., :M_HEADS]
    log_f = jax.nn.log_sigmoid(pre[..., M_HEADS:])
    h, C, n, m = mlstm_chunkwise(q, k, v, i_pre, log_f, C0, n0, m0)
    h = h * lax.rsqrt(jnp.mean(h * h, axis=-1, keepdims=True) + EPS) * norm_g.astype(jnp.float32).reshape(M_HEADS, M_HEAD_DIM)
    out = h.reshape(B, T, M_WIDTH) * jax.nn.sigmoid(o.astype(jnp.float32))
    return out.astype(u.dtype), conv_new, C, n, m


def compress(raw, pe, w1, w2):
    B, L = raw.shape[:2]
    n_chunk = L // CMP_STRIDE
    ch = raw[:, :n_chunk * CMP_STRIDE].reshape(B, n_chunk, CMP_STRIDE, N_KV, HEAD_DIM)
    blocks = jnp.concatenate([ch[:, :-1], ch[:, 1:]], axis=2) + pe[None, None, :, None, :]
    hid = jax.nn.gelu(jnp.einsum('bnlgd,lde->bnge', blocks, w1))
    return jnp.einsum('bnge,ef->bngf', hid, w2)


def sel_overlap(n_cmp, n_blk):
    start = np.arange(n_cmp) * CMP_STRIDE
    bs = np.arange(n_blk) * SEL_BLOCK
    ov = np.minimum(start[:, None] + CMP_LEN, bs[None, :] + SEL_BLOCK) - np.maximum(start[:, None], bs[None, :])
    return jnp.asarray(np.clip(ov, 0, None) / CMP_LEN, dtype=jnp.float32)


def gather_rows(kv, idx):
    return jax.vmap(jax.vmap(lambda a, i: a[i]))(kv, idx)


def nsa_core(q, q_rot, qpos, kc, vc, cmp_end, k_sel, v_sel, kw, vw, kwpos, gates):
    B, Q = q.shape[:2]
    L = k_sel.shape[1]
    R = N_HEADS // N_KV
    scale = HEAD_DIM ** -0.5
    qp = q.reshape(B, Q, N_KV, R, HEAD_DIM)
    qr = q_rot.reshape(B, Q, N_KV, R, HEAD_DIM)
    p_c = masked_softmax(jnp.einsum('bqgrd,bngd->bgrqn', qp, kc) * scale, cmp_end[None, :] <= qpos[:, None])
    o_c = jnp.einsum('bgrqn,bngd->bqgrd', p_c, vc)
    n_blk = -(-L // SEL_BLOCK)
    imp = jnp.einsum('bgrqn,nj->bgqj', p_c, sel_overlap(kc.shape[1], n_blk))
    blk = jnp.arange(n_blk)[None, :]
    cur = (qpos // SEL_BLOCK)[:, None]
    forced = (blk == 0) | (blk == cur) | (blk == cur - 1)
    score = jnp.where(blk <= cur, jnp.where(forced, FORCE_SCORE, imp), -1.0)
    _, top = lax.top_k(score, min(SEL_TOP, n_blk))
    tok = (top[..., None] * SEL_BLOCK + jnp.arange(SEL_BLOCK)).reshape(B, N_KV, Q, -1)
    idx = jnp.minimum(tok, L - 1)
    ks = gather_rows(jnp.swapaxes(k_sel, 1, 2), idx)
    vs = gather_rows(jnp.swapaxes(v_sel, 1, 2), idx)
    p_s = masked_softmax(jnp.einsum('bqgrd,bgqnd->bgrqn', qr, ks) * scale, (tok <= qpos[:, None])[:, :, None])
    o_s = jnp.einsum('bgrqn,bgqnd->bqgrd', p_s, vs)
    dpos = qpos[:, None] - kwpos[None, :]
    mask_w = (dpos >= 0) & (dpos < WINDOW) & (kwpos[None, :] >= 0)
    p_w = masked_softmax(jnp.einsum('bqgrd,bkgd->bgrqk', qr, kw) * scale, mask_w)
    o_w = jnp.einsum('bgrqk,bkgd->bqgrd', p_w, vw)
    g = gates.reshape(B, Q, N_KV, R, 3)
    out = g[..., 0:1] * o_c + g[..., 1:2] * o_s + g[..., 2:3] * o_w
    return out.reshape(B, Q, N_WIDTH).astype(q.dtype)


def nsa_prompt(q, q_rot, pos, k_cmp, v_cmp, k_sel, v_sel, k_win, v_win, gates, cmp):
    pe_k, pe_v, w1_k, w2_k, w1_v, w2_v = cmp
    B, T = q.shape[:2]
    kc = compress(k_cmp, pe_k, w1_k, w2_k)
    vc = compress(v_cmp, pe_v, w1_v, w2_v)
    cmp_end = jnp.arange(kc.shape[1]) * CMP_STRIDE + (CMP_LEN - 1)
    pad = ((0, 0), (WINDOW, 0), (0, 0), (0, 0))
    kw_pad = jnp.pad(k_win, pad)
    vw_pad = jnp.pad(v_win, pad)

    def one_block(i):
        s0 = i * Q_BLOCK
        take = lambda a, n: lax.dynamic_slice_in_dim(a, s0, n, axis=1)
        qpos = lax.dynamic_slice_in_dim(pos, s0, Q_BLOCK)
        kwpos = qpos[0] - WINDOW + jnp.arange(WINDOW + Q_BLOCK)
        return nsa_core(take(q, Q_BLOCK), take(q_rot, Q_BLOCK), qpos, kc, vc, cmp_end, k_sel, v_sel,
                        take(kw_pad, WINDOW + Q_BLOCK), take(vw_pad, WINDOW + Q_BLOCK), kwpos, take(gates, Q_BLOCK))

    out = lax.map(one_block, jnp.arange(T // Q_BLOCK))
    out = jnp.moveaxis(out, 0, 1).reshape(B, T, N_WIDTH)
    wp = min(WINDOW, T)
    return out, jnp.stack([k_win[:, -wp:], v_win[:, -wp:]], axis=2)


def nsa_sample(q, q_rot, pos, k_cmp, v_cmp, k_sel, v_sel, k_win, v_win, gates, cmp, past, win_buf):
    pe_k, pe_v, w1_k, w2_k, w1_v, w2_v = cmp
    T = q.shape[1]
    P = past.shape[1]
    WB = win_buf.shape[1]
    full = lambda i, new: jnp.concatenate([past[:, :, i].astype(new.dtype), new], axis=1)
    kc = compress(full(0, k_cmp), pe_k, w1_k, w2_k)
    vc = compress(full(1, v_cmp), pe_v, w1_v, w2_v)
    cmp_end = jnp.arange(kc.shape[1]) * CMP_STRIDE + (CMP_LEN - 1)
    kw = jnp.concatenate([win_buf[:, :, 0].astype(k_win.dtype), k_win], axis=1)
    vw = jnp.concatenate([win_buf[:, :, 1].astype(v_win.dtype), v_win], axis=1)
    kwpos = P - WB + jnp.arange(WB + T)
    out = nsa_core(q, q_rot, pos, kc, vc, cmp_end, full(2, k_sel), full(3, v_sel), kw, vw, kwpos, gates)
    return out, jnp.stack([kw[:, -WB:], vw[:, -WB:]], axis=2)


def layer_step(x, pos, W, l, conv_buf, C0, n0, m0, nsa_fn):
    B, T, _ = x.shape
    h = rmsnorm(x, W['norm1_g'][l])
    u, v, o, ig, fg, q, kv, ng, ga, gb = split_cols(h @ W['w_in'][l])
    h_a, conv_new, C, n, m = mlstm_branch(u, v, o, ig, fg, conv_buf, C0, n0, m0, W['conv_w'][l], W['conv_b'][l],
                                          W['w_mq'][l], W['w_mk'][l], W['b_gates'][l], W['mlstm_norm_g'][l])
    q = q.reshape(B, T, N_HEADS, HEAD_DIM)
    kv = kv.reshape(B, T, 6, N_KV, HEAD_DIM)
    k_cmp, v_cmp, v_sel, v_win = kv[:, :, 0], kv[:, :, 1], kv[:, :, 3], kv[:, :, 5]
    k_sel = rope_partial(kv[:, :, 2], pos)
    k_win = rope_partial(kv[:, :, 4], pos)
    q_rot = rope_partial(q, pos)
    gates = jax.nn.sigmoid(ng.astype(jnp.float32)).reshape(B, T, N_HEADS, 3)
    h_b, win_new = nsa_fn(q, q_rot, pos, k_cmp, v_cmp, k_sel, v_sel, k_win, v_win, gates)
    merged = jax.nn.sigmoid(ga) * (h_a @ W['w_branch_a'][l]) + jax.nn.sigmoid(gb) * (h_b @ W['w_branch_b'][l])
    x = x + merged.astype(x.dtype) @ W['w_out'][l]
    g_ff, u_ff = jnp.split(rmsnorm(x, W['norm2_g'][l]) @ W['w_ffn_up'][l], 2, axis=-1)
    x = x + (jax.nn.silu(g_ff) * u_ff) @ W['w_ffn_down'][l]
    kv_rows = jnp.stack([k_cmp, v_cmp, k_sel, v_sel], axis=2)
    return x, kv_rows, win_new, C, n, m, conv_new


def setup_inputs(seed: int = 0) -> dict:
    key = jax.random.key(seed)
    ks = jax.random.split(key, 40)
    nrm = lambda i, shape, s: s * jax.random.normal(ks[i], shape, dtype=jnp.float32)
    n_pages = PAST_LEN // PAGE_SIZE
    n_used = DEC_BATCH * n_pages
    n_pool = n_used + n_used // 4
    page_table = jax.random.permutation(ks[0], n_pool)[:n_used].reshape(DEC_BATCH, n_pages).astype(jnp.int32)
    wb = min(WINDOW, PAST_LEN)
    b_gates = jnp.concatenate([nrm(1, (DEPTH, M_HEADS), 0.1),
                               jnp.linspace(3.0, 6.0, M_HEADS, dtype=jnp.float32)[None, :] + nrm(2, (DEPTH, M_HEADS), 0.1)], axis=1)
    return {
        'x_prompt': nrm(3, (BATCH, SEQ, D_MODEL), 1.0),
        'x_sample': nrm(4, (DEC_BATCH, DEC_SEQ, D_MODEL), 1.0),
        'cache_kv_pages': nrm(5, (DEPTH, n_pool, PAGE_SIZE, 4, N_KV, HEAD_DIM), 1.0),
        'page_table': page_table,
        'cache_win': nrm(6, (DEPTH, DEC_BATCH, wb, 2, N_KV, HEAD_DIM), 1.0),
        'state_mlstm_C': nrm(7, (DEPTH, DEC_BATCH, M_HEADS, M_HEAD_DIM, M_HEAD_DIM), 0.5),
        'state_mlstm_n': nrm(8, (DEPTH, DEC_BATCH, M_HEADS, M_HEAD_DIM), 0.5),
        'state_mlstm_m': nrm(9, (DEPTH, DEC_BATCH, M_HEADS), 1.0),
        'state_mlstm_conv': nrm(10, (DEPTH, DEC_BATCH, M_CONV - 1, M_WIDTH), 1.0),
        'norm1_g': 1.0 + nrm(11, (DEPTH, D_MODEL), 0.01),
        'w_in': nrm(12, (DEPTH, D_MODEL, IN_WIDTH), D_MODEL ** -0.5),
        'b_gates': b_gates,
        'conv_w': nrm(13, (DEPTH, M_CONV, M_WIDTH), M_CONV ** -0.5),
        'conv_b': nrm(14, (DEPTH, M_WIDTH), 0.01),
        'w_mq': nrm(15, (DEPTH, M_HEADS, M_HEAD_DIM, M_HEAD_DIM), M_HEAD_DIM ** -0.5),
        'w_mk': nrm(16, (DEPTH, M_HEADS, M_HEAD_DIM, M_HEAD_DIM), M_HEAD_DIM ** -0.5),
        'mlstm_norm_g': 1.0 + nrm(17, (DEPTH, M_WIDTH), 0.01),
        'cmp_pe_k': nrm(18, (DEPTH, CMP_LEN, HEAD_DIM), 0.02),
        'cmp_pe_v': nrm(19, (DEPTH, CMP_LEN, HEAD_DIM), 0.02),
        'cmp_w1_k': nrm(20, (DEPTH, CMP_LEN, HEAD_DIM, CMP_HID), (CMP_LEN * HEAD_DIM) ** -0.5),
        'cmp_w2_k': nrm(21, (DEPTH, CMP_HID, HEAD_DIM), CMP_HID ** -0.5),
        'cmp_w1_v': nrm(22, (DEPTH, CMP_LEN, HEAD_DIM, CMP_HID), (CMP_LEN * HEAD_DIM) ** -0.5),
        'cmp_w2_v': nrm(23, (DEPTH, CMP_HID, HEAD_DIM), CMP_HID ** -0.5),
        'w_branch_a': nrm(24, (DEPTH, M_WIDTH, D_MODEL), M_WIDTH ** -0.5),
        'w_branch_b': nrm(25, (DEPTH, N_WIDTH, D_MODEL), N_WIDTH ** -0.5),
        'w_out': nrm(26, (DEPTH, D_MODEL, D_MODEL), D_MODEL ** -0.5),
        'norm2_g': 1.0 + nrm(27, (DEPTH, D_MODEL), 0.01),
        'w_ffn_up': nrm(28, (DEPTH, D_MODEL, 2 * D_FF), D_MODEL ** -0.5),
        'w_ffn_down': nrm(29, (DEPTH, D_FF, D_MODEL), D_FF ** -0.5),
        'final_norm_g': 1.0 + nrm(30, (D_MODEL,), 0.01),
    }


def reference(x_prompt, x_sample, cache_kv_pages, page_table, cache_win, state_mlstm_C, state_mlstm_n, state_mlstm_m,
              state_mlstm_conv, norm1_g, w_in, b_gates, conv_w, conv_b, w_mq, w_mk, mlstm_norm_g, cmp_pe_k, cmp_pe_v,
              cmp_w1_k, cmp_w2_k, cmp_w1_v, cmp_w2_v, w_branch_a, w_branch_b, w_out, norm2_g, w_ffn_up, w_ffn_down,
              final_norm_g):
    W = {'norm1_g': norm1_g, 'w_in': w_in, 'b_gates': b_gates, 'conv_w': conv_w, 'conv_b': conv_b, 'w_mq': w_mq,
         'w_mk': w_mk, 'mlstm_norm_g': mlstm_norm_g, 'w_branch_a': w_branch_a, 'w_branch_b': w_branch_b,
         'w_out': w_out, 'norm2_g': norm2_g, 'w_ffn_up': w_ffn_up, 'w_ffn_down': w_ffn_down}
    Bp, Tp, _ = x_prompt.shape
    Bs, Ts, _ = x_sample.shape
    n_pages = page_table.shape[1]
    past_len = n_pages * cache_kv_pages.shape[2]
    pos_p = jnp.arange(Tp)
    pos_s = past_len + jnp.arange(Ts)

    xp = x_prompt
    xs = x_sample
    p_kv, p_win, p_C, p_n, p_m, p_conv = [], [], [], [], [], []
    s_kv, s_win, s_C, s_n, s_m, s_conv = [], [], [], [], [], []
    for l in range(DEPTH):
        cmp = (cmp_pe_k[l], cmp_pe_v[l], cmp_w1_k[l], cmp_w2_k[l], cmp_w1_v[l], cmp_w2_v[l])
        xp, kv_r, win_r, C, n, m, cv = layer_step(
            xp, pos_p, W, l,
            jnp.zeros((Bp, M_CONV - 1, M_WIDTH), xp.dtype),
            jnp.zeros((Bp, M_HEADS, M_HEAD_DIM, M_HEAD_DIM), jnp.float32),
            jnp.zeros((Bp, M_HEADS, M_HEAD_DIM), jnp.float32),
            jnp.zeros((Bp, M_HEADS), jnp.float32),
            functools.partial(nsa_prompt, cmp=cmp))
        p_kv.append(kv_r); p_win.append(win_r); p_C.append(C); p_n.append(n); p_m.append(m); p_conv.append(cv)
        past = cache_kv_pages[l][page_table].reshape(Bs, past_len, 4, N_KV, HEAD_DIM)
        xs, kv_r, win_r, C, n, m, cv = layer_step(
            xs, pos_s, W, l, state_mlstm_conv[l], state_mlstm_C[l], state_mlstm_n[l], state_mlstm_m[l],
            functools.partial(nsa_sample, cmp=cmp, past=past, win_buf=cache_win[l]))
        s_kv.append(kv_r); s_win.append(win_r); s_C.append(C); s_n.append(n); s_m.append(m); s_conv.append(cv)

    y_prompt = rmsnorm(xp, final_norm_g)
    y_sample = rmsnorm(xs, final_norm_g)
    new_p_kv = jnp.stack(p_kv)
    new_p_win = jnp.stack(p_win)
    new_p_C = jnp.stack(p_C)
    new_p_n = jnp.stack(p_n)
    new_p_m = jnp.stack(p_m)
    new_p_conv = jnp.stack(p_conv)
    new_s_kv = jnp.stack(s_kv)
    new_s_win = jnp.stack(s_win)
    new_s_C = jnp.stack(s_C)
    new_s_n = jnp.stack(s_n)
    new_s_m = jnp.stack(s_m)
    new_s_conv = jnp.stack(s_conv)
    return (y_prompt, y_sample, new_p_kv, new_p_win, new_p_C, new_p_n, new_p_m, new_p_conv,
            new_s_kv, new_s_win, new_s_C, new_s_n, new_s_m, new_s_conv)
```

```python
import functools

import jax
import jax.numpy as jnp
import numpy as np
from jax import lax
from jax.experimental import pallas as pl
from jax.experimental.pallas import tpu as pltpu

f32 = jnp.float32
bf16 = jnp.bfloat16

D_MODEL = 1024
M_HEADS = 4
M_WIDTH = 512
M_HEAD_DIM = 128
M_CONV = 4
M_CHUNK = 64
HEAD_DIM = 64
N_WIDTH = 512
N_HEADS = 8
N_KV = 2
KV_W = 128
CMP_STRIDE = 16
CMP_LEN = 32
SEL_BLOCK = 64
SEL_TOP = 16
WINDOW = 512
Q_BLOCK = 128
ROT_DIM = 16
ROPE_THETA = 500000.0
D_FF = 2816
EPS = 1e-6
NEG = -1e30
TINY = 1e-30
FORCE_SCORE = 1e9

SAMPLE_PAD = 8
ROW_TILE = 256
FF_CHUNK = 256
VMEM_LIMIT = 56 * 1024 * 1024

C_XM, C_SM, C_Q, C_KV, C_WIN, C_ATT, C_GATE = 0, 1536, 1664, 2688, 3200, 3456, 4480
C_TOTAL = 6528
O_U, O_V, O_O, O_IG, O_FG, O_Q, O_KV, O_NG, O_GA, O_GB = 0, 512, 1024, 1536, 1540, 1544, 2056, 2824, 2848, 3872
IN_WIDTH = 4896


def _in_proj_columns():
    z = IN_WIDTH
    idx = list(range(O_U, O_IG))
    idx += list(range(O_IG, O_IG + 8)) + list(range(O_NG, O_NG + 24)) + [z] * 96
    for h in range(N_HEADS):
        idx += list(range(O_Q + h * 64, O_Q + (h + 1) * 64)) + [z] * 64
    idx += list(range(O_KV, O_KV + 512))
    idx += list(range(O_KV + 512, O_KV + 768))
    for c in (2, 3, 4, 5):
        for g in range(N_KV):
            s = O_KV + c * 128 + g * 64
            idx += list(range(s, s + 64)) + [z] * 64
    idx += list(range(O_GA, O_GA + 2048))
    assert len(idx) == C_TOTAL
    scale = np.ones((C_TOTAL,), np.float32)
    scale[C_Q:C_KV] = HEAD_DIM ** -0.5
    return np.asarray(idx, np.int32), scale


def _rope_tables(seq, past_len, n_sample_rows):
    half = ROT_DIM // 2
    inv = ROPE_THETA ** (-jnp.arange(half, dtype=f32) / half)
    pos = jnp.concatenate([jnp.arange(seq), past_len + (jnp.arange(n_sample_rows) % SAMPLE_PAD)]).astype(f32)
    ang = pos[:, None] * inv[None, :]
    cos8, sin8 = jnp.cos(ang), jnp.sin(ang)
    n = pos.shape[0]
    one = jnp.ones((n, 64 - ROT_DIM), f32)
    zero = jnp.zeros((n, 64 - ROT_DIM), f32)
    z8 = jnp.zeros((n, half), f32)
    cos = jnp.concatenate([cos8, cos8, one], axis=1)
    sa = jnp.concatenate([-sin8, z8, zero], axis=1)
    sb = jnp.concatenate([z8, sin8, zero], axis=1)
    tile2 = lambda a: jnp.concatenate([a, a], axis=1)
    return tile2(cos), tile2(sa), tile2(sb)


def _rms(x, g):
    return x * lax.rsqrt(jnp.mean(x * x, axis=-1, keepdims=True) + EPS) * g


def _in_proj_body(x_ref, g_ref, w_ref, wt_ref, cos_ref, sa_ref, sb_ref,
                  xm_ref, sm_ref, smt_ref, qc_ref, qr_ref, kv_ref, win_ref, att_ref, gate_ref):
    hb = _rms(x_ref[...], g_ref[...]).astype(bf16)
    cos, sa, sb = cos_ref[...], sa_ref[...], sb_ref[...]
    lane = lax.broadcasted_iota(jnp.int32, cos.shape, 1)

    def rope(v):
        return v * cos + pltpu.roll(v, 128 - ROT_DIM // 2, axis=1) * sa + pltpu.roll(v, ROT_DIM // 2, axis=1) * sb

    def proj(c0, n):
        return jnp.dot(hb, w_ref[:, c0:c0 + n], preferred_element_type=f32)

    xm_ref[...] = proj(C_XM, 1536)
    sm_ref[...] = proj(C_SM, 128)
    smt_ref[...] = lax.dot_general(wt_ref[...], hb, (((1,), (1,)), ((), ())), preferred_element_type=f32)
    for h in range(N_HEADS):
        q = proj(C_Q + h * 128, 128)
        qc_ref[:, h * 128:(h + 1) * 128] = q.astype(bf16)
        qr_ref[:, h * 128:(h + 1) * 128] = rope(q).astype(bf16)
    kv = proj(C_KV, 512)
    kv_ref[:, 0:256] = kv[:, 0:256]
    kv_ref[:, 256:384] = rope(kv[:, 256:384])
    kv_ref[:, 384:512] = kv[:, 384:512]
    win = proj(C_WIN, 256)
    win_ref[:, 0:128] = rope(win[:, 0:128])
    win_ref[:, 128:256] = win[:, 128:256]
    for j in range(8):
        a = proj(C_ATT + j * 128, 128)
        if j in (0, 1, 4, 5):
            a = rope(a)
        else:
            a = jnp.where(lane == 64, 1.0, a)
        att_ref[:, j * 128:(j + 1) * 128] = a.astype(bf16)
    gate_ref[...] = proj(C_GATE, 2048)


def _in_proj(x_all, norm_g, w_all, w_t, tabs, seq):
    m = x_all.shape[0]
    n_prompt_tiles_per_seq = seq // ROW_TILE
    n_prompt_tiles = (m - (tabs[0].shape[0] - seq)) // ROW_TILE

    def tab_map(i):
        return (jnp.where(i < n_prompt_tiles, i % n_prompt_tiles_per_seq, n_prompt_tiles_per_seq + i - n_prompt_tiles), 0)

    row = lambda w: pl.BlockSpec((ROW_TILE, w), lambda i: (i, 0))
    const = lambda s: pl.BlockSpec(s, lambda i: (0, 0), pipeline_mode=pl.Buffered(1))
    tab = pl.BlockSpec((ROW_TILE, 128), tab_map)
    widths = (1536, 128, None, 1024, 1024, 512, 256, 1024, 2048)
    dts = (f32, f32, f32, bf16, bf16, f32, f32, bf16, f32)
    out_shape = [jax.ShapeDtypeStruct((8, m) if w is None else (m, w), d) for w, d in zip(widths, dts)]
    out_specs = [pl.BlockSpec((8, ROW_TILE), lambda i: (0, i)) if w is None else row(w) for w in widths]
    return pl.pallas_call(
        _in_proj_body, grid=(m // ROW_TILE,),
        in_specs=[row(D_MODEL), const((1, D_MODEL)), const((D_MODEL, C_TOTAL)), const((8, D_MODEL)), tab, tab, tab],
        out_specs=out_specs, out_shape=out_shape, name="in_proj",
        compiler_params=pltpu.CompilerParams(dimension_semantics=("parallel",), vmem_limit_bytes=VMEM_LIMIT),
    )(x_all, norm_g, w_all, w_t, *tabs)


def _mix_ffn_body(x_ref, ha_ref, hb_ref, gate_ref, wa_ref, wb_ref, wo_ref, g2_ref, wup_ref, wdn_ref, gf_ref,
                  xo_ref, y_ref):
    a = jnp.dot(ha_ref[...].astype(bf16), wa_ref[...], preferred_element_type=f32)
    b = jnp.dot(hb_ref[...].astype(bf16), wb_ref[...], preferred_element_type=f32)
    merged = jax.nn.sigmoid(gate_ref[:, 0:D_MODEL]) * a + jax.nn.sigmoid(gate_ref[:, D_MODEL:2 * D_MODEL]) * b
    x1 = x_ref[...] + jnp.dot(merged.astype(bf16), wo_ref[...], preferred_element_type=f32)
    hn = _rms(x1, g2_ref[...]).astype(bf16)
    acc = x1
    for c in range(D_FF // FF_CHUNK):
        lo = c * FF_CHUNK
        g = jnp.dot(hn, wup_ref[:, lo:lo + FF_CHUNK], preferred_element_type=f32)
        u = jnp.dot(hn, wup_ref[:, D_FF + lo:D_FF + lo + FF_CHUNK], preferred_element_type=f32)
        act = (g * jax.nn.sigmoid(g) * u).astype(bf16)
        acc = acc + jnp.dot(act, wdn_ref[lo:lo + FF_CHUNK, :], preferred_element_type=f32)
    xo_ref[...] = acc
    y_ref[...] = _rms(acc, gf_ref[...])


def _mix_ffn(x_all, h_a, h_b, gates, wa, wb, wo, g2, wup, wdn, gf):
    m = x_all.shape[0]
    row = lambda w: pl.BlockSpec((ROW_TILE, w), lambda i: (i, 0))
    const = lambda s: pl.BlockSpec(s, lambda i: (0, 0), pipeline_mode=pl.Buffered(1))
    return pl.pallas_call(
        _mix_ffn_body, grid=(m // ROW_TILE,),
        in_specs=[row(D_MODEL), row(h_a.shape[1]), row(h_b.shape[1]), row(2 * D_MODEL),
                  const(wa.shape), const(wb.shape), const(wo.shape), const((1, D_MODEL)),
                  const(wup.shape), const(wdn.shape), const((1, D_MODEL))],
        out_specs=[row(D_MODEL), row(D_MODEL)],
        out_shape=[jax.ShapeDtypeStruct((m, D_MODEL), f32)] * 2, name="mix_ffn",
        compiler_params=pltpu.CompilerParams(dimension_semantics=("parallel",), vmem_limit_bytes=VMEM_LIMIT),
    )(x_all, h_a, h_b, gates, wa, wb, wo, g2, wup, wdn, gf)


def _j_masked_softmax(s, mask):
    s = jnp.where(mask, s.astype(f32), NEG)
    p = jnp.exp(s - jnp.max(s, axis=-1, keepdims=True)) * mask
    return p / jnp.maximum(jnp.sum(p, axis=-1, keepdims=True), TINY)


def _j_causal_conv(u, buf, w, b):
    T = u.shape[1]
    xp = jnp.concatenate([buf.astype(u.dtype), u], axis=1)
    y = b + sum(xp[:, j:j + T] * w[j] for j in range(M_CONV))
    return jax.nn.silu(y), xp[:, T:]


def _j_mlstm_chunkwise(q, k, v, i_pre, log_f, C0, n0, m0):
    B, T, H, _ = q.shape
    cs = T if T <= M_CHUNK else M_CHUNK
    nc = T // cs

    def to_chunks(a):
        a = a.astype(f32).reshape((B, nc, cs, H) + a.shape[3:])
        return jnp.moveaxis(a, (1, 3), (0, 2))

    tril = jnp.tril(jnp.ones((cs, cs), dtype=bool))

    def step(carry, inp):
        C, n, m = carry
        qc, kc, vc, ic, fc = inp
        F = jnp.cumsum(fc, axis=-1)
        a = ic - F
        mt = F + jnp.maximum(m[..., None], lax.cummax(a, axis=a.ndim - 1))
        logD = jnp.where(tril, a[..., None, :] + (F - mt)[..., :, None], -jnp.inf)
        S = jnp.einsum('bhtd,bhsd->bhts', qc, kc) * jnp.exp(logD)
        inter = jnp.exp(m[..., None] + F - mt)
        num = jnp.einsum('bhts,bhse->bhte', S, vc) + inter[..., None] * jnp.einsum('bhtd,bhde->bhte', qc, C)
        den = jnp.sum(S, axis=-1) + inter * jnp.einsum('bhtd,bhd->bht', qc, n)
        h = num / jnp.maximum(jnp.abs(den), jnp.exp(-mt))[..., None]
        m_new = mt[..., -1]
        w = jnp.exp(a + (F[..., -1] - m_new)[..., None])
        decay = jnp.exp(m + F[..., -1] - m_new)
        C_new = decay[..., None, None] * C + jnp.einsum('bhs,bhsd,bhse->bhde', w, kc, vc)
        n_new = decay[..., None] * n + jnp.einsum('bhs,bhsd->bhd', w, kc)
        return (C_new, n_new, m_new), h

    init = (C0.astype(f32), n0.astype(f32), m0.astype(f32))
    (C, n, m), h = lax.scan(step, init, (to_chunks(q), to_chunks(k), to_chunks(v), to_chunks(i_pre), to_chunks(log_f)))
    h = jnp.moveaxis(h, (0, 2), (1, 3)).reshape(B, T, H, v.shape[-1])
    return h, C, n, m


def _j_mlstm_branch(u, v, o, ig, fg, conv_buf, C0, n0, m0, conv_w, conv_b, w_mq, w_mk, b_gates, norm_g):
    B, T, _ = u.shape
    c, conv_new = _j_causal_conv(u, conv_buf, conv_w, conv_b)
    c = c.reshape(B, T, M_HEADS, M_HEAD_DIM)
    q = jnp.einsum('bthd,hde->bthe', c, w_mq)
    k = jnp.einsum('bthd,hde->bthe', c, w_mk) * (M_HEAD_DIM ** -0.5)
    v = v.reshape(B, T, M_HEADS, M_HEAD_DIM)
    pre = jnp.concatenate([ig, fg], axis=-1).astype(f32) + b_gates.astype(f32)
    i_pre = pre[..., :M_HEADS]
    log_f = jax.nn.log_sigmoid(pre[..., M_HEADS:])
    h, C, n, m = _j_mlstm_chunkwise(q, k, v, i_pre, log_f, C0, n0, m0)
    h = h * lax.rsqrt(jnp.mean(h * h, axis=-1, keepdims=True) + EPS) * norm_g.astype(f32).reshape(M_HEADS, M_HEAD_DIM)
    out = h.reshape(B, T, M_WIDTH) * jax.nn.sigmoid(o.astype(f32))
    return out, conv_new, C, n, m


def _j_compress(raw, pe, w1, w2):
    B, L = raw.shape[:2]
    n_chunk = L // CMP_STRIDE
    ch = raw[:, :n_chunk * CMP_STRIDE].reshape(B, n_chunk, CMP_STRIDE, N_KV, HEAD_DIM)
    blocks = jnp.concatenate([ch[:, :-1], ch[:, 1:]], axis=2) + pe[None, None, :, None, :]
    hid = jax.nn.gelu(jnp.einsum('bnlgd,lde->bnge', blocks, w1))
    return jnp.einsum('bnge,ef->bngf', hid, w2)


def _j_sel_overlap(n_cmp, n_blk):
    start = np.arange(n_cmp) * CMP_STRIDE
    bs = np.arange(n_blk) * SEL_BLOCK
    ov = np.minimum(start[:, None] + CMP_LEN, bs[None, :] + SEL_BLOCK) - np.maximum(start[:, None], bs[None, :])
    return jnp.asarray(np.clip(ov, 0, None) / CMP_LEN, dtype=f32)


def _j_gather_rows(kv, idx):
    return jax.vmap(jax.vmap(lambda a, i: a[i]))(kv, idx)


def _j_nsa_core(q, q_rot, qpos, kc, vc, cmp_end, k_sel, v_sel, kw, vw, kwpos, gates):
    B, Q = q.shape[:2]
    L = k_sel.shape[1]
    R = N_HEADS // N_KV
    scale = HEAD_DIM ** -0.5
    qp = q.reshape(B, Q, N_KV, R, HEAD_DIM)
    qr = q_rot.reshape(B, Q, N_KV, R, HEAD_DIM)
    p_c = _j_masked_softmax(jnp.einsum('bqgrd,bngd->bgrqn', qp, kc) * scale, cmp_end[None, :] <= qpos[:, None])
    o_c = jnp.einsum('bgrqn,bngd->bqgrd', p_c, vc)
    n_blk = -(-L // SEL_BLOCK)
    imp = jnp.einsum('bgrqn,nj->bgqj', p_c, _j_sel_overlap(kc.shape[1], n_blk))
    blk = jnp.arange(n_blk)[None, :]
    cur = (qpos // SEL_BLOCK)[:, None]
    forced = (blk == 0) | (blk == cur) | (blk == cur - 1)
    score = jnp.where(blk <= cur, jnp.where(forced, FORCE_SCORE, imp), -1.0)
    _, top = lax.top_k(score, min(SEL_TOP, n_blk))
    tok = (top[..., None] * SEL_BLOCK + jnp.arange(SEL_BLOCK)).reshape(B, N_KV, Q, -1)
    idx = jnp.minimum(tok, L - 1)
    ks = _j_gather_rows(jnp.swapaxes(k_sel, 1, 2), idx)
    vs = _j_gather_rows(jnp.swapaxes(v_sel, 1, 2), idx)
    p_s = _j_masked_softmax(jnp.einsum('bqgrd,bgqnd->bgrqn', qr, ks) * scale, (tok <= qpos[:, None])[:, :, None])
    o_s = jnp.einsum('bgrqn,bgqnd->bqgrd', p_s, vs)
    dpos = qpos[:, None] - kwpos[None, :]
    mask_w = (dpos >= 0) & (dpos < WINDOW) & (kwpos[None, :] >= 0)
    p_w = _j_masked_softmax(jnp.einsum('bqgrd,bkgd->bgrqk', qr, kw) * scale, mask_w)
    o_w = jnp.einsum('bgrqk,bkgd->bqgrd', p_w, vw)
    g = gates.reshape(B, Q, N_KV, R, 3)
    out = g[..., 0:1] * o_c + g[..., 1:2] * o_s + g[..., 2:3] * o_w
    return out.reshape(B, Q, N_WIDTH)


def _j_nsa_prompt(q, q_rot, pos, k_cmp, v_cmp, k_sel, v_sel, k_win, v_win, gates, cmp):
    pe_k, pe_v, w1_k, w2_k, w1_v, w2_v = cmp
    B, T = q.shape[:2]
    kc = _j_compress(k_cmp, pe_k, w1_k, w2_k)
    vc = _j_compress(v_cmp, pe_v, w1_v, w2_v)
    cmp_end = jnp.arange(kc.shape[1]) * CMP_STRIDE + (CMP_LEN - 1)
    pad = ((0, 0), (WINDOW, 0), (0, 0), (0, 0))
    kw_pad = jnp.pad(k_win, pad)
    vw_pad = jnp.pad(v_win, pad)

    def one_block(i):
        s0 = i * Q_BLOCK
        take = lambda a, n: lax.dynamic_slice_in_dim(a, s0, n, axis=1)
        qpos = lax.dynamic_slice_in_dim(pos, s0, Q_BLOCK)
        kwpos = qpos[0] - WINDOW + jnp.arange(WINDOW + Q_BLOCK)
        return _j_nsa_core(take(q, Q_BLOCK), take(q_rot, Q_BLOCK), qpos, kc, vc, cmp_end, k_sel, v_sel,
                           take(kw_pad, WINDOW + Q_BLOCK), take(vw_pad, WINDOW + Q_BLOCK), kwpos, take(gates, Q_BLOCK))

    out = lax.map(one_block, jnp.arange(T // Q_BLOCK))
    return jnp.moveaxis(out, 0, 1).reshape(B, T, N_WIDTH)


def _j_nsa_sample(q, q_rot, pos, k_cmp, v_cmp, k_sel, v_sel, k_win, v_win, gates, cmp, past, win_buf):
    pe_k, pe_v, w1_k, w2_k, w1_v, w2_v = cmp
    T = q.shape[1]
    P = past.shape[1]
    WB = win_buf.shape[1]
    full = lambda i, new: jnp.concatenate([past[:, :, i].astype(new.dtype), new], axis=1)
    kc = _j_compress(full(0, k_cmp), pe_k, w1_k, w2_k)
    vc = _j_compress(full(1, v_cmp), pe_v, w1_v, w2_v)
    cmp_end = jnp.arange(kc.shape[1]) * CMP_STRIDE + (CMP_LEN - 1)
    kw = jnp.concatenate([win_buf[:, :, 0].astype(k_win.dtype), k_win], axis=1)
    vw = jnp.concatenate([win_buf[:, :, 1].astype(v_win.dtype), v_win], axis=1)
    kwpos = P - WB + jnp.arange(WB + T)
    out = _j_nsa_core(q, q_rot, pos, kc, vc, cmp_end, full(2, k_sel), full(3, v_sel), kw, vw, kwpos, gates)
    return out, jnp.stack([kw[:, -WB:], vw[:, -WB:]], axis=2)


def kernel(x_prompt, x_sample, cache_kv_pages, page_table, cache_win, state_mlstm_C, state_mlstm_n, state_mlstm_m, state_mlstm_conv, norm1_g, w_in, b_gates, conv_w, conv_b, w_mq, w_mk, mlstm_norm_g, cmp_pe_k, cmp_pe_v, cmp_w1_k, cmp_w2_k, cmp_w1_v, cmp_w2_v, w_branch_a, w_branch_b, w_out, norm2_g, w_ffn_up, w_ffn_down, final_norm_g):
    Bp, Tp, _ = x_prompt.shape
    Bs, Ts, _ = x_sample.shape
    depth = w_in.shape[0]
    n_pages = page_table.shape[1]
    past_len = n_pages * cache_kv_pages.shape[2]
    mp = Bp * Tp
    ms = Bs * SAMPLE_PAD
    pos_p = jnp.arange(Tp)
    pos_s = past_len + jnp.arange(Ts)

    xs_pad = jnp.pad(x_sample, ((0, 0), (0, SAMPLE_PAD - Ts), (0, 0)))
    x_all = jnp.concatenate([x_prompt.reshape(mp, D_MODEL), xs_pad.reshape(ms, D_MODEL)], axis=0)
    tabs = _rope_tables(Tp, past_len, ROW_TILE * (ms // ROW_TILE))
    col_idx, col_scale = _in_proj_columns()
    wb_rows = np.asarray([h * 64 + d if d < 64 else N_WIDTH for h in range(N_HEADS) for d in range(128)], np.int32)

    outs = {k: [] for k in ("p_kv", "p_win", "p_C", "p_n", "p_m", "p_conv", "s_kv", "s_win", "s_C", "s_n", "s_m", "s_conv")}
    y_all = None
    for l in range(depth):
        w_ext = jnp.concatenate([w_in[l], jnp.zeros((D_MODEL, 1), f32)], axis=1)
        w_all = (w_ext[:, col_idx] * col_scale).astype(bf16)
        w_t = w_in[l][:, O_IG:O_IG + 8].T.astype(bf16)
        xm, sm, smt, qc, qr, kvrow, winrow, att, gates = _in_proj(
            x_all, norm1_g[l].reshape(1, D_MODEL), w_all, w_t, tabs, Tp)
        cmp = (cmp_pe_k[l], cmp_pe_v[l], cmp_w1_k[l], cmp_w2_k[l], cmp_w1_v[l], cmp_w2_v[l])

        def split(a, w):
            ap = a[:mp].reshape(Bp, Tp, w)
            as_ = a[mp:].reshape(Bs, SAMPLE_PAD, w)[:, :Ts]
            return ap, as_

        xm_p, xm_s = split(xm, 1536)
        sm_p, sm_s = split(sm, 128)
        qc_p, qc_s = split(qc, 1024)
        qr_p, qr_s = split(qr, 1024)
        kv_p, kv_s = split(kvrow, 512)
        win_p, win_s = split(winrow, 256)

        def mixers(xm_g, sm_g, qc_g, qr_g, kv_g, win_g, conv_buf, C0, n0, m0, nsa_fn):
            B, T = xm_g.shape[:2]
            u, v, o = xm_g[..., 0:512], xm_g[..., 512:1024], xm_g[..., 1024:1536]
            ig, fg, ng = sm_g[..., 0:4], sm_g[..., 4:8], sm_g[..., 8:32]
            h_a, conv_new, C, n, m = _j_mlstm_branch(u, v, o, ig, fg, conv_buf, C0, n0, m0, conv_w[l], conv_b[l],
                                                     w_mq[l], w_mk[l], b_gates[l], mlstm_norm_g[l])
            q = qc_g.astype(f32).reshape(B, T, N_HEADS, 128)[..., :64] * 8.0
            q_rot = qr_g.astype(f32).reshape(B, T, N_HEADS, 128)[..., :64] * 8.0
            kv5 = kv_g.reshape(B, T, 4, N_KV, HEAD_DIM)
            w5 = win_g.reshape(B, T, 2, N_KV, HEAD_DIM)
            ngates = jax.nn.sigmoid(ng).reshape(B, T, N_HEADS, 3)
            h_b = nsa_fn(q, q_rot, kv5[:, :, 0], kv5[:, :, 1], kv5[:, :, 2], kv5[:, :, 3], w5[:, :, 0], w5[:, :, 1], ngates)
            return h_a, h_b, conv_new, C, n, m

        zc = jnp.zeros((Bp, M_CONV - 1, M_WIDTH), f32)
        zC = jnp.zeros((Bp, M_HEADS, M_HEAD_DIM, M_HEAD_DIM), f32)
        zn = jnp.zeros((Bp, M_HEADS, M_HEAD_DIM), f32)
        zm = jnp.zeros((Bp, M_HEADS), f32)
        ha_p, hb_p, cv, C, n, m = mixers(
            xm_p, sm_p, qc_p, qr_p, kv_p, win_p, zc, zC, zn, zm,
            lambda q, qrot, *a: _j_nsa_prompt(q, qrot, pos_p, *a, cmp=cmp))
        outs["p_kv"].append(kv_p.reshape(Bp, Tp, 4, N_KV, HEAD_DIM))
        wp = min(WINDOW, Tp)
        outs["p_win"].append(win_p[:, -wp:].reshape(Bp, wp, 2, N_KV, HEAD_DIM))
        outs["p_C"].append(C); outs["p_n"].append(n); outs["p_m"].append(m); outs["p_conv"].append(cv)

        past = cache_kv_pages[l][page_table].reshape(Bs, past_len, 4, N_KV, HEAD_DIM)
        win_new_box = []

        def nsa_s(q, qrot, *a):
            out, wn = _j_nsa_sample(q, qrot, pos_s, *a, cmp=cmp, past=past, win_buf=cache_win[l])
            win_new_box.append(wn)
            return out

        ha_s, hb_s, cv, C, n, m = mixers(xm_s, sm_s, qc_s, qr_s, kv_s, win_s, state_mlstm_conv[l], state_mlstm_C[l],
                                         state_mlstm_n[l], state_mlstm_m[l], nsa_s)
        outs["s_kv"].append(kv_s.reshape(Bs, Ts, 4, N_KV, HEAD_DIM))
        outs["s_win"].append(win_new_box[0])
        outs["s_C"].append(C); outs["s_n"].append(n); outs["s_m"].append(m); outs["s_conv"].append(cv)

        def join(ap, as_):
            w = ap.shape[-1]
            as_pad = jnp.pad(as_, ((0, 0), (0, SAMPLE_PAD - Ts), (0, 0)))
            return jnp.concatenate([ap.reshape(mp, w), as_pad.reshape(ms, w)], axis=0)

        h_a = join(ha_p, ha_s)
        h_b = join(hb_p, hb_s)
        wb_ext = jnp.concatenate([w_branch_b[l], jnp.zeros((1, D_MODEL), f32)], axis=0)
        x_all, y_all = _mix_ffn(
            x_all, h_a, h_b, gates, w_branch_a[l].astype(bf16), w_branch_b[l].astype(bf16), w_out[l].astype(bf16),
            norm2_g[l].reshape(1, D_MODEL), w_ffn_up[l].astype(bf16), w_ffn_down[l].astype(bf16),
            final_norm_g.reshape(1, D_MODEL))

    y_prompt = y_all[:mp].reshape(Bp, Tp, D_MODEL)
    y_sample = y_all[mp:].reshape(Bs, SAMPLE_PAD, D_MODEL)[:, :Ts]
    st = lambda k: jnp.stack(outs[k])
    return (y_prompt, y_sample, st("p_kv"), st("p_win"), st("p_C"), st("p_n"), st("p_m"), st("p_conv"),
            st("s_kv"), st("s_win"), st("s_C"), st("s_n"), st("s_m"), st("s_conv"))
```

```python
import functools

import jax
import jax.numpy as jnp
import numpy as np
from jax import lax
from jax.experimental import pallas as pl
from jax.experimental.pallas import tpu as pltpu

f32 = jnp.float32
bf16 = jnp.bfloat16

D_MODEL = 1024
M_HEADS = 4
M_WIDTH = 512
M_HEAD_DIM = 128
M_CONV = 4
MLSTM_CHUNK = 256
HEAD_DIM = 64
N_WIDTH = 512
N_HEADS = 8
N_KV = 2
KV_W = 128
CMP_STRIDE = 16
CMP_LEN = 32
SEL_BLOCK = 64
SEL_TOP = 16
WINDOW = 512
ROT_DIM = 16
ROPE_THETA = 500000.0
D_FF = 2816
EPS = 1e-6
NEG = -1e30
TINY = 1e-30
FORCE_SCORE = 1e9

SAMPLE_PAD = 8
SEG = SAMPLE_PAD
SEQ_PER_STEP = 16
GB = 2
PAGE = 128
HD = M_HEAD_DIM
R = N_HEADS // N_KV
CMP_HID = 128
TQ = 128
TK = 512
NBLK_PAD = 64
HIGHEST = lax.Precision.HIGHEST
NT = (((1,), (1,)), ((), ()))
ROW_TILE = 256
FF_CHUNK = 256
VMEM_LIMIT = 56 * 1024 * 1024

C_XM, C_SM, C_Q, C_KV, C_WIN, C_ATT, C_GATE = 0, 1536, 1664, 2688, 3200, 3456, 4480
C_TOTAL = 6528
O_U, O_V, O_O, O_IG, O_FG, O_Q, O_KV, O_NG, O_GA, O_GB = 0, 512, 1024, 1536, 1540, 1544, 2056, 2824, 2848, 3872
IN_WIDTH = 4896


def _in_proj_columns():
    z = IN_WIDTH
    idx = list(range(O_U, O_IG))
    idx += list(range(O_IG, O_IG + 8)) + list(range(O_NG, O_NG + 24)) + [z] * 96
    for h in range(N_HEADS):
        idx += list(range(O_Q + h * 64, O_Q + (h + 1) * 64)) + [z] * 64
    idx += list(range(O_KV, O_KV + 512))
    idx += list(range(O_KV + 512, O_KV + 768))
    for c in (2, 3, 4, 5):
        for g in range(N_KV):
            s = O_KV + c * 128 + g * 64
            idx += list(range(s, s + 64)) + [z] * 64
    idx += list(range(O_GA, O_GA + 2048))
    assert len(idx) == C_TOTAL
    scale = np.ones((C_TOTAL,), np.float32)
    scale[C_Q:C_KV] = HEAD_DIM ** -0.5
    return np.asarray(idx, np.int32), scale


def _rope_tables(seq, past_len, n_sample_rows):
    half = ROT_DIM // 2
    inv = ROPE_THETA ** (-jnp.arange(half, dtype=f32) / half)
    pos = jnp.concatenate([jnp.arange(seq), past_len + (jnp.arange(n_sample_rows) % SAMPLE_PAD)]).astype(f32)
    ang = pos[:, None] * inv[None, :]
    cos8, sin8 = jnp.cos(ang), jnp.sin(ang)
    n = pos.shape[0]
    one = jnp.ones((n, 64 - ROT_DIM), f32)
    zero = jnp.zeros((n, 64 - ROT_DIM), f32)
    z8 = jnp.zeros((n, half), f32)
    cos = jnp.concatenate([cos8, cos8, one], axis=1)
    sa = jnp.concatenate([-sin8, z8, zero], axis=1)
    sb = jnp.concatenate([z8, sin8, zero], axis=1)
    tile2 = lambda a: jnp.concatenate([a, a], axis=1)
    return tile2(cos), tile2(sa), tile2(sb)


def _rms(x, g):
    return x * lax.rsqrt(jnp.mean(x * x, axis=-1, keepdims=True) + EPS) * g


def _in_proj_body(x_ref, g_ref, w_ref, wt_ref, cos_ref, sa_ref, sb_ref,
                  xm_ref, sm_ref, smt_ref, qc_ref, qr_ref, kv_ref, win_ref, att_ref, gate_ref):
    hb = _rms(x_ref[...], g_ref[...]).astype(bf16)
    cos, sa, sb = cos_ref[...], sa_ref[...], sb_ref[...]
    lane = lax.broadcasted_iota(jnp.int32, cos.shape, 1)

    def rope(v):
        return v * cos + pltpu.roll(v, 128 - ROT_DIM // 2, axis=1) * sa + pltpu.roll(v, ROT_DIM // 2, axis=1) * sb

    def proj(c0, n):
        return jnp.dot(hb, w_ref[:, c0:c0 + n], preferred_element_type=f32)

    xm_ref[...] = proj(C_XM, 1536)
    sm_ref[...] = proj(C_SM, 128)
    smt_ref[...] = lax.dot_general(wt_ref[...], hb, NT, preferred_element_type=f32)
    for h in range(N_HEADS):
        q = proj(C_Q + h * 128, 128)
        qc_ref[:, h * 128:(h + 1) * 128] = q.astype(bf16)
        qr_ref[:, h * 128:(h + 1) * 128] = rope(q).astype(bf16)
    kv = proj(C_KV, 512)
    kv_ref[:, 0:256] = kv[:, 0:256]
    kv_ref[:, 256:384] = rope(kv[:, 256:384])
    kv_ref[:, 384:512] = kv[:, 384:512]
    win = proj(C_WIN, 256)
    win_ref[:, 0:128] = rope(win[:, 0:128])
    win_ref[:, 128:256] = win[:, 128:256]
    for j in range(8):
        a = proj(C_ATT + j * 128, 128)
        if j in (0, 1, 4, 5):
            a = rope(a)
        else:
            a = jnp.where(lane == 64, 1.0, a)
        att_ref[:, j * 128:(j + 1) * 128] = a.astype(bf16)
    gate_ref[...] = proj(C_GATE, 2048)


def _in_proj(x_all, norm_g, w_all, w_t, tabs, seq):
    m = x_all.shape[0]
    n_prompt_tiles_per_seq = seq // ROW_TILE
    n_prompt_tiles = (m - (tabs[0].shape[0] - seq)) // ROW_TILE

    def tab_map(i):
        return (jnp.where(i < n_prompt_tiles, i % n_prompt_tiles_per_seq, n_prompt_tiles_per_seq + i - n_prompt_tiles), 0)

    row = lambda w: pl.BlockSpec((ROW_TILE, w), lambda i: (i, 0))
    const = lambda s: pl.BlockSpec(s, lambda i: (0, 0), pipeline_mode=pl.Buffered(1))
    tab = pl.BlockSpec((ROW_TILE, 128), tab_map)
    widths = (1536, 128, None, 1024, 1024, 512, 256, 1024, 2048)
    dts = (f32, f32, f32, bf16, bf16, f32, f32, bf16, f32)
    out_shape = [jax.ShapeDtypeStruct((8, m) if w is None else (m, w), d) for w, d in zip(widths, dts)]
    out_specs = [pl.BlockSpec((8, ROW_TILE), lambda i: (0, i)) if w is None else row(w) for w in widths]
    return pl.pallas_call(
        _in_proj_body, grid=(m // ROW_TILE,),
        in_specs=[row(D_MODEL), const((1, D_MODEL)), const((D_MODEL, C_TOTAL)), const((8, D_MODEL)), tab, tab, tab],
        out_specs=out_specs, out_shape=out_shape, name="in_proj",
        compiler_params=pltpu.CompilerParams(dimension_semantics=("parallel",), vmem_limit_bytes=VMEM_LIMIT),
    )(x_all, norm_g, w_all, w_t, *tabs)


def _mix_ffn_body(x_ref, ha_ref, hb_ref, gate_ref, wa_ref, wb_ref, wo_ref, g2_ref, wup_ref, wdn_ref, gf_ref,
                  xo_ref, y_ref):
    a = jnp.dot(ha_ref[...].astype(bf16), wa_ref[...], preferred_element_type=f32)
    b = jnp.dot(hb_ref[...].astype(bf16), wb_ref[...], preferred_element_type=f32)
    merged = jax.nn.sigmoid(gate_ref[:, 0:D_MODEL]) * a + jax.nn.sigmoid(gate_ref[:, D_MODEL:2 * D_MODEL]) * b
    x1 = x_ref[...] + jnp.dot(merged.astype(bf16), wo_ref[...], preferred_element_type=f32)
    hn = _rms(x1, g2_ref[...]).astype(bf16)
    acc = x1
    for c in range(D_FF // FF_CHUNK):
        lo = c * FF_CHUNK
        g = jnp.dot(hn, wup_ref[:, lo:lo + FF_CHUNK], preferred_element_type=f32)
        u = jnp.dot(hn, wup_ref[:, D_FF + lo:D_FF + lo + FF_CHUNK], preferred_element_type=f32)
        act = (g * jax.nn.sigmoid(g) * u).astype(bf16)
        acc = acc + jnp.dot(act, wdn_ref[lo:lo + FF_CHUNK, :], preferred_element_type=f32)
    xo_ref[...] = acc
    y_ref[...] = _rms(acc, gf_ref[...])


def _mix_ffn(x_all, h_a, h_b, gates, wa, wb, wo, g2, wup, wdn, gf):
    m = x_all.shape[0]
    row = lambda w: pl.BlockSpec((ROW_TILE, w), lambda i: (i, 0))
    const = lambda s: pl.BlockSpec(s, lambda i: (0, 0), pipeline_mode=pl.Buffered(1))
    return pl.pallas_call(
        _mix_ffn_body, grid=(m // ROW_TILE,),
        in_specs=[row(D_MODEL), row(h_a.shape[1]), row(h_b.shape[1]), row(2 * D_MODEL),
                  const(wa.shape), const(wb.shape), const(wo.shape), const((1, D_MODEL)),
                  const(wup.shape), const(wdn.shape), const((1, D_MODEL))],
        out_specs=[row(D_MODEL), row(D_MODEL)],
        out_shape=[jax.ShapeDtypeStruct((m, D_MODEL), f32)] * 2, name="mix_ffn",
        compiler_params=pltpu.CompilerParams(dimension_semantics=("parallel",), vmem_limit_bytes=VMEM_LIMIT),
    )(x_all, h_a, h_b, gates, wa, wb, wo, g2, wup, wdn, gf)


def _log_sigmoid(x):
    return jnp.minimum(x, 0.0) - jnp.log1p(jnp.exp(-jnp.abs(x)))


def _seg_scan(x, axis, seg, op, fill):
    idx = lax.broadcasted_iota(jnp.int32, x.shape, axis) % seg
    d = 1
    while d < seg:
        x = op(x, jnp.where(idx >= d, pltpu.roll(x, d, axis=axis), fill))
        d *= 2
    return x


def _conv_silu(u, tail, conv_w, conv_b, seg):
    L = u.shape[0]
    row = lax.broadcasted_iota(jnp.int32, u.shape, 0) % seg
    y = conv_b + u * conv_w[3:4, :]
    for k in (1, 2, 3):
        prev = tail if k == 3 else pltpu.roll(tail, L - (3 - k), axis=0)
        sh = jnp.where(row < k, prev, pltpu.roll(u, k, axis=0))
        y = y + sh * conv_w[3 - k:4 - k, :]
    return y * jax.nn.sigmoid(y)


def _gate_columns(sm, smt, bg_row, bg_col, m_vec, seg, n_valid):
    G = sm + bg_row
    lf = _log_sigmoid(G)
    ipre = G
    if n_valid < seg:
        rowc = lax.broadcasted_iota(jnp.int32, G.shape, 0) % seg
        lf = jnp.where(rowc < n_valid, lf, 0.0)
        ipre = jnp.where(rowc < n_valid, G, NEG)
    F = _seg_scan(lf, 0, seg, jnp.add, 0.0)
    F_al = pltpu.roll(F, 128 - M_HEADS, axis=1)
    a_col = ipre - F_al
    g_col = jnp.maximum(m_vec, _seg_scan(a_col, 0, seg, jnp.maximum, NEG))
    mt_col = F_al + g_col
    Gr = smt + bg_col
    lfr = _log_sigmoid(Gr)
    ir = Gr
    if n_valid < seg:
        lanec = lax.broadcasted_iota(jnp.int32, Gr.shape, 1) % seg
        lfr = jnp.where(lanec < n_valid, lfr, 0.0)
        ir = jnp.where(lanec < n_valid, Gr, NEG)
    Fr = _seg_scan(lfr, 1, seg, jnp.add, 0.0)
    a_row = ir - pltpu.roll(Fr, M_HEADS, axis=0)
    return a_col, g_col, mt_col, a_row


def _head_norm_gate(hh, ng, o):
    return hh * lax.rsqrt(jnp.mean(hh * hh, axis=-1, keepdims=True) + EPS) * ng * jax.nn.sigmoid(o)


def _mlstm_prompt_body(xm_ref, sm_ref, smt_ref, cw_ref, cb_ref, wq_ref, wk_ref, bgr_ref, bgc_ref, ng_ref,
                       ha_ref, c_out, n_out, m_out, conv_out, c_scr, n_scr, m_scr, tail_scr, *, L):
    c_idx = pl.program_id(1)

    @pl.when(c_idx == 0)
    def _():
        c_scr[...] = jnp.zeros_like(c_scr)
        n_scr[...] = jnp.zeros_like(n_scr)
        m_scr[...] = jnp.zeros_like(m_scr)
        tail_scr[...] = jnp.zeros_like(tail_scr)

    u = xm_ref[:, 0:M_WIDTH]
    tail = jnp.concatenate([tail_scr[...], jnp.zeros((L - 8, M_WIDTH), f32)], axis=0)
    cact = _conv_silu(u, tail, cw_ref[...], cb_ref[...], L)
    tail_scr[...] = pltpu.roll(u[L - 8:L, :], 3, axis=0)
    conv_out[0] = u[L - 8:L, :]

    m_vec = m_scr[0:1, :]
    a_col, g_col, mt_col, a_row = _gate_columns(sm_ref[...], smt_ref[...], bgr_ref[...], bgc_ref[...], m_vec, L, L)
    inter_col = jnp.exp(m_vec - g_col)
    floor_col = jnp.exp(-mt_col)
    g_last = g_col[L - 1:L, :]
    w_col = jnp.exp(a_col - g_last)
    decay = jnp.exp(m_vec - g_last)
    m_new = mt_col[L - 1:L, :]
    ti = lax.broadcasted_iota(jnp.int32, (L, L), 0)
    si = lax.broadcasted_iota(jnp.int32, (L, L), 1)
    causal = si <= ti
    scale = HD ** -0.5
    for h in range(M_HEADS):
        sl = slice(h * HD, (h + 1) * HD)
        ch = cact[:, sl].astype(bf16)
        qh = jnp.dot(ch, wq_ref[h], preferred_element_type=f32)
        kh = jnp.dot(ch, wk_ref[h], preferred_element_type=f32) * scale
        vh = xm_ref[:, M_WIDTH + h * HD:M_WIDTH + (h + 1) * HD].astype(bf16)
        qb = qh.astype(bf16)
        D = jnp.exp(jnp.where(causal, a_row[h:h + 1, :] - g_col[:, h:h + 1], NEG))
        S = lax.dot_general(qb, kh.astype(bf16), NT, preferred_element_type=f32) * D
        C = c_scr[h]
        nrow = n_scr[h:h + 1, :]
        ic = inter_col[:, h:h + 1]
        num = jnp.dot(S.astype(bf16), vh, preferred_element_type=f32) + ic * jnp.dot(
            qb, C.astype(bf16), preferred_element_type=f32)
        den = jnp.sum(S, axis=-1, keepdims=True) + ic * jnp.sum(qh * nrow, axis=-1, keepdims=True)
        hh = num / jnp.maximum(jnp.abs(den), floor_col[:, h:h + 1])
        o = xm_ref[:, 2 * M_WIDTH + h * HD:2 * M_WIDTH + (h + 1) * HD]
        ha_ref[:, sl] = _head_norm_gate(hh, ng_ref[:, sl], o)
        kw = kh * w_col[:, h:h + 1]
        dh = decay[:, h:h + 1]
        c_scr[h] = dh * C + lax.dot_general(kw.astype(bf16), vh, (((0,), (0,)), ((), ())), preferred_element_type=f32)
        n_scr[h:h + 1, :] = dh * nrow + jnp.sum(kw, axis=0, keepdims=True)
    m_scr[0:1, :] = m_new

    @pl.when(c_idx == pl.num_programs(1) - 1)
    def _():
        c_out[0] = c_scr[...]
        n_out[0] = n_scr[...]
        m_out[0] = m_scr[...]


def _mlstm_prompt(xm, sm, smt, conv_w, conv_b, wq, wk, b_gates, norm_g, *, nb, T, L):
    nc = T // L
    bg_row = jnp.zeros((1, 128), f32).at[0, :8].set(b_gates)
    bg_col = b_gates.reshape(8, 1)
    const = lambda s: pl.BlockSpec(s, lambda b, c: (0,) * len(s))
    return pl.pallas_call(
        functools.partial(_mlstm_prompt_body, L=L), grid=(nb, nc),
        in_specs=[pl.BlockSpec((L, 3 * M_WIDTH), lambda b, c: (b * nc + c, 0)),
                  pl.BlockSpec((L, 128), lambda b, c: (b * nc + c, 0)),
                  pl.BlockSpec((8, L), lambda b, c: (0, b * nc + c)),
                  const((4, M_WIDTH)), const((1, M_WIDTH)), const((M_HEADS, HD, HD)), const((M_HEADS, HD, HD)),
                  const((1, 128)), const((8, 1)), const((1, M_WIDTH))],
        out_specs=[pl.BlockSpec((L, M_WIDTH), lambda b, c: (b * nc + c, 0)),
                   pl.BlockSpec((1, M_HEADS, HD, HD), lambda b, c: (b, 0, 0, 0)),
                   pl.BlockSpec((1, 8, HD), lambda b, c: (b, 0, 0)),
                   pl.BlockSpec((1, 8, 128), lambda b, c: (b, 0, 0)),
                   pl.BlockSpec((1, 8, M_WIDTH), lambda b, c: (b, 0, 0))],
        out_shape=[jax.ShapeDtypeStruct((nb * T, M_WIDTH), f32),
                   jax.ShapeDtypeStruct((nb, M_HEADS, HD, HD), f32),
                   jax.ShapeDtypeStruct((nb, 8, HD), f32),
                   jax.ShapeDtypeStruct((nb, 8, 128), f32),
                   jax.ShapeDtypeStruct((nb, 8, M_WIDTH), f32)],
        scratch_shapes=[pltpu.VMEM((M_HEADS, HD, HD), f32), pltpu.VMEM((8, HD), f32), pltpu.VMEM((8, 128), f32),
                        pltpu.VMEM((8, M_WIDTH), f32)],
        compiler_params=pltpu.CompilerParams(dimension_semantics=("parallel", "arbitrary")),
        name="mlstm_prompt",
    )(xm, sm, smt, conv_w, conv_b.reshape(1, M_WIDTH), wq, wk, bg_row, bg_col, norm_g.reshape(1, M_WIDTH))


def _mlstm_sample_body(xm_ref, sm_ref, smt_ref, tail_ref, c_in, n_in, m_in, cw_ref, cb_ref, wq_ref, wk_ref,
                       bgr_ref, bgc_ref, ng_ref, ha_ref, c_out, n_out, m_out, *, n_valid):
    L = SEG * SEQ_PER_STEP
    u = xm_ref[:, 0:M_WIDTH]
    cact = _conv_silu(u, tail_ref[...], cw_ref[...], cb_ref[...], SEG)
    m_rows = m_in[...]
    a_col, g_col, mt_col, a_row = _gate_columns(sm_ref[...], smt_ref[...], bgr_ref[...], bgc_ref[...], m_rows, SEG,
                                                n_valid)
    inter_col = jnp.exp(m_rows - g_col)
    floor_col = jnp.exp(-mt_col)
    ti = lax.broadcasted_iota(jnp.int32, (L, L), 0)
    si = lax.broadcasted_iota(jnp.int32, (L, L), 1)
    mask = (si <= ti) & ((si // SEG) == (ti // SEG))
    scale = HD ** -0.5
    for h in range(M_HEADS):
        sl = slice(h * HD, (h + 1) * HD)
        ch = cact[:, sl].astype(bf16)
        qh = jnp.dot(ch, wq_ref[h], preferred_element_type=f32)
        kh = jnp.dot(ch, wk_ref[h], preferred_element_type=f32) * scale
        vh = xm_ref[:, M_WIDTH + h * HD:M_WIDTH + (h + 1) * HD].astype(bf16)
        qb = qh.astype(bf16)
        D = jnp.exp(jnp.where(mask, a_row[h:h + 1, :] - g_col[:, h:h + 1], NEG))
        S = lax.dot_general(qb, kh.astype(bf16), NT, preferred_element_type=f32) * D
        num = jnp.dot(S.astype(bf16), vh, preferred_element_type=f32)
        den = jnp.sum(S, axis=-1, keepdims=True)
        inter_num, inter_den = [], []
        for s in range(SEQ_PER_STEP):
            rs = slice(s * SEG, (s + 1) * SEG)
            C = c_in[s, h]
            nrow = n_in[s, h:h + 1, :]
            inter_num.append(jnp.dot(qb[rs], C.astype(bf16), preferred_element_type=f32))
            inter_den.append(jnp.sum(qh[rs] * nrow, axis=-1, keepdims=True))
            g_last = g_col[s * SEG + SEG - 1:s * SEG + SEG, h:h + 1]
            m_prev = m_rows[s * SEG:s * SEG + 1, h:h + 1]
            w = jnp.exp(a_col[rs, h:h + 1] - g_last)
            dh = jnp.exp(m_prev - g_last)
            kw = kh[rs] * w
            c_out[s, h] = dh * C + lax.dot_general(kw.astype(bf16), vh[rs], (((0,), (0,)), ((), ())),
                                                    preferred_element_type=f32)
            n_out[s, h:h + 1, :] = dh * nrow + jnp.sum(kw, axis=0, keepdims=True)
        ic = inter_col[:, h:h + 1]
        num = num + ic * jnp.concatenate(inter_num, axis=0)
        den = den + ic * jnp.concatenate(inter_den, axis=0)
        hh = num / jnp.maximum(jnp.abs(den), floor_col[:, h:h + 1])
        o = xm_ref[:, 2 * M_WIDTH + h * HD:2 * M_WIDTH + (h + 1) * HD]
        ha_ref[:, sl] = _head_norm_gate(hh, ng_ref[:, sl], o)
    for s in range(SEQ_PER_STEP):
        n_out[s, M_HEADS:8, :] = jnp.zeros((8 - M_HEADS, HD), f32)
    m_out[...] = mt_col


def _mlstm_sample(xm, sm, smt, tail, c0, n0, m0, conv_w, conv_b, wq, wk, b_gates, norm_g, *, row0, nseq, n_valid):
    L = SEG * SEQ_PER_STEP
    b0 = row0 // L
    bg_row = jnp.zeros((1, 128), f32).at[0, :8].set(b_gates)
    bg_col = b_gates.reshape(8, 1)
    const = lambda s: pl.BlockSpec(s, lambda i: (0,) * len(s))
    return pl.pallas_call(
        functools.partial(_mlstm_sample_body, n_valid=n_valid), grid=(nseq // SEQ_PER_STEP,),
        in_specs=[pl.BlockSpec((L, 3 * M_WIDTH), lambda i: (b0 + i, 0)),
                  pl.BlockSpec((L, 128), lambda i: (b0 + i, 0)),
                  pl.BlockSpec((8, L), lambda i: (0, b0 + i)),
                  pl.BlockSpec((L, M_WIDTH), lambda i: (i, 0)),
                  pl.BlockSpec((SEQ_PER_STEP, M_HEADS, HD, HD), lambda i: (i, 0, 0, 0)),
                  pl.BlockSpec((SEQ_PER_STEP, 8, HD), lambda i: (i, 0, 0)),
                  pl.BlockSpec((L, 128), lambda i: (i, 0)),
                  const((4, M_WIDTH)), const((1, M_WIDTH)), const((M_HEADS, HD, HD)), const((M_HEADS, HD, HD)),
                  const((1, 128)), const((8, 1)), const((1, M_WIDTH))],
        out_specs=[pl.BlockSpec((L, M_WIDTH), lambda i: (i, 0)),
                   pl.BlockSpec((SEQ_PER_STEP, M_HEADS, HD, HD), lambda i: (i, 0, 0, 0)),
                   pl.BlockSpec((SEQ_PER_STEP, 8, HD), lambda i: (i, 0, 0)),
                   pl.BlockSpec((L, 128), lambda i: (i, 0))],
        out_shape=[jax.ShapeDtypeStruct((nseq * SEG, M_WIDTH), f32),
                   jax.ShapeDtypeStruct((nseq, M_HEADS, HD, HD), f32),
                   jax.ShapeDtypeStruct((nseq, 8, HD), f32),
                   jax.ShapeDtypeStruct((nseq * SEG, 128), f32)],
        compiler_params=pltpu.CompilerParams(dimension_semantics=("parallel",), vmem_limit_bytes=VMEM_LIMIT),
        name="mlstm_sample",
    )(xm, sm, smt, tail, c0, n0, m0, conv_w, conv_b.reshape(1, M_WIDTH), wq, wk, bg_row, bg_col,
      norm_g.reshape(1, M_WIDTH))


def _sel_overlap_t(n_chunk, n_blk):
    n_cmp = n_chunk - 1
    start = np.arange(n_cmp) * CMP_STRIDE
    bs = np.arange(n_blk) * SEL_BLOCK
    ov = np.minimum(start[:, None] + CMP_LEN, bs[None, :] + SEL_BLOCK) - np.maximum(start[:, None], bs[None, :])
    ov = np.clip(ov, 0, None) / CMP_LEN
    out = np.zeros((NBLK_PAD, n_chunk), np.float32)
    out[:n_blk, :n_cmp] = ov.T
    return jnp.asarray(out)


def _block_expand(n_keys):
    e = (np.arange(n_keys)[None, :] // SEL_BLOCK) == np.arange(128)[:, None]
    return jnp.asarray(e, dtype=bf16)


def _compress_weights(pe, w1, w2, padded=True):
    W = jnp.zeros((16, 2, 64, 2, 2, CMP_HID), f32)
    for g in range(N_KV):
        W = W.at[:, g, :, g, 0, :].set(w1[:16])
        W = W.at[:, g, :, g, 1, :].set(w1[16:])
    W = W.reshape(2048, 512)
    peA = jnp.broadcast_to(pe[:16, None, :], (16, 2, 64)).reshape(1, 2048)
    peB = jnp.broadcast_to(pe[16:, None, :], (16, 2, 64)).reshape(1, 2048)
    PE = jnp.concatenate([peA, peB, jnp.zeros((6, 2048), f32)], axis=0)
    if padded:
        W2 = jnp.zeros((2, CMP_HID, 256), f32).at[0, :, 0:64].set(w2).at[1, :, 128:192].set(w2)
    else:
        W2 = jnp.zeros((2, CMP_HID, 128), f32).at[0, :, 0:64].set(w2).at[1, :, 64:128].set(w2)
    return W.astype(bf16), PE.astype(bf16), W2.astype(bf16)


def _compress_rows(flat, w_ref, pe_ref, w2_ref):
    n = flat.shape[0]
    y = jnp.dot(flat, w_ref[...], preferred_element_type=f32)
    c = jnp.dot(pe_ref[...], w_ref[...], preferred_element_type=f32)
    out = jnp.zeros((n, w2_ref.shape[2]), f32)
    for g in range(N_KV):
        a = y[:, (2 * g) * CMP_HID:(2 * g + 1) * CMP_HID] + c[0:1, (2 * g) * CMP_HID:(2 * g + 1) * CMP_HID]
        b = y[:, (2 * g + 1) * CMP_HID:(2 * g + 2) * CMP_HID] + c[1:2, (2 * g + 1) * CMP_HID:(2 * g + 2) * CMP_HID]
        hid = a + pltpu.roll(b, n - 1, axis=0)
        out = out + jnp.dot(jax.nn.gelu(hid, approximate=True).astype(bf16), w2_ref[g], preferred_element_type=f32)
    return out


def _compress_prompt_body(xk_ref, xv_ref, wk_ref, pek_ref, w2k_ref, wv_ref, pev_ref, w2v_ref, kc_ref, vc_ref, *,
                          n_chunk):
    def flat(x_ref):
        return jnp.concatenate(
            [x_ref[pl.ds(l, n_chunk, stride=CMP_STRIDE), :].astype(bf16) for l in range(CMP_STRIDE)], axis=1)
    kc_ref[0] = _compress_rows(flat(xk_ref), wk_ref, pek_ref, w2k_ref).astype(bf16)
    vc_ref[0] = _compress_rows(flat(xv_ref), wv_ref, pev_ref, w2v_ref).astype(bf16)


def _compress_prompt(kvrow, cwk, cwv, *, nb, T):
    n_chunk = T // CMP_STRIDE
    const = lambda a: pl.BlockSpec(a.shape, lambda b: (0,) * a.ndim)
    return pl.pallas_call(
        functools.partial(_compress_prompt_body, n_chunk=n_chunk), grid=(nb,),
        in_specs=[pl.BlockSpec((T, 128), lambda b: (b, 0)), pl.BlockSpec((T, 128), lambda b: (b, 1))]
        + [const(a) for a in (*cwk, *cwv)],
        out_specs=[pl.BlockSpec((1, n_chunk, 256), lambda b: (b, 0, 0))] * 2,
        out_shape=[jax.ShapeDtypeStruct((nb, n_chunk, 256), bf16)] * 2,
        compiler_params=pltpu.CompilerParams(dimension_semantics=("parallel",), vmem_limit_bytes=VMEM_LIMIT),
        name="nsa_compress",
    )(kvrow, kvrow, *cwk, *cwv)


def _softmax_rows(s, mask):
    s = jnp.where(mask, s, NEG)
    e = jnp.exp(s - jnp.max(s, axis=-1, keepdims=True))
    e = jnp.where(mask, e, 0.0)
    return e / jnp.maximum(jnp.sum(e, axis=-1, keepdims=True), TINY)


def _select_blocks(imp_t, qpos_row, n_blk, score_scr):
    shape = imp_t.shape
    j = lax.broadcasted_iota(jnp.int32, shape, 0)
    cur = qpos_row // SEL_BLOCK
    forced = (j == 0) | (j == cur) | (j == cur - 1)
    score = jnp.where(j <= cur, jnp.where(forced, FORCE_SCORE, imp_t), -1.0)
    score = jnp.where(j < n_blk, score, -2.0)
    score_scr[...] = score

    def body(jp, cnt):
        sj = score_scr[pl.ds(jp, 1), :]
        return cnt + jnp.where(j > jp, jnp.where(sj >= score, 1.0, 0.0), jnp.where(sj > score, 1.0, 0.0))

    cnt = lax.fori_loop(0, n_blk, body, jnp.zeros(shape, f32), unroll=8 if n_blk % 8 == 0 else 1)
    return jnp.where(cnt < float(min(SEL_TOP, n_blk)), 1.0, 0.0)


def _transpose_sel(sel_t):
    return jnp.concatenate([sel_t, jnp.zeros((128 - NBLK_PAD, sel_t.shape[1]), f32)], axis=0).T


def _nsa_prompt_body(qc_ref, qr_ref, sm_ref, att_ref, kc_ref, vc_ref, ovt_ref, e_ref, hb_ref, score_scr, *, T):
    i = pl.program_id(1)
    n_chunk = T // CMP_STRIDE
    n_blk = T // SEL_BLOCK
    q0 = i * TQ
    rows = R * TQ
    qpos_col = q0 + lax.broadcasted_iota(jnp.int32, (rows, 1), 0) % TQ
    qpos_q = q0 + lax.broadcasted_iota(jnp.int32, (TQ, 1), 0)
    qpos_row = q0 + lax.broadcasted_iota(jnp.int32, (1, TQ), 1)
    gsig = jax.nn.sigmoid(sm_ref[...])
    n_kt = (q0 + TQ + TK - 1) // TK
    w0 = jnp.maximum(i - WINDOW // TQ, 0) * TQ
    WK = WINDOW + TQ
    for g in range(N_KV):
        stack = lambda ref: jnp.concatenate([ref[:, (g * R + r) * 128:(g * R + r + 1) * 128] for r in range(R)], axis=0)
        qc = stack(qc_ref)
        qr = stack(qr_ref)
        s = lax.dot_general(qc, kc_ref[0][:, g * 128:(g + 1) * 128], NT, preferred_element_type=f32)
        cmp_end = lax.broadcasted_iota(jnp.int32, (1, n_chunk), 1) * CMP_STRIDE + (CMP_LEN - 1)
        p = _softmax_rows(s, cmp_end <= qpos_col)
        o_c = jnp.dot(p.astype(bf16), vc_ref[0][:, g * 128:(g + 1) * 128], preferred_element_type=f32)
        psum = p[0:TQ] + p[TQ:2 * TQ] + p[2 * TQ:3 * TQ] + p[3 * TQ:4 * TQ]
        imp_t = lax.dot_general(ovt_ref[...], psum, NT, preferred_element_type=f32, precision=HIGHEST)
        sel_q = _transpose_sel(_select_blocks(imp_t, qpos_row, n_blk, score_scr)).astype(bf16)
        ks_l, vs_l = (0 + g) * 128, (2 + g) * 128

        def kv_step(kt, carry):
            m_i, acc = carry
            k0 = pl.multiple_of(kt * TK, TK)
            kk = att_ref[pl.ds(k0, TK), ks_l:ks_l + 128]
            vv = att_ref[pl.ds(k0, TK), vs_l:vs_l + 128]
            sc = lax.dot_general(qr, kk, NT, preferred_element_type=f32)
            mk = jnp.dot(sel_q, e_ref[:, pl.ds(k0, TK)], preferred_element_type=f32)
            kpos = k0 + lax.broadcasted_iota(jnp.int32, (1, TK), 1)
            bias = jnp.where((mk > 0.5) & (kpos <= qpos_q), 0.0, NEG)
            sc = sc + jnp.concatenate([bias] * R, axis=0)
            m_new = jnp.maximum(m_i, jnp.max(sc, axis=-1, keepdims=True))
            alpha = jnp.exp(m_i - m_new)
            pp = jnp.exp(sc - m_new)
            acc = alpha * acc + jnp.dot(pp.astype(bf16), vv, preferred_element_type=f32)
            return m_new, acc

        m0 = jnp.full((rows, 1), NEG, f32)
        _, acc = lax.fori_loop(0, n_kt, kv_step, (m0, jnp.zeros((rows, 128), f32)))
        o_s = acc / acc[:, 64:65]
        kw_l, vw_l = (4 + g) * 128, (6 + g) * 128
        w0a = pl.multiple_of(w0, TQ)
        kk = att_ref[pl.ds(w0a, WK), kw_l:kw_l + 128]
        vv = att_ref[pl.ds(w0a, WK), vw_l:vw_l + 128]
        sw = lax.dot_general(qr, kk, NT, preferred_element_type=f32)
        dpos = qpos_col - (w0 + lax.broadcasted_iota(jnp.int32, (1, WK), 1))
        sw = jnp.where((dpos >= 0) & (dpos < WINDOW), sw, NEG)
        pw = jnp.exp(sw - jnp.max(sw, axis=-1, keepdims=True))
        ow = jnp.dot(pw.astype(bf16), vv, preferred_element_type=f32)
        o_w = ow / ow[:, 64:65]
        for r in range(R):
            h = g * R + r
            rs = slice(r * TQ, (r + 1) * TQ)
            c0 = 8 + 3 * h
            out = (gsig[:, c0:c0 + 1] * o_c[rs] + gsig[:, c0 + 1:c0 + 2] * o_s[rs] + gsig[:, c0 + 2:c0 + 3] * o_w[rs])
            hb_ref[:, h * 128:(h + 1) * 128] = out.astype(bf16)


def _nsa_prompt(qc, qr, sm, att, kc, vc, *, nb, T):
    nq = T // TQ
    n_chunk = T // CMP_STRIDE
    ovt = _sel_overlap_t(n_chunk, T // SEL_BLOCK)
    e = _block_expand(T)
    row = lambda w: pl.BlockSpec((TQ, w), lambda b, i: (b * nq + i, 0))
    const = lambda a: pl.BlockSpec(a.shape, lambda b, i: (0,) * a.ndim)
    return pl.pallas_call(
        functools.partial(_nsa_prompt_body, T=T), grid=(nb, nq),
        in_specs=[row(1024), row(1024), row(128), pl.BlockSpec((T, 1024), lambda b, i: (b, 0)),
                  pl.BlockSpec((1, n_chunk, 256), lambda b, i: (b, 0, 0)),
                  pl.BlockSpec((1, n_chunk, 256), lambda b, i: (b, 0, 0)),
                  const(ovt), const(e)],
        out_specs=row(1024),
        out_shape=jax.ShapeDtypeStruct((nb * T, 1024), bf16),
        scratch_shapes=[pltpu.VMEM((NBLK_PAD, TQ), f32)],
        compiler_params=pltpu.CompilerParams(dimension_semantics=("parallel", "arbitrary"),
                                             vmem_limit_bytes=VMEM_LIMIT),
        name="nsa_prompt",
    )(qc, qr, sm, att, kc, vc, ovt, e)


def _nsa_sample_body(pt_ref, *refs, n_pages, ts):
    n_pg = GB * n_pages
    pages = refs[:n_pg]
    (win_ref, qc_ref, qr_ref, sm_ref, kvn_ref, wn_ref, wk_ref, pek_ref, w2k_ref, wv_ref, pev_ref, w2v_ref,
     ovt_ref, e_ref, hb_ref, wout_ref, xk_scr, xv_scr, score_scr) = refs[n_pg:]
    P = n_pages * PAGE
    n_chunk = P // CMP_STRIDE
    n_blk = -(-(P + ts) // SEL_BLOCK)
    WB = win_ref.shape[2]
    for bi in range(GB):
        for p in range(n_pages):
            xk_scr[pl.ds(bi * P + p * PAGE, PAGE), :] = pages[bi * n_pages + p][0, 0, :, 0:128]
            xv_scr[pl.ds(bi * P + p * PAGE, PAGE), :] = pages[bi * n_pages + p][0, 0, :, 128:256]

    def flat(x_ref):
        return jnp.concatenate(
            [x_ref[pl.ds(l, GB * n_chunk, stride=CMP_STRIDE), :].astype(bf16) for l in range(CMP_STRIDE)], axis=1)

    kc_all = _compress_rows(flat(xk_scr), wk_ref, pek_ref, w2k_ref).astype(bf16)
    vc_all = _compress_rows(flat(xv_scr), wv_ref, pev_ref, w2v_ref).astype(bf16)

    qc_all = qc_ref[...].astype(f32)
    qr_all = qr_ref[...].astype(f32)
    gsig = jax.nn.sigmoid(sm_ref[...])
    rows = R * SEG
    qpos_col = P + lax.broadcasted_iota(jnp.int32, (rows, 1), 0) % SEG
    qpos_row = P + lax.broadcasted_iota(jnp.int32, (1, 128), 1) % SEG
    new_lane = lax.broadcasted_iota(jnp.int32, (1, 128), 1)
    zpad = jnp.zeros((128 - SEG, 128), f32)
    out_rows = []
    for bi in range(GB):
        rs8 = slice(bi * SEG, (bi + 1) * SEG)
        pg = pages[bi * n_pages:(bi + 1) * n_pages]
        kc = kc_all[bi * n_chunk:(bi + 1) * n_chunk]
        vc = vc_all[bi * n_chunk:(bi + 1) * n_chunk]
        knew = jnp.concatenate([kvn_ref[rs8, 256:384], zpad], axis=0).astype(bf16)
        vnew = jnp.concatenate([kvn_ref[rs8, 384:512], zpad], axis=0).astype(bf16)
        kwnew = jnp.concatenate([wn_ref[rs8, 0:128], zpad], axis=0).astype(bf16)
        vwnew = jnp.concatenate([wn_ref[rs8, 128:256], zpad], axis=0).astype(bf16)
        kwin = win_ref[0, bi, :, 0:128].astype(bf16)
        vwin = win_ref[0, bi, :, 128:256].astype(bf16)
        heads = [None] * N_HEADS
        for g in range(N_KV):
            def stack(q_all):
                q = jnp.concatenate([q_all[rs8, (g * R + r) * 128:(g * R + r + 1) * 128] for r in range(R)], axis=0)
                if g:
                    q = pltpu.roll(q, g * HEAD_DIM, axis=1)
                return q.astype(bf16)
            qc = stack(qc_all)
            qr = stack(qr_all)
            s = lax.dot_general(qc, kc, NT, preferred_element_type=f32)
            cmp_end = lax.broadcasted_iota(jnp.int32, (1, n_chunk), 1) * CMP_STRIDE + (CMP_LEN - 1)
            p = _softmax_rows(s, cmp_end <= qpos_col)
            o_c = jnp.dot(p.astype(bf16), vc, preferred_element_type=f32)
            psum = p[0:SEG] + p[SEG:2 * SEG] + p[2 * SEG:3 * SEG] + p[3 * SEG:4 * SEG]
            psum = jnp.concatenate([psum, jnp.zeros((128 - SEG, n_chunk), f32)], axis=0)
            imp_t = lax.dot_general(ovt_ref[...], psum, NT, preferred_element_type=f32, precision=HIGHEST)
            sel_q = _transpose_sel(_select_blocks(imp_t, qpos_row, n_blk, score_scr))[0:SEG].astype(bf16)
            mk = jnp.concatenate([jnp.dot(sel_q, e_ref[...], preferred_element_type=f32)] * R, axis=0) > 0.5
            s_sel = jnp.concatenate(
                [lax.dot_general(qr, pg[pp][0, 0, :, 256:384].astype(bf16), NT, preferred_element_type=f32)
                 for pp in range(n_pages)]
                + [lax.dot_general(qr, knew, NT, preferred_element_type=f32)], axis=1)
            kpos = jnp.concatenate([lax.broadcasted_iota(jnp.int32, (1, P), 1), P + new_lane], axis=1)
            valid = jnp.concatenate([jnp.full((1, P), True), new_lane < SEG], axis=1)
            p_s = _softmax_rows(s_sel, mk & (kpos <= qpos_col) & valid).astype(bf16)
            o_s = jnp.dot(p_s[:, P:P + 128], vnew, preferred_element_type=f32)
            for pp in range(n_pages):
                o_s = o_s + jnp.dot(p_s[:, pp * PAGE:(pp + 1) * PAGE], pg[pp][0, 0, :, 384:512].astype(bf16),
                                    preferred_element_type=f32)
            s_w = jnp.concatenate([lax.dot_general(qr, kwin, NT, preferred_element_type=f32),
                                   lax.dot_general(qr, kwnew, NT, preferred_element_type=f32)], axis=1)
            kwpos = jnp.concatenate([P - WB + lax.broadcasted_iota(jnp.int32, (1, WB), 1), P + new_lane], axis=1)
            validw = jnp.concatenate([jnp.full((1, WB), True), new_lane < SEG], axis=1)
            dpos = qpos_col - kwpos
            p_w = _softmax_rows(s_w, (dpos >= 0) & (dpos < WINDOW) & (kwpos >= 0) & validw).astype(bf16)
            o_w = (jnp.dot(p_w[:, 0:WB], vwin, preferred_element_type=f32)
                   + jnp.dot(p_w[:, WB:WB + 128], vwnew, preferred_element_type=f32))
            for r in range(R):
                h = g * R + r
                rr = slice(r * SEG, (r + 1) * SEG)
                c0 = 8 + 3 * h
                gs = gsig[rs8]
                out = gs[:, c0:c0 + 1] * o_c[rr] + gs[:, c0 + 1:c0 + 2] * o_s[rr] + gs[:, c0 + 2:c0 + 3] * o_w[rr]
                if g:
                    out = pltpu.roll(out, 128 - g * HEAD_DIM, axis=1)
                heads[h] = out
        out_rows.append(jnp.concatenate(heads, axis=1))
        w_old = pltpu.roll(win_ref[0, bi], WB - ts, axis=0)
        wout_ref[bi, 0:WB - 8, :] = w_old[0:WB - 8]
        row8 = lax.broadcasted_iota(jnp.int32, (8, 1), 0)
        wout_ref[bi, WB - 8:WB, :] = jnp.where(row8 >= 8 - ts, pltpu.roll(wn_ref[rs8, :], 8 - ts, axis=0),
                                                w_old[WB - 8:WB])
    hb_ref[...] = jnp.concatenate(out_rows, axis=0).astype(bf16)


def _nsa_sample(page_table, cache_l, cache_win_l, qc, qr, sm, kvrow, winrow, cwk, cwv, *, layer, row0, ts):
    nseq, n_pages = page_table.shape
    P = n_pages * PAGE
    WB = cache_win_l.shape[2]
    n_chunk = P // CMP_STRIDE
    ovt = _sel_overlap_t(n_chunk, -(-(P + ts) // SEL_BLOCK))
    e = _block_expand(P + 128)
    b0 = row0 // (GB * SEG)

    def page_map(i, pt, *, bi, p):
        return (layer, pt[i * GB + bi, p], 0, 0)

    page_specs = [pl.BlockSpec((1, 1, PAGE, 512), functools.partial(page_map, bi=bi, p=p))
                  for bi in range(GB) for p in range(n_pages)]
    row = lambda w: pl.BlockSpec((GB * SEG, w), lambda i, pt: (b0 + i, 0))
    const = lambda a: pl.BlockSpec(a.shape, lambda i, pt: (0,) * a.ndim)
    grid_spec = pltpu.PrefetchScalarGridSpec(
        num_scalar_prefetch=1, grid=(nseq // GB,),
        in_specs=page_specs + [pl.BlockSpec((1, GB, WB, 256), lambda i, pt: (layer, i, 0, 0)),
                               row(1024), row(1024), row(128), row(512), row(256)]
        + [const(a) for a in (*cwk, *cwv, ovt, e)],
        out_specs=[pl.BlockSpec((GB * SEG, 1024), lambda i, pt: (i, 0)),
                   pl.BlockSpec((GB, WB, 256), lambda i, pt: (i, 0, 0))],
        scratch_shapes=[pltpu.VMEM((GB * P, 128), f32), pltpu.VMEM((GB * P, 128), f32),
                        pltpu.VMEM((NBLK_PAD, 128), f32)])
    return pl.pallas_call(
        functools.partial(_nsa_sample_body, n_pages=n_pages, ts=ts), grid_spec=grid_spec,
        out_shape=[jax.ShapeDtypeStruct((nseq * SEG, 1024), bf16), jax.ShapeDtypeStruct((nseq, WB, 256), f32)],
        compiler_params=pltpu.CompilerParams(dimension_semantics=("parallel",), vmem_limit_bytes=VMEM_LIMIT),
        name="nsa_sample",
    )(page_table, *([cache_l] * (GB * n_pages)), cache_win_l, qc, qr, sm, kvrow, winrow, *cwk, *cwv, ovt, e)


def kernel(x_prompt, x_sample, cache_kv_pages, page_table, cache_win, state_mlstm_C, state_mlstm_n, state_mlstm_m, state_mlstm_conv, norm1_g, w_in, b_gates, conv_w, conv_b, w_mq, w_mk, mlstm_norm_g, cmp_pe_k, cmp_pe_v, cmp_w1_k, cmp_w2_k, cmp_w1_v, cmp_w2_v, w_branch_a, w_branch_b, w_out, norm2_g, w_ffn_up, w_ffn_down, final_norm_g):
    Bp, Tp, _ = x_prompt.shape
    Bs, Ts, _ = x_sample.shape
    depth = w_in.shape[0]
    n_pool, page = cache_kv_pages.shape[1:3]
    n_pages = page_table.shape[1]
    past_len = n_pages * page
    wb_len = cache_win.shape[2]
    mp = Bp * Tp
    ms = Bs * SAMPLE_PAD
    assert page == PAGE and M_CONV - 1 <= Ts <= SAMPLE_PAD and Tp % MLSTM_CHUNK == 0 and Tp >= WINDOW + TQ
    assert mp % (SEG * SEQ_PER_STEP) == 0 and Bs % SEQ_PER_STEP == 0 and Bs % GB == 0

    pad_seq = lambda a: jnp.pad(a, ((0, 0), (0, SAMPLE_PAD - a.shape[1]), (0, 0)))
    x_all = jnp.concatenate([x_prompt.reshape(mp, D_MODEL), pad_seq(x_sample).reshape(ms, D_MODEL)], axis=0)
    tabs = _rope_tables(Tp, past_len, ROW_TILE * (ms // ROW_TILE))
    col_idx, col_scale = _in_proj_columns()
    wb_rows = np.asarray([h * 64 + d if d < 64 else N_WIDTH for h in range(N_HEADS) for d in range(128)], np.int32)
    cache4 = cache_kv_pages.reshape(depth, n_pool, PAGE, 4 * KV_W)
    cwin4 = cache_win.reshape(depth, Bs, wb_len, 2 * KV_W)

    outs = {k: [] for k in ("p_kv", "p_win", "p_C", "p_n", "p_m", "p_conv", "s_kv", "s_win", "s_C", "s_n", "s_m", "s_conv")}
    y_all = None
    for l in range(depth):
        w_ext = jnp.concatenate([w_in[l], jnp.zeros((D_MODEL, 1), f32)], axis=1)
        w_all = (w_ext[:, col_idx] * col_scale).astype(bf16)
        w_t = w_in[l][:, O_IG:O_IG + 8].T.astype(bf16)
        xm, sm, smt, qc, qr, kvrow, winrow, att, gates = _in_proj(
            x_all, norm1_g[l].reshape(1, D_MODEL), w_all, w_t, tabs, Tp)

        wq, wk = w_mq[l].astype(bf16), w_mk[l].astype(bf16)
        ha_p, c_p, n_p, m_p, cv_p = _mlstm_prompt(xm, sm, smt, conv_w[l], conv_b[l], wq, wk, b_gates[l], mlstm_norm_g[l],
                                                   nb=Bp, T=Tp, L=MLSTM_CHUNK)
        tail = jnp.pad(state_mlstm_conv[l], ((0, 0), (0, SEG - (M_CONV - 1)), (0, 0))).reshape(ms, M_WIDTH)
        n0 = jnp.pad(state_mlstm_n[l], ((0, 0), (0, 8 - M_HEADS), (0, 0)))
        m0 = jnp.broadcast_to(jnp.pad(state_mlstm_m[l], ((0, 0), (0, 128 - M_HEADS)))[:, None, :], (Bs, SEG, 128))
        ha_s, c_s, n_s, mt_s = _mlstm_sample(xm, sm, smt, tail, state_mlstm_C[l], n0, m0.reshape(ms, 128), conv_w[l],
                                             conv_b[l], wq, wk, b_gates[l], mlstm_norm_g[l], row0=mp, nseq=Bs, n_valid=Ts)

        kc, vc = _compress_prompt(kvrow, _compress_weights(cmp_pe_k[l], cmp_w1_k[l], cmp_w2_k[l]),
                                  _compress_weights(cmp_pe_v[l], cmp_w1_v[l], cmp_w2_v[l]), nb=Bp, T=Tp)
        hb_p = _nsa_prompt(qc, qr, sm, att, kc, vc, nb=Bp, T=Tp)
        hb_s, win_s = _nsa_sample(page_table, cache4, cwin4, qc, qr, sm, kvrow, winrow,
                                  _compress_weights(cmp_pe_k[l], cmp_w1_k[l], cmp_w2_k[l], padded=False),
                                  _compress_weights(cmp_pe_v[l], cmp_w1_v[l], cmp_w2_v[l], padded=False),
                                  layer=l, row0=mp, ts=Ts)

        h_a = jnp.concatenate([ha_p, ha_s], axis=0)
        h_b = jnp.concatenate([hb_p, hb_s], axis=0)
        wb_ext = jnp.concatenate([w_branch_b[l], jnp.zeros((1, D_MODEL), f32)], axis=0)
        x_all, y_all = _mix_ffn(
            x_all, h_a, h_b, gates, w_branch_a[l].astype(bf16), wb_ext[wb_rows].astype(bf16), w_out[l].astype(bf16),
            norm2_g[l].reshape(1, D_MODEL), w_ffn_up[l].astype(bf16), w_ffn_down[l].astype(bf16),
            final_norm_g.reshape(1, D_MODEL))

        sample_rows = lambda a: a[mp:].reshape(Bs, SAMPLE_PAD, a.shape[1])
        wp = min(WINDOW, Tp)
        outs["p_kv"].append(kvrow[:mp].reshape(Bp, Tp, 4, N_KV, HEAD_DIM))
        outs["p_win"].append(winrow[:mp].reshape(Bp, Tp, 2 * KV_W)[:, -wp:].reshape(Bp, wp, 2, N_KV, HEAD_DIM))
        outs["p_C"].append(c_p)
        outs["p_n"].append(n_p[:, :M_HEADS])
        outs["p_m"].append(m_p[:, 0, :M_HEADS])
        outs["p_conv"].append(cv_p[:, 8 - (M_CONV - 1):])
        outs["s_kv"].append(sample_rows(kvrow)[:, :Ts].reshape(Bs, Ts, 4, N_KV, HEAD_DIM))
        outs["s_win"].append(win_s.reshape(Bs, wb_len, 2, N_KV, HEAD_DIM))
        outs["s_C"].append(c_s)
        outs["s_n"].append(n_s[:, :M_HEADS])
        outs["s_m"].append(mt_s.reshape(Bs, SEG, 128)[:, SEG - 1, :M_HEADS])
        outs["s_conv"].append(sample_rows(xm)[:, Ts - (M_CONV - 1):Ts, :M_WIDTH])

    y_prompt = y_all[:mp].reshape(Bp, Tp, D_MODEL)
    y_sample = y_all[mp:].reshape(Bs, SAMPLE_PAD, D_MODEL)[:, :Ts]
    st = lambda k: jnp.stack(outs[k])
    return (y_prompt, y_sample, st("p_kv"), st("p_win"), st("p_C"), st("p_n"), st("p_m"), st("p_conv"),
            st("s_kv"), st("s_win"), st("s_C"), st("s_n"), st("s_m"), st("s_conv"))
```

```python
import functools

import jax
import jax.numpy as jnp
import numpy as np
from jax import lax
from jax.experimental import pallas as pl
from jax.experimental.pallas import tpu as pltpu

f32 = jnp.float32
bf16 = jnp.bfloat16

D_MODEL = 1024
M_HEADS = 4
M_WIDTH = 512
M_HEAD_DIM = 128
M_CONV = 4
MLSTM_CHUNK = 256
HEAD_DIM = 64
N_WIDTH = 512
N_HEADS = 8
N_KV = 2
KV_W = 128
CMP_STRIDE = 16
CMP_LEN = 32
SEL_BLOCK = 64
SEL_TOP = 16
WINDOW = 512
ROT_DIM = 16
ROPE_THETA = 500000.0
D_FF = 2816
EPS = 1e-6
NEG = -1e30
TINY = 1e-30
FORCE_SCORE = 1e9

SAMPLE_PAD = 8
SEG = SAMPLE_PAD
SEQ_PER_STEP = 16
GB = 2
PAGE = 128
HD = M_HEAD_DIM
R = N_HEADS // N_KV
CMP_HID = 128
TQ = 128
TK = 512
NBLK_PAD = 64
HIGHEST = lax.Precision.HIGHEST
NT = (((1,), (1,)), ((), ()))
LOG2E = 1.4426950408889634
ROW_TILE = 256
FF_CHUNK = 256
VMEM_LIMIT = 56 * 1024 * 1024

C_XM, C_SM, C_Q, C_KV, C_WIN, C_ATT, C_GATE = 0, 1536, 1664, 2688, 3200, 3456, 4480
C_TOTAL = 6528
O_U, O_V, O_O, O_IG, O_FG, O_Q, O_KV, O_NG, O_GA, O_GB = 0, 512, 1024, 1536, 1540, 1544, 2056, 2824, 2848, 3872
IN_WIDTH = 4896


def _in_proj_columns():
    z = IN_WIDTH
    idx = list(range(O_U, O_IG))
    idx += list(range(O_IG, O_IG + 8)) + list(range(O_NG, O_NG + 24)) + [z] * 96
    for h in range(N_HEADS):
        idx += list(range(O_Q + h * 64, O_Q + (h + 1) * 64)) + [z] * 64
    idx += list(range(O_KV, O_KV + 512))
    idx += list(range(O_KV + 512, O_KV + 768))
    for c in (2, 3, 4, 5):
        for g in range(N_KV):
            s = O_KV + c * 128 + g * 64
            idx += list(range(s, s + 64)) + [z] * 64
    idx += list(range(O_GA, O_GA + 2048))
    assert len(idx) == C_TOTAL
    scale = np.ones((C_TOTAL,), np.float32)
    scale[C_Q:C_KV] = HEAD_DIM ** -0.5
    return np.asarray(idx, np.int32), scale


def _rope_tables(seq, past_len, n_sample_rows):
    half = ROT_DIM // 2
    inv = ROPE_THETA ** (-jnp.arange(half, dtype=f32) / half)
    pos = jnp.concatenate([jnp.arange(seq), past_len + (jnp.arange(n_sample_rows) % SAMPLE_PAD)]).astype(f32)
    ang = pos[:, None] * inv[None, :]
    cos8, sin8 = jnp.cos(ang), jnp.sin(ang)
    n = pos.shape[0]
    one = jnp.ones((n, 64 - ROT_DIM), f32)
    zero = jnp.zeros((n, 64 - ROT_DIM), f32)
    z8 = jnp.zeros((n, half), f32)
    cos = jnp.concatenate([cos8, cos8, one], axis=1)
    sa = jnp.concatenate([-sin8, z8, zero], axis=1)
    sb = jnp.concatenate([z8, sin8, zero], axis=1)
    tile2 = lambda a: jnp.concatenate([a, a], axis=1)
    return tile2(cos), tile2(sa), tile2(sb)


def _rms(x, g):
    return x * lax.rsqrt(jnp.mean(x * x, axis=-1, keepdims=True) + EPS) * g


def _in_proj_body(x_ref, g_ref, w_ref, wt_ref, cos_ref, sa_ref, sb_ref,
                  xm_ref, sm_ref, smt_ref, qc_ref, qr_ref, kv_ref, win_ref, att_ref, gate_ref):
    hb = _rms(x_ref[...], g_ref[...]).astype(bf16)
    cos, sa, sb = cos_ref[...], sa_ref[...], sb_ref[...]
    lane = lax.broadcasted_iota(jnp.int32, cos.shape, 1)

    def rope(v):
        return v * cos + pltpu.roll(v, 128 - ROT_DIM // 2, axis=1) * sa + pltpu.roll(v, ROT_DIM // 2, axis=1) * sb

    def proj(c0, n):
        return jnp.dot(hb, w_ref[:, c0:c0 + n], preferred_element_type=f32)

    xm_ref[...] = proj(C_XM, 1536)
    sm_ref[...] = proj(C_SM, 128)
    smt_ref[...] = lax.dot_general(wt_ref[...], hb, NT, preferred_element_type=f32)
    for h in range(N_HEADS):
        q = proj(C_Q + h * 128, 128) * LOG2E
        qc_ref[:, h * 128:(h + 1) * 128] = q.astype(bf16)
        qr_ref[:, h * 128:(h + 1) * 128] = rope(q).astype(bf16)
    kv = proj(C_KV, 512)
    kv_ref[:, 0:256] = kv[:, 0:256]
    kv_ref[:, 256:384] = rope(kv[:, 256:384])
    kv_ref[:, 384:512] = kv[:, 384:512]
    win = proj(C_WIN, 256)
    win_ref[:, 0:128] = rope(win[:, 0:128])
    win_ref[:, 128:256] = win[:, 128:256]
    for j in range(8):
        a = proj(C_ATT + j * 128, 128)
        if j in (0, 1, 4, 5):
            a = rope(a)
        else:
            a = jnp.where(lane == 64, 1.0, a)
        att_ref[:, j * 128:(j + 1) * 128] = a.astype(bf16)
    gate_ref[...] = proj(C_GATE, 2048)


def _in_proj(x_all, norm_g, w_all, w_t, tabs, seq):
    m = x_all.shape[0]
    n_prompt_tiles_per_seq = seq // ROW_TILE
    n_prompt_tiles = (m - (tabs[0].shape[0] - seq)) // ROW_TILE

    def tab_map(i):
        return (jnp.where(i < n_prompt_tiles, i % n_prompt_tiles_per_seq, n_prompt_tiles_per_seq + i - n_prompt_tiles), 0)

    row = lambda w: pl.BlockSpec((ROW_TILE, w), lambda i: (i, 0))
    const = lambda s: pl.BlockSpec(s, lambda i: (0, 0), pipeline_mode=pl.Buffered(1))
    tab = pl.BlockSpec((ROW_TILE, 128), tab_map)
    widths = (1536, 128, None, 1024, 1024, 512, 256, 1024, 2048)
    dts = (f32, f32, f32, bf16, bf16, f32, f32, bf16, f32)
    out_shape = [jax.ShapeDtypeStruct((8, m) if w is None else (m, w), d) for w, d in zip(widths, dts)]
    out_specs = [pl.BlockSpec((8, ROW_TILE), lambda i: (0, i)) if w is None else row(w) for w in widths]
    return pl.pallas_call(
        _in_proj_body, grid=(m // ROW_TILE,),
        in_specs=[row(D_MODEL), const((1, D_MODEL)), const((D_MODEL, C_TOTAL)), const((8, D_MODEL)), tab, tab, tab],
        out_specs=out_specs, out_shape=out_shape, name="in_proj",
        compiler_params=pltpu.CompilerParams(dimension_semantics=("parallel",), vmem_limit_bytes=VMEM_LIMIT),
    )(x_all, norm_g, w_all, w_t, *tabs)


def _mix_ffn_body(x_ref, ha_ref, hb_ref, gate_ref, wa_ref, wb_ref, wo_ref, g2_ref, wup_ref, wdn_ref, gf_ref,
                  xo_ref, y_ref):
    a = jnp.dot(ha_ref[...].astype(bf16), wa_ref[...], preferred_element_type=f32)
    b = jnp.dot(hb_ref[...].astype(bf16), wb_ref[...], preferred_element_type=f32)
    merged = jax.nn.sigmoid(gate_ref[:, 0:D_MODEL]) * a + jax.nn.sigmoid(gate_ref[:, D_MODEL:2 * D_MODEL]) * b
    x1 = x_ref[...] + jnp.dot(merged.astype(bf16), wo_ref[...], preferred_element_type=f32)
    hn = _rms(x1, g2_ref[...]).astype(bf16)
    acc = x1
    for c in range(D_FF // FF_CHUNK):
        lo = c * FF_CHUNK
        g = jnp.dot(hn, wup_ref[:, lo:lo + FF_CHUNK], preferred_element_type=f32)
        u = jnp.dot(hn, wup_ref[:, D_FF + lo:D_FF + lo + FF_CHUNK], preferred_element_type=f32)
        act = (g * jax.nn.sigmoid(g) * u).astype(bf16)
        acc = acc + jnp.dot(act, wdn_ref[lo:lo + FF_CHUNK, :], preferred_element_type=f32)
    xo_ref[...] = acc
    y_ref[...] = _rms(acc, gf_ref[...])


def _mix_ffn(x_all, h_a, h_b, gates, wa, wb, wo, g2, wup, wdn, gf):
    m = x_all.shape[0]
    row = lambda w: pl.BlockSpec((ROW_TILE, w), lambda i: (i, 0))
    const = lambda s: pl.BlockSpec(s, lambda i: (0, 0), pipeline_mode=pl.Buffered(1))
    return pl.pallas_call(
        _mix_ffn_body, grid=(m // ROW_TILE,),
        in_specs=[row(D_MODEL), row(h_a.shape[1]), row(h_b.shape[1]), row(2 * D_MODEL),
                  const(wa.shape), const(wb.shape), const(wo.shape), const((1, D_MODEL)),
                  const(wup.shape), const(wdn.shape), const((1, D_MODEL))],
        out_specs=[row(D_MODEL), row(D_MODEL)],
        out_shape=[jax.ShapeDtypeStruct((m, D_MODEL), f32)] * 2, name="mix_ffn",
        compiler_params=pltpu.CompilerParams(dimension_semantics=("parallel",), vmem_limit_bytes=VMEM_LIMIT),
    )(x_all, h_a, h_b, gates, wa, wb, wo, g2, wup, wdn, gf)


def _log_sigmoid(x):
    return jnp.minimum(x, 0.0) - jnp.log1p(jnp.exp(-jnp.abs(x)))


def _seg_scan(x, axis, seg, op, fill):
    idx = lax.broadcasted_iota(jnp.int32, x.shape, axis) % seg
    d = 1
    while d < seg:
        x = op(x, jnp.where(idx >= d, pltpu.roll(x, d, axis=axis), fill))
        d *= 2
    return x


def _conv_silu(u, tail, conv_w, conv_b, seg):
    L = u.shape[0]
    row = lax.broadcasted_iota(jnp.int32, u.shape, 0) % seg
    y = conv_b + u * conv_w[3:4, :]
    for k in (1, 2, 3):
        prev = tail if k == 3 else pltpu.roll(tail, L - (3 - k), axis=0)
        sh = jnp.where(row < k, prev, pltpu.roll(u, k, axis=0))
        y = y + sh * conv_w[3 - k:4 - k, :]
    return y * jax.nn.sigmoid(y)


def _gate_columns(sm, smt, bg_row, bg_col, m_vec, seg, n_valid):
    G = sm + bg_row
    lf = _log_sigmoid(G)
    ipre = G
    if n_valid < seg:
        rowc = lax.broadcasted_iota(jnp.int32, G.shape, 0) % seg
        lf = jnp.where(rowc < n_valid, lf, 0.0)
        ipre = jnp.where(rowc < n_valid, G, NEG)
    F = _seg_scan(lf, 0, seg, jnp.add, 0.0)
    F_al = pltpu.roll(F, 128 - M_HEADS, axis=1)
    a_col = ipre - F_al
    g_col = jnp.maximum(m_vec, _seg_scan(a_col, 0, seg, jnp.maximum, NEG))
    mt_col = F_al + g_col
    Gr = smt + bg_col
    lfr = _log_sigmoid(Gr)
    ir = Gr
    if n_valid < seg:
        lanec = lax.broadcasted_iota(jnp.int32, Gr.shape, 1) % seg
        lfr = jnp.where(lanec < n_valid, lfr, 0.0)
        ir = jnp.where(lanec < n_valid, Gr, NEG)
    Fr = _seg_scan(lfr, 1, seg, jnp.add, 0.0)
    a_row = ir - pltpu.roll(Fr, M_HEADS, axis=0)
    return a_col, g_col, mt_col, a_row


def _head_norm_gate(hh, ng, o):
    return hh * lax.rsqrt(jnp.mean(hh * hh, axis=-1, keepdims=True) + EPS) * ng * jax.nn.sigmoid(o)


def _mlstm_prompt_body(xm_ref, sm_ref, smt_ref, cw_ref, cb_ref, wq_ref, wk_ref, bgr_ref, bgc_ref, ng_ref,
                       ha_ref, c_out, n_out, m_out, conv_out, c_scr, n_scr, m_scr, tail_scr, *, L):
    c_idx = pl.program_id(1)

    @pl.when(c_idx == 0)
    def _():
        c_scr[...] = jnp.zeros_like(c_scr)
        n_scr[...] = jnp.zeros_like(n_scr)
        m_scr[...] = jnp.zeros_like(m_scr)
        tail_scr[...] = jnp.zeros_like(tail_scr)

    u = xm_ref[:, 0:M_WIDTH]
    tail = jnp.concatenate([tail_scr[...], jnp.zeros((L - 8, M_WIDTH), f32)], axis=0)
    cact = _conv_silu(u, tail, cw_ref[...], cb_ref[...], L)
    tail_scr[...] = pltpu.roll(u[L - 8:L, :], 3, axis=0)
    conv_out[0] = u[L - 8:L, :]

    m_vec = m_scr[0:1, :]
    a_col, g_col, mt_col, a_row = _gate_columns(sm_ref[...], smt_ref[...], bgr_ref[...], bgc_ref[...], m_vec, L, L)
    inter_col = jnp.exp(m_vec - g_col)
    floor_col = jnp.exp(-mt_col)
    g_last = g_col[L - 1:L, :]
    w_col = jnp.exp(a_col - g_last)
    decay = jnp.exp(m_vec - g_last)
    m_new = mt_col[L - 1:L, :]
    ti = lax.broadcasted_iota(jnp.int32, (L, L), 0)
    si = lax.broadcasted_iota(jnp.int32, (L, L), 1)
    causal = si <= ti
    scale = HD ** -0.5
    for h in range(M_HEADS):
        sl = slice(h * HD, (h + 1) * HD)
        ch = cact[:, sl].astype(bf16)
        qh = jnp.dot(ch, wq_ref[h], preferred_element_type=f32)
        kh = jnp.dot(ch, wk_ref[h], preferred_element_type=f32) * scale
        vh = xm_ref[:, M_WIDTH + h * HD:M_WIDTH + (h + 1) * HD].astype(bf16)
        qb = qh.astype(bf16)
        D = jnp.exp(jnp.where(causal, a_row[h:h + 1, :] - g_col[:, h:h + 1], NEG))
        S = lax.dot_general(qb, kh.astype(bf16), NT, preferred_element_type=f32) * D
        C = c_scr[h]
        nrow = n_scr[h:h + 1, :]
        ic = inter_col[:, h:h + 1]
        num = jnp.dot(S.astype(bf16), vh, preferred_element_type=f32) + ic * jnp.dot(
            qb, C.astype(bf16), preferred_element_type=f32)
        den = jnp.sum(S, axis=-1, keepdims=True) + ic * jnp.sum(qh * nrow, axis=-1, keepdims=True)
        hh = num / jnp.maximum(jnp.abs(den), floor_col[:, h:h + 1])
        o = xm_ref[:, 2 * M_WIDTH + h * HD:2 * M_WIDTH + (h + 1) * HD]
        ha_ref[:, sl] = _head_norm_gate(hh, ng_ref[:, sl], o)
        kw = kh * w_col[:, h:h + 1]
        dh = decay[:, h:h + 1]
        c_scr[h] = dh * C + lax.dot_general(kw.astype(bf16), vh, (((0,), (0,)), ((), ())), preferred_element_type=f32)
        n_scr[h:h + 1, :] = dh * nrow + jnp.sum(kw, axis=0, keepdims=True)
    m_scr[0:1, :] = m_new

    @pl.when(c_idx == pl.num_programs(1) - 1)
    def _():
        c_out[0] = c_scr[...]
        n_out[0] = n_scr[...]
        m_out[0] = m_scr[...]


def _mlstm_prompt(xm, sm, smt, conv_w, conv_b, wq, wk, b_gates, norm_g, *, nb, T, L):
    nc = T // L
    bg_row = jnp.zeros((1, 128), f32).at[0, :8].set(b_gates)
    bg_col = b_gates.reshape(8, 1)
    const = lambda s: pl.BlockSpec(s, lambda b, c: (0,) * len(s))
    return pl.pallas_call(
        functools.partial(_mlstm_prompt_body, L=L), grid=(nb, nc),
        in_specs=[pl.BlockSpec((L, 3 * M_WIDTH), lambda b, c: (b * nc + c, 0)),
                  pl.BlockSpec((L, 128), lambda b, c: (b * nc + c, 0)),
                  pl.BlockSpec((8, L), lambda b, c: (0, b * nc + c)),
                  const((4, M_WIDTH)), const((1, M_WIDTH)), const((M_HEADS, HD, HD)), const((M_HEADS, HD, HD)),
                  const((1, 128)), const((8, 1)), const((1, M_WIDTH))],
        out_specs=[pl.BlockSpec((L, M_WIDTH), lambda b, c: (b * nc + c, 0)),
                   pl.BlockSpec((1, M_HEADS, HD, HD), lambda b, c: (b, 0, 0, 0)),
                   pl.BlockSpec((1, 8, HD), lambda b, c: (b, 0, 0)),
                   pl.BlockSpec((1, 8, 128), lambda b, c: (b, 0, 0)),
                   pl.BlockSpec((1, 8, M_WIDTH), lambda b, c: (b, 0, 0))],
        out_shape=[jax.ShapeDtypeStruct((nb * T, M_WIDTH), f32),
                   jax.ShapeDtypeStruct((nb, M_HEADS, HD, HD), f32),
                   jax.ShapeDtypeStruct((nb, 8, HD), f32),
                   jax.ShapeDtypeStruct((nb, 8, 128), f32),
                   jax.ShapeDtypeStruct((nb, 8, M_WIDTH), f32)],
        scratch_shapes=[pltpu.VMEM((M_HEADS, HD, HD), f32), pltpu.VMEM((8, HD), f32), pltpu.VMEM((8, 128), f32),
                        pltpu.VMEM((8, M_WIDTH), f32)],
        compiler_params=pltpu.CompilerParams(dimension_semantics=("parallel", "arbitrary")),
        name="mlstm_prompt",
    )(xm, sm, smt, conv_w, conv_b.reshape(1, M_WIDTH), wq, wk, bg_row, bg_col, norm_g.reshape(1, M_WIDTH))


def _mlstm_sample_body(xm_ref, sm_ref, smt_ref, tail_ref, c_in, n_in, m_in, cw_ref, cb_ref, wq_ref, wk_ref,
                       bgr_ref, bgc_ref, ng_ref, ha_ref, c_out, n_out, m_out, *, n_valid):
    L = SEG * SEQ_PER_STEP
    u = xm_ref[:, 0:M_WIDTH]
    cact = _conv_silu(u, tail_ref[...], cw_ref[...], cb_ref[...], SEG)
    m_rows = m_in[...]
    a_col, g_col, mt_col, a_row = _gate_columns(sm_ref[...], smt_ref[...], bgr_ref[...], bgc_ref[...], m_rows, SEG,
                                                n_valid)
    inter_col = jnp.exp(m_rows - g_col)
    floor_col = jnp.exp(-mt_col)
    ti = lax.broadcasted_iota(jnp.int32, (L, L), 0)
    si = lax.broadcasted_iota(jnp.int32, (L, L), 1)
    mask = (si <= ti) & ((si // SEG) == (ti // SEG))
    scale = HD ** -0.5
    for h in range(M_HEADS):
        sl = slice(h * HD, (h + 1) * HD)
        ch = cact[:, sl].astype(bf16)
        qh = jnp.dot(ch, wq_ref[h], preferred_element_type=f32)
        kh = jnp.dot(ch, wk_ref[h], preferred_element_type=f32) * scale
        vh = xm_ref[:, M_WIDTH + h * HD:M_WIDTH + (h + 1) * HD].astype(bf16)
        qb = qh.astype(bf16)
        D = jnp.exp(jnp.where(mask, a_row[h:h + 1, :] - g_col[:, h:h + 1], NEG))
        S = lax.dot_general(qb, kh.astype(bf16), NT, preferred_element_type=f32) * D
        num = jnp.dot(S.astype(bf16), vh, preferred_element_type=f32)
        den = jnp.sum(S, axis=-1, keepdims=True)
        inter_num, inter_den = [], []
        for s in range(SEQ_PER_STEP):
            rs = slice(s * SEG, (s + 1) * SEG)
            C = c_in[s, h]
            nrow = n_in[s, h:h + 1, :]
            inter_num.append(jnp.dot(qb[rs], C.astype(bf16), preferred_element_type=f32))
            inter_den.append(jnp.sum(qh[rs] * nrow, axis=-1, keepdims=True))
            g_last = g_col[s * SEG + SEG - 1:s * SEG + SEG, h:h + 1]
            m_prev = m_rows[s * SEG:s * SEG + 1, h:h + 1]
            w = jnp.exp(a_col[rs, h:h + 1] - g_last)
            dh = jnp.exp(m_prev - g_last)
            kw = kh[rs] * w
            c_out[s, h] = dh * C + lax.dot_general(kw.astype(bf16), vh[rs], (((0,), (0,)), ((), ())),
                                                    preferred_element_type=f32)
            n_out[s, h:h + 1, :] = dh * nrow + jnp.sum(kw, axis=0, keepdims=True)
        ic = inter_col[:, h:h + 1]
        num = num + ic * jnp.concatenate(inter_num, axis=0)
        den = den + ic * jnp.concatenate(inter_den, axis=0)
        hh = num / jnp.maximum(jnp.abs(den), floor_col[:, h:h + 1])
        o = xm_ref[:, 2 * M_WIDTH + h * HD:2 * M_WIDTH + (h + 1) * HD]
        ha_ref[:, sl] = _head_norm_gate(hh, ng_ref[:, sl], o)
    for s in range(SEQ_PER_STEP):
        n_out[s, M_HEADS:8, :] = jnp.zeros((8 - M_HEADS, HD), f32)
    m_out[...] = mt_col


def _mlstm_sample(xm, sm, smt, tail, c0, n0, m0, conv_w, conv_b, wq, wk, b_gates, norm_g, *, row0, nseq, n_valid):
    L = SEG * SEQ_PER_STEP
    b0 = row0 // L
    bg_row = jnp.zeros((1, 128), f32).at[0, :8].set(b_gates)
    bg_col = b_gates.reshape(8, 1)
    const = lambda s: pl.BlockSpec(s, lambda i: (0,) * len(s))
    return pl.pallas_call(
        functools.partial(_mlstm_sample_body, n_valid=n_valid), grid=(nseq // SEQ_PER_STEP,),
        in_specs=[pl.BlockSpec((L, 3 * M_WIDTH), lambda i: (b0 + i, 0)),
                  pl.BlockSpec((L, 128), lambda i: (b0 + i, 0)),
                  pl.BlockSpec((8, L), lambda i: (0, b0 + i)),
                  pl.BlockSpec((L, M_WIDTH), lambda i: (i, 0)),
                  pl.BlockSpec((SEQ_PER_STEP, M_HEADS, HD, HD), lambda i: (i, 0, 0, 0)),
                  pl.BlockSpec((SEQ_PER_STEP, 8, HD), lambda i: (i, 0, 0)),
                  pl.BlockSpec((L, 128), lambda i: (i, 0)),
                  const((4, M_WIDTH)), const((1, M_WIDTH)), const((M_HEADS, HD, HD)), const((M_HEADS, HD, HD)),
                  const((1, 128)), const((8, 1)), const((1, M_WIDTH))],
        out_specs=[pl.BlockSpec((L, M_WIDTH), lambda i: (i, 0)),
                   pl.BlockSpec((SEQ_PER_STEP, M_HEADS, HD, HD), lambda i: (i, 0, 0, 0)),
                   pl.BlockSpec((SEQ_PER_STEP, 8, HD), lambda i: (i, 0, 0)),
                   pl.BlockSpec((L, 128), lambda i: (i, 0))],
        out_shape=[jax.ShapeDtypeStruct((nseq * SEG, M_WIDTH), f32),
                   jax.ShapeDtypeStruct((nseq, M_HEADS, HD, HD), f32),
                   jax.ShapeDtypeStruct((nseq, 8, HD), f32),
                   jax.ShapeDtypeStruct((nseq * SEG, 128), f32)],
        compiler_params=pltpu.CompilerParams(dimension_semantics=("parallel",), vmem_limit_bytes=VMEM_LIMIT),
        name="mlstm_sample",
    )(xm, sm, smt, tail, c0, n0, m0, conv_w, conv_b.reshape(1, M_WIDTH), wq, wk, bg_row, bg_col,
      norm_g.reshape(1, M_WIDTH))


def _sel_overlap_t(n_chunk, n_blk):
    n_cmp = n_chunk - 1
    start = np.arange(n_cmp) * CMP_STRIDE
    bs = np.arange(n_blk) * SEL_BLOCK
    ov = np.minimum(start[:, None] + CMP_LEN, bs[None, :] + SEL_BLOCK) - np.maximum(start[:, None], bs[None, :])
    ov = np.clip(ov, 0, None) / CMP_LEN
    out = np.zeros((NBLK_PAD, n_chunk), np.float32)
    out[:n_blk, :n_cmp] = ov.T
    return jnp.asarray(out)


def _block_expand(n_keys):
    e = (np.arange(n_keys)[None, :] // SEL_BLOCK) == np.arange(128)[:, None]
    return jnp.asarray(e, dtype=bf16)


def _compress_weights(pe, w1, w2, padded=True):
    W = jnp.zeros((16, 2, 64, 2, 2, CMP_HID), f32)
    for g in range(N_KV):
        W = W.at[:, g, :, g, 0, :].set(w1[:16])
        W = W.at[:, g, :, g, 1, :].set(w1[16:])
    W = W.reshape(2048, 512)
    peA = jnp.broadcast_to(pe[:16, None, :], (16, 2, 64)).reshape(1, 2048)
    peB = jnp.broadcast_to(pe[16:, None, :], (16, 2, 64)).reshape(1, 2048)
    PE = jnp.concatenate([peA, peB, jnp.zeros((6, 2048), f32)], axis=0)
    if padded:
        W2 = jnp.zeros((2, CMP_HID, 256), f32).at[0, :, 0:64].set(w2).at[1, :, 128:192].set(w2)
    else:
        W2 = jnp.zeros((2, CMP_HID, 128), f32).at[0, :, 0:64].set(w2).at[1, :, 64:128].set(w2)
    return W.astype(bf16), PE.astype(bf16), W2.astype(bf16)


def _compress_rows(flat, w_ref, pe_ref, w2_ref):
    n = flat.shape[0]
    y = jnp.dot(flat, w_ref[...], preferred_element_type=f32)
    c = jnp.dot(pe_ref[...], w_ref[...], preferred_element_type=f32)
    out = jnp.zeros((n, w2_ref.shape[2]), f32)
    for g in range(N_KV):
        a = y[:, (2 * g) * CMP_HID:(2 * g + 1) * CMP_HID] + c[0:1, (2 * g) * CMP_HID:(2 * g + 1) * CMP_HID]
        b = y[:, (2 * g + 1) * CMP_HID:(2 * g + 2) * CMP_HID] + c[1:2, (2 * g + 1) * CMP_HID:(2 * g + 2) * CMP_HID]
        hid = a + pltpu.roll(b, n - 1, axis=0)
        out = out + jnp.dot(jax.nn.gelu(hid, approximate=True).astype(bf16), w2_ref[g], preferred_element_type=f32)
    return out


def _compress_prompt_body(xk_ref, xv_ref, wk_ref, pek_ref, w2k_ref, wv_ref, pev_ref, w2v_ref, kc_ref, vc_ref, *,
                          n_chunk):
    def flat(x_ref):
        return jnp.concatenate(
            [x_ref[pl.ds(l, n_chunk, stride=CMP_STRIDE), :].astype(bf16) for l in range(CMP_STRIDE)], axis=1)
    kc_ref[0] = _compress_rows(flat(xk_ref), wk_ref, pek_ref, w2k_ref).astype(bf16)
    vc_ref[0] = _compress_rows(flat(xv_ref), wv_ref, pev_ref, w2v_ref).astype(bf16)


def _compress_prompt(kvrow, cwk, cwv, *, nb, T):
    n_chunk = T // CMP_STRIDE
    const = lambda a: pl.BlockSpec(a.shape, lambda b: (0,) * a.ndim)
    return pl.pallas_call(
        functools.partial(_compress_prompt_body, n_chunk=n_chunk), grid=(nb,),
        in_specs=[pl.BlockSpec((T, 128), lambda b: (b, 0)), pl.BlockSpec((T, 128), lambda b: (b, 1))]
        + [const(a) for a in (*cwk, *cwv)],
        out_specs=[pl.BlockSpec((1, n_chunk, 256), lambda b: (b, 0, 0))] * 2,
        out_shape=[jax.ShapeDtypeStruct((nb, n_chunk, 256), bf16)] * 2,
        compiler_params=pltpu.CompilerParams(dimension_semantics=("parallel",), vmem_limit_bytes=VMEM_LIMIT),
        name="nsa_compress",
    )(kvrow, kvrow, *cwk, *cwv)


def _softmax_rows(s, mask):
    s = jnp.where(mask, s, NEG)
    e = jnp.exp2(s - jnp.max(s, axis=-1, keepdims=True))
    e = jnp.where(mask, e, 0.0)
    return e / jnp.maximum(jnp.sum(e, axis=-1, keepdims=True), TINY)


def _select_blocks(imp_t, qpos_row, n_blk, score_scr):
    shape = imp_t.shape
    j = lax.broadcasted_iota(jnp.int32, shape, 0)
    cur = qpos_row // SEL_BLOCK
    forced = (j == 0) | (j == cur) | (j == cur - 1)
    score = jnp.where(j <= cur, jnp.where(forced, FORCE_SCORE, imp_t), -1.0)
    score = jnp.where(j < n_blk, score, -2.0)
    score_scr[...] = score

    def body(jp, cnt):
        sj = score_scr[pl.ds(jp, 1), :]
        return cnt + jnp.where(j > jp, jnp.where(sj >= score, 1.0, 0.0), jnp.where(sj > score, 1.0, 0.0))

    cnt = lax.fori_loop(0, n_blk, body, jnp.zeros(shape, f32), unroll=8 if n_blk % 8 == 0 else 1)
    return jnp.where(cnt < float(min(SEL_TOP, n_blk)), 1.0, 0.0)


def _transpose_sel(sel_t):
    return jnp.concatenate([sel_t, jnp.zeros((128 - NBLK_PAD, sel_t.shape[1]), f32)], axis=0).T


def _nsa_prompt_body(qc_ref, qr_ref, sm_ref, att_ref, kc_ref, vc_ref, ovt_ref, e_ref, hb_ref, score_scr, *, T):
    i = pl.program_id(1)
    n_chunk = T // CMP_STRIDE
    n_blk = T // SEL_BLOCK
    q0 = i * TQ
    rows = R * TQ
    qpos_col = q0 + lax.broadcasted_iota(jnp.int32, (rows, 1), 0) % TQ
    qpos_q = q0 + lax.broadcasted_iota(jnp.int32, (TQ, 1), 0)
    qpos_row = q0 + lax.broadcasted_iota(jnp.int32, (1, TQ), 1)
    gsig = jax.nn.sigmoid(sm_ref[...])
    n_kt = (q0 + TQ + TK - 1) // TK
    w0 = jnp.maximum(i - WINDOW // TQ, 0) * TQ
    WK = WINDOW + TQ
    for g in range(N_KV):
        stack = lambda ref: jnp.concatenate([ref[:, (g * R + r) * 128:(g * R + r + 1) * 128] for r in range(R)], axis=0)
        qc = stack(qc_ref)
        qr = stack(qr_ref)
        s = lax.dot_general(qc, kc_ref[0][:, g * 128:(g + 1) * 128], NT, preferred_element_type=f32)
        cmp_end = lax.broadcasted_iota(jnp.int32, (1, n_chunk), 1) * CMP_STRIDE + (CMP_LEN - 1)
        p = _softmax_rows(s, cmp_end <= qpos_col)
        o_c = jnp.dot(p.astype(bf16), vc_ref[0][:, g * 128:(g + 1) * 128], preferred_element_type=f32)
        psum = p[0:TQ] + p[TQ:2 * TQ] + p[2 * TQ:3 * TQ] + p[3 * TQ:4 * TQ]
        imp_t = lax.dot_general(ovt_ref[...], psum, NT, preferred_element_type=f32, precision=HIGHEST)
        sel_q = _transpose_sel(_select_blocks(imp_t, qpos_row, n_blk, score_scr)).astype(bf16)
        ks_l, vs_l = (0 + g) * 128, (2 + g) * 128

        def kv_step(kt, carry):
            m_i, acc = carry
            k0 = pl.multiple_of(kt * TK, TK)
            kk = att_ref[pl.ds(k0, TK), ks_l:ks_l + 128]
            vv = att_ref[pl.ds(k0, TK), vs_l:vs_l + 128]
            sc = lax.dot_general(qr, kk, NT, preferred_element_type=f32)
            mk = jnp.dot(sel_q, e_ref[:, pl.ds(k0, TK)], preferred_element_type=f32)
            kpos = k0 + lax.broadcasted_iota(jnp.int32, (1, TK), 1)
            bias = jnp.where((mk > 0.5) & (kpos <= qpos_q), 0.0, NEG)
            sc = sc + jnp.concatenate([bias] * R, axis=0)
            m_new = jnp.maximum(m_i, jnp.max(sc, axis=-1, keepdims=True))
            alpha = jnp.exp2(m_i - m_new)
            pp = jnp.exp2(sc - m_new)
            acc = alpha * acc + jnp.dot(pp.astype(bf16), vv, preferred_element_type=f32)
            return m_new, acc

        m0 = jnp.full((rows, 1), NEG, f32)
        _, acc = lax.fori_loop(0, n_kt, kv_step, (m0, jnp.zeros((rows, 128), f32)))
        o_s = acc / acc[:, 64:65]
        kw_l, vw_l = (4 + g) * 128, (6 + g) * 128
        w0a = pl.multiple_of(w0, TQ)
        kk = att_ref[pl.ds(w0a, WK), kw_l:kw_l + 128]
        vv = att_ref[pl.ds(w0a, WK), vw_l:vw_l + 128]
        sw = lax.dot_general(qr, kk, NT, preferred_element_type=f32)
        dpos = qpos_col - (w0 + lax.broadcasted_iota(jnp.int32, (1, WK), 1))
        sw = jnp.where((dpos >= 0) & (dpos < WINDOW), sw, NEG)
        pw = jnp.exp2(sw - jnp.max(sw, axis=-1, keepdims=True))
        ow = jnp.dot(pw.astype(bf16), vv, preferred_element_type=f32)
        o_w = ow / ow[:, 64:65]
        for r in range(R):
            h = g * R + r
            rs = slice(r * TQ, (r + 1) * TQ)
            c0 = 8 + 3 * h
            out = (gsig[:, c0:c0 + 1] * o_c[rs] + gsig[:, c0 + 1:c0 + 2] * o_s[rs] + gsig[:, c0 + 2:c0 + 3] * o_w[rs])
            hb_ref[:, h * 128:(h + 1) * 128] = out.astype(bf16)


def _nsa_prompt(qc, qr, sm, att, kc, vc, *, nb, T):
    nq = T // TQ
    n_chunk = T // CMP_STRIDE
    ovt = _sel_overlap_t(n_chunk, T // SEL_BLOCK)
    e = _block_expand(T)
    row = lambda w: pl.BlockSpec((TQ, w), lambda b, i: (b * nq + i, 0))
    const = lambda a: pl.BlockSpec(a.shape, lambda b, i: (0,) * a.ndim)
    return pl.pallas_call(
        functools.partial(_nsa_prompt_body, T=T), grid=(nb, nq),
        in_specs=[row(1024), row(1024), row(128), pl.BlockSpec((T, 1024), lambda b, i: (b, 0)),
                  pl.BlockSpec((1, n_chunk, 256), lambda b, i: (b, 0, 0)),
                  pl.BlockSpec((1, n_chunk, 256), lambda b, i: (b, 0, 0)),
                  const(ovt), const(e)],
        out_specs=row(1024),
        out_shape=jax.ShapeDtypeStruct((nb * T, 1024), bf16),
        scratch_shapes=[pltpu.VMEM((NBLK_PAD, TQ), f32)],
        compiler_params=pltpu.CompilerParams(dimension_semantics=("parallel", "arbitrary"),
                                             vmem_limit_bytes=VMEM_LIMIT),
        name="nsa_prompt",
    )(qc, qr, sm, att, kc, vc, ovt, e)


def _nsa_sample_body(pt_ref, *refs, n_pages, ts):
    n_pg = GB * n_pages
    pages = refs[:n_pg]
    (win_ref, qc_ref, qr_ref, sm_ref, kvn_ref, wn_ref, wk_ref, pek_ref, w2k_ref, wv_ref, pev_ref, w2v_ref,
     ovt_ref, e_ref, hb_ref, wout_ref, xk_scr, xv_scr, ks_scr, vs_scr, score_scr) = refs[n_pg:]
    P = n_pages * PAGE
    n_chunk = P // CMP_STRIDE
    n_blk = -(-(P + ts) // SEL_BLOCK)
    WB = win_ref.shape[4]
    for bi in range(GB):
        for p in range(n_pages):
            pg = pages[bi * n_pages + p]
            prow = pl.ds(bi * P + p * PAGE, PAGE)
            xk_scr[prow, :] = pg[0, 0, 0].T
            xv_scr[prow, :] = pg[0, 0, 1].T
            ks_scr[prow, :] = pg[0, 0, 2].T.astype(bf16)
            vs_scr[prow, :] = pg[0, 0, 3].T.astype(bf16)

    def flat(x_ref):
        return jnp.concatenate(
            [x_ref[pl.ds(l, GB * n_chunk, stride=CMP_STRIDE), :].astype(bf16) for l in range(CMP_STRIDE)], axis=1)

    kc_all = _compress_rows(flat(xk_scr), wk_ref, pek_ref, w2k_ref).astype(bf16)
    vc_all = _compress_rows(flat(xv_scr), wv_ref, pev_ref, w2v_ref).astype(bf16)

    qc_all = qc_ref[...].astype(f32)
    qr_all = qr_ref[...].astype(f32)
    gsig = jax.nn.sigmoid(sm_ref[...])
    rows = R * SEG
    qpos_col = P + lax.broadcasted_iota(jnp.int32, (rows, 1), 0) % SEG
    qpos_row = P + lax.broadcasted_iota(jnp.int32, (1, 128), 1) % SEG
    new_lane = lax.broadcasted_iota(jnp.int32, (1, 128), 1)
    zpad = jnp.zeros((128 - SEG, 128), f32)
    out_rows = []
    for bi in range(GB):
        rs8 = slice(bi * SEG, (bi + 1) * SEG)
        kc = kc_all[bi * n_chunk:(bi + 1) * n_chunk]
        vc = vc_all[bi * n_chunk:(bi + 1) * n_chunk]
        ksel = ks_scr[pl.ds(bi * P, P), :]
        vsel = vs_scr[pl.ds(bi * P, P), :]
        knew = jnp.concatenate([kvn_ref[rs8, 256:384], zpad], axis=0).astype(bf16)
        vnew = jnp.concatenate([kvn_ref[rs8, 384:512], zpad], axis=0).astype(bf16)
        kwnew_f = jnp.concatenate([wn_ref[rs8, 0:128], zpad], axis=0)
        vwnew_f = jnp.concatenate([wn_ref[rs8, 128:256], zpad], axis=0)
        kwnew, vwnew = kwnew_f.astype(bf16), vwnew_f.astype(bf16)
        kwin = win_ref[0, bi, 0].T.astype(bf16)
        vwin = win_ref[0, bi, 1].T.astype(bf16)
        heads = [None] * N_HEADS
        for g in range(N_KV):
            def stack(q_all):
                q = jnp.concatenate([q_all[rs8, (g * R + r) * 128:(g * R + r + 1) * 128] for r in range(R)], axis=0)
                if g:
                    q = pltpu.roll(q, g * HEAD_DIM, axis=1)
                return q.astype(bf16)
            qc = stack(qc_all)
            qr = stack(qr_all)
            s = lax.dot_general(qc, kc, NT, preferred_element_type=f32)
            cmp_end = lax.broadcasted_iota(jnp.int32, (1, n_chunk), 1) * CMP_STRIDE + (CMP_LEN - 1)
            p = _softmax_rows(s, cmp_end <= qpos_col)
            o_c = jnp.dot(p.astype(bf16), vc, preferred_element_type=f32)
            psum = p[0:SEG] + p[SEG:2 * SEG] + p[2 * SEG:3 * SEG] + p[3 * SEG:4 * SEG]
            psum = jnp.concatenate([psum, jnp.zeros((128 - SEG, n_chunk), f32)], axis=0)
            imp_t = lax.dot_general(ovt_ref[...], psum, NT, preferred_element_type=f32, precision=HIGHEST)
            sel_q = _transpose_sel(_select_blocks(imp_t, qpos_row, n_blk, score_scr))[0:SEG].astype(bf16)
            mk = jnp.concatenate([jnp.dot(sel_q, e_ref[...], preferred_element_type=f32)] * R, axis=0) > 0.5
            s_sel = jnp.concatenate([lax.dot_general(qr, ksel, NT, preferred_element_type=f32),
                                     lax.dot_general(qr, knew, NT, preferred_element_type=f32)], axis=1)
            kpos = jnp.concatenate([lax.broadcasted_iota(jnp.int32, (1, P), 1), P + new_lane], axis=1)
            valid = jnp.concatenate([jnp.full((1, P), True), new_lane < SEG], axis=1)
            p_s = _softmax_rows(s_sel, mk & (kpos <= qpos_col) & valid).astype(bf16)
            o_s = (jnp.dot(p_s[:, 0:P], vsel, preferred_element_type=f32)
                   + jnp.dot(p_s[:, P:P + 128], vnew, preferred_element_type=f32))
            s_w = jnp.concatenate([lax.dot_general(qr, kwin, NT, preferred_element_type=f32),
                                   lax.dot_general(qr, kwnew, NT, preferred_element_type=f32)], axis=1)
            kwpos = jnp.concatenate([P - WB + lax.broadcasted_iota(jnp.int32, (1, WB), 1), P + new_lane], axis=1)
            validw = jnp.concatenate([jnp.full((1, WB), True), new_lane < SEG], axis=1)
            dpos = qpos_col - kwpos
            p_w = _softmax_rows(s_w, (dpos >= 0) & (dpos < WINDOW) & (kwpos >= 0) & validw).astype(bf16)
            o_w = (jnp.dot(p_w[:, 0:WB], vwin, preferred_element_type=f32)
                   + jnp.dot(p_w[:, WB:WB + 128], vwnew, preferred_element_type=f32))
            for r in range(R):
                h = g * R + r
                rr = slice(r * SEG, (r + 1) * SEG)
                c0 = 8 + 3 * h
                gs = gsig[rs8]
                out = gs[:, c0:c0 + 1] * o_c[rr] + gs[:, c0 + 1:c0 + 2] * o_s[rr] + gs[:, c0 + 2:c0 + 3] * o_w[rr]
                if g:
                    out = pltpu.roll(out, 128 - g * HEAD_DIM, axis=1)
                heads[h] = out
        out_rows.append(jnp.concatenate(heads, axis=1))
        for kv, new_f in ((0, kwnew_f), (1, vwnew_f)):
            old = pltpu.roll(win_ref[0, bi, kv], WB - ts, axis=1)
            new_t = pltpu.roll(new_f.T, 128 - ts, axis=1)
            wout_ref[bi, kv, :, 0:WB - 128] = old[:, 0:WB - 128]
            wout_ref[bi, kv, :, WB - 128:WB] = jnp.where(new_lane >= 128 - ts, new_t, old[:, WB - 128:WB])
    hb_ref[...] = jnp.concatenate(out_rows, axis=0).astype(bf16)


def _nsa_sample(page_table, cache_t, cwin_t, qc, qr, sm, kvrow, winrow, cwk, cwv, *, layer, row0, ts):
    nseq, n_pages = page_table.shape
    P = n_pages * PAGE
    WB = cwin_t.shape[4]
    n_chunk = P // CMP_STRIDE
    ovt = _sel_overlap_t(n_chunk, -(-(P + ts) // SEL_BLOCK))
    e = _block_expand(P + 128)
    b0 = row0 // (GB * SEG)

    def page_map(i, pt, *, bi, p):
        return (layer, pt[i * GB + bi, p], 0, 0, 0)

    page_specs = [pl.BlockSpec((1, 1, 4, 128, PAGE), functools.partial(page_map, bi=bi, p=p))
                  for bi in range(GB) for p in range(n_pages)]
    row = lambda w: pl.BlockSpec((GB * SEG, w), lambda i, pt: (b0 + i, 0))
    const = lambda a: pl.BlockSpec(a.shape, lambda i, pt: (0,) * a.ndim)
    grid_spec = pltpu.PrefetchScalarGridSpec(
        num_scalar_prefetch=1, grid=(nseq // GB,),
        in_specs=page_specs + [pl.BlockSpec((1, GB, 2, 128, WB), lambda i, pt: (layer, i, 0, 0, 0)),
                               row(1024), row(1024), row(128), row(512), row(256)]
        + [const(a) for a in (*cwk, *cwv, ovt, e)],
        out_specs=[pl.BlockSpec((GB * SEG, 1024), lambda i, pt: (i, 0)),
                   pl.BlockSpec((GB, 2, 128, WB), lambda i, pt: (i, 0, 0, 0))],
        scratch_shapes=[pltpu.VMEM((GB * P, 128), f32), pltpu.VMEM((GB * P, 128), f32),
                        pltpu.VMEM((GB * P, 128), bf16), pltpu.VMEM((GB * P, 128), bf16),
                        pltpu.VMEM((NBLK_PAD, 128), f32)])
    return pl.pallas_call(
        functools.partial(_nsa_sample_body, n_pages=n_pages, ts=ts), grid_spec=grid_spec,
        out_shape=[jax.ShapeDtypeStruct((nseq * SEG, 1024), bf16), jax.ShapeDtypeStruct((nseq, 2, 128, WB), f32)],
        compiler_params=pltpu.CompilerParams(dimension_semantics=("parallel",), vmem_limit_bytes=VMEM_LIMIT),
        name="nsa_sample",
    )(page_table, *([cache_t] * (GB * n_pages)), cwin_t, qc, qr, sm, kvrow, winrow, *cwk, *cwv, ovt, e)


def kernel(x_prompt, x_sample, cache_kv_pages, page_table, cache_win, state_mlstm_C, state_mlstm_n, state_mlstm_m, state_mlstm_conv, norm1_g, w_in, b_gates, conv_w, conv_b, w_mq, w_mk, mlstm_norm_g, cmp_pe_k, cmp_pe_v, cmp_w1_k, cmp_w2_k, cmp_w1_v, cmp_w2_v, w_branch_a, w_branch_b, w_out, norm2_g, w_ffn_up, w_ffn_down, final_norm_g):
    Bp, Tp, _ = x_prompt.shape
    Bs, Ts, _ = x_sample.shape
    depth = w_in.shape[0]
    n_pool, page = cache_kv_pages.shape[1:3]
    n_pages = page_table.shape[1]
    past_len = n_pages * page
    wb_len = cache_win.shape[2]
    mp = Bp * Tp
    ms = Bs * SAMPLE_PAD
    assert page == PAGE and M_CONV - 1 <= Ts <= SAMPLE_PAD and Tp % MLSTM_CHUNK == 0 and Tp >= WINDOW + TQ
    assert mp % (SEG * SEQ_PER_STEP) == 0 and Bs % SEQ_PER_STEP == 0 and Bs % GB == 0

    pad_seq = lambda a: jnp.pad(a, ((0, 0), (0, SAMPLE_PAD - a.shape[1]), (0, 0)))
    x_all = jnp.concatenate([x_prompt.reshape(mp, D_MODEL), pad_seq(x_sample).reshape(ms, D_MODEL)], axis=0)
    tabs = _rope_tables(Tp, past_len, ROW_TILE * (ms // ROW_TILE))
    col_idx, col_scale = _in_proj_columns()
    wb_rows = np.asarray([h * 64 + d if d < 64 else N_WIDTH for h in range(N_HEADS) for d in range(128)], np.int32)
    cache_t = jnp.transpose(cache_kv_pages, (0, 1, 3, 4, 5, 2)).reshape(depth, n_pool, 4, KV_W, PAGE)
    cwin_t = jnp.transpose(cache_win, (0, 1, 3, 4, 5, 2)).reshape(depth, Bs, 2, KV_W, wb_len)

    outs = {k: [] for k in ("p_kv", "p_win", "p_C", "p_n", "p_m", "p_conv", "s_kv", "s_win", "s_C", "s_n", "s_m", "s_conv")}
    y_all = None
    for l in range(depth):
        w_ext = jnp.concatenate([w_in[l], jnp.zeros((D_MODEL, 1), f32)], axis=1)
        w_all = (w_ext[:, col_idx] * col_scale).astype(bf16)
        w_t = w_in[l][:, O_IG:O_IG + 8].T.astype(bf16)
        xm, sm, smt, qc, qr, kvrow, winrow, att, gates = _in_proj(
            x_all, norm1_g[l].reshape(1, D_MODEL), w_all, w_t, tabs, Tp)

        wq, wk = w_mq[l].astype(bf16), w_mk[l].astype(bf16)
        ha_p, c_p, n_p, m_p, cv_p = _mlstm_prompt(xm, sm, smt, conv_w[l], conv_b[l], wq, wk, b_gates[l], mlstm_norm_g[l],
                                                   nb=Bp, T=Tp, L=MLSTM_CHUNK)
        tail = jnp.pad(state_mlstm_conv[l], ((0, 0), (0, SEG - (M_CONV - 1)), (0, 0))).reshape(ms, M_WIDTH)
        n0 = jnp.pad(state_mlstm_n[l], ((0, 0), (0, 8 - M_HEADS), (0, 0)))
        m0 = jnp.broadcast_to(jnp.pad(state_mlstm_m[l], ((0, 0), (0, 128 - M_HEADS)))[:, None, :], (Bs, SEG, 128))
        ha_s, c_s, n_s, mt_s = _mlstm_sample(xm, sm, smt, tail, state_mlstm_C[l], n0, m0.reshape(ms, 128), conv_w[l],
                                             conv_b[l], wq, wk, b_gates[l], mlstm_norm_g[l], row0=mp, nseq=Bs, n_valid=Ts)

        kc, vc = _compress_prompt(kvrow, _compress_weights(cmp_pe_k[l], cmp_w1_k[l], cmp_w2_k[l]),
                                  _compress_weights(cmp_pe_v[l], cmp_w1_v[l], cmp_w2_v[l]), nb=Bp, T=Tp)
        hb_p = _nsa_prompt(qc, qr, sm, att, kc, vc, nb=Bp, T=Tp)
        hb_s, win_s = _nsa_sample(page_table, cache_t, cwin_t, qc, qr, sm, kvrow, winrow,
                                  _compress_weights(cmp_pe_k[l], cmp_w1_k[l], cmp_w2_k[l], padded=False),
                                  _compress_weights(cmp_pe_v[l], cmp_w1_v[l], cmp_w2_v[l], padded=False),
                                  layer=l, row0=mp, ts=Ts)

        h_a = jnp.concatenate([ha_p, ha_s], axis=0)
        h_b = jnp.concatenate([hb_p, hb_s], axis=0)
        wb_ext = jnp.concatenate([w_branch_b[l], jnp.zeros((1, D_MODEL), f32)], axis=0)
        x_all, y_all = _mix_ffn(
            x_all, h_a, h_b, gates, w_branch_a[l].astype(bf16), wb_ext[wb_rows].astype(bf16), w_out[l].astype(bf16),
            norm2_g[l].reshape(1, D_MODEL), w_ffn_up[l].astype(bf16), w_ffn_down[l].astype(bf16),
            final_norm_g.reshape(1, D_MODEL))

        sample_rows = lambda a: a[mp:].reshape(Bs, SAMPLE_PAD, a.shape[1])
        wp = min(WINDOW, Tp)
        outs["p_kv"].append(kvrow[:mp].reshape(Bp, Tp, 4, N_KV, HEAD_DIM))
        outs["p_win"].append(winrow[:mp].reshape(Bp, Tp, 2 * KV_W)[:, -wp:].reshape(Bp, wp, 2, N_KV, HEAD_DIM))
        outs["p_C"].append(c_p)
        outs["p_n"].append(n_p[:, :M_HEADS])
        outs["p_m"].append(m_p[:, 0, :M_HEADS])
        outs["p_conv"].append(cv_p[:, 8 - (M_CONV - 1):])
        outs["s_kv"].append(sample_rows(kvrow)[:, :Ts].reshape(Bs, Ts, 4, N_KV, HEAD_DIM))
        outs["s_win"].append(jnp.transpose(win_s.reshape(Bs, 2, N_KV, HEAD_DIM, wb_len), (0, 4, 1, 2, 3)))
        outs["s_C"].append(c_s)
        outs["s_n"].append(n_s[:, :M_HEADS])
        outs["s_m"].append(mt_s.reshape(Bs, SEG, 128)[:, SEG - 1, :M_HEADS])
        outs["s_conv"].append(sample_rows(xm)[:, Ts - (M_CONV - 1):Ts, :M_WIDTH])

    y_prompt = y_all[:mp].reshape(Bp, Tp, D_MODEL)
    y_sample = y_all[mp:].reshape(Bs, SAMPLE_PAD, D_MODEL)[:, :Ts]
    st = lambda k: jnp.stack(outs[k])
    return (y_prompt, y_sample, st("p_kv"), st("p_win"), st("p_C"), st("p_n"), st("p_m"), st("p_conv"),
            st("s_kv"), st("s_win"), st("s_C"), st("s_n"), st("s_m"), st("s_conv"))
```

```python
import functools

import jax
import jax.numpy as jnp
import numpy as np
from jax import lax
from jax.experimental import pallas as pl
from jax.experimental.pallas import tpu as pltpu

f32 = jnp.float32
bf16 = jnp.bfloat16

D_MODEL = 1024
M_HEADS = 4
M_WIDTH = 512
M_HEAD_DIM = 128
M_CONV = 4
MLSTM_CHUNK = 256
HEAD_DIM = 64
N_WIDTH = 512
N_HEADS = 8
N_KV = 2
KV_W = 128
CMP_STRIDE = 16
CMP_LEN = 32
SEL_BLOCK = 64
SEL_TOP = 16
WINDOW = 512
ROT_DIM = 16
ROPE_THETA = 500000.0
D_FF = 2816
EPS = 1e-6
NEG = -1e30
TINY = 1e-30
FORCE_SCORE = 1e9

SAMPLE_PAD = 8
SEG = SAMPLE_PAD
SEQ_PER_STEP = 16
GB = 2
PAGE = 128
HD = M_HEAD_DIM
R = N_HEADS // N_KV
CMP_HID = 128
TQ = 128
TK = 512
NBLK_PAD = 64
HIGHEST = lax.Precision.HIGHEST
NT = (((1,), (1,)), ((), ()))
LOG2E = 1.4426950408889634
ROW_TILE = 256
FF_CHUNK = 256
VMEM_LIMIT = 56 * 1024 * 1024

C_XM, C_SM, C_Q, C_KV, C_WIN, C_ATT, C_GATE = 0, 1536, 1664, 2688, 3200, 3456, 4480
C_TOTAL = 6528
O_U, O_V, O_O, O_IG, O_FG, O_Q, O_KV, O_NG, O_GA, O_GB = 0, 512, 1024, 1536, 1540, 1544, 2056, 2824, 2848, 3872
IN_WIDTH = 4896


def _in_proj_columns():
    z = IN_WIDTH
    idx = list(range(O_U, O_IG))
    idx += list(range(O_IG, O_IG + 8)) + list(range(O_NG, O_NG + 24)) + [z] * 96
    for h in range(N_HEADS):
        idx += list(range(O_Q + h * 64, O_Q + (h + 1) * 64)) + [z] * 64
    idx += list(range(O_KV, O_KV + 512))
    idx += list(range(O_KV + 512, O_KV + 768))
    for c in (2, 3, 4, 5):
        for g in range(N_KV):
            s = O_KV + c * 128 + g * 64
            idx += list(range(s, s + 64)) + [z] * 64
    idx += list(range(O_GA, O_GA + 2048))
    assert len(idx) == C_TOTAL
    scale = np.ones((C_TOTAL,), np.float32)
    scale[C_Q:C_KV] = HEAD_DIM ** -0.5
    return np.asarray(idx, np.int32), scale


def _rope_tables(seq, past_len, n_sample_rows):
    half = ROT_DIM // 2
    inv = ROPE_THETA ** (-jnp.arange(half, dtype=f32) / half)
    pos = jnp.concatenate([jnp.arange(seq), past_len + (jnp.arange(n_sample_rows) % SAMPLE_PAD)]).astype(f32)
    ang = pos[:, None] * inv[None, :]
    cos8, sin8 = jnp.cos(ang), jnp.sin(ang)
    n = pos.shape[0]
    one = jnp.ones((n, 64 - ROT_DIM), f32)
    zero = jnp.zeros((n, 64 - ROT_DIM), f32)
    z8 = jnp.zeros((n, half), f32)
    cos = jnp.concatenate([cos8, cos8, one], axis=1)
    sa = jnp.concatenate([-sin8, z8, zero], axis=1)
    sb = jnp.concatenate([z8, sin8, zero], axis=1)
    tile2 = lambda a: jnp.concatenate([a, a], axis=1)
    return tile2(cos), tile2(sa), tile2(sb)


def _rms(x, g):
    return x * lax.rsqrt(jnp.mean(x * x, axis=-1, keepdims=True) + EPS) * g


def _in_proj_body(x_ref, g_ref, w_ref, wt_ref, cos_ref, sa_ref, sb_ref,
                  xm_ref, sm_ref, smt_ref, qc_ref, qr_ref, qct_ref, qrt_ref, kv_ref, win_ref, kk_ref, vt_ref, gate_ref):
    hb = _rms(x_ref[...], g_ref[...]).astype(bf16)
    cos, sa, sb = cos_ref[...], sa_ref[...], sb_ref[...]
    lane = lax.broadcasted_iota(jnp.int32, cos.shape, 1)

    def rope(v):
        return v * cos + pltpu.roll(v, 128 - ROT_DIM // 2, axis=1) * sa + pltpu.roll(v, ROT_DIM // 2, axis=1) * sb

    def proj(c0, n):
        return jnp.dot(hb, w_ref[:, c0:c0 + n], preferred_element_type=f32)

    xm_ref[...] = proj(C_XM, 1536)
    sm_ref[...] = proj(C_SM, 128)
    smt_ref[...] = lax.dot_general(wt_ref[...], hb, NT, preferred_element_type=f32)
    for h in range(N_HEADS):
        q = proj(C_Q + h * 128, 128) * LOG2E
        qrot = rope(q)
        qc_ref[:, h * 128:(h + 1) * 128] = q.astype(bf16)
        qr_ref[:, h * 128:(h + 1) * 128] = qrot.astype(bf16)
        qct_ref[h * 128:(h + 1) * 128, :] = q.T.astype(bf16)
        qrt_ref[h * 128:(h + 1) * 128, :] = qrot.T.astype(bf16)
    kv = proj(C_KV, 512)
    kv_ref[:, 0:256] = kv[:, 0:256]
    kv_ref[:, 256:384] = rope(kv[:, 256:384])
    kv_ref[:, 384:512] = kv[:, 384:512]
    win = proj(C_WIN, 256)
    win_ref[:, 0:128] = rope(win[:, 0:128])
    win_ref[:, 128:256] = win[:, 128:256]
    for j in range(8):
        a = proj(C_ATT + j * 128, 128)
        slot = (j // 4) * 2 + j % 2
        if j % 4 < 2:
            kk_ref[:, slot * 128:(slot + 1) * 128] = rope(a).astype(bf16)
        else:
            a = jnp.where(lane == 64, 1.0, a)
            vt_ref[slot * 128:(slot + 1) * 128, :] = a.T.astype(bf16)
    gate_ref[...] = proj(C_GATE, 2048)


def _in_proj(x_all, norm_g, w_all, w_t, tabs, seq):
    m = x_all.shape[0]
    n_prompt_tiles_per_seq = seq // ROW_TILE
    n_prompt_tiles = (m - (tabs[0].shape[0] - seq)) // ROW_TILE

    def tab_map(i):
        return (jnp.where(i < n_prompt_tiles, i % n_prompt_tiles_per_seq, n_prompt_tiles_per_seq + i - n_prompt_tiles), 0)

    row = lambda w: pl.BlockSpec((ROW_TILE, w), lambda i: (i, 0))
    const = lambda s: pl.BlockSpec(s, lambda i: (0, 0), pipeline_mode=pl.Buffered(1))
    tab = pl.BlockSpec((ROW_TILE, 128), tab_map)
    widths = (1536, 128, -32, 1024, 1024, -1024, -1024, 512, 256, 512, -512, 2048)
    dts = (f32, f32, f32, bf16, bf16, bf16, bf16, f32, f32, bf16, bf16, f32)
    out_shape = [jax.ShapeDtypeStruct((-w, m) if w < 0 else (m, w), d) for w, d in zip(widths, dts)]
    out_specs = [pl.BlockSpec((-w, ROW_TILE), lambda i: (0, i)) if w < 0 else row(w) for w in widths]
    return pl.pallas_call(
        _in_proj_body, grid=(m // ROW_TILE,),
        in_specs=[row(D_MODEL), const((1, D_MODEL)), const((D_MODEL, C_TOTAL)), const((32, D_MODEL)), tab, tab, tab],
        out_specs=out_specs, out_shape=out_shape, name="in_proj",
        compiler_params=pltpu.CompilerParams(dimension_semantics=("parallel",), vmem_limit_bytes=VMEM_LIMIT),
    )(x_all, norm_g, w_all, w_t, *tabs)


def _mix_ffn_body(x_ref, ha_ref, hb_ref, gate_ref, wa_ref, wb_ref, wo_ref, g2_ref, wup_ref, wdn_ref, gf_ref,
                  xo_ref, y_ref):
    a = jnp.dot(ha_ref[...].astype(bf16), wa_ref[...], preferred_element_type=f32)
    b = lax.dot_general(hb_ref[...], wb_ref[...], (((0,), (0,)), ((), ())), preferred_element_type=f32)
    merged = jax.nn.sigmoid(gate_ref[:, 0:D_MODEL]) * a + jax.nn.sigmoid(gate_ref[:, D_MODEL:2 * D_MODEL]) * b
    x1 = x_ref[...] + jnp.dot(merged.astype(bf16), wo_ref[...], preferred_element_type=f32)
    hn = _rms(x1, g2_ref[...]).astype(bf16)
    acc = x1
    for c in range(D_FF // FF_CHUNK):
        lo = c * FF_CHUNK
        g = jnp.dot(hn, wup_ref[:, lo:lo + FF_CHUNK], preferred_element_type=f32)
        u = jnp.dot(hn, wup_ref[:, D_FF + lo:D_FF + lo + FF_CHUNK], preferred_element_type=f32)
        act = (g * jax.nn.sigmoid(g) * u).astype(bf16)
        acc = acc + jnp.dot(act, wdn_ref[lo:lo + FF_CHUNK, :], preferred_element_type=f32)
    xo_ref[...] = acc
    y_ref[...] = _rms(acc, gf_ref[...])


def _mix_ffn(x_all, h_a, h_b, gates, wa, wb, wo, g2, wup, wdn, gf):
    m = x_all.shape[0]
    row = lambda w: pl.BlockSpec((ROW_TILE, w), lambda i: (i, 0))
    const = lambda s: pl.BlockSpec(s, lambda i: (0, 0), pipeline_mode=pl.Buffered(1))
    return pl.pallas_call(
        _mix_ffn_body, grid=(m // ROW_TILE,),
        in_specs=[row(D_MODEL), row(h_a.shape[1]), pl.BlockSpec((h_b.shape[0], ROW_TILE), lambda i: (0, i)),
                  row(2 * D_MODEL),
                  const(wa.shape), const(wb.shape), const(wo.shape), const((1, D_MODEL)),
                  const(wup.shape), const(wdn.shape), const((1, D_MODEL))],
        out_specs=[row(D_MODEL), row(D_MODEL)],
        out_shape=[jax.ShapeDtypeStruct((m, D_MODEL), f32)] * 2, name="mix_ffn",
        compiler_params=pltpu.CompilerParams(dimension_semantics=("parallel",), vmem_limit_bytes=VMEM_LIMIT),
    )(x_all, h_a, h_b, gates, wa, wb, wo, g2, wup, wdn, gf)


def _log_sigmoid(x):
    return jnp.minimum(x, 0.0) - jnp.log1p(jnp.exp(-jnp.abs(x)))


def _seg_scan(x, axis, seg, op, fill):
    idx = lax.broadcasted_iota(jnp.int32, x.shape, axis) % seg
    d = 1
    while d < seg:
        x = op(x, jnp.where(idx >= d, pltpu.roll(x, d, axis=axis), fill))
        d *= 2
    return x


def _conv_silu(u, tail, conv_w, conv_b, seg):
    L = u.shape[0]
    row = lax.broadcasted_iota(jnp.int32, u.shape, 0) % seg
    y = conv_b + u * conv_w[3:4, :]
    for k in (1, 2, 3):
        prev = tail if k == 3 else pltpu.roll(tail, L - (3 - k), axis=0)
        sh = jnp.where(row < k, prev, pltpu.roll(u, k, axis=0))
        y = y + sh * conv_w[3 - k:4 - k, :]
    return y * jax.nn.sigmoid(y)


def _gate_columns(sm, smt, bg_row, bg_col, m_vec, seg, n_valid):
    G = sm + bg_row
    lf = _log_sigmoid(G)
    ipre = G
    if n_valid < seg:
        rowc = lax.broadcasted_iota(jnp.int32, G.shape, 0) % seg
        lf = jnp.where(rowc < n_valid, lf, 0.0)
        ipre = jnp.where(rowc < n_valid, G, NEG)
    F = _seg_scan(lf, 0, seg, jnp.add, 0.0)
    F_al = pltpu.roll(F, 128 - M_HEADS, axis=1)
    a_col = ipre - F_al
    g_col = jnp.maximum(m_vec, _seg_scan(a_col, 0, seg, jnp.maximum, NEG))
    mt_col = F_al + g_col
    Gr = smt + bg_col
    lfr = _log_sigmoid(Gr)
    ir = Gr
    if n_valid < seg:
        lanec = lax.broadcasted_iota(jnp.int32, Gr.shape, 1) % seg
        lfr = jnp.where(lanec < n_valid, lfr, 0.0)
        ir = jnp.where(lanec < n_valid, Gr, NEG)
    Fr = _seg_scan(lfr, 1, seg, jnp.add, 0.0)
    a_row = ir - pltpu.roll(Fr, M_HEADS, axis=0)
    return a_col, g_col, mt_col, a_row


def _head_norm_gate(hh, ng, o):
    return hh * lax.rsqrt(jnp.mean(hh * hh, axis=-1, keepdims=True) + EPS) * ng * jax.nn.sigmoid(o)


def _mlstm_prompt_body(xm_ref, sm_ref, smt_ref, cw_ref, cb_ref, wq_ref, wk_ref, bgr_ref, bgc_ref, ng_ref,
                       ha_ref, c_out, n_out, m_out, conv_out, c_scr, n_scr, m_scr, tail_scr, *, L):
    c_idx = pl.program_id(1)

    @pl.when(c_idx == 0)
    def _():
        c_scr[...] = jnp.zeros_like(c_scr)
        n_scr[...] = jnp.zeros_like(n_scr)
        m_scr[...] = jnp.zeros_like(m_scr)
        tail_scr[...] = jnp.zeros_like(tail_scr)

    u = xm_ref[:, 0:M_WIDTH]
    tail = jnp.concatenate([tail_scr[...], jnp.zeros((L - 8, M_WIDTH), f32)], axis=0)
    cact = _conv_silu(u, tail, cw_ref[...], cb_ref[...], L)
    tail_scr[...] = pltpu.roll(u[L - 8:L, :], 3, axis=0)
    conv_out[0] = u[L - 8:L, :]

    m_vec = m_scr[0:1, :]
    a_col, g_col, mt_col, a_row = _gate_columns(sm_ref[...], smt_ref[...], bgr_ref[...], bgc_ref[...], m_vec, L, L)
    inter_col = jnp.exp(m_vec - g_col)
    floor_col = jnp.exp(-mt_col)
    g_last = g_col[L - 1:L, :]
    w_col = jnp.exp(a_col - g_last)
    decay = jnp.exp(m_vec - g_last)
    m_new = mt_col[L - 1:L, :]
    ti = lax.broadcasted_iota(jnp.int32, (L, L), 0)
    si = lax.broadcasted_iota(jnp.int32, (L, L), 1)
    causal = si <= ti
    scale = HD ** -0.5
    for h in range(M_HEADS):
        sl = slice(h * HD, (h + 1) * HD)
        ch = cact[:, sl].astype(bf16)
        qh = jnp.dot(ch, wq_ref[h], preferred_element_type=f32)
        kh = jnp.dot(ch, wk_ref[h], preferred_element_type=f32) * scale
        vh = xm_ref[:, M_WIDTH + h * HD:M_WIDTH + (h + 1) * HD].astype(bf16)
        qb = qh.astype(bf16)
        D = jnp.exp(jnp.where(causal, a_row[h:h + 1, :] - g_col[:, h:h + 1], NEG))
        S = lax.dot_general(qb, kh.astype(bf16), NT, preferred_element_type=f32) * D
        C = c_scr[h]
        nrow = n_scr[h:h + 1, :]
        ic = inter_col[:, h:h + 1]
        num = jnp.dot(S.astype(bf16), vh, preferred_element_type=f32) + ic * jnp.dot(
            qb, C.astype(bf16), preferred_element_type=f32)
        den = jnp.sum(S, axis=-1, keepdims=True) + ic * jnp.sum(qh * nrow, axis=-1, keepdims=True)
        hh = num / jnp.maximum(jnp.abs(den), floor_col[:, h:h + 1])
        o = xm_ref[:, 2 * M_WIDTH + h * HD:2 * M_WIDTH + (h + 1) * HD]
        ha_ref[:, sl] = _head_norm_gate(hh, ng_ref[:, sl], o)
        kw = kh * w_col[:, h:h + 1]
        dh = decay[:, h:h + 1]
        c_scr[h] = dh * C + lax.dot_general(kw.astype(bf16), vh, (((0,), (0,)), ((), ())), preferred_element_type=f32)
        n_scr[h:h + 1, :] = dh * nrow + jnp.sum(kw, axis=0, keepdims=True)
    m_scr[0:1, :] = m_new

    @pl.when(c_idx == pl.num_programs(1) - 1)
    def _():
        c_out[0] = c_scr[...]
        n_out[0] = n_scr[...]
        m_out[0] = m_scr[...]


def _mlstm_prompt(xm, sm, smt, conv_w, conv_b, wq, wk, b_gates, norm_g, *, nb, T, L):
    nc = T // L
    bg_row = jnp.zeros((1, 128), f32).at[0, :8].set(b_gates)
    bg_col = b_gates.reshape(8, 1)
    const = lambda s: pl.BlockSpec(s, lambda b, c: (0,) * len(s))
    return pl.pallas_call(
        functools.partial(_mlstm_prompt_body, L=L), grid=(nb, nc),
        in_specs=[pl.BlockSpec((L, 3 * M_WIDTH), lambda b, c: (b * nc + c, 0)),
                  pl.BlockSpec((L, 128), lambda b, c: (b * nc + c, 0)),
                  pl.BlockSpec((8, L), lambda b, c: (0, b * nc + c)),
                  const((4, M_WIDTH)), const((1, M_WIDTH)), const((M_HEADS, HD, HD)), const((M_HEADS, HD, HD)),
                  const((1, 128)), const((8, 1)), const((1, M_WIDTH))],
        out_specs=[pl.BlockSpec((L, M_WIDTH), lambda b, c: (b * nc + c, 0)),
                   pl.BlockSpec((1, M_HEADS, HD, HD), lambda b, c: (b, 0, 0, 0)),
                   pl.BlockSpec((1, 8, HD), lambda b, c: (b, 0, 0)),
                   pl.BlockSpec((1, 8, 128), lambda b, c: (b, 0, 0)),
                   pl.BlockSpec((1, 8, M_WIDTH), lambda b, c: (b, 0, 0))],
        out_shape=[jax.ShapeDtypeStruct((nb * T, M_WIDTH), f32),
                   jax.ShapeDtypeStruct((nb, M_HEADS, HD, HD), f32),
                   jax.ShapeDtypeStruct((nb, 8, HD), f32),
                   jax.ShapeDtypeStruct((nb, 8, 128), f32),
                   jax.ShapeDtypeStruct((nb, 8, M_WIDTH), f32)],
        scratch_shapes=[pltpu.VMEM((M_HEADS, HD, HD), f32), pltpu.VMEM((8, HD), f32), pltpu.VMEM((8, 128), f32),
                        pltpu.VMEM((8, M_WIDTH), f32)],
        compiler_params=pltpu.CompilerParams(dimension_semantics=("parallel", "arbitrary")),
        name="mlstm_prompt",
    )(xm, sm, smt, conv_w, conv_b.reshape(1, M_WIDTH), wq, wk, bg_row, bg_col, norm_g.reshape(1, M_WIDTH))


def _mlstm_sample_body(xm_ref, sm_ref, smt_ref, tail_ref, c_in, n_in, m_in, cw_ref, cb_ref, wq_ref, wk_ref,
                       bgr_ref, bgc_ref, ng_ref, ha_ref, c_out, n_out, m_out, *, n_valid):
    L = SEG * SEQ_PER_STEP
    u = xm_ref[:, 0:M_WIDTH]
    cact = _conv_silu(u, tail_ref[...], cw_ref[...], cb_ref[...], SEG)
    m_rows = m_in[...]
    a_col, g_col, mt_col, a_row = _gate_columns(sm_ref[...], smt_ref[...], bgr_ref[...], bgc_ref[...], m_rows, SEG,
                                                n_valid)
    inter_col = jnp.exp(m_rows - g_col)
    floor_col = jnp.exp(-mt_col)
    ti = lax.broadcasted_iota(jnp.int32, (L, L), 0)
    si = lax.broadcasted_iota(jnp.int32, (L, L), 1)
    mask = (si <= ti) & ((si // SEG) == (ti // SEG))
    scale = HD ** -0.5
    for h in range(M_HEADS):
        sl = slice(h * HD, (h + 1) * HD)
        ch = cact[:, sl].astype(bf16)
        qh = jnp.dot(ch, wq_ref[h], preferred_element_type=f32)
        kh = jnp.dot(ch, wk_ref[h], preferred_element_type=f32) * scale
        vh = xm_ref[:, M_WIDTH + h * HD:M_WIDTH + (h + 1) * HD].astype(bf16)
        qb = qh.astype(bf16)
        D = jnp.exp(jnp.where(mask, a_row[h:h + 1, :] - g_col[:, h:h + 1], NEG))
        S = lax.dot_general(qb, kh.astype(bf16), NT, preferred_element_type=f32) * D
        num = jnp.dot(S.astype(bf16), vh, preferred_element_type=f32)
        den = jnp.sum(S, axis=-1, keepdims=True)
        inter_num, inter_den = [], []
        for s in range(SEQ_PER_STEP):
            rs = slice(s * SEG, (s + 1) * SEG)
            C = c_in[s, h]
            nrow = n_in[s, h:h + 1, :]
            inter_num.append(jnp.dot(qb[rs], C.astype(bf16), preferred_element_type=f32))
            inter_den.append(jnp.sum(qh[rs] * nrow, axis=-1, keepdims=True))
            g_last = g_col[s * SEG + SEG - 1:s * SEG + SEG, h:h + 1]
            m_prev = m_rows[s * SEG:s * SEG + 1, h:h + 1]
            w = jnp.exp(a_col[rs, h:h + 1] - g_last)
            dh = jnp.exp(m_prev - g_last)
            kw = kh[rs] * w
            c_out[s, h] = dh * C + lax.dot_general(kw.astype(bf16), vh[rs], (((0,), (0,)), ((), ())),
                                                    preferred_element_type=f32)
            n_out[s, h:h + 1, :] = dh * nrow + jnp.sum(kw, axis=0, keepdims=True)
        ic = inter_col[:, h:h + 1]
        num = num + ic * jnp.concatenate(inter_num, axis=0)
        den = den + ic * jnp.concatenate(inter_den, axis=0)
        hh = num / jnp.maximum(jnp.abs(den), floor_col[:, h:h + 1])
        o = xm_ref[:, 2 * M_WIDTH + h * HD:2 * M_WIDTH + (h + 1) * HD]
        ha_ref[:, sl] = _head_norm_gate(hh, ng_ref[:, sl], o)
    for s in range(SEQ_PER_STEP):
        n_out[s, M_HEADS:8, :] = jnp.zeros((8 - M_HEADS, HD), f32)
    m_out[...] = mt_col


def _mlstm_sample(xm, sm, smt, tail, c0, n0, m0, conv_w, conv_b, wq, wk, b_gates, norm_g, *, row0, nseq, n_valid):
    L = SEG * SEQ_PER_STEP
    b0 = row0 // L
    bg_row = jnp.zeros((1, 128), f32).at[0, :8].set(b_gates)
    bg_col = b_gates.reshape(8, 1)
    const = lambda s: pl.BlockSpec(s, lambda i: (0,) * len(s))
    return pl.pallas_call(
        functools.partial(_mlstm_sample_body, n_valid=n_valid), grid=(nseq // SEQ_PER_STEP,),
        in_specs=[pl.BlockSpec((L, 3 * M_WIDTH), lambda i: (b0 + i, 0)),
                  pl.BlockSpec((L, 128), lambda i: (b0 + i, 0)),
                  pl.BlockSpec((8, L), lambda i: (0, b0 + i)),
                  pl.BlockSpec((L, M_WIDTH), lambda i: (i, 0)),
                  pl.BlockSpec((SEQ_PER_STEP, M_HEADS, HD, HD), lambda i: (i, 0, 0, 0)),
                  pl.BlockSpec((SEQ_PER_STEP, 8, HD), lambda i: (i, 0, 0)),
                  pl.BlockSpec((L, 128), lambda i: (i, 0)),
                  const((4, M_WIDTH)), const((1, M_WIDTH)), const((M_HEADS, HD, HD)), const((M_HEADS, HD, HD)),
                  const((1, 128)), const((8, 1)), const((1, M_WIDTH))],
        out_specs=[pl.BlockSpec((L, M_WIDTH), lambda i: (i, 0)),
                   pl.BlockSpec((SEQ_PER_STEP, M_HEADS, HD, HD), lambda i: (i, 0, 0, 0)),
                   pl.BlockSpec((SEQ_PER_STEP, 8, HD), lambda i: (i, 0, 0)),
                   pl.BlockSpec((L, 128), lambda i: (i, 0))],
        out_shape=[jax.ShapeDtypeStruct((nseq * SEG, M_WIDTH), f32),
                   jax.ShapeDtypeStruct((nseq, M_HEADS, HD, HD), f32),
                   jax.ShapeDtypeStruct((nseq, 8, HD), f32),
                   jax.ShapeDtypeStruct((nseq * SEG, 128), f32)],
        compiler_params=pltpu.CompilerParams(dimension_semantics=("parallel",), vmem_limit_bytes=VMEM_LIMIT),
        name="mlstm_sample",
    )(xm, sm, smt, tail, c0, n0, m0, conv_w, conv_b.reshape(1, M_WIDTH), wq, wk, bg_row, bg_col,
      norm_g.reshape(1, M_WIDTH))


def _sel_overlap_t(n_chunk, n_blk):
    n_cmp = n_chunk - 1
    start = np.arange(n_cmp) * CMP_STRIDE
    bs = np.arange(n_blk) * SEL_BLOCK
    ov = np.minimum(start[:, None] + CMP_LEN, bs[None, :] + SEL_BLOCK) - np.maximum(start[:, None], bs[None, :])
    ov = np.clip(ov, 0, None) / CMP_LEN
    out = np.zeros((NBLK_PAD, n_chunk), np.float32)
    out[:n_blk, :n_cmp] = ov.T
    return jnp.asarray(out)


def _block_expand(n_keys):
    e = (np.arange(n_keys)[None, :] // SEL_BLOCK) == np.arange(128)[:, None]
    return jnp.asarray(e, dtype=bf16)


def _compress_weights(pe, w1, w2, padded=True):
    W = jnp.zeros((16, 2, 64, 2, 2, CMP_HID), f32)
    for g in range(N_KV):
        W = W.at[:, g, :, g, 0, :].set(w1[:16])
        W = W.at[:, g, :, g, 1, :].set(w1[16:])
    W = W.reshape(2048, 512)
    peA = jnp.broadcast_to(pe[:16, None, :], (16, 2, 64)).reshape(1, 2048)
    peB = jnp.broadcast_to(pe[16:, None, :], (16, 2, 64)).reshape(1, 2048)
    PE = jnp.concatenate([peA, peB, jnp.zeros((6, 2048), f32)], axis=0)
    if padded:
        W2 = jnp.zeros((2, CMP_HID, 256), f32).at[0, :, 0:64].set(w2).at[1, :, 128:192].set(w2)
    else:
        W2 = jnp.zeros((2, CMP_HID, 128), f32).at[0, :, 0:64].set(w2).at[1, :, 64:128].set(w2)
    return W.astype(bf16), PE.astype(bf16), W2.astype(bf16)


def _compress_rows(flat, w_ref, pe_ref, w2_ref):
    n = flat.shape[0]
    y = jnp.dot(flat, w_ref[...], preferred_element_type=f32)
    c = jnp.dot(pe_ref[...], w_ref[...], preferred_element_type=f32)
    out = jnp.zeros((n, w2_ref.shape[2]), f32)
    for g in range(N_KV):
        a = y[:, (2 * g) * CMP_HID:(2 * g + 1) * CMP_HID] + c[0:1, (2 * g) * CMP_HID:(2 * g + 1) * CMP_HID]
        b = y[:, (2 * g + 1) * CMP_HID:(2 * g + 2) * CMP_HID] + c[1:2, (2 * g + 1) * CMP_HID:(2 * g + 2) * CMP_HID]
        hid = a + pltpu.roll(b, n - 1, axis=0)
        out = out + jnp.dot(jax.nn.gelu(hid, approximate=True).astype(bf16), w2_ref[g], preferred_element_type=f32)
    return out


def _compress_prompt_body(xk_ref, xv_ref, wk_ref, pek_ref, w2k_ref, wv_ref, pev_ref, w2v_ref, kc_ref, vc_ref, *,
                          n_chunk):
    def flat(x_ref):
        return jnp.concatenate(
            [x_ref[pl.ds(l, n_chunk, stride=CMP_STRIDE), :].astype(bf16) for l in range(CMP_STRIDE)], axis=1)
    kc_ref[0] = _compress_rows(flat(xk_ref), wk_ref, pek_ref, w2k_ref).astype(bf16)
    vc_ref[0] = _compress_rows(flat(xv_ref), wv_ref, pev_ref, w2v_ref).T.astype(bf16)


def _compress_prompt(kvrow, cwk, cwv, *, nb, T):
    n_chunk = T // CMP_STRIDE
    const = lambda a: pl.BlockSpec(a.shape, lambda b: (0,) * a.ndim)
    return pl.pallas_call(
        functools.partial(_compress_prompt_body, n_chunk=n_chunk), grid=(nb,),
        in_specs=[pl.BlockSpec((T, 128), lambda b: (b, 0)), pl.BlockSpec((T, 128), lambda b: (b, 1))]
        + [const(a) for a in (*cwk, *cwv)],
        out_specs=[pl.BlockSpec((1, n_chunk, 256), lambda b: (b, 0, 0)),
                   pl.BlockSpec((1, 256, n_chunk), lambda b: (b, 0, 0))],
        out_shape=[jax.ShapeDtypeStruct((nb, n_chunk, 256), bf16), jax.ShapeDtypeStruct((nb, 256, n_chunk), bf16)],
        compiler_params=pltpu.CompilerParams(dimension_semantics=("parallel",), vmem_limit_bytes=VMEM_LIMIT),
        name="nsa_compress",
    )(kvrow, kvrow, *cwk, *cwv)


def _softmax_rows(s, mask):
    s = jnp.where(mask, s, NEG)
    e = jnp.exp2(s - jnp.max(s, axis=-1, keepdims=True))
    e = jnp.where(mask, e, 0.0)
    return e / jnp.maximum(jnp.sum(e, axis=-1, keepdims=True), TINY)


def _select_blocks(imp_t, qpos_row, n_blk, score_scr):
    shape = imp_t.shape
    j = lax.broadcasted_iota(jnp.int32, shape, 0)
    cur = qpos_row // SEL_BLOCK
    forced = (j == 0) | (j == cur) | (j == cur - 1)
    score = jnp.where(j <= cur, jnp.where(forced, FORCE_SCORE, imp_t), -1.0)
    score = jnp.where(j < n_blk, score, -2.0)
    score_scr[...] = score

    def body(jp, cnt):
        sj = score_scr[pl.ds(jp, 1), :]
        return cnt + jnp.where(j > jp, jnp.where(sj >= score, 1.0, 0.0), jnp.where(sj > score, 1.0, 0.0))

    cnt = lax.fori_loop(0, n_blk, body, jnp.zeros(shape, f32), unroll=8 if n_blk % 8 == 0 else 1)
    return jnp.where(cnt < float(min(SEL_TOP, n_blk)), 1.0, 0.0)


def _transpose_sel(sel_t):
    return jnp.concatenate([sel_t, jnp.zeros((128 - NBLK_PAD, sel_t.shape[1]), f32)], axis=0).T


def _softmax_cols(s, mask):
    s = jnp.where(mask, s, NEG)
    e = jnp.exp2(s - jnp.max(s, axis=0, keepdims=True))
    e = jnp.where(mask, e, 0.0)
    return e * (1.0 / jnp.maximum(jnp.sum(e, axis=0, keepdims=True), TINY))


def _lane_tile(x, n):
    return jnp.concatenate([x] * n, axis=1)


def _nsa_prompt_body(qct_ref, qrt_ref, gt_ref, kk_ref, vt_ref, kc_ref, vct_ref, ovt_ref, hbt_ref, sel_scr, score_scr,
                     *, T):
    i = pl.program_id(1)
    n_chunk = T // CMP_STRIDE
    n_blk = T // SEL_BLOCK
    q0 = i * TQ
    qpos_row = q0 + lax.broadcasted_iota(jnp.int32, (1, TQ), 1)
    gsig = jax.nn.sigmoid(gt_ref[...])
    n_kt = (q0 + TQ + TK - 1) // TK
    w0 = jnp.maximum(i - WINDOW // TQ, 0) * TQ
    WK = WINDOW + TQ
    blk_per_tile = TK // SEL_BLOCK
    qrts, o_cs = [], []
    for g in range(N_KV):
        qct = jnp.concatenate([qct_ref[(g * R + r) * 128:(g * R + r + 1) * 128, :] for r in range(R)], axis=1)
        qrts.append(jnp.concatenate([qrt_ref[(g * R + r) * 128:(g * R + r + 1) * 128, :] for r in range(R)], axis=1))
        s = jnp.dot(kc_ref[0][:, g * 128:(g + 1) * 128], qct, preferred_element_type=f32)
        cmp_end = lax.broadcasted_iota(jnp.int32, (n_chunk, 1), 0) * CMP_STRIDE + (CMP_LEN - 1)
        p = _softmax_cols(s, _lane_tile(cmp_end <= qpos_row, R))
        o_cs.append(jnp.dot(vct_ref[0][g * 128:(g + 1) * 128, :], p.astype(bf16), preferred_element_type=f32))
        psum = p[:, 0:TQ] + p[:, TQ:2 * TQ] + p[:, 2 * TQ:3 * TQ] + p[:, 3 * TQ:4 * TQ]
        imp_t = jnp.dot(ovt_ref[...], psum, preferred_element_type=f32, precision=HIGHEST)
        sel_scr[g] = jnp.where(_select_blocks(imp_t, qpos_row, n_blk, score_scr) > 0.5, 0.0, NEG)

    def kv_step(kt, carry):
        k0 = pl.multiple_of(kt * TK, TK)
        b0 = pl.multiple_of(kt * blk_per_tile, blk_per_tile)
        causal = (k0 + lax.broadcasted_iota(jnp.int32, (TK, 1), 0)) <= qpos_row
        out = []
        for g in range(N_KV):
            m_i, acc = carry[g]
            sc = jnp.dot(kk_ref[pl.ds(k0, TK), g * 128:(g + 1) * 128], qrts[g], preferred_element_type=f32)
            sel8 = sel_scr[g, pl.ds(b0, blk_per_tile), :]
            bias = jnp.broadcast_to(sel8[:, None, :], (blk_per_tile, SEL_BLOCK, TQ)).reshape(TK, TQ)
            sc = sc + _lane_tile(jnp.where(causal, bias, NEG), R)
            m_new = jnp.maximum(m_i, jnp.max(sc, axis=0, keepdims=True))
            alpha = jnp.exp2(m_i - m_new)
            pp = jnp.exp2(sc - m_new)
            acc = alpha * acc + jnp.dot(vt_ref[g * 128:(g + 1) * 128, pl.ds(k0, TK)], pp.astype(bf16),
                                        preferred_element_type=f32)
            out.append((m_new, acc))
        return tuple(out)

    init = (jnp.full((1, R * TQ), NEG, f32), jnp.zeros((128, R * TQ), f32))
    sel_out = lax.fori_loop(0, n_kt, kv_step, (init,) * N_KV)

    for g in range(N_KV):
        qrt, o_c, acc = qrts[g], o_cs[g], sel_out[g][1]
        o_s = acc * (1.0 / acc[64:65, :])
        w0a = pl.multiple_of(w0, TQ)
        sw = jnp.dot(kk_ref[pl.ds(w0a, WK), (2 + g) * 128:(3 + g) * 128], qrt, preferred_element_type=f32)
        dpos = qpos_row - (w0 + lax.broadcasted_iota(jnp.int32, (WK, 1), 0))
        biasw = jnp.where((dpos >= 0) & (dpos < WINDOW), 0.0, NEG)
        sw = sw + _lane_tile(biasw, R)
        pw = jnp.exp2(sw - jnp.max(sw, axis=0, keepdims=True))
        ow = jnp.dot(vt_ref[(2 + g) * 128:(3 + g) * 128, pl.ds(w0a, WK)], pw.astype(bf16), preferred_element_type=f32)
        o_w = ow * (1.0 / ow[64:65, :])
        for r in range(R):
            h = g * R + r
            cs = slice(r * TQ, (r + 1) * TQ)
            c0 = 8 + 3 * h
            out = (gsig[c0:c0 + 1, :] * o_c[0:64, cs] + gsig[c0 + 1:c0 + 2, :] * o_s[0:64, cs]
                   + gsig[c0 + 2:c0 + 3, :] * o_w[0:64, cs])
            hbt_ref[h * 64:(h + 1) * 64, :] = out.astype(bf16)


def _nsa_prompt(qct, qrt, gt, kk, vt, kc, vct, *, nb, T):
    nq = T // TQ
    n_chunk = T // CMP_STRIDE
    ovt = _sel_overlap_t(n_chunk, T // SEL_BLOCK)
    col = lambda h: pl.BlockSpec((h, TQ), lambda b, i: (0, b * nq + i))
    const = lambda a: pl.BlockSpec(a.shape, lambda b, i: (0,) * a.ndim)
    return pl.pallas_call(
        functools.partial(_nsa_prompt_body, T=T), grid=(nb, nq),
        in_specs=[col(1024), col(1024), col(32),
                  pl.BlockSpec((T, 512), lambda b, i: (b, 0)),
                  pl.BlockSpec((512, T), lambda b, i: (0, b)),
                  pl.BlockSpec((1, n_chunk, 256), lambda b, i: (b, 0, 0)),
                  pl.BlockSpec((1, 256, n_chunk), lambda b, i: (b, 0, 0)),
                  const(ovt)],
        out_specs=col(512),
        out_shape=jax.ShapeDtypeStruct((512, nb * T), bf16),
        scratch_shapes=[pltpu.VMEM((N_KV, NBLK_PAD, TQ), f32), pltpu.VMEM((NBLK_PAD, TQ), f32)],
        compiler_params=pltpu.CompilerParams(dimension_semantics=("parallel", "arbitrary"),
                                             vmem_limit_bytes=VMEM_LIMIT),
        name="nsa_prompt",
    )(qct, qrt, gt, kk, vt, kc, vct, ovt)


def _nsa_sample_body(pt_ref, *refs, n_pages, ts):
    n_pg = GB * n_pages
    pages = refs[:n_pg]
    (win_ref, qc_ref, qr_ref, sm_ref, kvn_ref, wn_ref, wk_ref, pek_ref, w2k_ref, wv_ref, pev_ref, w2v_ref,
     ovt_ref, e_ref, hb_ref, wout_ref, xk_scr, xv_scr, ks_scr, vs_scr, score_scr) = refs[n_pg:]
    P = n_pages * PAGE
    n_chunk = P // CMP_STRIDE
    n_blk = -(-(P + ts) // SEL_BLOCK)
    WB = win_ref.shape[4]
    for bi in range(GB):
        for p in range(n_pages):
            pg = pages[bi * n_pages + p]
            prow = pl.ds(bi * P + p * PAGE, PAGE)
            xk_scr[prow, :] = pg[0, 0, 0].T
            xv_scr[prow, :] = pg[0, 0, 1].T
            ks_scr[prow, :] = pg[0, 0, 2].T.astype(bf16)
            vs_scr[prow, :] = pg[0, 0, 3].T.astype(bf16)

    def flat(x_ref):
        return jnp.concatenate(
            [x_ref[pl.ds(l, GB * n_chunk, stride=CMP_STRIDE), :].astype(bf16) for l in range(CMP_STRIDE)], axis=1)

    kc_all = _compress_rows(flat(xk_scr), wk_ref, pek_ref, w2k_ref).astype(bf16)
    vc_all = _compress_rows(flat(xv_scr), wv_ref, pev_ref, w2v_ref).astype(bf16)

    qc_all = qc_ref[...].astype(f32)
    qr_all = qr_ref[...].astype(f32)
    gsig = jax.nn.sigmoid(sm_ref[...])
    rows = R * SEG
    qpos_col = P + lax.broadcasted_iota(jnp.int32, (rows, 1), 0) % SEG
    qpos_row = P + lax.broadcasted_iota(jnp.int32, (1, 128), 1) % SEG
    new_lane = lax.broadcasted_iota(jnp.int32, (1, 128), 1)
    zpad = jnp.zeros((128 - SEG, 128), f32)
    out_rows = []
    for bi in range(GB):
        rs8 = slice(bi * SEG, (bi + 1) * SEG)
        kc = kc_all[bi * n_chunk:(bi + 1) * n_chunk]
        vc = vc_all[bi * n_chunk:(bi + 1) * n_chunk]
        ksel = ks_scr[pl.ds(bi * P, P), :]
        vsel = vs_scr[pl.ds(bi * P, P), :]
        knew = jnp.concatenate([kvn_ref[rs8, 256:384], zpad], axis=0).astype(bf16)
        vnew = jnp.concatenate([kvn_ref[rs8, 384:512], zpad], axis=0).astype(bf16)
        kwnew_f = jnp.concatenate([wn_ref[rs8, 0:128], zpad], axis=0)
        vwnew_f = jnp.concatenate([wn_ref[rs8, 128:256], zpad], axis=0)
        kwnew, vwnew = kwnew_f.astype(bf16), vwnew_f.astype(bf16)
        kwin = win_ref[0, bi, 0].T.astype(bf16)
        vwin = win_ref[0, bi, 1].T.astype(bf16)
        heads = [None] * N_HEADS
        for g in range(N_KV):
            def stack(q_all):
                q = jnp.concatenate([q_all[rs8, (g * R + r) * 128:(g * R + r + 1) * 128] for r in range(R)], axis=0)
                if g:
                    q = pltpu.roll(q, g * HEAD_DIM, axis=1)
                return q.astype(bf16)
            qc = stack(qc_all)
            qr = stack(qr_all)
            s = lax.dot_general(qc, kc, NT, preferred_element_type=f32)
            cmp_end = lax.broadcasted_iota(jnp.int32, (1, n_chunk), 1) * CMP_STRIDE + (CMP_LEN - 1)
            p = _softmax_rows(s, cmp_end <= qpos_col)
            o_c = jnp.dot(p.astype(bf16), vc, preferred_element_type=f32)
            psum = p[0:SEG] + p[SEG:2 * SEG] + p[2 * SEG:3 * SEG] + p[3 * SEG:4 * SEG]
            psum = jnp.concatenate([psum, jnp.zeros((128 - SEG, n_chunk), f32)], axis=0)
            imp_t = lax.dot_general(ovt_ref[...], psum, NT, preferred_element_type=f32, precision=HIGHEST)
            sel_q = _transpose_sel(_select_blocks(imp_t, qpos_row, n_blk, score_scr))[0:SEG].astype(bf16)
            mk = jnp.concatenate([jnp.dot(sel_q, e_ref[...], preferred_element_type=f32)] * R, axis=0) > 0.5
            s_sel = jnp.concatenate([lax.dot_general(qr, ksel, NT, preferred_element_type=f32),
                                     lax.dot_general(qr, knew, NT, preferred_element_type=f32)], axis=1)
            kpos = jnp.concatenate([lax.broadcasted_iota(jnp.int32, (1, P), 1), P + new_lane], axis=1)
            valid = jnp.concatenate([jnp.full((1, P), True), new_lane < SEG], axis=1)
            p_s = _softmax_rows(s_sel, mk & (kpos <= qpos_col) & valid).astype(bf16)
            o_s = (jnp.dot(p_s[:, 0:P], vsel, preferred_element_type=f32)
                   + jnp.dot(p_s[:, P:P + 128], vnew, preferred_element_type=f32))
            s_w = jnp.concatenate([lax.dot_general(qr, kwin, NT, preferred_element_type=f32),
                                   lax.dot_general(qr, kwnew, NT, preferred_element_type=f32)], axis=1)
            kwpos = jnp.concatenate([P - WB + lax.broadcasted_iota(jnp.int32, (1, WB), 1), P + new_lane], axis=1)
            validw = jnp.concatenate([jnp.full((1, WB), True), new_lane < SEG], axis=1)
            dpos = qpos_col - kwpos
            p_w = _softmax_rows(s_w, (dpos >= 0) & (dpos < WINDOW) & (kwpos >= 0) & validw).astype(bf16)
            o_w = (jnp.dot(p_w[:, 0:WB], vwin, preferred_element_type=f32)
                   + jnp.dot(p_w[:, WB:WB + 128], vwnew, preferred_element_type=f32))
            for r in range(R):
                h = g * R + r
                rr = slice(r * SEG, (r + 1) * SEG)
                c0 = 8 + 3 * h
                gs = gsig[rs8]
                out = gs[:, c0:c0 + 1] * o_c[rr] + gs[:, c0 + 1:c0 + 2] * o_s[rr] + gs[:, c0 + 2:c0 + 3] * o_w[rr]
                if g:
                    out = pltpu.roll(out, 128 - g * HEAD_DIM, axis=1)
                heads[h] = out
        out_rows.append(jnp.concatenate(heads, axis=1))
        for kv, new_f in ((0, kwnew_f), (1, vwnew_f)):
            old = pltpu.roll(win_ref[0, bi, kv], WB - ts, axis=1)
            new_t = pltpu.roll(new_f.T, 128 - ts, axis=1)
            wout_ref[bi, kv, :, 0:WB - 128] = old[:, 0:WB - 128]
            wout_ref[bi, kv, :, WB - 128:WB] = jnp.where(new_lane >= 128 - ts, new_t, old[:, WB - 128:WB])
    hb_ref[...] = jnp.concatenate(out_rows, axis=0).astype(bf16)


def _nsa_sample(page_table, cache_t, cwin_t, qc, qr, sm, kvrow, winrow, cwk, cwv, *, layer, row0, ts):
    nseq, n_pages = page_table.shape
    P = n_pages * PAGE
    WB = cwin_t.shape[4]
    n_chunk = P // CMP_STRIDE
    ovt = _sel_overlap_t(n_chunk, -(-(P + ts) // SEL_BLOCK))
    e = _block_expand(P + 128)
    b0 = row0 // (GB * SEG)

    def page_map(i, pt, *, bi, p):
        return (layer, pt[i * GB + bi, p], 0, 0, 0)

    page_specs = [pl.BlockSpec((1, 1, 4, 128, PAGE), functools.partial(page_map, bi=bi, p=p))
                  for bi in range(GB) for p in range(n_pages)]
    row = lambda w: pl.BlockSpec((GB * SEG, w), lambda i, pt: (b0 + i, 0))
    const = lambda a: pl.BlockSpec(a.shape, lambda i, pt: (0,) * a.ndim)
    grid_spec = pltpu.PrefetchScalarGridSpec(
        num_scalar_prefetch=1, grid=(nseq // GB,),
        in_specs=page_specs + [pl.BlockSpec((1, GB, 2, 128, WB), lambda i, pt: (layer, i, 0, 0, 0)),
                               row(1024), row(1024), row(128), row(512), row(256)]
        + [const(a) for a in (*cwk, *cwv, ovt, e)],
        out_specs=[pl.BlockSpec((GB * SEG, 1024), lambda i, pt: (i, 0)),
                   pl.BlockSpec((GB, 2, 128, WB), lambda i, pt: (i, 0, 0, 0))],
        scratch_shapes=[pltpu.VMEM((GB * P, 128), f32), pltpu.VMEM((GB * P, 128), f32),
                        pltpu.VMEM((GB * P, 128), bf16), pltpu.VMEM((GB * P, 128), bf16),
                        pltpu.VMEM((NBLK_PAD, 128), f32)])
    return pl.pallas_call(
        functools.partial(_nsa_sample_body, n_pages=n_pages, ts=ts), grid_spec=grid_spec,
        out_shape=[jax.ShapeDtypeStruct((nseq * SEG, 1024), bf16), jax.ShapeDtypeStruct((nseq, 2, 128, WB), f32)],
        compiler_params=pltpu.CompilerParams(dimension_semantics=("parallel",), vmem_limit_bytes=VMEM_LIMIT),
        name="nsa_sample",
    )(page_table, *([cache_t] * (GB * n_pages)), cwin_t, qc, qr, sm, kvrow, winrow, *cwk, *cwv, ovt, e)


def kernel(x_prompt, x_sample, cache_kv_pages, page_table, cache_win, state_mlstm_C, state_mlstm_n, state_mlstm_m, state_mlstm_conv, norm1_g, w_in, b_gates, conv_w, conv_b, w_mq, w_mk, mlstm_norm_g, cmp_pe_k, cmp_pe_v, cmp_w1_k, cmp_w2_k, cmp_w1_v, cmp_w2_v, w_branch_a, w_branch_b, w_out, norm2_g, w_ffn_up, w_ffn_down, final_norm_g):
    Bp, Tp, _ = x_prompt.shape
    Bs, Ts, _ = x_sample.shape
    depth = w_in.shape[0]
    n_pool, page = cache_kv_pages.shape[1:3]
    n_pages = page_table.shape[1]
    past_len = n_pages * page
    wb_len = cache_win.shape[2]
    mp = Bp * Tp
    ms = Bs * SAMPLE_PAD
    assert page == PAGE and M_CONV - 1 <= Ts <= SAMPLE_PAD and Tp % MLSTM_CHUNK == 0 and Tp >= WINDOW + TQ
    assert mp % (SEG * SEQ_PER_STEP) == 0 and Bs % SEQ_PER_STEP == 0 and Bs % GB == 0

    pad_seq = lambda a: jnp.pad(a, ((0, 0), (0, SAMPLE_PAD - a.shape[1]), (0, 0)))
    x_all = jnp.concatenate([x_prompt.reshape(mp, D_MODEL), pad_seq(x_sample).reshape(ms, D_MODEL)], axis=0)
    tabs = _rope_tables(Tp, past_len, ROW_TILE * (ms // ROW_TILE))
    col_idx, col_scale = _in_proj_columns()
    cache_t = jnp.transpose(cache_kv_pages, (0, 1, 3, 4, 5, 2)).reshape(depth, n_pool, 4, KV_W, PAGE)
    cwin_t = jnp.transpose(cache_win, (0, 1, 3, 4, 5, 2)).reshape(depth, Bs, 2, KV_W, wb_len)

    outs = {k: [] for k in ("p_kv", "p_win", "p_C", "p_n", "p_m", "p_conv", "s_kv", "s_win", "s_C", "s_n", "s_m", "s_conv")}
    y_all = None
    for l in range(depth):
        w_ext = jnp.concatenate([w_in[l], jnp.zeros((D_MODEL, 1), f32)], axis=1)
        w_all = (w_ext[:, col_idx] * col_scale).astype(bf16)
        w_t = jnp.concatenate([w_in[l][:, O_IG:O_IG + 8], w_in[l][:, O_NG:O_NG + 24]], axis=1).T.astype(bf16)
        xm, sm, smt, qc, qr, qct, qrt, kvrow, winrow, kk, vt, gates = _in_proj(
            x_all, norm1_g[l].reshape(1, D_MODEL), w_all, w_t, tabs, Tp)

        wq, wk = w_mq[l].astype(bf16), w_mk[l].astype(bf16)
        ha_p, c_p, n_p, m_p, cv_p = _mlstm_prompt(xm, sm, smt, conv_w[l], conv_b[l], wq, wk, b_gates[l], mlstm_norm_g[l],
                                                   nb=Bp, T=Tp, L=MLSTM_CHUNK)
        tail = jnp.pad(state_mlstm_conv[l], ((0, 0), (0, SEG - (M_CONV - 1)), (0, 0))).reshape(ms, M_WIDTH)
        n0 = jnp.pad(state_mlstm_n[l], ((0, 0), (0, 8 - M_HEADS), (0, 0)))
        m0 = jnp.broadcast_to(jnp.pad(state_mlstm_m[l], ((0, 0), (0, 128 - M_HEADS)))[:, None, :], (Bs, SEG, 128))
        ha_s, c_s, n_s, mt_s = _mlstm_sample(xm, sm, smt, tail, state_mlstm_C[l], n0, m0.reshape(ms, 128), conv_w[l],
                                             conv_b[l], wq, wk, b_gates[l], mlstm_norm_g[l], row0=mp, nseq=Bs, n_valid=Ts)

        kc, vc = _compress_prompt(kvrow, _compress_weights(cmp_pe_k[l], cmp_w1_k[l], cmp_w2_k[l]),
                                  _compress_weights(cmp_pe_v[l], cmp_w1_v[l], cmp_w2_v[l]), nb=Bp, T=Tp)
        hbt_p = _nsa_prompt(qct, qrt, smt, kk, vt, kc, vc, nb=Bp, T=Tp)
        hb_s, win_s = _nsa_sample(page_table, cache_t, cwin_t, qc, qr, sm, kvrow, winrow,
                                  _compress_weights(cmp_pe_k[l], cmp_w1_k[l], cmp_w2_k[l], padded=False),
                                  _compress_weights(cmp_pe_v[l], cmp_w1_v[l], cmp_w2_v[l], padded=False),
                                  layer=l, row0=mp, ts=Ts)

        h_a = jnp.concatenate([ha_p, ha_s], axis=0)
        hbt_s = hb_s.reshape(ms, N_HEADS, 128)[:, :, :HEAD_DIM].reshape(ms, N_WIDTH).T
        h_bt = jnp.concatenate([hbt_p, hbt_s], axis=1)
        x_all, y_all = _mix_ffn(
            x_all, h_a, h_bt, gates, w_branch_a[l].astype(bf16), w_branch_b[l].astype(bf16), w_out[l].astype(bf16),
            norm2_g[l].reshape(1, D_MODEL), w_ffn_up[l].astype(bf16), w_ffn_down[l].astype(bf16),
            final_norm_g.reshape(1, D_MODEL))

        sample_rows = lambda a: a[mp:].reshape(Bs, SAMPLE_PAD, a.shape[1])
        wp = min(WINDOW, Tp)
        outs["p_kv"].append(kvrow[:mp].reshape(Bp, Tp, 4, N_KV, HEAD_DIM))
        outs["p_win"].append(winrow[:mp].reshape(Bp, Tp, 2 * KV_W)[:, -wp:].reshape(Bp, wp, 2, N_KV, HEAD_DIM))
        outs["p_C"].append(c_p)
        outs["p_n"].append(n_p[:, :M_HEADS])
        outs["p_m"].append(m_p[:, 0, :M_HEADS])
        outs["p_conv"].append(cv_p[:, 8 - (M_CONV - 1):])
        outs["s_kv"].append(sample_rows(kvrow)[:, :Ts].reshape(Bs, Ts, 4, N_KV, HEAD_DIM))
        outs["s_win"].append(jnp.transpose(win_s.reshape(Bs, 2, N_KV, HEAD_DIM, wb_len), (0, 4, 1, 2, 3)))
        outs["s_C"].append(c_s)
        outs["s_n"].append(n_s[:, :M_HEADS])
        outs["s_m"].append(mt_s.reshape(Bs, SEG, 128)[:, SEG - 1, :M_HEADS])
        outs["s_conv"].append(sample_rows(xm)[:, Ts - (M_CONV - 1):Ts, :M_WIDTH])

    y_prompt = y_all[:mp].reshape(Bp, Tp, D_MODEL)
    y_sample = y_all[mp:].reshape(Bs, SAMPLE_PAD, D_MODEL)[:, :Ts]
    st = lambda k: jnp.stack(outs[k])
    return (y_prompt, y_sample, st("p_kv"), st("p_win"), st("p_C"), st("p_n"), st("p_m"), st("p_conv"),
            st("s_kv"), st("s_win"), st("s_C"), st("s_n"), st("s_m"), st("s_conv"))
```

```python
import functools

import jax
import jax.numpy as jnp
import numpy as np
from jax import lax
from jax.experimental import pallas as pl
from jax.experimental.pallas import tpu as pltpu

f32 = jnp.float32
bf16 = jnp.bfloat16

D_MODEL = 1024
M_HEADS = 4
M_WIDTH = 512
M_HEAD_DIM = 128
M_CONV = 4
MLSTM_CHUNK = 256
HEAD_DIM = 64
N_WIDTH = 512
N_HEADS = 8
N_KV = 2
KV_W = 128
CMP_STRIDE = 16
CMP_LEN = 32
SEL_BLOCK = 64
SEL_TOP = 16
WINDOW = 512
ROT_DIM = 16
ROPE_THETA = 500000.0
D_FF = 2816
EPS = 1e-6
NEG = -1e30
TINY = 1e-30
FORCE_SCORE = 1e9

SAMPLE_PAD = 8
SEG = SAMPLE_PAD
SEQ_PER_STEP = 16
GB = 2
PAGE = 128
HD = M_HEAD_DIM
R = N_HEADS // N_KV
CMP_HID = 128
TQ = 128
TK = 512
NBLK_PAD = 64
HIGHEST = lax.Precision.HIGHEST
NT = (((1,), (1,)), ((), ()))
LOG2E = 1.4426950408889634
ROW_TILE = 256
FF_CHUNK = 256
VMEM_LIMIT = 56 * 1024 * 1024

C_XM, C_Q, C_KV, C_WIN, C_GATE, C_SM = 0, 1536, 2048, 2560, 2816, 4864
C_TOTAL = 4992
O_U, O_V, O_O, O_IG, O_FG, O_Q, O_KV, O_NG, O_GA, O_GB = 0, 512, 1024, 1536, 1540, 1544, 2056, 2824, 2848, 3872
IN_WIDTH = 4896


def _in_proj_columns():
    z = IN_WIDTH
    idx = list(range(O_U, O_IG))
    idx += list(range(O_Q, O_Q + N_WIDTH))
    idx += list(range(O_KV, O_KV + 768))
    idx += list(range(O_GA, O_GA + 2048))
    idx += list(range(O_IG, O_IG + 8)) + list(range(O_NG, O_NG + 24)) + [z] * 96
    assert len(idx) == C_TOTAL
    scale = np.ones((C_TOTAL,), np.float32)
    scale[C_Q:C_KV] = HEAD_DIM ** -0.5
    return np.asarray(idx, np.int32), scale


def _rope_tables(seq, past_len, n_sample_rows):
    half = ROT_DIM // 2
    inv = ROPE_THETA ** (-jnp.arange(half, dtype=f32) / half)
    pos = jnp.concatenate([jnp.arange(seq), past_len + (jnp.arange(n_sample_rows) % SAMPLE_PAD)]).astype(f32)
    ang = pos[:, None] * inv[None, :]
    cos8, sin8 = jnp.cos(ang), jnp.sin(ang)
    n = pos.shape[0]
    one = jnp.ones((n, 64 - ROT_DIM), f32)
    zero = jnp.zeros((n, 64 - ROT_DIM), f32)
    z8 = jnp.zeros((n, half), f32)
    cos = jnp.concatenate([cos8, cos8, one], axis=1)
    sa = jnp.concatenate([-sin8, z8, zero], axis=1)
    sb = jnp.concatenate([z8, sin8, zero], axis=1)
    tile2 = lambda a: jnp.concatenate([a, a], axis=1)
    return tile2(cos), tile2(sa), tile2(sb)


def _rms(x, g):
    return x * lax.rsqrt(jnp.mean(x * x, axis=-1, keepdims=True) + EPS) * g


def _in_proj_body(x_ref, g_ref, w_ref, wt_ref, cos_ref, sa_ref, sb_ref,
                  xm_ref, sm_ref, smt_ref, qc_ref, qr_ref, qct_ref, qrt_ref, kv_ref, win_ref, kk_ref, vt_ref, gate_ref):
    hb = _rms(x_ref[...], g_ref[...]).astype(bf16)
    cos, sa, sb = cos_ref[...], sa_ref[...], sb_ref[...]

    def rope(v):
        return v * cos + pltpu.roll(v, 128 - ROT_DIM // 2, axis=1) * sa + pltpu.roll(v, ROT_DIM // 2, axis=1) * sb

    def proj(c0, n):
        return jnp.dot(hb, w_ref[:, c0:c0 + n], preferred_element_type=f32)

    xm_ref[...] = proj(C_XM, 1536)
    sm_ref[...] = proj(C_SM, 128)
    smt_ref[...] = lax.dot_general(wt_ref[...], hb, NT, preferred_element_type=f32)
    qf = proj(C_Q, N_WIDTH) * LOG2E
    for j in range(N_HEADS // 2):
        sl = slice(j * 128, (j + 1) * 128)
        q = qf[:, sl]
        qrot = rope(q)
        qc_ref[:, sl] = q.astype(bf16)
        qr_ref[:, sl] = qrot.astype(bf16)
        qct_ref[sl, :] = q.T.astype(bf16)
        qrt_ref[sl, :] = qrot.T.astype(bf16)
    kv = proj(C_KV, 512)
    ksel = rope(kv[:, 256:384])
    kv_ref[:, 0:256] = kv[:, 0:256]
    kv_ref[:, 256:384] = ksel
    kv_ref[:, 384:512] = kv[:, 384:512]
    win = proj(C_WIN, 256)
    kwin = rope(win[:, 0:128])
    win_ref[:, 0:128] = kwin
    win_ref[:, 128:256] = win[:, 128:256]
    kk_ref[:, 0:128] = ksel.astype(bf16)
    kk_ref[:, 128:256] = kwin.astype(bf16)
    vt_ref[0:128, :] = kv[:, 384:512].T.astype(bf16)
    vt_ref[128:256, :] = win[:, 128:256].T.astype(bf16)
    gate_ref[...] = proj(C_GATE, 2048)


def _in_proj(x_all, norm_g, w_all, w_t, tabs, seq):
    m = x_all.shape[0]
    n_prompt_tiles_per_seq = seq // ROW_TILE
    n_prompt_tiles = (m - (tabs[0].shape[0] - seq)) // ROW_TILE

    def tab_map(i):
        return (jnp.where(i < n_prompt_tiles, i % n_prompt_tiles_per_seq, n_prompt_tiles_per_seq + i - n_prompt_tiles), 0)

    row = lambda w: pl.BlockSpec((ROW_TILE, w), lambda i: (i, 0))
    const = lambda s: pl.BlockSpec(s, lambda i: (0, 0), pipeline_mode=pl.Buffered(1))
    tab = pl.BlockSpec((ROW_TILE, 128), tab_map)
    widths = (1536, 128, -32, 512, 512, -512, -512, 512, 256, 256, -256, 2048)
    dts = (f32, f32, f32, bf16, bf16, bf16, bf16, f32, f32, bf16, bf16, f32)
    out_shape = [jax.ShapeDtypeStruct((-w, m) if w < 0 else (m, w), d) for w, d in zip(widths, dts)]
    out_specs = [pl.BlockSpec((-w, ROW_TILE), lambda i: (0, i)) if w < 0 else row(w) for w in widths]
    return pl.pallas_call(
        _in_proj_body, grid=(m // ROW_TILE,),
        in_specs=[row(D_MODEL), const((1, D_MODEL)), const((D_MODEL, C_TOTAL)), const((32, D_MODEL)), tab, tab, tab],
        out_specs=out_specs, out_shape=out_shape, name="in_proj",
        compiler_params=pltpu.CompilerParams(dimension_semantics=("parallel",), vmem_limit_bytes=VMEM_LIMIT),
    )(x_all, norm_g, w_all, w_t, *tabs)


def _mix_ffn_body(x_ref, ha_ref, hb_ref, gate_ref, wa_ref, wb_ref, wo_ref, g2_ref, wup_ref, wdn_ref, gf_ref,
                  xo_ref, y_ref):
    a = jnp.dot(ha_ref[...].astype(bf16), wa_ref[...], preferred_element_type=f32)
    b = lax.dot_general(hb_ref[...], wb_ref[...], (((0,), (0,)), ((), ())), preferred_element_type=f32)
    merged = jax.nn.sigmoid(gate_ref[:, 0:D_MODEL]) * a + jax.nn.sigmoid(gate_ref[:, D_MODEL:2 * D_MODEL]) * b
    x1 = x_ref[...] + jnp.dot(merged.astype(bf16), wo_ref[...], preferred_element_type=f32)
    hn = _rms(x1, g2_ref[...]).astype(bf16)
    acc = x1
    for c in range(D_FF // FF_CHUNK):
        lo = c * FF_CHUNK
        g = jnp.dot(hn, wup_ref[:, lo:lo + FF_CHUNK], preferred_element_type=f32)
        u = jnp.dot(hn, wup_ref[:, D_FF + lo:D_FF + lo + FF_CHUNK], preferred_element_type=f32)
        act = (g * jax.nn.sigmoid(g) * u).astype(bf16)
        acc = acc + jnp.dot(act, wdn_ref[lo:lo + FF_CHUNK, :], preferred_element_type=f32)
    xo_ref[...] = acc
    y_ref[...] = _rms(acc, gf_ref[...])


def _mix_ffn(x_all, h_a, h_b, gates, wa, wb, wo, g2, wup, wdn, gf):
    m = x_all.shape[0]
    row = lambda w: pl.BlockSpec((ROW_TILE, w), lambda i: (i, 0))
    const = lambda s: pl.BlockSpec(s, lambda i: (0, 0), pipeline_mode=pl.Buffered(1))
    return pl.pallas_call(
        _mix_ffn_body, grid=(m // ROW_TILE,),
        in_specs=[row(D_MODEL), row(h_a.shape[1]), pl.BlockSpec((h_b.shape[0], ROW_TILE), lambda i: (0, i)),
                  row(2 * D_MODEL),
                  const(wa.shape), const(wb.shape), const(wo.shape), const((1, D_MODEL)),
                  const(wup.shape), const(wdn.shape), const((1, D_MODEL))],
        out_specs=[row(D_MODEL), row(D_MODEL)],
        out_shape=[jax.ShapeDtypeStruct((m, D_MODEL), f32)] * 2, name="mix_ffn",
        compiler_params=pltpu.CompilerParams(dimension_semantics=("parallel",), vmem_limit_bytes=VMEM_LIMIT),
    )(x_all, h_a, h_b, gates, wa, wb, wo, g2, wup, wdn, gf)


def _log_sigmoid(x):
    return jnp.minimum(x, 0.0) - jnp.log1p(jnp.exp(-jnp.abs(x)))


def _seg_scan(x, axis, seg, op, fill):
    idx = lax.broadcasted_iota(jnp.int32, x.shape, axis) % seg
    d = 1
    while d < seg:
        x = op(x, jnp.where(idx >= d, pltpu.roll(x, d, axis=axis), fill))
        d *= 2
    return x


def _conv_silu(u, tail, conv_w, conv_b, seg):
    L = u.shape[0]
    row = lax.broadcasted_iota(jnp.int32, u.shape, 0) % seg
    y = conv_b + u * conv_w[3:4, :]
    for k in (1, 2, 3):
        prev = tail if k == 3 else pltpu.roll(tail, L - (3 - k), axis=0)
        sh = jnp.where(row < k, prev, pltpu.roll(u, k, axis=0))
        y = y + sh * conv_w[3 - k:4 - k, :]
    return y * jax.nn.sigmoid(y)


def _gate_columns(sm, smt, bg_row, bg_col, m_vec, seg, n_valid):
    G = sm + bg_row
    lf = _log_sigmoid(G)
    ipre = G
    if n_valid < seg:
        rowc = lax.broadcasted_iota(jnp.int32, G.shape, 0) % seg
        lf = jnp.where(rowc < n_valid, lf, 0.0)
        ipre = jnp.where(rowc < n_valid, G, NEG)
    F = _seg_scan(lf, 0, seg, jnp.add, 0.0)
    F_al = pltpu.roll(F, 128 - M_HEADS, axis=1)
    a_col = ipre - F_al
    g_col = jnp.maximum(m_vec, _seg_scan(a_col, 0, seg, jnp.maximum, NEG))
    mt_col = F_al + g_col
    Gr = smt + bg_col
    lfr = _log_sigmoid(Gr)
    ir = Gr
    if n_valid < seg:
        lanec = lax.broadcasted_iota(jnp.int32, Gr.shape, 1) % seg
        lfr = jnp.where(lanec < n_valid, lfr, 0.0)
        ir = jnp.where(lanec < n_valid, Gr, NEG)
    Fr = _seg_scan(lfr, 1, seg, jnp.add, 0.0)
    a_row = ir - pltpu.roll(Fr, M_HEADS, axis=0)
    return a_col, g_col, mt_col, a_row


def _head_norm_gate(hh, ng, o):
    return hh * lax.rsqrt(jnp.mean(hh * hh, axis=-1, keepdims=True) + EPS) * ng * jax.nn.sigmoid(o)


def _mlstm_prompt_body(xm_ref, sm_ref, smt_ref, cw_ref, cb_ref, wq_ref, wk_ref, bgr_ref, bgc_ref, ng_ref,
                       ha_ref, c_out, n_out, m_out, conv_out, c_scr, n_scr, m_scr, tail_scr, *, L):
    c_idx = pl.program_id(1)

    @pl.when(c_idx == 0)
    def _():
        c_scr[...] = jnp.zeros_like(c_scr)
        n_scr[...] = jnp.zeros_like(n_scr)
        m_scr[...] = jnp.zeros_like(m_scr)
        tail_scr[...] = jnp.zeros_like(tail_scr)

    u = xm_ref[:, 0:M_WIDTH]
    tail = jnp.concatenate([tail_scr[...], jnp.zeros((L - 8, M_WIDTH), f32)], axis=0)
    cact = _conv_silu(u, tail, cw_ref[...], cb_ref[...], L)
    tail_scr[...] = pltpu.roll(u[L - 8:L, :], 3, axis=0)
    conv_out[0] = u[L - 8:L, :]

    m_vec = m_scr[0:1, :]
    a_col, g_col, mt_col, a_row = _gate_columns(sm_ref[...], smt_ref[...], bgr_ref[...], bgc_ref[...], m_vec, L, L)
    inter_col = jnp.exp(m_vec - g_col)
    floor_col = jnp.exp(-mt_col)
    g_last = g_col[L - 1:L, :]
    w_col = jnp.exp(a_col - g_last)
    decay = jnp.exp(m_vec - g_last)
    m_new = mt_col[L - 1:L, :]
    ti = lax.broadcasted_iota(jnp.int32, (L, L), 0)
    si = lax.broadcasted_iota(jnp.int32, (L, L), 1)
    causal = si <= ti
    scale = HD ** -0.5
    for h in range(M_HEADS):
        sl = slice(h * HD, (h + 1) * HD)
        ch = cact[:, sl].astype(bf16)
        qh = jnp.dot(ch, wq_ref[h], preferred_element_type=f32)
        kh = jnp.dot(ch, wk_ref[h], preferred_element_type=f32) * scale
        vh = xm_ref[:, M_WIDTH + h * HD:M_WIDTH + (h + 1) * HD].astype(bf16)
        qb = qh.astype(bf16)
        D = jnp.exp(jnp.where(causal, a_row[h:h + 1, :] - g_col[:, h:h + 1], NEG))
        S = lax.dot_general(qb, kh.astype(bf16), NT, preferred_element_type=f32) * D
        C = c_scr[h]
        nrow = n_scr[h:h + 1, :]
        ic = inter_col[:, h:h + 1]
        num = jnp.dot(S.astype(bf16), vh, preferred_element_type=f32) + ic * jnp.dot(
            qb, C.astype(bf16), preferred_element_type=f32)
        den = jnp.sum(S, axis=-1, keepdims=True) + ic * jnp.sum(qh * nrow, axis=-1, keepdims=True)
        hh = num / jnp.maximum(jnp.abs(den), floor_col[:, h:h + 1])
        o = xm_ref[:, 2 * M_WIDTH + h * HD:2 * M_WIDTH + (h + 1) * HD]
        ha_ref[:, sl] = _head_norm_gate(hh, ng_ref[:, sl], o)
        kw = kh * w_col[:, h:h + 1]
        dh = decay[:, h:h + 1]
        c_scr[h] = dh * C + lax.dot_general(kw.astype(bf16), vh, (((0,), (0,)), ((), ())), preferred_element_type=f32)
        n_scr[h:h + 1, :] = dh * nrow + jnp.sum(kw, axis=0, keepdims=True)
    m_scr[0:1, :] = m_new

    @pl.when(c_idx == pl.num_programs(1) - 1)
    def _():
        c_out[0] = c_scr[...]
        n_out[0] = n_scr[...]
        m_out[0] = m_scr[...]


def _mlstm_prompt(xm, sm, smt, conv_w, conv_b, wq, wk, b_gates, norm_g, *, nb, T, L):
    nc = T // L
    bg_row = jnp.zeros((1, 128), f32).at[0, :8].set(b_gates)
    bg_col = b_gates.reshape(8, 1)
    const = lambda s: pl.BlockSpec(s, lambda b, c: (0,) * len(s))
    return pl.pallas_call(
        functools.partial(_mlstm_prompt_body, L=L), grid=(nb, nc),
        in_specs=[pl.BlockSpec((L, 3 * M_WIDTH), lambda b, c: (b * nc + c, 0)),
                  pl.BlockSpec((L, 128), lambda b, c: (b * nc + c, 0)),
                  pl.BlockSpec((8, L), lambda b, c: (0, b * nc + c)),
                  const((4, M_WIDTH)), const((1, M_WIDTH)), const((M_HEADS, HD, HD)), const((M_HEADS, HD, HD)),
                  const((1, 128)), const((8, 1)), const((1, M_WIDTH))],
        out_specs=[pl.BlockSpec((L, M_WIDTH), lambda b, c: (b * nc + c, 0)),
                   pl.BlockSpec((1, M_HEADS, HD, HD), lambda b, c: (b, 0, 0, 0)),
                   pl.BlockSpec((1, 8, HD), lambda b, c: (b, 0, 0)),
                   pl.BlockSpec((1, 8, 128), lambda b, c: (b, 0, 0)),
                   pl.BlockSpec((1, 8, M_WIDTH), lambda b, c: (b, 0, 0))],
        out_shape=[jax.ShapeDtypeStruct((nb * T, M_WIDTH), f32),
                   jax.ShapeDtypeStruct((nb, M_HEADS, HD, HD), f32),
                   jax.ShapeDtypeStruct((nb, 8, HD), f32),
                   jax.ShapeDtypeStruct((nb, 8, 128), f32),
                   jax.ShapeDtypeStruct((nb, 8, M_WIDTH), f32)],
        scratch_shapes=[pltpu.VMEM((M_HEADS, HD, HD), f32), pltpu.VMEM((8, HD), f32), pltpu.VMEM((8, 128), f32),
                        pltpu.VMEM((8, M_WIDTH), f32)],
        compiler_params=pltpu.CompilerParams(dimension_semantics=("parallel", "arbitrary")),
        name="mlstm_prompt",
    )(xm, sm, smt, conv_w, conv_b.reshape(1, M_WIDTH), wq, wk, bg_row, bg_col, norm_g.reshape(1, M_WIDTH))


def _mlstm_sample_body(xm_ref, sm_ref, smt_ref, tail_ref, c_in, n_in, m_in, cw_ref, cb_ref, wq_ref, wk_ref,
                       bgr_ref, bgc_ref, ng_ref, ha_ref, c_out, n_out, m_out, *, n_valid):
    L = SEG * SEQ_PER_STEP
    u = xm_ref[:, 0:M_WIDTH]
    cact = _conv_silu(u, tail_ref[...], cw_ref[...], cb_ref[...], SEG)
    m_rows = m_in[...]
    a_col, g_col, mt_col, a_row = _gate_columns(sm_ref[...], smt_ref[...], bgr_ref[...], bgc_ref[...], m_rows, SEG,
                                                n_valid)
    inter_col = jnp.exp(m_rows - g_col)
    floor_col = jnp.exp(-mt_col)
    ti = lax.broadcasted_iota(jnp.int32, (L, L), 0)
    si = lax.broadcasted_iota(jnp.int32, (L, L), 1)
    mask = (si <= ti) & ((si // SEG) == (ti // SEG))
    scale = HD ** -0.5
    for h in range(M_HEADS):
        sl = slice(h * HD, (h + 1) * HD)
        ch = cact[:, sl].astype(bf16)
        qh = jnp.dot(ch, wq_ref[h], preferred_element_type=f32)
        kh = jnp.dot(ch, wk_ref[h], preferred_element_type=f32) * scale
        vh = xm_ref[:, M_WIDTH + h * HD:M_WIDTH + (h + 1) * HD].astype(bf16)
        qb = qh.astype(bf16)
        D = jnp.exp(jnp.where(mask, a_row[h:h + 1, :] - g_col[:, h:h + 1], NEG))
        S = lax.dot_general(qb, kh.astype(bf16), NT, preferred_element_type=f32) * D
        num = jnp.dot(S.astype(bf16), vh, preferred_element_type=f32)
        den = jnp.sum(S, axis=-1, keepdims=True)
        inter_num, inter_den = [], []
        for s in range(SEQ_PER_STEP):
            rs = slice(s * SEG, (s + 1) * SEG)
            C = c_in[s, h]
            nrow = n_in[s, h:h + 1, :]
            inter_num.append(jnp.dot(qb[rs], C.astype(bf16), preferred_element_type=f32))
            inter_den.append(jnp.sum(qh[rs] * nrow, axis=-1, keepdims=True))
            g_last = g_col[s * SEG + SEG - 1:s * SEG + SEG, h:h + 1]
            m_prev = m_rows[s * SEG:s * SEG + 1, h:h + 1]
            w = jnp.exp(a_col[rs, h:h + 1] - g_last)
            dh = jnp.exp(m_prev - g_last)
            kw = kh[rs] * w
            c_out[s, h] = dh * C + lax.dot_general(kw.astype(bf16), vh[rs], (((0,), (0,)), ((), ())),
                                                    preferred_element_type=f32)
            n_out[s, h:h + 1, :] = dh * nrow + jnp.sum(kw, axis=0, keepdims=True)
        ic = inter_col[:, h:h + 1]
        num = num + ic * jnp.concatenate(inter_num, axis=0)
        den = den + ic * jnp.concatenate(inter_den, axis=0)
        hh = num / jnp.maximum(jnp.abs(den), floor_col[:, h:h + 1])
        o = xm_ref[:, 2 * M_WIDTH + h * HD:2 * M_WIDTH + (h + 1) * HD]
        ha_ref[:, sl] = _head_norm_gate(hh, ng_ref[:, sl], o)
    for s in range(SEQ_PER_STEP):
        n_out[s, M_HEADS:8, :] = jnp.zeros((8 - M_HEADS, HD), f32)
    m_out[...] = mt_col


def _mlstm_sample(xm, sm, smt, tail, c0, n0, m0, conv_w, conv_b, wq, wk, b_gates, norm_g, *, row0, nseq, n_valid):
    L = SEG * SEQ_PER_STEP
    b0 = row0 // L
    bg_row = jnp.zeros((1, 128), f32).at[0, :8].set(b_gates)
    bg_col = b_gates.reshape(8, 1)
    const = lambda s: pl.BlockSpec(s, lambda i: (0,) * len(s))
    return pl.pallas_call(
        functools.partial(_mlstm_sample_body, n_valid=n_valid), grid=(nseq // SEQ_PER_STEP,),
        in_specs=[pl.BlockSpec((L, 3 * M_WIDTH), lambda i: (b0 + i, 0)),
                  pl.BlockSpec((L, 128), lambda i: (b0 + i, 0)),
                  pl.BlockSpec((8, L), lambda i: (0, b0 + i)),
                  pl.BlockSpec((L, M_WIDTH), lambda i: (i, 0)),
                  pl.BlockSpec((SEQ_PER_STEP, M_HEADS, HD, HD), lambda i: (i, 0, 0, 0)),
                  pl.BlockSpec((SEQ_PER_STEP, 8, HD), lambda i: (i, 0, 0)),
                  pl.BlockSpec((L, 128), lambda i: (i, 0)),
                  const((4, M_WIDTH)), const((1, M_WIDTH)), const((M_HEADS, HD, HD)), const((M_HEADS, HD, HD)),
                  const((1, 128)), const((8, 1)), const((1, M_WIDTH))],
        out_specs=[pl.BlockSpec((L, M_WIDTH), lambda i: (i, 0)),
                   pl.BlockSpec((SEQ_PER_STEP, M_HEADS, HD, HD), lambda i: (i, 0, 0, 0)),
                   pl.BlockSpec((SEQ_PER_STEP, 8, HD), lambda i: (i, 0, 0)),
                   pl.BlockSpec((L, 128), lambda i: (i, 0))],
        out_shape=[jax.ShapeDtypeStruct((nseq * SEG, M_WIDTH), f32),
                   jax.ShapeDtypeStruct((nseq, M_HEADS, HD, HD), f32),
                   jax.ShapeDtypeStruct((nseq, 8, HD), f32),
                   jax.ShapeDtypeStruct((nseq * SEG, 128), f32)],
        compiler_params=pltpu.CompilerParams(dimension_semantics=("parallel",), vmem_limit_bytes=VMEM_LIMIT),
        name="mlstm_sample",
    )(xm, sm, smt, tail, c0, n0, m0, conv_w, conv_b.reshape(1, M_WIDTH), wq, wk, bg_row, bg_col,
      norm_g.reshape(1, M_WIDTH))


def _sel_overlap_t(n_chunk, n_blk):
    n_cmp = n_chunk - 1
    start = np.arange(n_cmp) * CMP_STRIDE
    bs = np.arange(n_blk) * SEL_BLOCK
    ov = np.minimum(start[:, None] + CMP_LEN, bs[None, :] + SEL_BLOCK) - np.maximum(start[:, None], bs[None, :])
    ov = np.clip(ov, 0, None) / CMP_LEN
    out = np.zeros((NBLK_PAD, n_chunk), np.float32)
    out[:n_blk, :n_cmp] = ov.T
    return jnp.asarray(out)


def _block_expand(n_keys):
    e = (np.arange(n_keys)[None, :] // SEL_BLOCK) == np.arange(128)[:, None]
    return jnp.asarray(e, dtype=bf16)


def _compress_weights(pe, w1, w2):
    W = jnp.zeros((16, 2, 64, 2, 2, CMP_HID), f32)
    for g in range(N_KV):
        W = W.at[:, g, :, g, 0, :].set(w1[:16])
        W = W.at[:, g, :, g, 1, :].set(w1[16:])
    W = W.reshape(2048, 512)
    peA = jnp.broadcast_to(pe[:16, None, :], (16, 2, 64)).reshape(1, 2048)
    peB = jnp.broadcast_to(pe[16:, None, :], (16, 2, 64)).reshape(1, 2048)
    PE = jnp.concatenate([peA, peB, jnp.zeros((14, 2048), f32)], axis=0)
    W2 = jnp.zeros((2, CMP_HID, 128), f32).at[0, :, 0:64].set(w2).at[1, :, 64:128].set(w2)
    return W.astype(bf16), PE.astype(bf16), W2.astype(bf16)


def _compress_rows(flat, w_ref, pe_ref, w2_ref):
    n = flat.shape[0]
    y = jnp.dot(jnp.concatenate([flat, pe_ref[...]], axis=0), w_ref[...], preferred_element_type=f32)
    c = y[n:n + 8]
    out = jnp.zeros((n, w2_ref.shape[2]), f32)
    for g in range(N_KV):
        a = y[0:n, (2 * g) * CMP_HID:(2 * g + 1) * CMP_HID] + c[0:1, (2 * g) * CMP_HID:(2 * g + 1) * CMP_HID]
        b = (y[0:n, (2 * g + 1) * CMP_HID:(2 * g + 2) * CMP_HID]
             + c[1:2, (2 * g + 1) * CMP_HID:(2 * g + 2) * CMP_HID])
        hid = a + pltpu.roll(b, n - 1, axis=0)
        out = out + jnp.dot(jax.nn.gelu(hid, approximate=True).astype(bf16), w2_ref[g], preferred_element_type=f32)
    return out


def _compress_prompt_body(xk_ref, xv_ref, wk_ref, pek_ref, w2k_ref, wv_ref, pev_ref, w2v_ref, kc_ref, vc_ref, *,
                          n_chunk):
    def flat(x_ref):
        return jnp.concatenate(
            [x_ref[pl.ds(l, n_chunk, stride=CMP_STRIDE), :].astype(bf16) for l in range(CMP_STRIDE)], axis=1)
    kc_ref[0] = _compress_rows(flat(xk_ref), wk_ref, pek_ref, w2k_ref).astype(bf16)
    vc_ref[0] = _compress_rows(flat(xv_ref), wv_ref, pev_ref, w2v_ref).T.astype(bf16)


def _compress_prompt(kvrow, cwk, cwv, *, nb, T):
    n_chunk = T // CMP_STRIDE
    const = lambda a: pl.BlockSpec(a.shape, lambda b: (0,) * a.ndim)
    return pl.pallas_call(
        functools.partial(_compress_prompt_body, n_chunk=n_chunk), grid=(nb,),
        in_specs=[pl.BlockSpec((T, 128), lambda b: (b, 0)), pl.BlockSpec((T, 128), lambda b: (b, 1))]
        + [const(a) for a in (*cwk, *cwv)],
        out_specs=[pl.BlockSpec((1, n_chunk, 128), lambda b: (b, 0, 0)),
                   pl.BlockSpec((1, 128, n_chunk), lambda b: (b, 0, 0))],
        out_shape=[jax.ShapeDtypeStruct((nb, n_chunk, 128), bf16), jax.ShapeDtypeStruct((nb, 128, n_chunk), bf16)],
        compiler_params=pltpu.CompilerParams(dimension_semantics=("parallel",), vmem_limit_bytes=VMEM_LIMIT),
        name="nsa_compress",
    )(kvrow, kvrow, *cwk, *cwv)


def _softmax_rows(s, mask):
    s = jnp.where(mask, s, NEG)
    e = jnp.exp2(s - jnp.max(s, axis=-1, keepdims=True))
    e = jnp.where(mask, e, 0.0)
    return e / jnp.maximum(jnp.sum(e, axis=-1, keepdims=True), TINY)


def _select_blocks(imp_t, qpos_row, n_blk, score_scr):
    shape = imp_t.shape
    j = lax.broadcasted_iota(jnp.int32, shape, 0)
    cur = qpos_row // SEL_BLOCK
    forced = (j == 0) | (j == cur) | (j == cur - 1)
    score = jnp.where(j <= cur, jnp.where(forced, FORCE_SCORE, imp_t), -1.0)
    score = jnp.where(j < n_blk, score, -2.0)
    score_scr[...] = score

    def body(jp, cnt):
        sj = score_scr[pl.ds(jp, 1), :]
        return cnt + jnp.where(j > jp, jnp.where(sj >= score, 1.0, 0.0), jnp.where(sj > score, 1.0, 0.0))

    cnt = lax.fori_loop(0, n_blk, body, jnp.zeros(shape, f32), unroll=8 if n_blk % 8 == 0 else 1)
    return jnp.where(cnt < float(min(SEL_TOP, n_blk)), 1.0, 0.0)


def _transpose_sel(sel_t):
    return jnp.concatenate([sel_t, jnp.zeros((128 - NBLK_PAD, sel_t.shape[1]), f32)], axis=0).T


def _softmax_cols(s, mask):
    s = jnp.where(mask, s, NEG)
    e = jnp.exp2(s - jnp.max(s, axis=0, keepdims=True))
    e = jnp.where(mask, e, 0.0)
    return e * (1.0 / jnp.maximum(jnp.sum(e, axis=0, keepdims=True), TINY))


def _lane_tile(x, n):
    return jnp.concatenate([x] * n, axis=1)


def _nsa_prompt_body(qct_ref, qrt_ref, gt_ref, kk_ref, vt_ref, kc_ref, vct_ref, ovt_ref, hbt_ref, sel_scr, score_scr,
                     *, T):
    i = pl.program_id(1)
    n_chunk = T // CMP_STRIDE
    n_blk = T // SEL_BLOCK
    q0 = i * TQ
    qpos_row = q0 + lax.broadcasted_iota(jnp.int32, (1, TQ), 1)
    gsig = jax.nn.sigmoid(gt_ref[...])
    n_kt = (q0 + TQ + TK - 1) // TK
    w0 = jnp.maximum(i - WINDOW // TQ, 0) * TQ
    WK = WINDOW + TQ
    blk_per_tile = TK // SEL_BLOCK
    ones_lhs = jnp.ones((16, TK), bf16)
    zeros_q = jnp.zeros((HEAD_DIM, TQ), bf16)

    def group_queries(ref, g):
        cols = []
        for r in range(R):
            q = ref[(g * R + r) * HEAD_DIM:(g * R + r + 1) * HEAD_DIM, :]
            cols.append(jnp.concatenate([q, zeros_q] if g == 0 else [zeros_q, q], axis=0))
        return jnp.concatenate(cols, axis=1)

    qrts, o_cs = [], []
    for g in range(N_KV):
        qct = group_queries(qct_ref, g)
        qrts.append(group_queries(qrt_ref, g))
        s = jnp.dot(kc_ref[0], qct, preferred_element_type=f32)
        cmp_end = lax.broadcasted_iota(jnp.int32, (n_chunk, 1), 0) * CMP_STRIDE + (CMP_LEN - 1)
        p = _softmax_cols(s, _lane_tile(cmp_end <= qpos_row, R))
        o_cs.append(jnp.dot(vct_ref[0], p.astype(bf16), preferred_element_type=f32))
        psum = p[:, 0:TQ] + p[:, TQ:2 * TQ] + p[:, 2 * TQ:3 * TQ] + p[:, 3 * TQ:4 * TQ]
        imp_t = jnp.dot(ovt_ref[...], psum, preferred_element_type=f32, precision=HIGHEST)
        sel_scr[g] = jnp.where(_select_blocks(imp_t, qpos_row, n_blk, score_scr) > 0.5, 0.0, NEG)

    def kv_step(kt, carry):
        k0 = pl.multiple_of(kt * TK, TK)
        b0 = pl.multiple_of(kt * blk_per_tile, blk_per_tile)
        causal = (k0 + lax.broadcasted_iota(jnp.int32, (TK, 1), 0)) <= qpos_row
        out = []
        for g in range(N_KV):
            m_i, l_i, acc = carry[g]
            sc = jnp.dot(kk_ref[pl.ds(k0, TK), 0:128], qrts[g], preferred_element_type=f32)
            sel8 = sel_scr[g, pl.ds(b0, blk_per_tile), :]
            bias = jnp.broadcast_to(sel8[:, None, :], (blk_per_tile, SEL_BLOCK, TQ)).reshape(TK, TQ)
            sc = sc + _lane_tile(jnp.where(causal, bias, NEG), R)
            m_new = jnp.maximum(m_i, jnp.max(sc, axis=0, keepdims=True))
            alpha = jnp.exp2(m_i - m_new)
            pp = jnp.exp2(sc - m_new).astype(bf16)
            acc = alpha * acc + jnp.dot(vt_ref[0:128, pl.ds(k0, TK)], pp, preferred_element_type=f32)
            l_i = alpha * l_i + jnp.dot(ones_lhs, pp, preferred_element_type=f32)[0:1, :]
            out.append((m_new, l_i, acc))
        return tuple(out)

    init = (jnp.full((1, R * TQ), NEG, f32), jnp.zeros((1, R * TQ), f32), jnp.zeros((128, R * TQ), f32))
    sel_out = lax.fori_loop(0, n_kt, kv_step, (init,) * N_KV)

    ones_w = jnp.ones((16, WINDOW + TQ), bf16)
    for g in range(N_KV):
        qrt, o_c = qrts[g], o_cs[g]
        o_s = sel_out[g][2] * (1.0 / sel_out[g][1])
        w0a = pl.multiple_of(w0, TQ)
        sw = jnp.dot(kk_ref[pl.ds(w0a, WK), 128:256], qrt, preferred_element_type=f32)
        dpos = qpos_row - (w0 + lax.broadcasted_iota(jnp.int32, (WK, 1), 0))
        biasw = jnp.where((dpos >= 0) & (dpos < WINDOW), 0.0, NEG)
        sw = sw + _lane_tile(biasw, R)
        pw = jnp.exp2(sw - jnp.max(sw, axis=0, keepdims=True)).astype(bf16)
        ow = jnp.dot(vt_ref[128:256, pl.ds(w0a, WK)], pw, preferred_element_type=f32)
        o_w = ow * (1.0 / jnp.dot(ones_w, pw, preferred_element_type=f32)[0:1, :])
        rows = slice(g * HEAD_DIM, (g + 1) * HEAD_DIM)
        for r in range(R):
            h = g * R + r
            cs = slice(r * TQ, (r + 1) * TQ)
            c0 = 8 + 3 * h
            out = (gsig[c0:c0 + 1, :] * o_c[rows, cs] + gsig[c0 + 1:c0 + 2, :] * o_s[rows, cs]
                   + gsig[c0 + 2:c0 + 3, :] * o_w[rows, cs])
            hbt_ref[h * 64:(h + 1) * 64, :] = out.astype(bf16)


def _nsa_prompt(qct, qrt, gt, kk, vt, kc, vct, *, nb, T):
    nq = T // TQ
    n_chunk = T // CMP_STRIDE
    ovt = _sel_overlap_t(n_chunk, T // SEL_BLOCK)
    col = lambda h: pl.BlockSpec((h, TQ), lambda b, i: (0, b * nq + i))
    const = lambda a: pl.BlockSpec(a.shape, lambda b, i: (0,) * a.ndim)
    return pl.pallas_call(
        functools.partial(_nsa_prompt_body, T=T), grid=(nb, nq),
        in_specs=[col(512), col(512), col(32),
                  pl.BlockSpec((T, 256), lambda b, i: (b, 0)),
                  pl.BlockSpec((256, T), lambda b, i: (0, b)),
                  pl.BlockSpec((1, n_chunk, 128), lambda b, i: (b, 0, 0)),
                  pl.BlockSpec((1, 128, n_chunk), lambda b, i: (b, 0, 0)),
                  const(ovt)],
        out_specs=col(512),
        out_shape=jax.ShapeDtypeStruct((512, nb * T), bf16),
        scratch_shapes=[pltpu.VMEM((N_KV, NBLK_PAD, TQ), f32), pltpu.VMEM((NBLK_PAD, TQ), f32)],
        compiler_params=pltpu.CompilerParams(dimension_semantics=("parallel", "arbitrary"),
                                             vmem_limit_bytes=VMEM_LIMIT),
        name="nsa_prompt",
    )(qct, qrt, gt, kk, vt, kc, vct, ovt)


def _nsa_sample_body(pt_ref, *refs, n_pages, ts):
    n_pg = GB * n_pages
    pages = refs[:n_pg]
    (win_ref, qc_ref, qr_ref, sm_ref, kvn_ref, wn_ref, wk_ref, pek_ref, w2k_ref, wv_ref, pev_ref, w2v_ref,
     ovt_ref, e_ref, hb_ref, wout_ref, xk_scr, xv_scr, ks_scr, vs_scr, score_scr) = refs[n_pg:]
    P = n_pages * PAGE
    n_chunk = P // CMP_STRIDE
    n_blk = -(-(P + ts) // SEL_BLOCK)
    WB = win_ref.shape[4]
    for bi in range(GB):
        for p in range(n_pages):
            pg = pages[bi * n_pages + p]
            prow = pl.ds(bi * P + p * PAGE, PAGE)
            xk_scr[prow, :] = pg[0, 0, 0].T
            xv_scr[prow, :] = pg[0, 0, 1].T
            ks_scr[prow, :] = pg[0, 0, 2].T.astype(bf16)
            vs_scr[prow, :] = pg[0, 0, 3].T.astype(bf16)

    def flat(x_ref):
        return jnp.concatenate(
            [x_ref[pl.ds(l, GB * n_chunk, stride=CMP_STRIDE), :].astype(bf16) for l in range(CMP_STRIDE)], axis=1)

    kc_all = _compress_rows(flat(xk_scr), wk_ref, pek_ref, w2k_ref).astype(bf16)
    vc_all = _compress_rows(flat(xv_scr), wv_ref, pev_ref, w2v_ref).astype(bf16)

    qc_all = qc_ref[...].astype(f32)
    qr_all = qr_ref[...].astype(f32)
    gsig = jax.nn.sigmoid(sm_ref[...])
    rows = R * SEG
    qpos_col = P + lax.broadcasted_iota(jnp.int32, (rows, 1), 0) % SEG
    qpos_row = P + lax.broadcasted_iota(jnp.int32, (1, 128), 1) % SEG
    new_lane = lax.broadcasted_iota(jnp.int32, (1, 128), 1)
    zpad = jnp.zeros((128 - SEG, 128), f32)
    out_rows = []
    for bi in range(GB):
        rs8 = slice(bi * SEG, (bi + 1) * SEG)
        kc = kc_all[bi * n_chunk:(bi + 1) * n_chunk]
        vc = vc_all[bi * n_chunk:(bi + 1) * n_chunk]
        ksel = ks_scr[pl.ds(bi * P, P), :]
        vsel = vs_scr[pl.ds(bi * P, P), :]
        knew = jnp.concatenate([kvn_ref[rs8, 256:384], zpad], axis=0).astype(bf16)
        vnew = jnp.concatenate([kvn_ref[rs8, 384:512], zpad], axis=0).astype(bf16)
        kwnew_f = jnp.concatenate([wn_ref[rs8, 0:128], zpad], axis=0)
        vwnew_f = jnp.concatenate([wn_ref[rs8, 128:256], zpad], axis=0)
        kwnew, vwnew = kwnew_f.astype(bf16), vwnew_f.astype(bf16)
        kwin = win_ref[0, bi, 0].T.astype(bf16)
        vwin = win_ref[0, bi, 1].T.astype(bf16)
        heads = [None] * N_HEADS
        for g in range(N_KV):
            def stack(q_all):
                parts = []
                for r in range(R):
                    h = g * R + r
                    q = q_all[rs8, (h // 2) * 128:(h // 2 + 1) * 128]
                    if h % 2 != g:
                        q = pltpu.roll(q, HEAD_DIM, axis=1)
                    parts.append(jnp.where(new_lane // HEAD_DIM == g, q, 0.0))
                return jnp.concatenate(parts, axis=0).astype(bf16)
            qc = stack(qc_all)
            qr = stack(qr_all)
            s = lax.dot_general(qc, kc, NT, preferred_element_type=f32)
            cmp_end = lax.broadcasted_iota(jnp.int32, (1, n_chunk), 1) * CMP_STRIDE + (CMP_LEN - 1)
            p = _softmax_rows(s, cmp_end <= qpos_col)
            o_c = jnp.dot(p.astype(bf16), vc, preferred_element_type=f32)
            psum = p[0:SEG] + p[SEG:2 * SEG] + p[2 * SEG:3 * SEG] + p[3 * SEG:4 * SEG]
            psum = jnp.concatenate([psum, jnp.zeros((128 - SEG, n_chunk), f32)], axis=0)
            imp_t = lax.dot_general(ovt_ref[...], psum, NT, preferred_element_type=f32, precision=HIGHEST)
            sel_q = _transpose_sel(_select_blocks(imp_t, qpos_row, n_blk, score_scr))[0:SEG].astype(bf16)
            mk = jnp.concatenate([jnp.dot(sel_q, e_ref[...], preferred_element_type=f32)] * R, axis=0) > 0.5
            s_sel = jnp.concatenate([lax.dot_general(qr, ksel, NT, preferred_element_type=f32),
                                     lax.dot_general(qr, knew, NT, preferred_element_type=f32)], axis=1)
            kpos = jnp.concatenate([lax.broadcasted_iota(jnp.int32, (1, P), 1), P + new_lane], axis=1)
            valid = jnp.concatenate([jnp.full((1, P), True), new_lane < SEG], axis=1)
            p_s = _softmax_rows(s_sel, mk & (kpos <= qpos_col) & valid).astype(bf16)
            o_s = (jnp.dot(p_s[:, 0:P], vsel, preferred_element_type=f32)
                   + jnp.dot(p_s[:, P:P + 128], vnew, preferred_element_type=f32))
            s_w = jnp.concatenate([lax.dot_general(qr, kwin, NT, preferred_element_type=f32),
                                   lax.dot_general(qr, kwnew, NT, preferred_element_type=f32)], axis=1)
            kwpos = jnp.concatenate([P - WB + lax.broadcasted_iota(jnp.int32, (1, WB), 1), P + new_lane], axis=1)
            validw = jnp.concatenate([jnp.full((1, WB), True), new_lane < SEG], axis=1)
            dpos = qpos_col - kwpos
            p_w = _softmax_rows(s_w, (dpos >= 0) & (dpos < WINDOW) & (kwpos >= 0) & validw).astype(bf16)
            o_w = (jnp.dot(p_w[:, 0:WB], vwin, preferred_element_type=f32)
                   + jnp.dot(p_w[:, WB:WB + 128], vwnew, preferred_element_type=f32))
            for r in range(R):
                h = g * R + r
                rr = slice(r * SEG, (r + 1) * SEG)
                c0 = 8 + 3 * h
                gs = gsig[rs8]
                out = gs[:, c0:c0 + 1] * o_c[rr] + gs[:, c0 + 1:c0 + 2] * o_s[rr] + gs[:, c0 + 2:c0 + 3] * o_w[rr]
                if h % 2 != g:
                    out = pltpu.roll(out, HEAD_DIM, axis=1)
                heads[h] = out
        out_rows.append(jnp.concatenate(
            [jnp.where(new_lane < HEAD_DIM, heads[2 * j], heads[2 * j + 1]) for j in range(N_HEADS // 2)], axis=1))
        for kv, new_f in ((0, kwnew_f), (1, vwnew_f)):
            old = pltpu.roll(win_ref[0, bi, kv], WB - ts, axis=1)
            new_t = pltpu.roll(new_f.T, 128 - ts, axis=1)
            wout_ref[bi, kv, :, 0:WB - 128] = old[:, 0:WB - 128]
            wout_ref[bi, kv, :, WB - 128:WB] = jnp.where(new_lane >= 128 - ts, new_t, old[:, WB - 128:WB])
    hb_ref[...] = jnp.concatenate(out_rows, axis=0).astype(bf16)


def _nsa_sample(page_table, cache_t, cwin_t, qc, qr, sm, kvrow, winrow, cwk, cwv, *, layer, row0, ts):
    nseq, n_pages = page_table.shape
    P = n_pages * PAGE
    WB = cwin_t.shape[4]
    n_chunk = P // CMP_STRIDE
    ovt = _sel_overlap_t(n_chunk, -(-(P + ts) // SEL_BLOCK))
    e = _block_expand(P + 128)
    b0 = row0 // (GB * SEG)

    def page_map(i, pt, *, bi, p):
        return (layer, pt[i * GB + bi, p], 0, 0, 0)

    page_specs = [pl.BlockSpec((1, 1, 4, 128, PAGE), functools.partial(page_map, bi=bi, p=p))
                  for bi in range(GB) for p in range(n_pages)]
    row = lambda w: pl.BlockSpec((GB * SEG, w), lambda i, pt: (b0 + i, 0))
    const = lambda a: pl.BlockSpec(a.shape, lambda i, pt: (0,) * a.ndim)
    grid_spec = pltpu.PrefetchScalarGridSpec(
        num_scalar_prefetch=1, grid=(nseq // GB,),
        in_specs=page_specs + [pl.BlockSpec((1, GB, 2, 128, WB), lambda i, pt: (layer, i, 0, 0, 0)),
                               row(N_WIDTH), row(N_WIDTH), row(128), row(512), row(256)]
        + [const(a) for a in (*cwk, *cwv, ovt, e)],
        out_specs=[pl.BlockSpec((GB * SEG, N_WIDTH), lambda i, pt: (i, 0)),
                   pl.BlockSpec((GB, 2, 128, WB), lambda i, pt: (i, 0, 0, 0))],
        scratch_shapes=[pltpu.VMEM((GB * P, 128), f32), pltpu.VMEM((GB * P, 128), f32),
                        pltpu.VMEM((GB * P, 128), bf16), pltpu.VMEM((GB * P, 128), bf16),
                        pltpu.VMEM((NBLK_PAD, 128), f32)])
    return pl.pallas_call(
        functools.partial(_nsa_sample_body, n_pages=n_pages, ts=ts), grid_spec=grid_spec,
        out_shape=[jax.ShapeDtypeStruct((nseq * SEG, N_WIDTH), bf16), jax.ShapeDtypeStruct((nseq, 2, 128, WB), f32)],
        compiler_params=pltpu.CompilerParams(dimension_semantics=("parallel",), vmem_limit_bytes=VMEM_LIMIT),
        name="nsa_sample",
    )(page_table, *([cache_t] * (GB * n_pages)), cwin_t, qc, qr, sm, kvrow, winrow, *cwk, *cwv, ovt, e)


def kernel(x_prompt, x_sample, cache_kv_pages, page_table, cache_win, state_mlstm_C, state_mlstm_n, state_mlstm_m, state_mlstm_conv, norm1_g, w_in, b_gates, conv_w, conv_b, w_mq, w_mk, mlstm_norm_g, cmp_pe_k, cmp_pe_v, cmp_w1_k, cmp_w2_k, cmp_w1_v, cmp_w2_v, w_branch_a, w_branch_b, w_out, norm2_g, w_ffn_up, w_ffn_down, final_norm_g):
    Bp, Tp, _ = x_prompt.shape
    Bs, Ts, _ = x_sample.shape
    depth = w_in.shape[0]
    n_pool, page = cache_kv_pages.shape[1:3]
    n_pages = page_table.shape[1]
    past_len = n_pages * page
    wb_len = cache_win.shape[2]
    mp = Bp * Tp
    ms = Bs * SAMPLE_PAD
    assert page == PAGE and M_CONV - 1 <= Ts <= SAMPLE_PAD and Tp % MLSTM_CHUNK == 0 and Tp >= WINDOW + TQ
    assert mp % (SEG * SEQ_PER_STEP) == 0 and Bs % SEQ_PER_STEP == 0 and Bs % GB == 0

    pad_seq = lambda a: jnp.pad(a, ((0, 0), (0, SAMPLE_PAD - a.shape[1]), (0, 0)))
    x_all = jnp.concatenate([x_prompt.reshape(mp, D_MODEL), pad_seq(x_sample).reshape(ms, D_MODEL)], axis=0)
    tabs = _rope_tables(Tp, past_len, ROW_TILE * (ms // ROW_TILE))
    col_idx, col_scale = _in_proj_columns()
    cache_t = jnp.transpose(cache_kv_pages, (0, 1, 3, 4, 5, 2)).reshape(depth, n_pool, 4, KV_W, PAGE)
    cwin_t = jnp.transpose(cache_win, (0, 1, 3, 4, 5, 2)).reshape(depth, Bs, 2, KV_W, wb_len)

    outs = {k: [] for k in ("p_kv", "p_win", "p_C", "p_n", "p_m", "p_conv", "s_kv", "s_win", "s_C", "s_n", "s_m", "s_conv")}
    y_all = None
    for l in range(depth):
        w_ext = jnp.concatenate([w_in[l], jnp.zeros((D_MODEL, 1), f32)], axis=1)
        w_all = (w_ext[:, col_idx] * col_scale).astype(bf16)
        w_t = jnp.concatenate([w_in[l][:, O_IG:O_IG + 8], w_in[l][:, O_NG:O_NG + 24]], axis=1).T.astype(bf16)
        xm, sm, smt, qc, qr, qct, qrt, kvrow, winrow, kk, vt, gates = _in_proj(
            x_all, norm1_g[l].reshape(1, D_MODEL), w_all, w_t, tabs, Tp)

        wq, wk = w_mq[l].astype(bf16), w_mk[l].astype(bf16)
        ha_p, c_p, n_p, m_p, cv_p = _mlstm_prompt(xm, sm, smt, conv_w[l], conv_b[l], wq, wk, b_gates[l], mlstm_norm_g[l],
                                                   nb=Bp, T=Tp, L=MLSTM_CHUNK)
        tail = jnp.pad(state_mlstm_conv[l], ((0, 0), (0, SEG - (M_CONV - 1)), (0, 0))).reshape(ms, M_WIDTH)
        n0 = jnp.pad(state_mlstm_n[l], ((0, 0), (0, 8 - M_HEADS), (0, 0)))
        m0 = jnp.broadcast_to(jnp.pad(state_mlstm_m[l], ((0, 0), (0, 128 - M_HEADS)))[:, None, :], (Bs, SEG, 128))
        ha_s, c_s, n_s, mt_s = _mlstm_sample(xm, sm, smt, tail, state_mlstm_C[l], n0, m0.reshape(ms, 128), conv_w[l],
                                             conv_b[l], wq, wk, b_gates[l], mlstm_norm_g[l], row0=mp, nseq=Bs, n_valid=Ts)

        cwk = _compress_weights(cmp_pe_k[l], cmp_w1_k[l], cmp_w2_k[l])
        cwv = _compress_weights(cmp_pe_v[l], cmp_w1_v[l], cmp_w2_v[l])
        kc, vct = _compress_prompt(kvrow, cwk, cwv, nb=Bp, T=Tp)
        hbt_p = _nsa_prompt(qct, qrt, smt, kk, vt, kc, vct, nb=Bp, T=Tp)
        hb_s, win_s = _nsa_sample(page_table, cache_t, cwin_t, qc, qr, sm, kvrow, winrow, cwk, cwv,
                                  layer=l, row0=mp, ts=Ts)

        h_a = jnp.concatenate([ha_p, ha_s], axis=0)
        h_bt = jnp.concatenate([hbt_p, hb_s.T], axis=1)
        x_all, y_all = _mix_ffn(
            x_all, h_a, h_bt, gates, w_branch_a[l].astype(bf16), w_branch_b[l].astype(bf16), w_out[l].astype(bf16),
            norm2_g[l].reshape(1, D_MODEL), w_ffn_up[l].astype(bf16), w_ffn_down[l].astype(bf16),
            final_norm_g.reshape(1, D_MODEL))

        sample_rows = lambda a: a[mp:].reshape(Bs, SAMPLE_PAD, a.shape[1])
        wp = min(WINDOW, Tp)
        outs["p_kv"].append(kvrow[:mp].reshape(Bp, Tp, 4, N_KV, HEAD_DIM))
        outs["p_win"].append(winrow[:mp].reshape(Bp, Tp, 2 * KV_W)[:, -wp:].reshape(Bp, wp, 2, N_KV, HEAD_DIM))
        outs["p_C"].append(c_p)
        outs["p_n"].append(n_p[:, :M_HEADS])
        outs["p_m"].append(m_p[:, 0, :M_HEADS])
        outs["p_conv"].append(cv_p[:, 8 - (M_CONV - 1):])
        outs["s_kv"].append(sample_rows(kvrow)[:, :Ts].reshape(Bs, Ts, 4, N_KV, HEAD_DIM))
        outs["s_win"].append(jnp.transpose(win_s.reshape(Bs, 2, N_KV, HEAD_DIM, wb_len), (0, 4, 1, 2, 3)))
        outs["s_C"].append(c_s)
        outs["s_n"].append(n_s[:, :M_HEADS])
        outs["s_m"].append(mt_s.reshape(Bs, SEG, 128)[:, SEG - 1, :M_HEADS])
        outs["s_conv"].append(sample_rows(xm)[:, Ts - (M_CONV - 1):Ts, :M_WIDTH])

    y_prompt = y_all[:mp].reshape(Bp, Tp, D_MODEL)
    y_sample = y_all[mp:].reshape(Bs, SAMPLE_PAD, D_MODEL)[:, :Ts]
    st = lambda k: jnp.stack(outs[k])
    return (y_prompt, y_sample, st("p_kv"), st("p_win"), st("p_C"), st("p_n"), st("p_m"), st("p_conv"),
            st("s_kv"), st("s_win"), st("s_C"), st("s_n"), st("s_m"), st("s_conv"))
```

```python
import functools

import jax
import jax.numpy as jnp
import numpy as np
from jax import lax
from jax.experimental import pallas as pl
from jax.experimental.pallas import tpu as pltpu

f32 = jnp.float32
bf16 = jnp.bfloat16

D_MODEL = 1024
M_HEADS = 4
M_WIDTH = 512
M_HEAD_DIM = 128
M_CONV = 4
MLSTM_CHUNK = 256
HEAD_DIM = 64
N_WIDTH = 512
N_HEADS = 8
N_KV = 2
KV_W = 128
CMP_STRIDE = 16
CMP_LEN = 32
SEL_BLOCK = 64
SEL_TOP = 16
WINDOW = 512
ROT_DIM = 16
ROPE_THETA = 500000.0
D_FF = 2816
EPS = 1e-6
NEG = -1e30
TINY = 1e-30
FORCE_SCORE = 1e9

SAMPLE_PAD = 8
SEG = SAMPLE_PAD
SEQ_PER_STEP = 16
GB = 2
PAGE = 128
HD = M_HEAD_DIM
R = N_HEADS // N_KV
CMP_HID = 128
TQ = 128
TK = 512
NBLK_PAD = 64
HIGHEST = lax.Precision.HIGHEST
NT = (((1,), (1,)), ((), ()))
LOG2E = 1.4426950408889634
ROW_TILE = 256
FF_CHUNK = 256
VMEM_LIMIT = 56 * 1024 * 1024

C_XM, C_Q, C_KV, C_WIN, C_GATE, C_SM = 0, 1536, 2048, 2560, 2816, 4864
C_TOTAL = 4992
O_U, O_V, O_O, O_IG, O_FG, O_Q, O_KV, O_NG, O_GA, O_GB = 0, 512, 1024, 1536, 1540, 1544, 2056, 2824, 2848, 3872
IN_WIDTH = 4896


def _in_proj_columns():
    z = IN_WIDTH
    idx = list(range(O_U, O_IG))
    idx += list(range(O_Q, O_Q + N_WIDTH))
    idx += list(range(O_KV, O_KV + 768))
    idx += list(range(O_GA, O_GA + 2048))
    idx += list(range(O_IG, O_IG + 8)) + list(range(O_NG, O_NG + 24)) + [z] * 96
    assert len(idx) == C_TOTAL
    scale = np.ones((C_TOTAL,), np.float32)
    scale[C_Q:C_KV] = HEAD_DIM ** -0.5
    return np.asarray(idx, np.int32), scale


def _rope_tables(seq, past_len, n_sample_rows):
    half = ROT_DIM // 2
    inv = ROPE_THETA ** (-jnp.arange(half, dtype=f32) / half)
    pos = jnp.concatenate([jnp.arange(seq), past_len + (jnp.arange(n_sample_rows) % SAMPLE_PAD)]).astype(f32)
    ang = pos[:, None] * inv[None, :]
    cos8, sin8 = jnp.cos(ang), jnp.sin(ang)
    n = pos.shape[0]
    one = jnp.ones((n, 64 - ROT_DIM), f32)
    zero = jnp.zeros((n, 64 - ROT_DIM), f32)
    z8 = jnp.zeros((n, half), f32)
    cos = jnp.concatenate([cos8, cos8, one], axis=1)
    sa = jnp.concatenate([-sin8, z8, zero], axis=1)
    sb = jnp.concatenate([z8, sin8, zero], axis=1)
    tile2 = lambda a: jnp.concatenate([a, a], axis=1)
    return tile2(cos), tile2(sa), tile2(sb)


def _rms(x, g):
    return x * lax.rsqrt(jnp.mean(x * x, axis=-1, keepdims=True) + EPS) * g


def _in_proj_body(x_ref, g_ref, w_ref, wt_ref, cos_ref, sa_ref, sb_ref,
                  xm_ref, sm_ref, smt_ref, qc_ref, qr_ref, qct_ref, qrt_ref, kv_ref, win_ref, kk_ref, vt_ref, gate_ref):
    hb = _rms(x_ref[...], g_ref[...]).astype(bf16)
    cos, sa, sb = cos_ref[...], sa_ref[...], sb_ref[...]

    def rope(v):
        return v * cos + pltpu.roll(v, 128 - ROT_DIM // 2, axis=1) * sa + pltpu.roll(v, ROT_DIM // 2, axis=1) * sb

    def proj(c0, n):
        return jnp.dot(hb, w_ref[:, c0:c0 + n], preferred_element_type=f32)

    xm_ref[...] = proj(C_XM, 1536)
    sm_ref[...] = proj(C_SM, 128)
    smt_ref[...] = lax.dot_general(wt_ref[...], hb, NT, preferred_element_type=f32)
    qf = proj(C_Q, N_WIDTH) * LOG2E
    for j in range(N_HEADS // 2):
        sl = slice(j * 128, (j + 1) * 128)
        q = qf[:, sl]
        qrot = rope(q)
        qc_ref[:, sl] = q.astype(bf16)
        qr_ref[:, sl] = qrot.astype(bf16)
        qct_ref[sl, :] = q.T.astype(bf16)
        qrt_ref[sl, :] = qrot.T.astype(bf16)
    kv = proj(C_KV, 512)
    ksel = rope(kv[:, 256:384])
    kv_ref[:, 0:256] = kv[:, 0:256]
    kv_ref[:, 256:384] = ksel
    kv_ref[:, 384:512] = kv[:, 384:512]
    win = proj(C_WIN, 256)
    kwin = rope(win[:, 0:128])
    win_ref[:, 0:128] = kwin
    win_ref[:, 128:256] = win[:, 128:256]
    kk_ref[:, 0:128] = ksel.astype(bf16)
    kk_ref[:, 128:256] = kwin.astype(bf16)
    vt_ref[0:128, :] = kv[:, 384:512].T.astype(bf16)
    vt_ref[128:256, :] = win[:, 128:256].T.astype(bf16)
    gate_ref[...] = proj(C_GATE, 2048)


def _in_proj(x_all, norm_g, w_all, w_t, tabs, seq):
    m = x_all.shape[0]
    n_prompt_tiles_per_seq = seq // ROW_TILE
    n_prompt_tiles = (m - (tabs[0].shape[0] - seq)) // ROW_TILE

    def tab_map(i):
        return (jnp.where(i < n_prompt_tiles, i % n_prompt_tiles_per_seq, n_prompt_tiles_per_seq + i - n_prompt_tiles), 0)

    row = lambda w: pl.BlockSpec((ROW_TILE, w), lambda i: (i, 0))
    const = lambda s: pl.BlockSpec(s, lambda i: (0, 0), pipeline_mode=pl.Buffered(1))
    tab = pl.BlockSpec((ROW_TILE, 128), tab_map)
    widths = (1536, 128, -32, 512, 512, -512, -512, 512, 256, 256, -256, 2048)
    dts = (f32, f32, f32, bf16, bf16, bf16, bf16, f32, f32, bf16, bf16, f32)
    out_shape = [jax.ShapeDtypeStruct((-w, m) if w < 0 else (m, w), d) for w, d in zip(widths, dts)]
    out_specs = [pl.BlockSpec((-w, ROW_TILE), lambda i: (0, i)) if w < 0 else row(w) for w in widths]
    return pl.pallas_call(
        _in_proj_body, grid=(m // ROW_TILE,),
        in_specs=[row(D_MODEL), const((1, D_MODEL)), const((D_MODEL, C_TOTAL)), const((32, D_MODEL)), tab, tab, tab],
        out_specs=out_specs, out_shape=out_shape, name="in_proj",
        compiler_params=pltpu.CompilerParams(dimension_semantics=("parallel",), vmem_limit_bytes=VMEM_LIMIT),
    )(x_all, norm_g, w_all, w_t, *tabs)


def _mix_ffn_body(x_ref, ha_ref, hb_ref, gate_ref, wa_ref, wb_ref, wo_ref, g2_ref, wup_ref, wdn_ref, gf_ref,
                  xo_ref, *y_refs, n_prompt_tiles):
    a = jnp.dot(ha_ref[...].astype(bf16), wa_ref[...], preferred_element_type=f32)
    b = lax.dot_general(hb_ref[...], wb_ref[...], (((0,), (0,)), ((), ())), preferred_element_type=f32)
    merged = jax.nn.sigmoid(gate_ref[:, 0:D_MODEL]) * a + jax.nn.sigmoid(gate_ref[:, D_MODEL:2 * D_MODEL]) * b
    x1 = x_ref[...] + jnp.dot(merged.astype(bf16), wo_ref[...], preferred_element_type=f32)
    hn = _rms(x1, g2_ref[...]).astype(bf16)
    acc = x1
    for c in range(D_FF // FF_CHUNK):
        lo = c * FF_CHUNK
        g = jnp.dot(hn, wup_ref[:, lo:lo + FF_CHUNK], preferred_element_type=f32)
        u = jnp.dot(hn, wup_ref[:, D_FF + lo:D_FF + lo + FF_CHUNK], preferred_element_type=f32)
        act = (g * jax.nn.sigmoid(g) * u).astype(bf16)
        acc = acc + jnp.dot(act, wdn_ref[lo:lo + FF_CHUNK, :], preferred_element_type=f32)
    xo_ref[...] = acc
    if y_refs:
        y = _rms(acc, gf_ref[...])
        yp_ref, ys_ref = y_refs
        i = pl.program_id(0)

        @pl.when(i < n_prompt_tiles)
        def _():
            yp_ref[...] = y

        @pl.when(i >= n_prompt_tiles)
        def _():
            ys_ref[...] = y


def _mix_ffn(x_all, h_a, h_b, gates, wa, wb, wo, g2, wup, wdn, gf, *, n_prompt_rows, final):
    m = x_all.shape[0]
    npt = n_prompt_rows // ROW_TILE
    row = lambda w: pl.BlockSpec((ROW_TILE, w), lambda i: (i, 0))
    const = lambda s: pl.BlockSpec(s, lambda i: (0, 0), pipeline_mode=pl.Buffered(1))
    out_specs = [row(D_MODEL)]
    out_shape = [jax.ShapeDtypeStruct((m, D_MODEL), f32)]
    if final:
        out_specs += [pl.BlockSpec((ROW_TILE, D_MODEL), lambda i: (jnp.minimum(i, npt - 1), 0)),
                      pl.BlockSpec((ROW_TILE, D_MODEL), lambda i: (jnp.maximum(i - npt, 0), 0))]
        out_shape += [jax.ShapeDtypeStruct((n_prompt_rows, D_MODEL), f32),
                      jax.ShapeDtypeStruct((m - n_prompt_rows, D_MODEL), f32)]
    return pl.pallas_call(
        functools.partial(_mix_ffn_body, n_prompt_tiles=npt), grid=(m // ROW_TILE,),
        in_specs=[row(D_MODEL), row(h_a.shape[1]), pl.BlockSpec((h_b.shape[0], ROW_TILE), lambda i: (0, i)),
                  row(2 * D_MODEL),
                  const(wa.shape), const(wb.shape), const(wo.shape), const((1, D_MODEL)),
                  const(wup.shape), const(wdn.shape), const((1, D_MODEL))],
        out_specs=out_specs, out_shape=out_shape, name="mix_ffn",
        compiler_params=pltpu.CompilerParams(dimension_semantics=("arbitrary",), vmem_limit_bytes=VMEM_LIMIT),
    )(x_all, h_a, h_b, gates, wa, wb, wo, g2, wup, wdn, gf)


def _log_sigmoid(x):
    return jnp.minimum(x, 0.0) - jnp.log1p(jnp.exp(-jnp.abs(x)))


def _seg_scan(x, axis, seg, op, fill):
    idx = lax.broadcasted_iota(jnp.int32, x.shape, axis) % seg
    d = 1
    while d < seg:
        x = op(x, jnp.where(idx >= d, pltpu.roll(x, d, axis=axis), fill))
        d *= 2
    return x


def _conv_silu(u, tail, conv_w, conv_b, seg):
    L = u.shape[0]
    row = lax.broadcasted_iota(jnp.int32, u.shape, 0) % seg
    y = conv_b + u * conv_w[3:4, :]
    for k in (1, 2, 3):
        prev = tail if k == 3 else pltpu.roll(tail, L - (3 - k), axis=0)
        sh = jnp.where(row < k, prev, pltpu.roll(u, k, axis=0))
        y = y + sh * conv_w[3 - k:4 - k, :]
    return y * jax.nn.sigmoid(y)


def _gate_columns(sm, smt, bg_row, bg_col, m_vec, seg, n_valid):
    G = sm + bg_row
    lf = _log_sigmoid(G)
    ipre = G
    if n_valid < seg:
        rowc = lax.broadcasted_iota(jnp.int32, G.shape, 0) % seg
        lf = jnp.where(rowc < n_valid, lf, 0.0)
        ipre = jnp.where(rowc < n_valid, G, NEG)
    F = _seg_scan(lf, 0, seg, jnp.add, 0.0)
    F_al = pltpu.roll(F, 128 - M_HEADS, axis=1)
    a_col = ipre - F_al
    g_col = jnp.maximum(m_vec, _seg_scan(a_col, 0, seg, jnp.maximum, NEG))
    mt_col = F_al + g_col
    Gr = smt + bg_col
    lfr = _log_sigmoid(Gr)
    ir = Gr
    if n_valid < seg:
        lanec = lax.broadcasted_iota(jnp.int32, Gr.shape, 1) % seg
        lfr = jnp.where(lanec < n_valid, lfr, 0.0)
        ir = jnp.where(lanec < n_valid, Gr, NEG)
    Fr = _seg_scan(lfr, 1, seg, jnp.add, 0.0)
    a_row = ir - pltpu.roll(Fr, M_HEADS, axis=0)
    return a_col, g_col, mt_col, a_row


def _head_norm_gate(hh, ng, o):
    return hh * lax.rsqrt(jnp.mean(hh * hh, axis=-1, keepdims=True) + EPS) * ng * jax.nn.sigmoid(o)


def _mlstm_prompt_body(xm_ref, sm_ref, smt_ref, cw_ref, cb_ref, wq_ref, wk_ref, bgr_ref, bgc_ref, ng_ref,
                       ha_ref, c_out, n_out, m_out, conv_out, c_scr, n_scr, m_scr, tail_scr, *, L):
    c_idx = pl.program_id(1)

    @pl.when(c_idx == 0)
    def _():
        c_scr[...] = jnp.zeros_like(c_scr)
        n_scr[...] = jnp.zeros_like(n_scr)
        m_scr[...] = jnp.zeros_like(m_scr)
        tail_scr[...] = jnp.zeros_like(tail_scr)

    u = xm_ref[:, 0:M_WIDTH]
    tail = jnp.concatenate([tail_scr[...], jnp.zeros((L - 8, M_WIDTH), f32)], axis=0)
    cact = _conv_silu(u, tail, cw_ref[...], cb_ref[...], L)
    tail_scr[...] = pltpu.roll(u[L - 8:L, :], 3, axis=0)
    conv_out[0] = u[L - 8:L, :]

    m_vec = m_scr[0:1, :]
    a_col, g_col, mt_col, a_row = _gate_columns(sm_ref[...], smt_ref[...], bgr_ref[...], bgc_ref[...], m_vec, L, L)
    inter_col = jnp.exp(m_vec - g_col)
    floor_col = jnp.exp(-mt_col)
    g_last = g_col[L - 1:L, :]
    w_col = jnp.exp(a_col - g_last)
    decay = jnp.exp(m_vec - g_last)
    m_new = mt_col[L - 1:L, :]
    ti = lax.broadcasted_iota(jnp.int32, (L, L), 0)
    si = lax.broadcasted_iota(jnp.int32, (L, L), 1)
    causal = si <= ti
    scale = HD ** -0.5
    for h in range(M_HEADS):
        sl = slice(h * HD, (h + 1) * HD)
        ch = cact[:, sl].astype(bf16)
        qh = jnp.dot(ch, wq_ref[h], preferred_element_type=f32)
        kh = jnp.dot(ch, wk_ref[h], preferred_element_type=f32) * scale
        vh = xm_ref[:, M_WIDTH + h * HD:M_WIDTH + (h + 1) * HD].astype(bf16)
        qb = qh.astype(bf16)
        D = jnp.exp(jnp.where(causal, a_row[h:h + 1, :] - g_col[:, h:h + 1], NEG))
        S = lax.dot_general(qb, kh.astype(bf16), NT, preferred_element_type=f32) * D
        C = c_scr[h]
        nrow = n_scr[h:h + 1, :]
        ic = inter_col[:, h:h + 1]
        num = jnp.dot(S.astype(bf16), vh, preferred_element_type=f32) + ic * jnp.dot(
            qb, C.astype(bf16), preferred_element_type=f32)
        den = jnp.sum(S, axis=-1, keepdims=True) + ic * jnp.sum(qh * nrow, axis=-1, keepdims=True)
        hh = num / jnp.maximum(jnp.abs(den), floor_col[:, h:h + 1])
        o = xm_ref[:, 2 * M_WIDTH + h * HD:2 * M_WIDTH + (h + 1) * HD]
        ha_ref[:, sl] = _head_norm_gate(hh, ng_ref[:, sl], o)
        kw = kh * w_col[:, h:h + 1]
        dh = decay[:, h:h + 1]
        c_scr[h] = dh * C + lax.dot_general(kw.astype(bf16), vh, (((0,), (0,)), ((), ())), preferred_element_type=f32)
        n_scr[h:h + 1, :] = dh * nrow + jnp.sum(kw, axis=0, keepdims=True)
    m_scr[0:1, :] = m_new

    @pl.when(c_idx == pl.num_programs(1) - 1)
    def _():
        c_out[0] = c_scr[...]
        n_out[0] = n_scr[...]
        m_out[0] = m_scr[...]


def _mlstm_prompt(xm, sm, smt, conv_w, conv_b, wq, wk, b_gates, norm_g, *, nb, T, L):
    nc = T // L
    bg_row = jnp.zeros((1, 128), f32).at[0, :8].set(b_gates)
    bg_col = b_gates.reshape(8, 1)
    const = lambda s: pl.BlockSpec(s, lambda b, c: (0,) * len(s))
    return pl.pallas_call(
        functools.partial(_mlstm_prompt_body, L=L), grid=(nb, nc),
        in_specs=[pl.BlockSpec((L, 3 * M_WIDTH), lambda b, c: (b * nc + c, 0)),
                  pl.BlockSpec((L, 128), lambda b, c: (b * nc + c, 0)),
                  pl.BlockSpec((8, L), lambda b, c: (0, b * nc + c)),
                  const((4, M_WIDTH)), const((1, M_WIDTH)), const((M_HEADS, HD, HD)), const((M_HEADS, HD, HD)),
                  const((1, 128)), const((8, 1)), const((1, M_WIDTH))],
        out_specs=[pl.BlockSpec((L, M_WIDTH), lambda b, c: (b * nc + c, 0)),
                   pl.BlockSpec((1, M_HEADS, HD, HD), lambda b, c: (b, 0, 0, 0)),
                   pl.BlockSpec((1, 8, HD), lambda b, c: (b, 0, 0)),
                   pl.BlockSpec((1, 8, 128), lambda b, c: (b, 0, 0)),
                   pl.BlockSpec((1, 8, M_WIDTH), lambda b, c: (b, 0, 0))],
        out_shape=[jax.ShapeDtypeStruct((nb * T, M_WIDTH), f32),
                   jax.ShapeDtypeStruct((nb, M_HEADS, HD, HD), f32),
                   jax.ShapeDtypeStruct((nb, 8, HD), f32),
                   jax.ShapeDtypeStruct((nb, 8, 128), f32),
                   jax.ShapeDtypeStruct((nb, 8, M_WIDTH), f32)],
        scratch_shapes=[pltpu.VMEM((M_HEADS, HD, HD), f32), pltpu.VMEM((8, HD), f32), pltpu.VMEM((8, 128), f32),
                        pltpu.VMEM((8, M_WIDTH), f32)],
        compiler_params=pltpu.CompilerParams(dimension_semantics=("parallel", "arbitrary")),
        name="mlstm_prompt",
    )(xm, sm, smt, conv_w, conv_b.reshape(1, M_WIDTH), wq, wk, bg_row, bg_col, norm_g.reshape(1, M_WIDTH))


def _mlstm_sample_body(xm_ref, sm_ref, smt_ref, tail_ref, c_in, n_in, m_in, cw_ref, cb_ref, wq_ref, wk_ref,
                       bgr_ref, bgc_ref, ng_ref, *out_refs, n_valid):
    ha_ref, c_stack, n_out, m_out = out_refs[-4:]
    c_out = c_stack.at[0]
    L = SEG * SEQ_PER_STEP
    u = xm_ref[:, 0:M_WIDTH]
    cact = _conv_silu(u, tail_ref[...], cw_ref[...], cb_ref[...], SEG)
    m_rows = m_in[...]
    a_col, g_col, mt_col, a_row = _gate_columns(sm_ref[...], smt_ref[...], bgr_ref[...], bgc_ref[...], m_rows, SEG,
                                                n_valid)
    inter_col = jnp.exp(m_rows - g_col)
    floor_col = jnp.exp(-mt_col)
    ti = lax.broadcasted_iota(jnp.int32, (L, L), 0)
    si = lax.broadcasted_iota(jnp.int32, (L, L), 1)
    mask = (si <= ti) & ((si // SEG) == (ti // SEG))
    scale = HD ** -0.5
    for h in range(M_HEADS):
        sl = slice(h * HD, (h + 1) * HD)
        ch = cact[:, sl].astype(bf16)
        qh = jnp.dot(ch, wq_ref[h], preferred_element_type=f32)
        kh = jnp.dot(ch, wk_ref[h], preferred_element_type=f32) * scale
        vh = xm_ref[:, M_WIDTH + h * HD:M_WIDTH + (h + 1) * HD].astype(bf16)
        qb = qh.astype(bf16)
        D = jnp.exp(jnp.where(mask, a_row[h:h + 1, :] - g_col[:, h:h + 1], NEG))
        S = lax.dot_general(qb, kh.astype(bf16), NT, preferred_element_type=f32) * D
        num = jnp.dot(S.astype(bf16), vh, preferred_element_type=f32)
        den = jnp.sum(S, axis=-1, keepdims=True)
        inter_num, inter_den = [], []
        for s in range(SEQ_PER_STEP):
            rs = slice(s * SEG, (s + 1) * SEG)
            C = c_in[s, h]
            nrow = n_in[s, h:h + 1, :]
            inter_num.append(jnp.dot(qb[rs], C.astype(bf16), preferred_element_type=f32))
            inter_den.append(jnp.sum(qh[rs] * nrow, axis=-1, keepdims=True))
            g_last = g_col[s * SEG + SEG - 1:s * SEG + SEG, h:h + 1]
            m_prev = m_rows[s * SEG:s * SEG + 1, h:h + 1]
            w = jnp.exp(a_col[rs, h:h + 1] - g_last)
            dh = jnp.exp(m_prev - g_last)
            kw = kh[rs] * w
            c_out[s, h] = dh * C + lax.dot_general(kw.astype(bf16), vh[rs], (((0,), (0,)), ((), ())),
                                                    preferred_element_type=f32)
            n_out[s, h:h + 1, :] = dh * nrow + jnp.sum(kw, axis=0, keepdims=True)
        ic = inter_col[:, h:h + 1]
        num = num + ic * jnp.concatenate(inter_num, axis=0)
        den = den + ic * jnp.concatenate(inter_den, axis=0)
        hh = num / jnp.maximum(jnp.abs(den), floor_col[:, h:h + 1])
        o = xm_ref[:, 2 * M_WIDTH + h * HD:2 * M_WIDTH + (h + 1) * HD]
        ha_ref[:, sl] = _head_norm_gate(hh, ng_ref[:, sl], o)
    for s in range(SEQ_PER_STEP):
        n_out[s, M_HEADS:8, :] = jnp.zeros((8 - M_HEADS, HD), f32)
    m_out[...] = mt_col


def _stacked_out(prev, depth, shape):
    extra_in = [] if prev is None else [prev]
    extra_spec = [] if prev is None else [pl.BlockSpec(memory_space=pl.ANY)]
    return extra_in, extra_spec, jax.ShapeDtypeStruct((depth,) + shape, f32)


def _mlstm_sample(xm, sm, smt, tail, c0, n0, m0, conv_w, conv_b, wq, wk, b_gates, norm_g, c_stack, *, layer, depth,
                  row0, nseq, n_valid):
    L = SEG * SEQ_PER_STEP
    b0 = row0 // L
    bg_row = jnp.zeros((1, 128), f32).at[0, :8].set(b_gates)
    bg_col = b_gates.reshape(8, 1)
    const = lambda s: pl.BlockSpec(s, lambda i: (0,) * len(s))
    extra_in, extra_spec, c_shape = _stacked_out(c_stack, depth, (nseq, M_HEADS, HD, HD))
    n_in = 14
    return pl.pallas_call(
        functools.partial(_mlstm_sample_body, n_valid=n_valid), grid=(nseq // SEQ_PER_STEP,),
        in_specs=[pl.BlockSpec((L, 3 * M_WIDTH), lambda i: (b0 + i, 0)),
                  pl.BlockSpec((L, 128), lambda i: (b0 + i, 0)),
                  pl.BlockSpec((8, L), lambda i: (0, b0 + i)),
                  pl.BlockSpec((L, M_WIDTH), lambda i: (i, 0)),
                  pl.BlockSpec((SEQ_PER_STEP, M_HEADS, HD, HD), lambda i: (i, 0, 0, 0)),
                  pl.BlockSpec((SEQ_PER_STEP, 8, HD), lambda i: (i, 0, 0)),
                  pl.BlockSpec((L, 128), lambda i: (i, 0)),
                  const((4, M_WIDTH)), const((1, M_WIDTH)), const((M_HEADS, HD, HD)), const((M_HEADS, HD, HD)),
                  const((1, 128)), const((8, 1)), const((1, M_WIDTH))] + extra_spec,
        out_specs=[pl.BlockSpec((L, M_WIDTH), lambda i: (i, 0)),
                   pl.BlockSpec((1, SEQ_PER_STEP, M_HEADS, HD, HD), lambda i: (layer, i, 0, 0, 0)),
                   pl.BlockSpec((SEQ_PER_STEP, 8, HD), lambda i: (i, 0, 0)),
                   pl.BlockSpec((L, 128), lambda i: (i, 0))],
        out_shape=[jax.ShapeDtypeStruct((nseq * SEG, M_WIDTH), f32), c_shape,
                   jax.ShapeDtypeStruct((nseq, 8, HD), f32),
                   jax.ShapeDtypeStruct((nseq * SEG, 128), f32)],
        input_output_aliases={n_in: 1} if extra_in else {},
        compiler_params=pltpu.CompilerParams(dimension_semantics=("parallel",), vmem_limit_bytes=VMEM_LIMIT),
        name="mlstm_sample",
    )(xm, sm, smt, tail, c0, n0, m0, conv_w, conv_b.reshape(1, M_WIDTH), wq, wk, bg_row, bg_col,
      norm_g.reshape(1, M_WIDTH), *extra_in)


def _sel_overlap_t(n_chunk, n_blk):
    n_cmp = n_chunk - 1
    start = np.arange(n_cmp) * CMP_STRIDE
    bs = np.arange(n_blk) * SEL_BLOCK
    ov = np.minimum(start[:, None] + CMP_LEN, bs[None, :] + SEL_BLOCK) - np.maximum(start[:, None], bs[None, :])
    ov = np.clip(ov, 0, None) / CMP_LEN
    out = np.zeros((NBLK_PAD, n_chunk), np.float32)
    out[:n_blk, :n_cmp] = ov.T
    return jnp.asarray(out)


def _block_expand(n_keys):
    e = (np.arange(n_keys)[None, :] // SEL_BLOCK) == np.arange(128)[:, None]
    return jnp.asarray(e, dtype=bf16)


def _compress_weights(pe, w1, w2):
    W = jnp.zeros((16, 2, 64, 2, 2, CMP_HID), f32)
    for g in range(N_KV):
        W = W.at[:, g, :, g, 0, :].set(w1[:16])
        W = W.at[:, g, :, g, 1, :].set(w1[16:])
    W = W.reshape(2048, 512)
    peA = jnp.broadcast_to(pe[:16, None, :], (16, 2, 64)).reshape(1, 2048)
    peB = jnp.broadcast_to(pe[16:, None, :], (16, 2, 64)).reshape(1, 2048)
    PE = jnp.concatenate([peA, peB, jnp.zeros((14, 2048), f32)], axis=0)
    W2 = jnp.zeros((2, CMP_HID, 128), f32).at[0, :, 0:64].set(w2).at[1, :, 64:128].set(w2)
    return W.astype(bf16), PE.astype(bf16), W2.astype(bf16)


def _compress_rows(flat, w_ref, pe_ref, w2_ref):
    n = flat.shape[0]
    y = jnp.dot(jnp.concatenate([flat, pe_ref[...]], axis=0), w_ref[...], preferred_element_type=f32)
    c = y[n:n + 8]
    out = jnp.zeros((n, w2_ref.shape[2]), f32)
    for g in range(N_KV):
        a = y[0:n, (2 * g) * CMP_HID:(2 * g + 1) * CMP_HID] + c[0:1, (2 * g) * CMP_HID:(2 * g + 1) * CMP_HID]
        b = (y[0:n, (2 * g + 1) * CMP_HID:(2 * g + 2) * CMP_HID]
             + c[1:2, (2 * g + 1) * CMP_HID:(2 * g + 2) * CMP_HID])
        hid = a + pltpu.roll(b, n - 1, axis=0)
        out = out + jnp.dot(jax.nn.gelu(hid, approximate=True).astype(bf16), w2_ref[g], preferred_element_type=f32)
    return out


def _compress_prompt_body(xk_ref, xv_ref, wk_ref, pek_ref, w2k_ref, wv_ref, pev_ref, w2v_ref, kc_ref, vc_ref, *,
                          n_chunk):
    def flat(x_ref):
        return jnp.concatenate(
            [x_ref[pl.ds(l, n_chunk, stride=CMP_STRIDE), :].astype(bf16) for l in range(CMP_STRIDE)], axis=1)
    kc_ref[0] = _compress_rows(flat(xk_ref), wk_ref, pek_ref, w2k_ref).astype(bf16)
    vc_ref[0] = _compress_rows(flat(xv_ref), wv_ref, pev_ref, w2v_ref).T.astype(bf16)


def _compress_prompt(kvrow, cwk, cwv, *, nb, T):
    n_chunk = T // CMP_STRIDE
    const = lambda a: pl.BlockSpec(a.shape, lambda b: (0,) * a.ndim)
    return pl.pallas_call(
        functools.partial(_compress_prompt_body, n_chunk=n_chunk), grid=(nb,),
        in_specs=[pl.BlockSpec((T, 128), lambda b: (b, 0)), pl.BlockSpec((T, 128), lambda b: (b, 1))]
        + [const(a) for a in (*cwk, *cwv)],
        out_specs=[pl.BlockSpec((1, n_chunk, 128), lambda b: (b, 0, 0)),
                   pl.BlockSpec((1, 128, n_chunk), lambda b: (b, 0, 0))],
        out_shape=[jax.ShapeDtypeStruct((nb, n_chunk, 128), bf16), jax.ShapeDtypeStruct((nb, 128, n_chunk), bf16)],
        compiler_params=pltpu.CompilerParams(dimension_semantics=("parallel",), vmem_limit_bytes=VMEM_LIMIT),
        name="nsa_compress",
    )(kvrow, kvrow, *cwk, *cwv)


def _softmax_rows(s, mask):
    s = jnp.where(mask, s, NEG)
    e = jnp.exp2(s - jnp.max(s, axis=-1, keepdims=True))
    e = jnp.where(mask, e, 0.0)
    return e / jnp.maximum(jnp.sum(e, axis=-1, keepdims=True), TINY)


def _select_blocks(imp_t, qpos_row, n_blk):
    n = imp_t.shape[1]
    j = lax.broadcasted_iota(jnp.int32, imp_t.shape, 0)
    cur = qpos_row // SEL_BLOCK
    forced = (j == 0) | (j == cur) | (j == cur - 1)
    score = jnp.where(j <= cur, jnp.where(forced, FORCE_SCORE, imp_t), -1.0)
    n_tiles = -(-n_blk // 8)
    sub = lax.broadcasted_iota(jnp.int32, (8, n), 0)
    tiles = [score[8 * v:8 * v + 8] for v in range(n_tiles)]
    if n_blk % 8:
        tiles[-1] = jnp.where(sub < n_blk % 8, tiles[-1], -2.0)
    cnts = [jnp.zeros((8, n), f32)] * n_tiles
    for jp in range(n_blk):
        sj = tiles[jp // 8][jp % 8:jp % 8 + 1, :]
        for v in range(n_tiles):
            if v > jp // 8:
                beat = jnp.where(sj >= tiles[v], 1.0, 0.0)
            elif v < jp // 8:
                beat = jnp.where(sj > tiles[v], 1.0, 0.0)
            else:
                beat = jnp.where(sub > jp % 8, jnp.where(sj >= tiles[v], 1.0, 0.0), jnp.where(sj > tiles[v], 1.0, 0.0))
            cnts[v] = cnts[v] + beat
    keep = [jnp.where(c < float(min(SEL_TOP, n_blk)), 1.0, 0.0) for c in cnts]
    if n_blk % 8:
        keep[-1] = jnp.where(sub < n_blk % 8, keep[-1], 0.0)
    return jnp.concatenate(keep + [jnp.zeros((NBLK_PAD - 8 * n_tiles, n), f32)] * (NBLK_PAD > 8 * n_tiles), axis=0)


def _transpose_sel(sel_t):
    return jnp.concatenate([sel_t, jnp.zeros((128 - NBLK_PAD, sel_t.shape[1]), f32)], axis=0).T


def _softmax_cols(s, mask):
    s = jnp.where(mask, s, NEG)
    e = jnp.exp2(s - jnp.max(s, axis=0, keepdims=True))
    e = jnp.where(mask, e, 0.0)
    return e * (1.0 / jnp.maximum(jnp.sum(e, axis=0, keepdims=True), TINY))


def _lane_tile(x, n):
    return jnp.concatenate([x] * n, axis=1)


def _nsa_prompt_body(qct_ref, qrt_ref, gt_ref, kk_ref, vt_ref, kc_ref, vct_ref, ovt_ref, hbt_ref, sel_scr, *, T):
    i = pl.program_id(1)
    n_chunk = T // CMP_STRIDE
    n_blk = T // SEL_BLOCK
    q0 = i * TQ
    qpos_row = q0 + lax.broadcasted_iota(jnp.int32, (1, TQ), 1)
    gsig = jax.nn.sigmoid(gt_ref[...])
    n_kt = (q0 + TQ + TK - 1) // TK
    w0 = jnp.maximum(i - WINDOW // TQ, 0) * TQ
    WK = WINDOW + TQ
    blk_per_tile = TK // SEL_BLOCK
    ones_lhs = jnp.ones((16, TK), bf16)
    zeros_q = jnp.zeros((HEAD_DIM, TQ), bf16)

    def group_queries(ref, g):
        cols = []
        for r in range(R):
            q = ref[(g * R + r) * HEAD_DIM:(g * R + r + 1) * HEAD_DIM, :]
            cols.append(jnp.concatenate([q, zeros_q] if g == 0 else [zeros_q, q], axis=0))
        return jnp.concatenate(cols, axis=1)

    qrts, o_cs = [], []
    for g in range(N_KV):
        qct = group_queries(qct_ref, g)
        qrts.append(group_queries(qrt_ref, g))
        s = jnp.dot(kc_ref[0], qct, preferred_element_type=f32)
        cmp_end = lax.broadcasted_iota(jnp.int32, (n_chunk, 1), 0) * CMP_STRIDE + (CMP_LEN - 1)
        p = _softmax_cols(s, _lane_tile(cmp_end <= qpos_row, R))
        o_cs.append(jnp.dot(vct_ref[0], p.astype(bf16), preferred_element_type=f32))
        psum = p[:, 0:TQ] + p[:, TQ:2 * TQ] + p[:, 2 * TQ:3 * TQ] + p[:, 3 * TQ:4 * TQ]
        imp_t = jnp.dot(ovt_ref[...], psum, preferred_element_type=f32, precision=HIGHEST)
        sel_scr[g] = jnp.where(_select_blocks(imp_t, qpos_row, n_blk) > 0.5, 0.0, NEG)

    def kv_step(kt, carry):
        k0 = pl.multiple_of(kt * TK, TK)
        b0 = pl.multiple_of(kt * blk_per_tile, blk_per_tile)
        causal = (k0 + lax.broadcasted_iota(jnp.int32, (TK, 1), 0)) <= qpos_row
        out = []
        for g in range(N_KV):
            m_i, acc = carry[g]
            sc = jnp.dot(kk_ref[pl.ds(k0, TK), 0:128], qrts[g], preferred_element_type=f32)
            sel8 = sel_scr[g, pl.ds(b0, blk_per_tile), :]
            bias = jnp.broadcast_to(sel8[:, None, :], (blk_per_tile, SEL_BLOCK, TQ)).reshape(TK, TQ)
            sc = sc + _lane_tile(jnp.where(causal, bias, NEG), R)
            m_new = jnp.maximum(m_i, jnp.max(sc, axis=0, keepdims=True))
            alpha = jnp.exp2(m_i - m_new)
            pp = jnp.exp2(sc - m_new).astype(bf16)
            v_ext = jnp.concatenate([vt_ref[0:128, pl.ds(k0, TK)], ones_lhs], axis=0)
            acc = alpha * acc + jnp.dot(v_ext, pp, preferred_element_type=f32)
            out.append((m_new, acc))
        return tuple(out)

    init = (jnp.full((1, R * TQ), NEG, f32), jnp.zeros((128 + 16, R * TQ), f32))
    sel_out = lax.fori_loop(0, n_kt, kv_step, (init,) * N_KV)

    ones_w = jnp.ones((16, WINDOW + TQ), bf16)
    for g in range(N_KV):
        qrt, o_c = qrts[g], o_cs[g]
        acc = sel_out[g][1]
        o_s = acc[0:128] * (1.0 / acc[128:129])
        w0a = pl.multiple_of(w0, TQ)
        sw = jnp.dot(kk_ref[pl.ds(w0a, WK), 128:256], qrt, preferred_element_type=f32)
        dpos = qpos_row - (w0 + lax.broadcasted_iota(jnp.int32, (WK, 1), 0))
        biasw = jnp.where((dpos >= 0) & (dpos < WINDOW), 0.0, NEG)
        sw = sw + _lane_tile(biasw, R)
        pw = jnp.exp2(sw - jnp.max(sw, axis=0, keepdims=True)).astype(bf16)
        ow = jnp.dot(jnp.concatenate([vt_ref[128:256, pl.ds(w0a, WK)], ones_w], axis=0), pw,
                     preferred_element_type=f32)
        o_w = ow[0:128] * (1.0 / ow[128:129])
        rows = slice(g * HEAD_DIM, (g + 1) * HEAD_DIM)
        for r in range(R):
            h = g * R + r
            cs = slice(r * TQ, (r + 1) * TQ)
            c0 = 8 + 3 * h
            out = (gsig[c0:c0 + 1, :] * o_c[rows, cs] + gsig[c0 + 1:c0 + 2, :] * o_s[rows, cs]
                   + gsig[c0 + 2:c0 + 3, :] * o_w[rows, cs])
            hbt_ref[h * 64:(h + 1) * 64, :] = out.astype(bf16)


def _nsa_prompt(qct, qrt, gt, kk, vt, kc, vct, *, nb, T):
    nq = T // TQ
    n_chunk = T // CMP_STRIDE
    ovt = _sel_overlap_t(n_chunk, T // SEL_BLOCK)
    col = lambda h: pl.BlockSpec((h, TQ), lambda b, i: (0, b * nq + i))
    const = lambda a: pl.BlockSpec(a.shape, lambda b, i: (0,) * a.ndim)
    return pl.pallas_call(
        functools.partial(_nsa_prompt_body, T=T), grid=(nb, nq),
        in_specs=[col(512), col(512), col(32),
                  pl.BlockSpec((T, 256), lambda b, i: (b, 0)),
                  pl.BlockSpec((256, T), lambda b, i: (0, b)),
                  pl.BlockSpec((1, n_chunk, 128), lambda b, i: (b, 0, 0)),
                  pl.BlockSpec((1, 128, n_chunk), lambda b, i: (b, 0, 0)),
                  const(ovt)],
        out_specs=col(512),
        out_shape=jax.ShapeDtypeStruct((512, nb * T), bf16),
        scratch_shapes=[pltpu.VMEM((N_KV, NBLK_PAD, TQ), f32)],
        compiler_params=pltpu.CompilerParams(dimension_semantics=("parallel", "arbitrary"),
                                             vmem_limit_bytes=VMEM_LIMIT),
        name="nsa_prompt",
    )(qct, qrt, gt, kk, vt, kc, vct, ovt)


def _nsa_sample_body(pt_ref, *refs, n_pages, ts):
    n_pg = GB * n_pages
    pages = refs[:n_pg]
    (win_ref, qc_ref, qr_ref, sm_ref, kvn_ref, wn_ref, wk_ref, pek_ref, w2k_ref, wv_ref, pev_ref, w2v_ref,
     ovt_ref, e_ref) = refs[n_pg:n_pg + 14]
    hb_ref, wout_stack, xk_scr, xv_scr = refs[-4:]
    wout_ref = wout_stack.at[0]
    P = n_pages * PAGE
    n_chunk = P // CMP_STRIDE
    n_blk = -(-(P + ts) // SEL_BLOCK)
    WB = win_ref.shape[4]
    for bi in range(GB):
        for p in range(n_pages):
            pg = pages[bi * n_pages + p]
            prow = pl.ds(bi * P + p * PAGE, PAGE)
            xk_scr[prow, :] = pg[0, 0, 0].T
            xv_scr[prow, :] = pg[0, 0, 1].T

    def flat(x_ref):
        return jnp.concatenate(
            [x_ref[pl.ds(l, GB * n_chunk, stride=CMP_STRIDE), :].astype(bf16) for l in range(CMP_STRIDE)], axis=1)

    kc_all = _compress_rows(flat(xk_scr), wk_ref, pek_ref, w2k_ref).astype(bf16)
    vc_all = _compress_rows(flat(xv_scr), wv_ref, pev_ref, w2v_ref).astype(bf16)

    qc_all = qc_ref[...].astype(f32)
    qr_all = qr_ref[...].astype(f32)
    gsig = jax.nn.sigmoid(sm_ref[...])
    rows = R * SEG
    qpos_col = P + lax.broadcasted_iota(jnp.int32, (rows, 1), 0) % SEG
    qpos_row = P + lax.broadcasted_iota(jnp.int32, (1, 128), 1) % SEG
    new_lane = lax.broadcasted_iota(jnp.int32, (1, 128), 1)
    zpad = jnp.zeros((128 - SEG, 128), f32)
    out_rows = []
    for bi in range(GB):
        rs8 = slice(bi * SEG, (bi + 1) * SEG)
        kc = kc_all[bi * n_chunk:(bi + 1) * n_chunk]
        vc = vc_all[bi * n_chunk:(bi + 1) * n_chunk]
        ksel_t = [pages[bi * n_pages + pp][0, 0, 2].astype(bf16) for pp in range(n_pages)]
        vsel_t = [pages[bi * n_pages + pp][0, 0, 3].astype(bf16) for pp in range(n_pages)]
        knew =jnp.concatenate([kvn_ref[rs8, 256:384], zpad], axis=0).astype(bf16)
        vnew = jnp.concatenate([kvn_ref[rs8, 384:512], zpad], axis=0).astype(bf16)
        kwnew_f = jnp.concatenate([wn_ref[rs8, 0:128], zpad], axis=0)
        vwnew_f = jnp.concatenate([wn_ref[rs8, 128:256], zpad], axis=0)
        kwnew, vwnew = kwnew_f.astype(bf16), vwnew_f.astype(bf16)
        kwin_t = win_ref[0, bi, 0].astype(bf16)
        vwin_t = win_ref[0, bi, 1].astype(bf16)
        heads = [None] * N_HEADS
        for g in range(N_KV):
            def stack(q_all):
                parts = []
                for r in range(R):
                    h = g * R + r
                    q = q_all[rs8, (h // 2) * 128:(h // 2 + 1) * 128]
                    if h % 2 != g:
                        q = pltpu.roll(q, HEAD_DIM, axis=1)
                    parts.append(jnp.where(new_lane // HEAD_DIM == g, q, 0.0))
                return jnp.concatenate(parts, axis=0).astype(bf16)
            qc = stack(qc_all)
            qr = stack(qr_all)
            s = lax.dot_general(qc, kc, NT, preferred_element_type=f32)
            cmp_end = lax.broadcasted_iota(jnp.int32, (1, n_chunk), 1) * CMP_STRIDE + (CMP_LEN - 1)
            p = _softmax_rows(s, cmp_end <= qpos_col)
            o_c = jnp.dot(p.astype(bf16), vc, preferred_element_type=f32)
            psum = p[0:SEG] + p[SEG:2 * SEG] + p[2 * SEG:3 * SEG] + p[3 * SEG:4 * SEG]
            psum = jnp.concatenate([psum, jnp.zeros((128 - SEG, n_chunk), f32)], axis=0)
            imp_t = lax.dot_general(ovt_ref[...], psum, NT, preferred_element_type=f32, precision=HIGHEST)
            sel_q = _transpose_sel(_select_blocks(imp_t, qpos_row, n_blk))[0:SEG].astype(bf16)
            mk = jnp.concatenate([jnp.dot(sel_q, e_ref[...], preferred_element_type=f32)] * R, axis=0) > 0.5
            s_sel = jnp.concatenate(
                [jnp.dot(qr, ksel_t[pp], preferred_element_type=f32) for pp in range(n_pages)]
                + [lax.dot_general(qr, knew, NT, preferred_element_type=f32)], axis=1)
            kpos = jnp.concatenate([lax.broadcasted_iota(jnp.int32, (1, P), 1), P + new_lane], axis=1)
            valid = jnp.concatenate([jnp.full((1, P), True), new_lane < SEG], axis=1)
            p_s = _softmax_rows(s_sel, mk & (kpos <= qpos_col) & valid).astype(bf16)
            o_s = jnp.dot(p_s[:, P:P + 128], vnew, preferred_element_type=f32)
            for pp in range(n_pages):
                o_s = o_s + lax.dot_general(p_s[:, pp * PAGE:(pp + 1) * PAGE], vsel_t[pp], NT,
                                            preferred_element_type=f32)
            s_w = jnp.concatenate([jnp.dot(qr, kwin_t, preferred_element_type=f32),
                                   lax.dot_general(qr, kwnew, NT, preferred_element_type=f32)], axis=1)
            kwpos = jnp.concatenate([P - WB + lax.broadcasted_iota(jnp.int32, (1, WB), 1), P + new_lane], axis=1)
            validw = jnp.concatenate([jnp.full((1, WB), True), new_lane < SEG], axis=1)
            dpos = qpos_col - kwpos
            p_w = _softmax_rows(s_w, (dpos >= 0) & (dpos < WINDOW) & (kwpos >= 0) & validw).astype(bf16)
            o_w = (lax.dot_general(p_w[:, 0:WB], vwin_t, NT, preferred_element_type=f32)
                   + jnp.dot(p_w[:, WB:WB + 128], vwnew, preferred_element_type=f32))
            for r in range(R):
                h = g * R + r
                rr = slice(r * SEG, (r + 1) * SEG)
                c0 = 8 + 3 * h
                gs = gsig[rs8]
                out = gs[:, c0:c0 + 1] * o_c[rr] + gs[:, c0 + 1:c0 + 2] * o_s[rr] + gs[:, c0 + 2:c0 + 3] * o_w[rr]
                if h % 2 != g:
                    out = pltpu.roll(out, HEAD_DIM, axis=1)
                heads[h] = out
        out_rows.append(jnp.concatenate(
            [jnp.where(new_lane < HEAD_DIM, heads[2 * j], heads[2 * j + 1]) for j in range(N_HEADS // 2)], axis=1))
        for kv, new_f in ((0, kwnew_f), (1, vwnew_f)):
            old = pltpu.roll(win_ref[0, bi, kv], WB - ts, axis=1)
            new_t = pltpu.roll(new_f.T, 128 - ts, axis=1)
            wout_ref[bi, kv, :, 0:WB - 128] = old[:, 0:WB - 128]
            wout_ref[bi, kv, :, WB - 128:WB] = jnp.where(new_lane >= 128 - ts, new_t, old[:, WB - 128:WB])
    hb_ref[...] = jnp.concatenate(out_rows, axis=0).astype(bf16)


def _nsa_sample(page_table, cache_t, cwin_t, qc, qr, sm, kvrow, winrow, cwk, cwv, win_stack, *, layer, row0, ts):
    nseq, n_pages = page_table.shape
    depth = cwin_t.shape[0]
    P = n_pages * PAGE
    WB = cwin_t.shape[4]
    n_chunk = P // CMP_STRIDE
    ovt = _sel_overlap_t(n_chunk, -(-(P + ts) // SEL_BLOCK))
    e = _block_expand(P + 128)
    b0 = row0 // (GB * SEG)

    def page_map(i, pt, *, bi, p):
        return (layer, pt[i * GB + bi, p], 0, 0, 0)

    page_specs = [pl.BlockSpec((1, 1, 4, 128, PAGE), functools.partial(page_map, bi=bi, p=p))
                  for bi in range(GB) for p in range(n_pages)]
    row = lambda w: pl.BlockSpec((GB * SEG, w), lambda i, pt: (b0 + i, 0))
    const = lambda a: pl.BlockSpec(a.shape, lambda i, pt: (0,) * a.ndim)
    extra_in, extra_spec, win_shape = _stacked_out(win_stack, depth, (nseq, 2, 128, WB))
    args = (page_table, *([cache_t] * (GB * n_pages)), cwin_t, qc, qr, sm, kvrow, winrow, *cwk, *cwv, ovt, e)
    grid_spec = pltpu.PrefetchScalarGridSpec(
        num_scalar_prefetch=1, grid=(nseq // GB,),
        in_specs=page_specs + [pl.BlockSpec((1, GB, 2, 128, WB), lambda i, pt: (layer, i, 0, 0, 0)),
                               row(N_WIDTH), row(N_WIDTH), row(128), row(512), row(256)]
        + [const(a) for a in (*cwk, *cwv, ovt, e)] + extra_spec,
        out_specs=[pl.BlockSpec((GB * SEG, N_WIDTH), lambda i, pt: (i, 0)),
                   pl.BlockSpec((1, GB, 2, 128, WB), lambda i, pt: (layer, i, 0, 0, 0))],
        scratch_shapes=[pltpu.VMEM((GB * P, 128), f32), pltpu.VMEM((GB * P, 128), f32)])
    return pl.pallas_call(
        functools.partial(_nsa_sample_body, n_pages=n_pages, ts=ts), grid_spec=grid_spec,
        out_shape=[jax.ShapeDtypeStruct((nseq * SEG, N_WIDTH), bf16), win_shape],
        input_output_aliases={len(args): 1} if extra_in else {},
        compiler_params=pltpu.CompilerParams(dimension_semantics=("parallel",), vmem_limit_bytes=VMEM_LIMIT),
        name="nsa_sample",
    )(*args, *extra_in)


def kernel(x_prompt, x_sample, cache_kv_pages, page_table, cache_win, state_mlstm_C, state_mlstm_n, state_mlstm_m, state_mlstm_conv, norm1_g, w_in, b_gates, conv_w, conv_b, w_mq, w_mk, mlstm_norm_g, cmp_pe_k, cmp_pe_v, cmp_w1_k, cmp_w2_k, cmp_w1_v, cmp_w2_v, w_branch_a, w_branch_b, w_out, norm2_g, w_ffn_up, w_ffn_down, final_norm_g):
    Bp, Tp, _ = x_prompt.shape
    Bs, Ts, _ = x_sample.shape
    depth = w_in.shape[0]
    n_pool, page = cache_kv_pages.shape[1:3]
    n_pages = page_table.shape[1]
    past_len = n_pages * page
    wb_len = cache_win.shape[2]
    mp = Bp * Tp
    ms = Bs * SAMPLE_PAD
    assert page == PAGE and M_CONV - 1 <= Ts <= SAMPLE_PAD and Tp % MLSTM_CHUNK == 0 and Tp >= WINDOW + TQ
    assert mp % (SEG * SEQ_PER_STEP) == 0 and Bs % SEQ_PER_STEP == 0 and Bs % GB == 0

    pad_seq = lambda a: jnp.pad(a, ((0, 0), (0, SAMPLE_PAD - a.shape[1]), (0, 0)))
    x_all = jnp.concatenate([x_prompt.reshape(mp, D_MODEL), pad_seq(x_sample).reshape(ms, D_MODEL)], axis=0)
    tabs = _rope_tables(Tp, past_len, ROW_TILE * (ms // ROW_TILE))
    col_idx, col_scale = _in_proj_columns()
    cache_t = jnp.transpose(cache_kv_pages, (0, 1, 3, 4, 5, 2)).reshape(depth, n_pool, 4, KV_W, PAGE)
    cwin_t = jnp.transpose(cache_win, (0, 1, 3, 4, 5, 2)).reshape(depth, Bs, 2, KV_W, wb_len)

    outs = {k: [] for k in ("p_kv", "p_win", "p_C", "p_n", "p_m", "p_conv", "s_kv", "s_n", "s_m", "s_conv")}
    c_stack = win_stack = None
    y_parts = None
    for l in range(depth):
        w_ext = jnp.concatenate([w_in[l], jnp.zeros((D_MODEL, 1), f32)], axis=1)
        w_all = (w_ext[:, col_idx] * col_scale).astype(bf16)
        w_t = jnp.concatenate([w_in[l][:, O_IG:O_IG + 8], w_in[l][:, O_NG:O_NG + 24]], axis=1).T.astype(bf16)
        xm, sm, smt, qc, qr, qct, qrt, kvrow, winrow, kk, vt, gates = _in_proj(
            x_all, norm1_g[l].reshape(1, D_MODEL), w_all, w_t, tabs, Tp)

        wq, wk = w_mq[l].astype(bf16), w_mk[l].astype(bf16)
        ha_p, c_p, n_p, m_p, cv_p = _mlstm_prompt(xm, sm, smt, conv_w[l], conv_b[l], wq, wk, b_gates[l], mlstm_norm_g[l],
                                                   nb=Bp, T=Tp, L=MLSTM_CHUNK)
        tail = jnp.pad(state_mlstm_conv[l], ((0, 0), (0, SEG - (M_CONV - 1)), (0, 0))).reshape(ms, M_WIDTH)
        n0 = jnp.pad(state_mlstm_n[l], ((0, 0), (0, 8 - M_HEADS), (0, 0)))
        m0 = jnp.broadcast_to(jnp.pad(state_mlstm_m[l], ((0, 0), (0, 128 - M_HEADS)))[:, None, :], (Bs, SEG, 128))
        ha_s, c_stack, n_s, mt_s = _mlstm_sample(
            xm, sm, smt, tail, state_mlstm_C[l], n0, m0.reshape(ms, 128), conv_w[l], conv_b[l], wq, wk, b_gates[l],
            mlstm_norm_g[l], c_stack, layer=l, depth=depth, row0=mp, nseq=Bs, n_valid=Ts)

        cwk = _compress_weights(cmp_pe_k[l], cmp_w1_k[l], cmp_w2_k[l])
        cwv = _compress_weights(cmp_pe_v[l], cmp_w1_v[l], cmp_w2_v[l])
        kc, vct = _compress_prompt(kvrow, cwk, cwv, nb=Bp, T=Tp)
        hbt_p = _nsa_prompt(qct, qrt, smt, kk, vt, kc, vct, nb=Bp, T=Tp)
        hb_s, win_stack = _nsa_sample(page_table, cache_t, cwin_t, qc, qr, sm, kvrow, winrow, cwk, cwv, win_stack,
                                      layer=l, row0=mp, ts=Ts)

        h_a = jnp.concatenate([ha_p, ha_s], axis=0)
        h_bt = jnp.concatenate([hbt_p, hb_s.T], axis=1)
        x_all, *y_parts = _mix_ffn(
            x_all, h_a, h_bt, gates, w_branch_a[l].astype(bf16), w_branch_b[l].astype(bf16), w_out[l].astype(bf16),
            norm2_g[l].reshape(1, D_MODEL), w_ffn_up[l].astype(bf16), w_ffn_down[l].astype(bf16),
            final_norm_g.reshape(1, D_MODEL), n_prompt_rows=mp, final=l == depth - 1)

        sample_rows = lambda a: a[mp:].reshape(Bs, SAMPLE_PAD, a.shape[1])
        wp = min(WINDOW, Tp)
        outs["p_kv"].append(kvrow[:mp].reshape(Bp, Tp, 4, N_KV, HEAD_DIM))
        outs["p_win"].append(winrow[:mp].reshape(Bp, Tp, 2 * KV_W)[:, -wp:].reshape(Bp, wp, 2, N_KV, HEAD_DIM))
        outs["p_C"].append(c_p)
        outs["p_n"].append(n_p[:, :M_HEADS])
        outs["p_m"].append(m_p[:, 0, :M_HEADS])
        outs["p_conv"].append(cv_p[:, 8 - (M_CONV - 1):])
        outs["s_kv"].append(sample_rows(kvrow)[:, :Ts].reshape(Bs, Ts, 4, N_KV, HEAD_DIM))
        outs["s_n"].append(n_s[:, :M_HEADS])
        outs["s_m"].append(mt_s.reshape(Bs, SEG, 128)[:, SEG - 1, :M_HEADS])
        outs["s_conv"].append(sample_rows(xm)[:, Ts - (M_CONV - 1):Ts, :M_WIDTH])

    s_win = jnp.transpose(win_stack.reshape(depth, Bs, 2, N_KV, HEAD_DIM, wb_len), (0, 1, 5, 2, 3, 4))
    y_prompt = y_parts[0].reshape(Bp, Tp, D_MODEL)
    y_sample = y_parts[1].reshape(Bs, SAMPLE_PAD, D_MODEL)[:, :Ts]
    st = lambda k: jnp.stack(outs[k])
    return (y_prompt, y_sample, st("p_kv"), st("p_win"), st("p_C"), st("p_n"), st("p_m"), st("p_conv"),
            st("s_kv"), s_win, c_stack, st("s_n"), st("s_m"), st("s_conv"))
```

```python
import functools

import jax
import jax.numpy as jnp
import numpy as np
from jax import lax
from jax.experimental import pallas as pl
from jax.experimental.pallas import tpu as pltpu

f32 = jnp.float32
bf16 = jnp.bfloat16

D_MODEL = 1024
M_HEADS = 4
M_WIDTH = 512
M_HEAD_DIM = 128
M_CONV = 4
MLSTM_CHUNK = 256
HEAD_DIM = 64
N_WIDTH = 512
N_HEADS = 8
N_KV = 2
KV_W = 128
CMP_STRIDE = 16
CMP_LEN = 32
SEL_BLOCK = 64
SEL_TOP = 16
WINDOW = 512
ROT_DIM = 16
ROPE_THETA = 500000.0
D_FF = 2816
EPS = 1e-6
NEG = -1e30
TINY = 1e-30
FORCE_SCORE = 1e9

SAMPLE_PAD = 8
SEG = SAMPLE_PAD
SEQ_PER_STEP = 16
GB = 2
PAGE = 128
HD = M_HEAD_DIM
R = N_HEADS // N_KV
CMP_HID = 128
TQ = 128
TK = 1024
NBLK_PAD = 64
HIGHEST = lax.Precision.HIGHEST
NT = (((1,), (1,)), ((), ()))
LOG2E = 1.4426950408889634
ROW_TILE = 256
FF_CHUNK = 1408
VMEM_LIMIT = 56 * 1024 * 1024

C_XM, C_Q, C_KV, C_WIN, C_GATE, C_SM = 0, 1536, 2048, 2560, 2816, 4864
C_TOTAL = 4992
O_U, O_V, O_O, O_IG, O_FG, O_Q, O_KV, O_NG, O_GA, O_GB = 0, 512, 1024, 1536, 1540, 1544, 2056, 2824, 2848, 3872
IN_WIDTH = 4896


def _in_proj_columns():
    z = IN_WIDTH
    idx = list(range(O_U, O_IG))
    idx += list(range(O_Q, O_Q + N_WIDTH))
    idx += list(range(O_KV, O_KV + 768))
    idx += list(range(O_GA, O_GA + 2048))
    idx += list(range(O_IG, O_IG + 8)) + list(range(O_NG, O_NG + 24)) + [z] * 96
    assert len(idx) == C_TOTAL
    scale = np.ones((C_TOTAL,), np.float32)
    scale[C_Q:C_KV] = HEAD_DIM ** -0.5
    return np.asarray(idx, np.int32), scale


def _rope_tables(seq, past_len, n_sample_rows):
    half = ROT_DIM // 2
    inv = ROPE_THETA ** (-jnp.arange(half, dtype=f32) / half)
    pos = jnp.concatenate([jnp.arange(seq), past_len + (jnp.arange(n_sample_rows) % SAMPLE_PAD)]).astype(f32)
    ang = pos[:, None] * inv[None, :]
    cos8, sin8 = jnp.cos(ang), jnp.sin(ang)
    n = pos.shape[0]
    one = jnp.ones((n, 64 - ROT_DIM), f32)
    zero = jnp.zeros((n, 64 - ROT_DIM), f32)
    z8 = jnp.zeros((n, half), f32)
    cos = jnp.concatenate([cos8, cos8, one], axis=1)
    sa = jnp.concatenate([-sin8, z8, zero], axis=1)
    sb = jnp.concatenate([z8, sin8, zero], axis=1)
    tile2 = lambda a: jnp.concatenate([a, a], axis=1)
    return tile2(cos), tile2(sa), tile2(sb)


def _rms(x, g):
    return x * lax.rsqrt(jnp.mean(x * x, axis=-1, keepdims=True) + EPS) * g


def _in_proj_body(x_ref, g_ref, w_ref, wt_ref, cos_ref, sa_ref, sb_ref, *refs):
    (xm_ref, sm_ref, smt_ref, qc_ref, qr_ref, qct_ref, qrt_ref, kv_ref, win_ref, kk_ref, vt_ref, gate_ref,
     kvt_stack) = refs[-13:]
    hb = _rms(x_ref[...], g_ref[...]).astype(bf16)
    cos, sa, sb = cos_ref[...], sa_ref[...], sb_ref[...]

    def rope(v):
        return v * cos + pltpu.roll(v, 128 - ROT_DIM // 2, axis=1) * sa + pltpu.roll(v, ROT_DIM // 2, axis=1) * sb

    def proj(c0, n):
        return jnp.dot(hb, w_ref[:, c0:c0 + n], preferred_element_type=f32)

    xm_ref[...] = proj(C_XM, 1536)
    sm_ref[...] = proj(C_SM, 128)
    smt_ref[...] = lax.dot_general(wt_ref[...], hb, NT, preferred_element_type=f32)
    qf = proj(C_Q, N_WIDTH) * LOG2E
    for j in range(N_HEADS // 2):
        sl = slice(j * 128, (j + 1) * 128)
        q = qf[:, sl]
        qrot = rope(q)
        qc_ref[:, sl] = q.astype(bf16)
        qr_ref[:, sl] = qrot.astype(bf16)
        qct_ref[sl, :] = q.T.astype(bf16)
        qrt_ref[sl, :] = qrot.T.astype(bf16)
    kv = proj(C_KV, 512)
    ksel = rope(kv[:, 256:384])
    kv_ref[:, 0:256] = kv[:, 0:256]
    kv_ref[:, 256:384] = ksel
    kv_ref[:, 384:512] = kv[:, 384:512]
    kvt_stack[0, 0, 0:256, :] = kv[:, 0:256].T
    kvt_stack[0, 0, 256:384, :] = ksel.T
    kvt_stack[0, 0, 384:512, :] = kv[:, 384:512].T
    win = proj(C_WIN, 256)
    kwin = rope(win[:, 0:128])
    win_ref[:, 0:128] = kwin
    win_ref[:, 128:256] = win[:, 128:256]
    kk_ref[:, 0:128] = ksel.astype(bf16)
    kk_ref[:, 128:256] = kwin.astype(bf16)
    vt_ref[0:128, :] = kv[:, 384:512].T.astype(bf16)
    vt_ref[128:256, :] = win[:, 128:256].T.astype(bf16)
    gate_ref[...] = proj(C_GATE, 2048)


def _in_proj(x_all, norm_g, w_all, w_t, tabs, seq, kvt_stack, *, layer, depth):
    m = x_all.shape[0]
    n_prompt_tiles_per_seq = seq // ROW_TILE
    n_prompt_tiles = (m - (tabs[0].shape[0] - seq)) // ROW_TILE
    n_seq = n_prompt_tiles // n_prompt_tiles_per_seq
    n_tiles = m // ROW_TILE
    n_sample_tiles = n_tiles - n_prompt_tiles

    def tile(i):
        return (i + n_prompt_tiles) % n_tiles

    def kvt_map(i):
        j = jnp.maximum(i - n_sample_tiles, 0)
        return (layer, j // n_prompt_tiles_per_seq, 0, j % n_prompt_tiles_per_seq)

    def tab_map(i):
        t = tile(i)
        return (jnp.where(t < n_prompt_tiles, t % n_prompt_tiles_per_seq, n_prompt_tiles_per_seq + t - n_prompt_tiles), 0)

    row = lambda w: pl.BlockSpec((ROW_TILE, w), lambda i: (tile(i), 0))
    const = lambda s: pl.BlockSpec(s, lambda i: (0, 0), pipeline_mode=pl.Buffered(1))
    tab = pl.BlockSpec((ROW_TILE, 128), tab_map)
    widths = (1536, 128, -32, 512, 512, -512, -512, 512, 256, 256, -256, 2048)
    dts = (f32, f32, f32, bf16, bf16, bf16, bf16, f32, f32, bf16, bf16, f32)
    out_shape = [jax.ShapeDtypeStruct((-w, m) if w < 0 else (m, w), d) for w, d in zip(widths, dts)]
    out_specs = [pl.BlockSpec((-w, ROW_TILE), lambda i: (0, tile(i))) if w < 0 else row(w) for w in widths]
    extra_in, extra_spec, kvt_shape = _stacked_out(kvt_stack, depth, (n_seq, 4 * KV_W, seq))
    args = (x_all, norm_g, w_all, w_t, *tabs)
    return pl.pallas_call(
        _in_proj_body, grid=(n_tiles,),
        in_specs=[row(D_MODEL), const((1, D_MODEL)), const((D_MODEL, C_TOTAL)), const((32, D_MODEL)), tab, tab, tab]
        + extra_spec,
        out_specs=out_specs + [pl.BlockSpec((1, 1, 4 * KV_W, ROW_TILE), kvt_map)],
        out_shape=out_shape + [kvt_shape], name="in_proj",
        input_output_aliases={len(args): len(out_shape)} if extra_in else {},
        compiler_params=pltpu.CompilerParams(dimension_semantics=("arbitrary",), vmem_limit_bytes=VMEM_LIMIT),
    )(*args, *extra_in)


def _mix_ffn_body(x_ref, ha_ref, hb_ref, gate_ref, wa_ref, wb_ref, wo_ref, g2_ref, wup_ref, wdn_ref, gf_ref,
                  xo_ref, *y_refs, n_prompt_tiles):
    a = jnp.dot(ha_ref[...].astype(bf16), wa_ref[...], preferred_element_type=f32)
    b = lax.dot_general(hb_ref[...], wb_ref[...], (((0,), (0,)), ((), ())), preferred_element_type=f32)
    merged = jax.nn.sigmoid(gate_ref[:, 0:D_MODEL]) * a + jax.nn.sigmoid(gate_ref[:, D_MODEL:2 * D_MODEL]) * b
    x1 = x_ref[...] + jnp.dot(merged.astype(bf16), wo_ref[...], preferred_element_type=f32)
    hn = _rms(x1, g2_ref[...]).astype(bf16)
    acc = x1
    for c in range(D_FF // FF_CHUNK):
        lo = c * FF_CHUNK
        g = jnp.dot(hn, wup_ref[:, lo:lo + FF_CHUNK], preferred_element_type=f32)
        u = jnp.dot(hn, wup_ref[:, D_FF + lo:D_FF + lo + FF_CHUNK], preferred_element_type=f32)
        act = (g * jax.nn.sigmoid(g) * u).astype(bf16)
        acc = acc + jnp.dot(act, wdn_ref[lo:lo + FF_CHUNK, :], preferred_element_type=f32)
    xo_ref[...] = acc
    if y_refs:
        y = _rms(acc, gf_ref[...])
        yp_ref, ys_ref = y_refs
        i = pl.program_id(0)

        @pl.when(i < n_prompt_tiles)
        def _():
            yp_ref[...] = y

        @pl.when(i >= n_prompt_tiles)
        def _():
            ys_ref[...] = y


def _mix_ffn(x_all, h_a, h_b, gates, wa, wb, wo, g2, wup, wdn, gf, *, n_prompt_rows, final):
    m = x_all.shape[0]
    npt = n_prompt_rows // ROW_TILE
    row = lambda w: pl.BlockSpec((ROW_TILE, w), lambda i: (i, 0))
    const = lambda s: pl.BlockSpec(s, lambda i: (0, 0), pipeline_mode=pl.Buffered(1))
    out_specs = [row(D_MODEL)]
    out_shape = [jax.ShapeDtypeStruct((m, D_MODEL), f32)]
    if final:
        out_specs += [pl.BlockSpec((ROW_TILE, D_MODEL), lambda i: (jnp.minimum(i, npt - 1), 0)),
                      pl.BlockSpec((ROW_TILE, D_MODEL), lambda i: (jnp.maximum(i - npt, 0), 0))]
        out_shape += [jax.ShapeDtypeStruct((n_prompt_rows, D_MODEL), f32),
                      jax.ShapeDtypeStruct((m - n_prompt_rows, D_MODEL), f32)]
    return pl.pallas_call(
        functools.partial(_mix_ffn_body, n_prompt_tiles=npt), grid=(m // ROW_TILE,),
        in_specs=[row(D_MODEL), row(h_a.shape[1]), pl.BlockSpec((h_b.shape[0], ROW_TILE), lambda i: (0, i)),
                  row(2 * D_MODEL),
                  const(wa.shape), const(wb.shape), const(wo.shape), const((1, D_MODEL)),
                  const(wup.shape), const(wdn.shape), const((1, D_MODEL))],
        out_specs=out_specs, out_shape=out_shape, name="mix_ffn",
        compiler_params=pltpu.CompilerParams(dimension_semantics=("arbitrary",), vmem_limit_bytes=VMEM_LIMIT),
    )(x_all, h_a, h_b, gates, wa, wb, wo, g2, wup, wdn, gf)


def _log_sigmoid(x):
    return jnp.minimum(x, 0.0) - jnp.log1p(jnp.exp(-jnp.abs(x)))


def _seg_scan(x, axis, seg, op, fill):
    idx = lax.broadcasted_iota(jnp.int32, x.shape, axis) % seg
    d = 1
    while d < seg:
        x = op(x, jnp.where(idx >= d, pltpu.roll(x, d, axis=axis), fill))
        d *= 2
    return x


def _conv_silu(u, tail, conv_w, conv_b, seg):
    L = u.shape[0]
    row = lax.broadcasted_iota(jnp.int32, u.shape, 0) % seg
    y = conv_b + u * conv_w[3:4, :]
    for k in (1, 2, 3):
        prev = tail if k == 3 else pltpu.roll(tail, L - (3 - k), axis=0)
        sh = jnp.where(row < k, prev, pltpu.roll(u, k, axis=0))
        y = y + sh * conv_w[3 - k:4 - k, :]
    return y * jax.nn.sigmoid(y)


def _gate_columns(sm, smt, bg_row, bg_col, m_vec, seg, n_valid):
    G = sm + bg_row
    lf = _log_sigmoid(G)
    ipre = G
    if n_valid < seg:
        rowc = lax.broadcasted_iota(jnp.int32, G.shape, 0) % seg
        lf = jnp.where(rowc < n_valid, lf, 0.0)
        ipre = jnp.where(rowc < n_valid, G, NEG)
    F = _seg_scan(lf, 0, seg, jnp.add, 0.0)
    F_al = pltpu.roll(F, 128 - M_HEADS, axis=1)
    a_col = ipre - F_al
    g_col = jnp.maximum(m_vec, _seg_scan(a_col, 0, seg, jnp.maximum, NEG))
    mt_col = F_al + g_col
    Gr = smt + bg_col
    lfr = _log_sigmoid(Gr)
    ir = Gr
    if n_valid < seg:
        lanec = lax.broadcasted_iota(jnp.int32, Gr.shape, 1) % seg
        lfr = jnp.where(lanec < n_valid, lfr, 0.0)
        ir = jnp.where(lanec < n_valid, Gr, NEG)
    Fr = _seg_scan(lfr, 1, seg, jnp.add, 0.0)
    a_row = ir - pltpu.roll(Fr, M_HEADS, axis=0)
    return a_col, g_col, mt_col, a_row


def _head_norm_gate(hh, ng, o):
    return hh * lax.rsqrt(jnp.mean(hh * hh, axis=-1, keepdims=True) + EPS) * ng * jax.nn.sigmoid(o)


def _mlstm_prompt_body(xm_ref, sm_ref, smt_ref, cw_ref, cb_ref, wq_ref, wk_ref, bgr_ref, bgc_ref, ng_ref,
                       ha_ref, c_out, n_out, m_out, conv_out, c_scr, n_scr, m_scr, tail_scr, *, L):
    c_idx = pl.program_id(1)

    @pl.when(c_idx == 0)
    def _():
        c_scr[...] = jnp.zeros_like(c_scr)
        n_scr[...] = jnp.zeros_like(n_scr)
        m_scr[...] = jnp.zeros_like(m_scr)
        tail_scr[...] = jnp.zeros_like(tail_scr)

    u = xm_ref[:, 0:M_WIDTH]
    tail = jnp.concatenate([tail_scr[...], jnp.zeros((L - 8, M_WIDTH), f32)], axis=0)
    cact = _conv_silu(u, tail, cw_ref[...], cb_ref[...], L)
    tail_scr[...] = pltpu.roll(u[L - 8:L, :], 3, axis=0)
    conv_out[0] = u[L - 8:L, :]

    m_vec = m_scr[0:1, :]
    a_col, g_col, mt_col, a_row = _gate_columns(sm_ref[...], smt_ref[...], bgr_ref[...], bgc_ref[...], m_vec, L, L)
    inter_col = jnp.exp(m_vec - g_col)
    floor_col = jnp.exp(-mt_col)
    g_last = g_col[L - 1:L, :]
    w_col = jnp.exp(a_col - g_last)
    decay = jnp.exp(m_vec - g_last)
    m_new = mt_col[L - 1:L, :]
    ti = lax.broadcasted_iota(jnp.int32, (L, L), 0)
    si = lax.broadcasted_iota(jnp.int32, (L, L), 1)
    causal = si <= ti
    scale = HD ** -0.5
    for h in range(M_HEADS):
        sl = slice(h * HD, (h + 1) * HD)
        ch = cact[:, sl].astype(bf16)
        qh = jnp.dot(ch, wq_ref[h], preferred_element_type=f32)
        kh = jnp.dot(ch, wk_ref[h], preferred_element_type=f32) * scale
        vh = xm_ref[:, M_WIDTH + h * HD:M_WIDTH + (h + 1) * HD].astype(bf16)
        qb = qh.astype(bf16)
        D = jnp.exp(jnp.where(causal, a_row[h:h + 1, :] - g_col[:, h:h + 1], NEG))
        S = lax.dot_general(qb, kh.astype(bf16), NT, preferred_element_type=f32) * D
        C = c_scr[h]
        nrow = n_scr[h:h + 1, :]
        ic = inter_col[:, h:h + 1]
        num = jnp.dot(S.astype(bf16), vh, preferred_element_type=f32) + ic * jnp.dot(
            qb, C.astype(bf16), preferred_element_type=f32)
        den = jnp.sum(S, axis=-1, keepdims=True) + ic * jnp.sum(qh * nrow, axis=-1, keepdims=True)
        hh = num / jnp.maximum(jnp.abs(den), floor_col[:, h:h + 1])
        o = xm_ref[:, 2 * M_WIDTH + h * HD:2 * M_WIDTH + (h + 1) * HD]
        ha_ref[:, sl] = _head_norm_gate(hh, ng_ref[:, sl], o)
        kw = kh * w_col[:, h:h + 1]
        dh = decay[:, h:h + 1]
        c_scr[h] = dh * C + lax.dot_general(kw.astype(bf16), vh, (((0,), (0,)), ((), ())), preferred_element_type=f32)
        n_scr[h:h + 1, :] = dh * nrow + jnp.sum(kw, axis=0, keepdims=True)
    m_scr[0:1, :] = m_new

    @pl.when(c_idx == pl.num_programs(1) - 1)
    def _():
        c_out[0] = c_scr[...]
        n_out[0] = n_scr[...]
        m_out[0] = m_scr[...]


def _mlstm_prompt(xm, sm, smt, conv_w, conv_b, wq, wk, b_gates, norm_g, *, nb, T, L):
    nc = T // L
    bg_row = jnp.zeros((1, 128), f32).at[0, :8].set(b_gates)
    bg_col = b_gates.reshape(8, 1)
    const = lambda s: pl.BlockSpec(s, lambda b, c: (0,) * len(s))
    return pl.pallas_call(
        functools.partial(_mlstm_prompt_body, L=L), grid=(nb, nc),
        in_specs=[pl.BlockSpec((L, 3 * M_WIDTH), lambda b, c: (b * nc + c, 0)),
                  pl.BlockSpec((L, 128), lambda b, c: (b * nc + c, 0)),
                  pl.BlockSpec((8, L), lambda b, c: (0, b * nc + c)),
                  const((4, M_WIDTH)), const((1, M_WIDTH)), const((M_HEADS, HD, HD)), const((M_HEADS, HD, HD)),
                  const((1, 128)), const((8, 1)), const((1, M_WIDTH))],
        out_specs=[pl.BlockSpec((L, M_WIDTH), lambda b, c: (b * nc + c, 0)),
                   pl.BlockSpec((1, M_HEADS, HD, HD), lambda b, c: (b, 0, 0, 0)),
                   pl.BlockSpec((1, 8, HD), lambda b, c: (b, 0, 0)),
                   pl.BlockSpec((1, 8, 128), lambda b, c: (b, 0, 0)),
                   pl.BlockSpec((1, 8, M_WIDTH), lambda b, c: (b, 0, 0))],
        out_shape=[jax.ShapeDtypeStruct((nb * T, M_WIDTH), f32),
                   jax.ShapeDtypeStruct((nb, M_HEADS, HD, HD), f32),
                   jax.ShapeDtypeStruct((nb, 8, HD), f32),
                   jax.ShapeDtypeStruct((nb, 8, 128), f32),
                   jax.ShapeDtypeStruct((nb, 8, M_WIDTH), f32)],
        scratch_shapes=[pltpu.VMEM((M_HEADS, HD, HD), f32), pltpu.VMEM((8, HD), f32), pltpu.VMEM((8, 128), f32),
                        pltpu.VMEM((8, M_WIDTH), f32)],
        compiler_params=pltpu.CompilerParams(dimension_semantics=("parallel", "arbitrary")),
        name="mlstm_prompt",
    )(xm, sm, smt, conv_w, conv_b.reshape(1, M_WIDTH), wq, wk, bg_row, bg_col, norm_g.reshape(1, M_WIDTH))


def _mlstm_sample_body(xm_ref, sm_ref, smt_ref, tail_ref, c_in, n_in, m_in, cw_ref, cb_ref, wq_ref, wk_ref,
                       bgr_ref, bgc_ref, ng_ref, *out_refs, n_valid):
    ha_ref, c_stack, n_out, m_out = out_refs[-4:]
    c_out = c_stack.at[0]
    L = SEG * SEQ_PER_STEP
    u = xm_ref[:, 0:M_WIDTH]
    cact = _conv_silu(u, tail_ref[...], cw_ref[...], cb_ref[...], SEG)
    m_rows = m_in[...]
    a_col, g_col, mt_col, a_row = _gate_columns(sm_ref[...], smt_ref[...], bgr_ref[...], bgc_ref[...], m_rows, SEG,
                                                n_valid)
    inter_col = jnp.exp(m_rows - g_col)
    floor_col = jnp.exp(-mt_col)
    ti = lax.broadcasted_iota(jnp.int32, (L, L), 0)
    si = lax.broadcasted_iota(jnp.int32, (L, L), 1)
    mask = (si <= ti) & ((si // SEG) == (ti // SEG))
    scale = HD ** -0.5
    for h in range(M_HEADS):
        sl = slice(h * HD, (h + 1) * HD)
        ch = cact[:, sl].astype(bf16)
        qh = jnp.dot(ch, wq_ref[h], preferred_element_type=f32)
        kh = jnp.dot(ch, wk_ref[h], preferred_element_type=f32) * scale
        vh = xm_ref[:, M_WIDTH + h * HD:M_WIDTH + (h + 1) * HD].astype(bf16)
        qb = qh.astype(bf16)
        D = jnp.exp(jnp.where(mask, a_row[h:h + 1, :] - g_col[:, h:h + 1], NEG))
        S = lax.dot_general(qb, kh.astype(bf16), NT, preferred_element_type=f32) * D
        num = jnp.dot(S.astype(bf16), vh, preferred_element_type=f32)
        den = jnp.sum(S, axis=-1, keepdims=True)
        inter_num, inter_den = [], []
        for s in range(SEQ_PER_STEP):
            rs = slice(s * SEG, (s + 1) * SEG)
            C = c_in[s, h]
            nrow = n_in[s, h:h + 1, :]
            inter_num.append(jnp.dot(qb[rs], C.astype(bf16), preferred_element_type=f32))
            inter_den.append(jnp.sum(qh[rs] * nrow, axis=-1, keepdims=True))
            g_last = g_col[s * SEG + SEG - 1:s * SEG + SEG, h:h + 1]
            m_prev = m_rows[s * SEG:s * SEG + 1, h:h + 1]
            w = jnp.exp(a_col[rs, h:h + 1] - g_last)
            dh = jnp.exp(m_prev - g_last)
            kw = kh[rs] * w
            c_out[s, h] = dh * C + lax.dot_general(kw.astype(bf16), vh[rs], (((0,), (0,)), ((), ())),
                                                    preferred_element_type=f32)
            n_out[s, h:h + 1, :] = dh * nrow + jnp.sum(kw, axis=0, keepdims=True)
        ic = inter_col[:, h:h + 1]
        num = num + ic * jnp.concatenate(inter_num, axis=0)
        den = den + ic * jnp.concatenate(inter_den, axis=0)
        hh = num / jnp.maximum(jnp.abs(den), floor_col[:, h:h + 1])
        o = xm_ref[:, 2 * M_WIDTH + h * HD:2 * M_WIDTH + (h + 1) * HD]
        ha_ref[:, sl] = _head_norm_gate(hh, ng_ref[:, sl], o)
    for s in range(SEQ_PER_STEP):
        n_out[s, M_HEADS:8, :] = jnp.zeros((8 - M_HEADS, HD), f32)
    m_out[...] = mt_col


def _stacked_out(prev, depth, shape):
    extra_in = [] if prev is None else [prev]
    extra_spec = [] if prev is None else [pl.BlockSpec(memory_space=pl.ANY)]
    return extra_in, extra_spec, jax.ShapeDtypeStruct((depth,) + shape, f32)


def _mlstm_sample(xm, sm, smt, tail, c0, n0, m0, conv_w, conv_b, wq, wk, b_gates, norm_g, c_stack, *, layer, depth,
                  row0, nseq, n_valid):
    L = SEG * SEQ_PER_STEP
    b0 = row0 // L
    bg_row = jnp.zeros((1, 128), f32).at[0, :8].set(b_gates)
    bg_col = b_gates.reshape(8, 1)
    const = lambda s: pl.BlockSpec(s, lambda i: (0,) * len(s))
    extra_in, extra_spec, c_shape = _stacked_out(c_stack, depth, (nseq, M_HEADS, HD, HD))
    n_in = 14
    return pl.pallas_call(
        functools.partial(_mlstm_sample_body, n_valid=n_valid), grid=(nseq // SEQ_PER_STEP,),
        in_specs=[pl.BlockSpec((L, 3 * M_WIDTH), lambda i: (b0 + i, 0)),
                  pl.BlockSpec((L, 128), lambda i: (b0 + i, 0)),
                  pl.BlockSpec((8, L), lambda i: (0, b0 + i)),
                  pl.BlockSpec((L, M_WIDTH), lambda i: (i, 0)),
                  pl.BlockSpec((SEQ_PER_STEP, M_HEADS, HD, HD), lambda i: (i, 0, 0, 0)),
                  pl.BlockSpec((SEQ_PER_STEP, 8, HD), lambda i: (i, 0, 0)),
                  pl.BlockSpec((L, 128), lambda i: (i, 0)),
                  const((4, M_WIDTH)), const((1, M_WIDTH)), const((M_HEADS, HD, HD)), const((M_HEADS, HD, HD)),
                  const((1, 128)), const((8, 1)), const((1, M_WIDTH))] + extra_spec,
        out_specs=[pl.BlockSpec((L, M_WIDTH), lambda i: (i, 0)),
                   pl.BlockSpec((1, SEQ_PER_STEP, M_HEADS, HD, HD), lambda i: (layer, i, 0, 0, 0)),
                   pl.BlockSpec((SEQ_PER_STEP, 8, HD), lambda i: (i, 0, 0)),
                   pl.BlockSpec((L, 128), lambda i: (i, 0))],
        out_shape=[jax.ShapeDtypeStruct((nseq * SEG, M_WIDTH), f32), c_shape,
                   jax.ShapeDtypeStruct((nseq, 8, HD), f32),
                   jax.ShapeDtypeStruct((nseq * SEG, 128), f32)],
        input_output_aliases={n_in: 1} if extra_in else {},
        compiler_params=pltpu.CompilerParams(dimension_semantics=("parallel",), vmem_limit_bytes=VMEM_LIMIT),
        name="mlstm_sample",
    )(xm, sm, smt, tail, c0, n0, m0, conv_w, conv_b.reshape(1, M_WIDTH), wq, wk, bg_row, bg_col,
      norm_g.reshape(1, M_WIDTH), *extra_in)


def _sel_overlap_t(n_chunk, n_blk):
    n_cmp = n_chunk - 1
    start = np.arange(n_cmp) * CMP_STRIDE
    bs = np.arange(n_blk) * SEL_BLOCK
    ov = np.minimum(start[:, None] + CMP_LEN, bs[None, :] + SEL_BLOCK) - np.maximum(start[:, None], bs[None, :])
    ov = np.clip(ov, 0, None) / CMP_LEN
    out = np.zeros((NBLK_PAD, n_chunk), np.float32)
    out[:n_blk, :n_cmp] = ov.T
    return jnp.asarray(out)


def _block_expand(n_keys):
    e = (np.arange(n_keys)[None, :] // SEL_BLOCK) == np.arange(128)[:, None]
    return jnp.asarray(e, dtype=bf16)


def _compress_weights(pe, w1, w2):
    W = jnp.zeros((16, 2, 64, 2, 2, CMP_HID), f32)
    for g in range(N_KV):
        W = W.at[:, g, :, g, 0, :].set(w1[:16])
        W = W.at[:, g, :, g, 1, :].set(w1[16:])
    W = W.reshape(2048, 512)
    peA = jnp.broadcast_to(pe[:16, None, :], (16, 2, 64)).reshape(1, 2048)
    peB = jnp.broadcast_to(pe[16:, None, :], (16, 2, 64)).reshape(1, 2048)
    PE = jnp.concatenate([peA, peB, jnp.zeros((14, 2048), f32)], axis=0)
    W2 = jnp.zeros((2, CMP_HID, 128), f32).at[0, :, 0:64].set(w2).at[1, :, 64:128].set(w2)
    return W.astype(bf16), PE.astype(bf16), W2.astype(bf16)


def _compress_rows(flat, w_ref, pe_ref, w2_ref):
    n = flat.shape[0]
    y = jnp.dot(jnp.concatenate([flat, pe_ref[...]], axis=0), w_ref[...], preferred_element_type=f32)
    c = y[n:n + 8]
    out = jnp.zeros((n, w2_ref.shape[2]), f32)
    for g in range(N_KV):
        a = y[0:n, (2 * g) * CMP_HID:(2 * g + 1) * CMP_HID] + c[0:1, (2 * g) * CMP_HID:(2 * g + 1) * CMP_HID]
        b = (y[0:n, (2 * g + 1) * CMP_HID:(2 * g + 2) * CMP_HID]
             + c[1:2, (2 * g + 1) * CMP_HID:(2 * g + 2) * CMP_HID])
        hid = a + pltpu.roll(b, n - 1, axis=0)
        out = out + jnp.dot(jax.nn.gelu(hid, approximate=True).astype(bf16), w2_ref[g], preferred_element_type=f32)
    return out


def _compress_prompt_body(xk_ref, xv_ref, wk_ref, pek_ref, w2k_ref, wv_ref, pev_ref, w2v_ref, kc_ref, vc_ref, *,
                          n_chunk):
    def flat(x_ref):
        return jnp.concatenate(
            [x_ref[pl.ds(l, n_chunk, stride=CMP_STRIDE), :].astype(bf16) for l in range(CMP_STRIDE)], axis=1)
    kc_ref[0] = _compress_rows(flat(xk_ref), wk_ref, pek_ref, w2k_ref).astype(bf16)
    vc_ref[0] = _compress_rows(flat(xv_ref), wv_ref, pev_ref, w2v_ref).T.astype(bf16)


def _compress_prompt(kvrow, cwk, cwv, *, nb, T):
    n_chunk = T // CMP_STRIDE
    const = lambda a: pl.BlockSpec(a.shape, lambda b: (0,) * a.ndim)
    return pl.pallas_call(
        functools.partial(_compress_prompt_body, n_chunk=n_chunk), grid=(nb,),
        in_specs=[pl.BlockSpec((T, 128), lambda b: (b, 0)), pl.BlockSpec((T, 128), lambda b: (b, 1))]
        + [const(a) for a in (*cwk, *cwv)],
        out_specs=[pl.BlockSpec((1, n_chunk, 128), lambda b: (b, 0, 0)),
                   pl.BlockSpec((1, 128, n_chunk), lambda b: (b, 0, 0))],
        out_shape=[jax.ShapeDtypeStruct((nb, n_chunk, 128), bf16), jax.ShapeDtypeStruct((nb, 128, n_chunk), bf16)],
        compiler_params=pltpu.CompilerParams(dimension_semantics=("parallel",), vmem_limit_bytes=VMEM_LIMIT),
        name="nsa_compress",
    )(kvrow, kvrow, *cwk, *cwv)


def _softmax_rows(s, mask):
    s = jnp.where(mask, s, NEG)
    e = jnp.exp2(s - jnp.max(s, axis=-1, keepdims=True))
    e = jnp.where(mask, e, 0.0)
    return e / jnp.maximum(jnp.sum(e, axis=-1, keepdims=True), TINY)


def _select_blocks(imp_t, qpos_row, n_blk):
    n = imp_t.shape[1]
    j = lax.broadcasted_iota(jnp.int32, imp_t.shape, 0)
    cur = qpos_row // SEL_BLOCK
    forced = (j == 0) | (j == cur) | (j == cur - 1)
    score = jnp.where(j <= cur, jnp.where(forced, FORCE_SCORE, imp_t), -1.0)
    n_tiles = -(-n_blk // 8)
    sub = lax.broadcasted_iota(jnp.int32, (8, n), 0)
    tiles = [score[8 * v:8 * v + 8] for v in range(n_tiles)]
    if n_blk % 8:
        tiles[-1] = jnp.where(sub < n_blk % 8, tiles[-1], -2.0)
    cnts = [jnp.zeros((8, n), f32)] * n_tiles
    for jp in range(n_blk):
        sj = tiles[jp // 8][jp % 8:jp % 8 + 1, :]
        for v in range(n_tiles):
            if v > jp // 8:
                beat = jnp.where(sj >= tiles[v], 1.0, 0.0)
            elif v < jp // 8:
                beat = jnp.where(sj > tiles[v], 1.0, 0.0)
            else:
                beat = jnp.where(sub > jp % 8, jnp.where(sj >= tiles[v], 1.0, 0.0), jnp.where(sj > tiles[v], 1.0, 0.0))
            cnts[v] = cnts[v] + beat
    keep = [jnp.where(c < float(min(SEL_TOP, n_blk)), 1.0, 0.0) for c in cnts]
    if n_blk % 8:
        keep[-1] = jnp.where(sub < n_blk % 8, keep[-1], 0.0)
    return jnp.concatenate(keep + [jnp.zeros((NBLK_PAD - 8 * n_tiles, n), f32)] * (NBLK_PAD > 8 * n_tiles), axis=0)


def _transpose_sel(sel_t):
    return jnp.concatenate([sel_t, jnp.zeros((128 - NBLK_PAD, sel_t.shape[1]), f32)], axis=0).T


def _softmax_cols(s, mask):
    s = jnp.where(mask, s, NEG)
    e = jnp.exp2(s - jnp.max(s, axis=0, keepdims=True))
    e = jnp.where(mask, e, 0.0)
    return e * (1.0 / jnp.maximum(jnp.sum(e, axis=0, keepdims=True), TINY))


def _lane_tile(x, n):
    return jnp.concatenate([x] * n, axis=1)


def _nsa_prompt_body(qct_ref, qrt_ref, gt_ref, kk_ref, vt_ref, kc_ref, vct_ref, ovt_ref, hbt_ref, sel_scr, *, T):
    i = pl.program_id(1)
    n_chunk = T // CMP_STRIDE
    n_blk = T // SEL_BLOCK
    q0 = i * TQ
    qpos_row = q0 + lax.broadcasted_iota(jnp.int32, (1, TQ), 1)
    gsig = jax.nn.sigmoid(gt_ref[...])
    n_kt = (q0 + TQ + TK - 1) // TK
    w0 = jnp.maximum(i - WINDOW // TQ, 0) * TQ
    WK = WINDOW + TQ
    blk_per_tile = TK // SEL_BLOCK
    ones_lhs = jnp.ones((16, TK), bf16)
    zeros_q = jnp.zeros((HEAD_DIM, TQ), bf16)

    def group_queries(ref, g):
        cols = []
        for r in range(R):
            q = ref[(g * R + r) * HEAD_DIM:(g * R + r + 1) * HEAD_DIM, :]
            cols.append(jnp.concatenate([q, zeros_q] if g == 0 else [zeros_q, q], axis=0))
        return jnp.concatenate(cols, axis=1)

    qrts, o_cs = [], []
    for g in range(N_KV):
        qct = group_queries(qct_ref, g)
        qrts.append(group_queries(qrt_ref, g))
        s = jnp.dot(kc_ref[0], qct, preferred_element_type=f32)
        cmp_end = lax.broadcasted_iota(jnp.int32, (n_chunk, 1), 0) * CMP_STRIDE + (CMP_LEN - 1)
        p = _softmax_cols(s, _lane_tile(cmp_end <= qpos_row, R))
        o_cs.append(jnp.dot(vct_ref[0], p.astype(bf16), preferred_element_type=f32))
        psum = p[:, 0:TQ] + p[:, TQ:2 * TQ] + p[:, 2 * TQ:3 * TQ] + p[:, 3 * TQ:4 * TQ]
        imp_t = jnp.dot(ovt_ref[...], psum, preferred_element_type=f32, precision=HIGHEST)
        sel_scr[g] = jnp.where(_select_blocks(imp_t, qpos_row, n_blk) > 0.5, 0.0, NEG)

    def kv_step(kt, carry):
        k0 = pl.multiple_of(kt * TK, TK)
        b0 = pl.multiple_of(kt * blk_per_tile, blk_per_tile)
        causal = (k0 + lax.broadcasted_iota(jnp.int32, (TK, 1), 0)) <= qpos_row
        out = []
        for g in range(N_KV):
            m_i, acc = carry[g]
            sc = jnp.dot(kk_ref[pl.ds(k0, TK), 0:128], qrts[g], preferred_element_type=f32)
            sel8 = sel_scr[g, pl.ds(b0, blk_per_tile), :]
            bias = jnp.broadcast_to(sel8[:, None, :], (blk_per_tile, SEL_BLOCK, TQ)).reshape(TK, TQ)
            sc = sc + _lane_tile(jnp.where(causal, bias, NEG), R)
            m_new = jnp.maximum(m_i, jnp.max(sc, axis=0, keepdims=True))
            alpha = jnp.exp2(m_i - m_new)
            pp = jnp.exp2(sc - m_new).astype(bf16)
            v_ext = jnp.concatenate([vt_ref[0:128, pl.ds(k0, TK)], ones_lhs], axis=0)
            acc = alpha * acc + jnp.dot(v_ext, pp, preferred_element_type=f32)
            out.append((m_new, acc))
        return tuple(out)

    init = (jnp.full((1, R * TQ), NEG, f32), jnp.zeros((128 + 16, R * TQ), f32))
    sel_out = lax.fori_loop(0, n_kt, kv_step, (init,) * N_KV)

    ones_w = jnp.ones((16, WINDOW + TQ), bf16)
    for g in range(N_KV):
        qrt, o_c = qrts[g], o_cs[g]
        acc = sel_out[g][1]
        o_s = acc[0:128] * (1.0 / acc[128:129])
        w0a = pl.multiple_of(w0, TQ)
        sw = jnp.dot(kk_ref[pl.ds(w0a, WK), 128:256], qrt, preferred_element_type=f32)
        dpos = qpos_row - (w0 + lax.broadcasted_iota(jnp.int32, (WK, 1), 0))
        biasw = jnp.where((dpos >= 0) & (dpos < WINDOW), 0.0, NEG)
        sw = sw + _lane_tile(biasw, R)
        pw = jnp.exp2(sw - jnp.max(sw, axis=0, keepdims=True)).astype(bf16)
        ow = jnp.dot(jnp.concatenate([vt_ref[128:256, pl.ds(w0a, WK)], ones_w], axis=0), pw,
                     preferred_element_type=f32)
        o_w = ow[0:128] * (1.0 / ow[128:129])
        rows = slice(g * HEAD_DIM, (g + 1) * HEAD_DIM)
        for r in range(R):
            h = g * R + r
            cs = slice(r * TQ, (r + 1) * TQ)
            c0 = 8 + 3 * h
            out = (gsig[c0:c0 + 1, :] * o_c[rows, cs] + gsig[c0 + 1:c0 + 2, :] * o_s[rows, cs]
                   + gsig[c0 + 2:c0 + 3, :] * o_w[rows, cs])
            hbt_ref[h * 64:(h + 1) * 64, :] = out.astype(bf16)


def _nsa_prompt(qct, qrt, gt, kk, vt, kc, vct, *, nb, T):
    nq = T // TQ
    n_chunk = T // CMP_STRIDE
    ovt = _sel_overlap_t(n_chunk, T // SEL_BLOCK)
    col = lambda h: pl.BlockSpec((h, TQ), lambda b, i: (0, b * nq + i))
    const = lambda a: pl.BlockSpec(a.shape, lambda b, i: (0,) * a.ndim)
    return pl.pallas_call(
        functools.partial(_nsa_prompt_body, T=T), grid=(nb, nq),
        in_specs=[col(512), col(512), col(32),
                  pl.BlockSpec((T, 256), lambda b, i: (b, 0)),
                  pl.BlockSpec((256, T), lambda b, i: (0, b)),
                  pl.BlockSpec((1, n_chunk, 128), lambda b, i: (b, 0, 0)),
                  pl.BlockSpec((1, 128, n_chunk), lambda b, i: (b, 0, 0)),
                  const(ovt)],
        out_specs=col(512),
        out_shape=jax.ShapeDtypeStruct((512, nb * T), bf16),
        scratch_shapes=[pltpu.VMEM((N_KV, NBLK_PAD, TQ), f32)],
        compiler_params=pltpu.CompilerParams(dimension_semantics=("parallel", "arbitrary"),
                                             vmem_limit_bytes=VMEM_LIMIT),
        name="nsa_prompt",
    )(qct, qrt, gt, kk, vt, kc, vct, ovt)


def _nsa_sample_body(pt_ref, *refs, n_pages, ts):
    n_pg = GB * n_pages
    pages = refs[:n_pg]
    (win_ref, qc_ref, qr_ref, sm_ref, kvn_ref, wn_ref, wk_ref, pek_ref, w2k_ref, wv_ref, pev_ref, w2v_ref,
     ovt_ref, e_ref) = refs[n_pg:n_pg + 14]
    hb_ref, wout_stack, xk_scr, xv_scr = refs[-4:]
    wout_ref = wout_stack.at[0]
    P = n_pages * PAGE
    n_chunk = P // CMP_STRIDE
    n_blk = -(-(P + ts) // SEL_BLOCK)
    WB = win_ref.shape[4]
    for bi in range(GB):
        for p in range(n_pages):
            pg = pages[bi * n_pages + p]
            prow = pl.ds(bi * P + p * PAGE, PAGE)
            xk_scr[prow, :] = pg[0, 0, 0].T
            xv_scr[prow, :] = pg[0, 0, 1].T

    def flat(x_ref):
        return jnp.concatenate(
            [x_ref[pl.ds(l, GB * n_chunk, stride=CMP_STRIDE), :].astype(bf16) for l in range(CMP_STRIDE)], axis=1)

    kc_all = _compress_rows(flat(xk_scr), wk_ref, pek_ref, w2k_ref).astype(bf16)
    vc_all = _compress_rows(flat(xv_scr), wv_ref, pev_ref, w2v_ref).astype(bf16)

    qc_all = qc_ref[...].astype(f32)
    qr_all = qr_ref[...].astype(f32)
    gsig = jax.nn.sigmoid(sm_ref[...])
    rows = R * SEG
    qpos_col = P + lax.broadcasted_iota(jnp.int32, (rows, 1), 0) % SEG
    qpos_row = P + lax.broadcasted_iota(jnp.int32, (1, 128), 1) % SEG
    new_lane = lax.broadcasted_iota(jnp.int32, (1, 128), 1)
    zpad = jnp.zeros((128 - SEG, 128), f32)

    def stack(q_all, bi, g):
        parts = []
        for r in range(R):
            h = g * R + r
            q = q_all[bi * SEG:(bi + 1) * SEG, (h // 2) * 128:(h // 2 + 1) * 128]
            if h % 2 != g:
                q = pltpu.roll(q, HEAD_DIM, axis=1)
            parts.append(jnp.where(new_lane // HEAD_DIM == g, q, 0.0))
        return jnp.concatenate(parts, axis=0).astype(bf16)

    cmp_end = lax.broadcasted_iota(jnp.int32, (1, n_chunk), 1) * CMP_STRIDE + (CMP_LEN - 1)
    o_cs, psums = [], []
    for bi in range(GB):
        for g in range(N_KV):
            s = lax.dot_general(stack(qc_all, bi, g), kc_all[bi * n_chunk:(bi + 1) * n_chunk], NT,
                                preferred_element_type=f32)
            p = _softmax_rows(s, cmp_end <= qpos_col)
            o_cs.append(jnp.dot(p.astype(bf16), vc_all[bi * n_chunk:(bi + 1) * n_chunk], preferred_element_type=f32))
            psums.append(p[0:SEG] + p[SEG:2 * SEG] + p[2 * SEG:3 * SEG] + p[3 * SEG:4 * SEG])
    psum_all = jnp.concatenate(psums + [jnp.zeros((128 - SEG * GB * N_KV, n_chunk), f32)], axis=0)
    imp_t = lax.dot_general(ovt_ref[...], psum_all, NT, preferred_element_type=f32, precision=HIGHEST)
    sel_all = _transpose_sel(_select_blocks(imp_t, qpos_row, n_blk)).astype(bf16)
    mk_all = jnp.dot(sel_all, e_ref[...], preferred_element_type=f32)

    out_rows = []
    for bi in range(GB):
        rs8 = slice(bi * SEG, (bi + 1) * SEG)
        ksel_t = [pages[bi * n_pages + pp][0, 0, 2].astype(bf16) for pp in range(n_pages)]
        vsel_t = [pages[bi * n_pages + pp][0, 0, 3].astype(bf16) for pp in range(n_pages)]
        knew = jnp.concatenate([kvn_ref[rs8, 256:384], zpad], axis=0).astype(bf16)
        vnew = jnp.concatenate([kvn_ref[rs8, 384:512], zpad], axis=0).astype(bf16)
        kwnew_f = jnp.concatenate([wn_ref[rs8, 0:128], zpad], axis=0)
        vwnew_f = jnp.concatenate([wn_ref[rs8, 128:256], zpad], axis=0)
        kwnew, vwnew = kwnew_f.astype(bf16), vwnew_f.astype(bf16)
        kwin_t = win_ref[0, bi, 0].astype(bf16)
        vwin_t = win_ref[0, bi, 1].astype(bf16)
        heads = [None] * N_HEADS
        for g in range(N_KV):
            idx = bi * N_KV + g
            qr = stack(qr_all, bi, g)
            o_c = o_cs[idx]
            mk = jnp.concatenate([mk_all[idx * SEG:(idx + 1) * SEG]] * R, axis=0) > 0.5
            s_sel = jnp.concatenate(
                [jnp.dot(qr, ksel_t[pp], preferred_element_type=f32) for pp in range(n_pages)]
                + [lax.dot_general(qr, knew, NT, preferred_element_type=f32)], axis=1)
            kpos = jnp.concatenate([lax.broadcasted_iota(jnp.int32, (1, P), 1), P + new_lane], axis=1)
            valid = jnp.concatenate([jnp.full((1, P), True), new_lane < SEG], axis=1)
            p_s = _softmax_rows(s_sel, mk & (kpos <= qpos_col) & valid).astype(bf16)
            o_s = jnp.dot(p_s[:, P:P + 128], vnew, preferred_element_type=f32)
            for pp in range(n_pages):
                o_s = o_s + lax.dot_general(p_s[:, pp * PAGE:(pp + 1) * PAGE], vsel_t[pp], NT,
                                            preferred_element_type=f32)
            s_w = jnp.concatenate([jnp.dot(qr, kwin_t, preferred_element_type=f32),
                                   lax.dot_general(qr, kwnew, NT, preferred_element_type=f32)], axis=1)
            kwpos = jnp.concatenate([P - WB + lax.broadcasted_iota(jnp.int32, (1, WB), 1), P + new_lane], axis=1)
            validw = jnp.concatenate([jnp.full((1, WB), True), new_lane < SEG], axis=1)
            dpos = qpos_col - kwpos
            p_w = _softmax_rows(s_w, (dpos >= 0) & (dpos < WINDOW) & (kwpos >= 0) & validw).astype(bf16)
            o_w = (lax.dot_general(p_w[:, 0:WB], vwin_t, NT, preferred_element_type=f32)
                   + jnp.dot(p_w[:, WB:WB + 128], vwnew, preferred_element_type=f32))
            for r in range(R):
                h = g * R + r
                rr = slice(r * SEG, (r + 1) * SEG)
                c0 = 8 + 3 * h
                gs = gsig[rs8]
                out = gs[:, c0:c0 + 1] * o_c[rr] + gs[:, c0 + 1:c0 + 2] * o_s[rr] + gs[:, c0 + 2:c0 + 3] * o_w[rr]
                if h % 2 != g:
                    out = pltpu.roll(out, HEAD_DIM, axis=1)
                heads[h] = out
        out_rows.append(jnp.concatenate(
            [jnp.where(new_lane < HEAD_DIM, heads[2 * j], heads[2 * j + 1]) for j in range(N_HEADS // 2)], axis=1))
        for kv, new_f in ((0, kwnew_f), (1, vwnew_f)):
            old = pltpu.roll(win_ref[0, bi, kv], WB - ts, axis=1)
            new_t = pltpu.roll(new_f.T, 128 - ts, axis=1)
            wout_ref[bi, kv, :, 0:WB - 128] = old[:, 0:WB - 128]
            wout_ref[bi, kv, :, WB - 128:WB] = jnp.where(new_lane >= 128 - ts, new_t, old[:, WB - 128:WB])
    hb_ref[...] = jnp.concatenate(out_rows, axis=0).astype(bf16)


def _nsa_sample(page_table, cache_t, cwin_t, qc, qr, sm, kvrow, winrow, cwk, cwv, win_stack, *, layer, row0, ts):
    nseq, n_pages = page_table.shape
    depth = cwin_t.shape[0]
    P = n_pages * PAGE
    WB = cwin_t.shape[4]
    n_chunk = P // CMP_STRIDE
    ovt = _sel_overlap_t(n_chunk, -(-(P + ts) // SEL_BLOCK))
    e = _block_expand(P + 128)
    b0 = row0 // (GB * SEG)

    def page_map(i, pt, *, bi, p):
        return (layer, pt[i * GB + bi, p], 0, 0, 0)

    page_specs = [pl.BlockSpec((1, 1, 4, 128, PAGE), functools.partial(page_map, bi=bi, p=p))
                  for bi in range(GB) for p in range(n_pages)]
    row = lambda w: pl.BlockSpec((GB * SEG, w), lambda i, pt: (b0 + i, 0))
    const = lambda a: pl.BlockSpec(a.shape, lambda i, pt: (0,) * a.ndim)
    extra_in, extra_spec, win_shape = _stacked_out(win_stack, depth, (nseq, 2, 128, WB))
    args = (page_table, *([cache_t] * (GB * n_pages)), cwin_t, qc, qr, sm, kvrow, winrow, *cwk, *cwv, ovt, e)
    grid_spec = pltpu.PrefetchScalarGridSpec(
        num_scalar_prefetch=1, grid=(nseq // GB,),
        in_specs=page_specs + [pl.BlockSpec((1, GB, 2, 128, WB), lambda i, pt: (layer, i, 0, 0, 0)),
                               row(N_WIDTH), row(N_WIDTH), row(128), row(512), row(256)]
        + [const(a) for a in (*cwk, *cwv, ovt, e)] + extra_spec,
        out_specs=[pl.BlockSpec((GB * SEG, N_WIDTH), lambda i, pt: (i, 0)),
                   pl.BlockSpec((1, GB, 2, 128, WB), lambda i, pt: (layer, i, 0, 0, 0))],
        scratch_shapes=[pltpu.VMEM((GB * P, 128), f32), pltpu.VMEM((GB * P, 128), f32)])
    return pl.pallas_call(
        functools.partial(_nsa_sample_body, n_pages=n_pages, ts=ts), grid_spec=grid_spec,
        out_shape=[jax.ShapeDtypeStruct((nseq * SEG, N_WIDTH), bf16), win_shape],
        input_output_aliases={len(args): 1} if extra_in else {},
        compiler_params=pltpu.CompilerParams(dimension_semantics=("parallel",), vmem_limit_bytes=VMEM_LIMIT),
        name="nsa_sample",
    )(*args, *extra_in)


def kernel(x_prompt, x_sample, cache_kv_pages, page_table, cache_win, state_mlstm_C, state_mlstm_n, state_mlstm_m, state_mlstm_conv, norm1_g, w_in, b_gates, conv_w, conv_b, w_mq, w_mk, mlstm_norm_g, cmp_pe_k, cmp_pe_v, cmp_w1_k, cmp_w2_k, cmp_w1_v, cmp_w2_v, w_branch_a, w_branch_b, w_out, norm2_g, w_ffn_up, w_ffn_down, final_norm_g):
    Bp, Tp, _ = x_prompt.shape
    Bs, Ts, _ = x_sample.shape
    depth = w_in.shape[0]
    n_pool, page = cache_kv_pages.shape[1:3]
    n_pages = page_table.shape[1]
    past_len = n_pages * page
    wb_len = cache_win.shape[2]
    mp = Bp * Tp
    ms = Bs * SAMPLE_PAD
    assert page == PAGE and M_CONV - 1 <= Ts <= SAMPLE_PAD and Tp % MLSTM_CHUNK == 0 and Tp >= WINDOW + TQ
    assert mp % (SEG * SEQ_PER_STEP) == 0 and Bs % SEQ_PER_STEP == 0 and Bs % GB == 0

    pad_seq = lambda a: jnp.pad(a, ((0, 0), (0, SAMPLE_PAD - a.shape[1]), (0, 0)))
    x_all = jnp.concatenate([x_prompt.reshape(mp, D_MODEL), pad_seq(x_sample).reshape(ms, D_MODEL)], axis=0)
    tabs = _rope_tables(Tp, past_len, ROW_TILE * (ms // ROW_TILE))
    col_idx, col_scale = _in_proj_columns()
    cache_t = jnp.transpose(cache_kv_pages, (0, 1, 3, 4, 5, 2)).reshape(depth, n_pool, 4, KV_W, PAGE)
    cwin_t = jnp.transpose(cache_win, (0, 1, 3, 4, 5, 2)).reshape(depth, Bs, 2, KV_W, wb_len)

    outs = {k: [] for k in ("p_win", "p_C", "p_n", "p_m", "p_conv", "s_kv", "s_n", "s_m", "s_conv")}
    c_stack = win_stack = kvt_stack = None
    y_parts = None
    for l in range(depth):
        w_ext = jnp.concatenate([w_in[l], jnp.zeros((D_MODEL, 1), f32)], axis=1)
        w_all = (w_ext[:, col_idx] * col_scale).astype(bf16)
        w_t = jnp.concatenate([w_in[l][:, O_IG:O_IG + 8], w_in[l][:, O_NG:O_NG + 24]], axis=1).T.astype(bf16)
        xm, sm, smt, qc, qr, qct, qrt, kvrow, winrow, kk, vt, gates, kvt_stack = _in_proj(
            x_all, norm1_g[l].reshape(1, D_MODEL), w_all, w_t, tabs, Tp, kvt_stack, layer=l, depth=depth)

        wq, wk = w_mq[l].astype(bf16), w_mk[l].astype(bf16)
        ha_p, c_p, n_p, m_p, cv_p = _mlstm_prompt(xm, sm, smt, conv_w[l], conv_b[l], wq, wk, b_gates[l], mlstm_norm_g[l],
                                                   nb=Bp, T=Tp, L=MLSTM_CHUNK)
        tail = jnp.pad(state_mlstm_conv[l], ((0, 0), (0, SEG - (M_CONV - 1)), (0, 0))).reshape(ms, M_WIDTH)
        n0 = jnp.pad(state_mlstm_n[l], ((0, 0), (0, 8 - M_HEADS), (0, 0)))
        m0 = jnp.broadcast_to(jnp.pad(state_mlstm_m[l], ((0, 0), (0, 128 - M_HEADS)))[:, None, :], (Bs, SEG, 128))
        ha_s, c_stack, n_s, mt_s = _mlstm_sample(
            xm, sm, smt, tail, state_mlstm_C[l], n0, m0.reshape(ms, 128), conv_w[l], conv_b[l], wq, wk, b_gates[l],
            mlstm_norm_g[l], c_stack, layer=l, depth=depth, row0=mp, nseq=Bs, n_valid=Ts)

        cwk = _compress_weights(cmp_pe_k[l], cmp_w1_k[l], cmp_w2_k[l])
        cwv = _compress_weights(cmp_pe_v[l], cmp_w1_v[l], cmp_w2_v[l])
        kc, vct = _compress_prompt(kvrow, cwk, cwv, nb=Bp, T=Tp)
        hbt_p = _nsa_prompt(qct, qrt, smt, kk, vt, kc, vct, nb=Bp, T=Tp)
        hb_s, win_stack = _nsa_sample(page_table, cache_t, cwin_t, qc, qr, sm, kvrow, winrow, cwk, cwv, win_stack,
                                      layer=l, row0=mp, ts=Ts)

        h_a = jnp.concatenate([ha_p, ha_s], axis=0)
        h_bt = jnp.concatenate([hbt_p, hb_s.T], axis=1)
        x_all, *y_parts = _mix_ffn(
            x_all, h_a, h_bt, gates, w_branch_a[l].astype(bf16), w_branch_b[l].astype(bf16), w_out[l].astype(bf16),
            norm2_g[l].reshape(1, D_MODEL), w_ffn_up[l].astype(bf16), w_ffn_down[l].astype(bf16),
            final_norm_g.reshape(1, D_MODEL), n_prompt_rows=mp, final=l == depth - 1)

        sample_rows = lambda a: a[mp:].reshape(Bs, SAMPLE_PAD, a.shape[1])
        wp = min(WINDOW, Tp)
        outs["p_win"].append(winrow[:mp].reshape(Bp, Tp, 2 * KV_W)[:, -wp:].reshape(Bp, wp, 2, N_KV, HEAD_DIM))
        outs["p_C"].append(c_p)
        outs["p_n"].append(n_p[:, :M_HEADS])
        outs["p_m"].append(m_p[:, 0, :M_HEADS])
        outs["p_conv"].append(cv_p[:, 8 - (M_CONV - 1):])
        outs["s_kv"].append(sample_rows(kvrow)[:, :Ts].reshape(Bs, Ts, 4, N_KV, HEAD_DIM))
        outs["s_n"].append(n_s[:, :M_HEADS])
        outs["s_m"].append(mt_s.reshape(Bs, SEG, 128)[:, SEG - 1, :M_HEADS])
        outs["s_conv"].append(sample_rows(xm)[:, Ts - (M_CONV - 1):Ts, :M_WIDTH])

    s_win = jnp.transpose(win_stack.reshape(depth, Bs, 2, N_KV, HEAD_DIM, wb_len), (0, 1, 5, 2, 3, 4))
    y_prompt = y_parts[0].reshape(Bp, Tp, D_MODEL)
    y_sample = y_parts[1].reshape(Bs, SAMPLE_PAD, D_MODEL)[:, :Ts]
    st = lambda k: jnp.stack(outs[k])
    p_kv = jnp.transpose(kvt_stack.reshape(depth, Bp, 4, N_KV, HEAD_DIM, Tp), (0, 1, 5, 2, 3, 4))
    return (y_prompt, y_sample, p_kv, st("p_win"), st("p_C"), st("p_n"), st("p_m"), st("p_conv"),
            st("s_kv"), s_win, c_stack, st("s_n"), st("s_m"), st("s_conv"))
```

```python
import functools

import jax
import jax.numpy as jnp
import numpy as np
from jax import lax
from jax.experimental import pallas as pl
from jax.experimental.pallas import tpu as pltpu

f32 = jnp.float32
bf16 = jnp.bfloat16

D_MODEL = 1024
M_HEADS = 4
M_WIDTH = 512
M_HEAD_DIM = 128
M_CONV = 4
MLSTM_CHUNK = 256
HEAD_DIM = 64
N_WIDTH = 512
N_HEADS = 8
N_KV = 2
KV_W = 128
CMP_STRIDE = 16
CMP_LEN = 32
SEL_BLOCK = 64
SEL_TOP = 16
WINDOW = 512
ROT_DIM = 16
ROPE_THETA = 500000.0
D_FF = 2816
EPS = 1e-6
NEG = -1e30
TINY = 1e-30
FORCE_SCORE = 1e9

SAMPLE_PAD = 8
SEG = SAMPLE_PAD
SEQ_PER_STEP = 16
GB = 2
PAGE = 128
HD = M_HEAD_DIM
R = N_HEADS // N_KV
CMP_HID = 128
TQ = 128
TK = 1024
NBLK_PAD = 64
HIGHEST = lax.Precision.HIGHEST
NT = (((1,), (1,)), ((), ()))
LOG2E = 1.4426950408889634
ROW_TILE = 256
FF_CHUNK = 1408
VMEM_LIMIT = 56 * 1024 * 1024

C_XM, C_Q, C_KV, C_WIN, C_GATE, C_SM = 0, 1536, 2048, 2560, 2816, 4864
C_TOTAL = 4992
O_U, O_V, O_O, O_IG, O_FG, O_Q, O_KV, O_NG, O_GA, O_GB = 0, 512, 1024, 1536, 1540, 1544, 2056, 2824, 2848, 3872
IN_WIDTH = 4896


def _in_proj_columns():
    z = IN_WIDTH
    idx = list(range(O_U, O_IG))
    idx += list(range(O_Q, O_Q + N_WIDTH))
    idx += list(range(O_KV, O_KV + 768))
    idx += list(range(O_GA, O_GA + 2048))
    idx += list(range(O_IG, O_IG + 8)) + list(range(O_NG, O_NG + 24)) + [z] * 96
    assert len(idx) == C_TOTAL
    scale = np.ones((C_TOTAL,), np.float32)
    scale[C_Q:C_KV] = HEAD_DIM ** -0.5
    return np.asarray(idx, np.int32), scale


def _rope_tables(seq, past_len, n_sample_rows):
    half = ROT_DIM // 2
    inv = ROPE_THETA ** (-jnp.arange(half, dtype=f32) / half)
    pos = jnp.concatenate([jnp.arange(seq), past_len + (jnp.arange(n_sample_rows) % SAMPLE_PAD)]).astype(f32)
    ang = pos[:, None] * inv[None, :]
    cos8, sin8 = jnp.cos(ang), jnp.sin(ang)
    n = pos.shape[0]
    one = jnp.ones((n, 64 - ROT_DIM), f32)
    zero = jnp.zeros((n, 64 - ROT_DIM), f32)
    z8 = jnp.zeros((n, half), f32)
    cos = jnp.concatenate([cos8, cos8, one], axis=1)
    sa = jnp.concatenate([-sin8, z8, zero], axis=1)
    sb = jnp.concatenate([z8, sin8, zero], axis=1)
    tile2 = lambda a: jnp.concatenate([a, a], axis=1)
    return tile2(cos), tile2(sa), tile2(sb)


def _rms(x, g):
    return x * lax.rsqrt(jnp.mean(x * x, axis=-1, keepdims=True) + EPS) * g


def _in_proj_body(x_ref, g_ref, w_ref, wt_ref, cos_ref, sa_ref, sb_ref, *refs):
    (xm_ref, sm_ref, smt_ref, qc_ref, qr_ref, qct_ref, qrt_ref, kv_ref, win_ref, kk_ref, vt_ref, gate_ref,
     kvt_stack) = refs[-13:]
    hb = _rms(x_ref[...], g_ref[...]).astype(bf16)
    cos, sa, sb = cos_ref[...], sa_ref[...], sb_ref[...]

    def rope(v):
        return v * cos + pltpu.roll(v, 128 - ROT_DIM // 2, axis=1) * sa + pltpu.roll(v, ROT_DIM // 2, axis=1) * sb

    def proj(c0, n):
        return jnp.dot(hb, w_ref[:, c0:c0 + n], preferred_element_type=f32)

    xm_ref[...] = proj(C_XM, 1536)
    sm_ref[...] = proj(C_SM, 128)
    smt_ref[...] = lax.dot_general(wt_ref[...], hb, NT, preferred_element_type=f32)
    qf = proj(C_Q, N_WIDTH) * LOG2E
    for j in range(N_HEADS // 2):
        sl = slice(j * 128, (j + 1) * 128)
        q = qf[:, sl]
        qrot = rope(q)
        qc_ref[:, sl] = q.astype(bf16)
        qr_ref[:, sl] = qrot.astype(bf16)
        qct_ref[sl, :] = q.T.astype(bf16)
        qrt_ref[sl, :] = qrot.T.astype(bf16)
    kv = proj(C_KV, 512)
    ksel = rope(kv[:, 256:384])
    kv_ref[:, 0:256] = kv[:, 0:256]
    kv_ref[:, 256:384] = ksel
    kv_ref[:, 384:512] = kv[:, 384:512]
    kvt_stack[0, 0, 0:256, :] = kv[:, 0:256].T
    kvt_stack[0, 0, 256:384, :] = ksel.T
    kvt_stack[0, 0, 384:512, :] = kv[:, 384:512].T
    win = proj(C_WIN, 256)
    kwin = rope(win[:, 0:128])
    win_ref[:, 0:128] = kwin
    win_ref[:, 128:256] = win[:, 128:256]
    kk_ref[:, 0:128] = ksel.astype(bf16)
    kk_ref[:, 128:256] = kwin.astype(bf16)
    vt_ref[0:128, :] = kv[:, 384:512].T.astype(bf16)
    vt_ref[128:256, :] = win[:, 128:256].T.astype(bf16)
    gate_ref[...] = proj(C_GATE, 2048)


def _in_proj(x_all, norm_g, w_all, w_t, tabs, seq, kvt_stack, *, layer, depth):
    m = x_all.shape[0]
    n_prompt_tiles_per_seq = seq // ROW_TILE
    n_prompt_tiles = (m - (tabs[0].shape[0] - seq)) // ROW_TILE
    n_seq = n_prompt_tiles // n_prompt_tiles_per_seq
    n_tiles = m // ROW_TILE
    n_sample_tiles = n_tiles - n_prompt_tiles

    def tile(i):
        return (i + n_prompt_tiles) % n_tiles

    def kvt_map(i):
        j = jnp.maximum(i - n_sample_tiles, 0)
        return (layer, j // n_prompt_tiles_per_seq, 0, j % n_prompt_tiles_per_seq)

    def tab_map(i):
        t = tile(i)
        return (jnp.where(t < n_prompt_tiles, t % n_prompt_tiles_per_seq, n_prompt_tiles_per_seq + t - n_prompt_tiles), 0)

    row = lambda w: pl.BlockSpec((ROW_TILE, w), lambda i: (tile(i), 0))
    const = lambda s: pl.BlockSpec(s, lambda i: (0, 0), pipeline_mode=pl.Buffered(1))
    tab = pl.BlockSpec((ROW_TILE, 128), tab_map)
    widths = (1536, 128, -32, 512, 512, -512, -512, 512, 256, 256, -256, 2048)
    dts = (f32, f32, f32, bf16, bf16, bf16, bf16, f32, f32, bf16, bf16, f32)
    out_shape = [jax.ShapeDtypeStruct((-w, m) if w < 0 else (m, w), d) for w, d in zip(widths, dts)]
    out_specs = [pl.BlockSpec((-w, ROW_TILE), lambda i: (0, tile(i))) if w < 0 else row(w) for w in widths]
    extra_in, extra_spec, kvt_shape = _stacked_out(kvt_stack, depth, (n_seq, 4 * KV_W, seq))
    args = (x_all, norm_g, w_all, w_t, *tabs)
    return pl.pallas_call(
        _in_proj_body, grid=(n_tiles,),
        in_specs=[row(D_MODEL), const((1, D_MODEL)), const((D_MODEL, C_TOTAL)), const((32, D_MODEL)), tab, tab, tab]
        + extra_spec,
        out_specs=out_specs + [pl.BlockSpec((1, 1, 4 * KV_W, ROW_TILE), kvt_map)],
        out_shape=out_shape + [kvt_shape], name="in_proj",
        input_output_aliases={len(args): len(out_shape)} if extra_in else {},
        compiler_params=pltpu.CompilerParams(dimension_semantics=("arbitrary",), vmem_limit_bytes=VMEM_LIMIT),
    )(*args, *extra_in)


def _mix_ffn_body(x_ref, ha_ref, hb_ref, gate_ref, wa_ref, wb_ref, wo_ref, g2_ref, wup_ref, wdn_ref, gf_ref,
                  xo_ref, *y_refs, n_prompt_tiles):
    a = jnp.dot(ha_ref[...].astype(bf16), wa_ref[...], preferred_element_type=f32)
    b = lax.dot_general(hb_ref[...], wb_ref[...], (((0,), (0,)), ((), ())), preferred_element_type=f32)
    merged = jax.nn.sigmoid(gate_ref[:, 0:D_MODEL]) * a + jax.nn.sigmoid(gate_ref[:, D_MODEL:2 * D_MODEL]) * b
    x1 = x_ref[...] + jnp.dot(merged.astype(bf16), wo_ref[...], preferred_element_type=f32)
    hn = _rms(x1, g2_ref[...]).astype(bf16)
    acc = x1
    for c in range(D_FF // FF_CHUNK):
        lo = c * FF_CHUNK
        g = jnp.dot(hn, wup_ref[:, lo:lo + FF_CHUNK], preferred_element_type=f32)
        u = jnp.dot(hn, wup_ref[:, D_FF + lo:D_FF + lo + FF_CHUNK], preferred_element_type=f32)
        act = (g * jax.nn.sigmoid(g) * u).astype(bf16)
        acc = acc + jnp.dot(act, wdn_ref[lo:lo + FF_CHUNK, :], preferred_element_type=f32)
    xo_ref[...] = acc
    if y_refs:
        y = _rms(acc, gf_ref[...])
        yp_ref, ys_ref = y_refs
        i = pl.program_id(0)

        @pl.when(i < n_prompt_tiles)
        def _():
            yp_ref[...] = y

        @pl.when(i >= n_prompt_tiles)
        def _():
            ys_ref[...] = y


def _mix_ffn(x_all, h_a, h_b, gates, wa, wb, wo, g2, wup, wdn, gf, *, n_prompt_rows, final):
    m = x_all.shape[0]
    npt = n_prompt_rows // ROW_TILE
    row = lambda w: pl.BlockSpec((ROW_TILE, w), lambda i: (i, 0))
    const = lambda s: pl.BlockSpec(s, lambda i: (0, 0), pipeline_mode=pl.Buffered(1))
    out_specs = [row(D_MODEL)]
    out_shape = [jax.ShapeDtypeStruct((m, D_MODEL), f32)]
    if final:
        out_specs += [pl.BlockSpec((ROW_TILE, D_MODEL), lambda i: (jnp.minimum(i, npt - 1), 0)),
                      pl.BlockSpec((ROW_TILE, D_MODEL), lambda i: (jnp.maximum(i - npt, 0), 0))]
        out_shape += [jax.ShapeDtypeStruct((n_prompt_rows, D_MODEL), f32),
                      jax.ShapeDtypeStruct((m - n_prompt_rows, D_MODEL), f32)]
    return pl.pallas_call(
        functools.partial(_mix_ffn_body, n_prompt_tiles=npt), grid=(m // ROW_TILE,),
        in_specs=[row(D_MODEL), row(h_a.shape[1]), pl.BlockSpec((h_b.shape[0], ROW_TILE), lambda i: (0, i)),
                  row(2 * D_MODEL),
                  const(wa.shape), const(wb.shape), const(wo.shape), const((1, D_MODEL)),
                  const(wup.shape), const(wdn.shape), const((1, D_MODEL))],
        out_specs=out_specs, out_shape=out_shape, name="mix_ffn",
        compiler_params=pltpu.CompilerParams(dimension_semantics=("arbitrary",), vmem_limit_bytes=VMEM_LIMIT),
    )(x_all, h_a, h_b, gates, wa, wb, wo, g2, wup, wdn, gf)


def _log_sigmoid(x):
    return jnp.minimum(x, 0.0) - jnp.log1p(jnp.exp(-jnp.abs(x)))


def _seg_scan(x, axis, seg, op, fill):
    idx = lax.broadcasted_iota(jnp.int32, x.shape, axis) % seg
    d = 1
    while d < seg:
        x = op(x, jnp.where(idx >= d, pltpu.roll(x, d, axis=axis), fill))
        d *= 2
    return x


def _conv_silu(u, tail, conv_w, conv_b, seg):
    L = u.shape[0]
    row = lax.broadcasted_iota(jnp.int32, u.shape, 0) % seg
    y = conv_b + u * conv_w[3:4, :]
    for k in (1, 2, 3):
        prev = tail if k == 3 else pltpu.roll(tail, L - (3 - k), axis=0)
        sh = jnp.where(row < k, prev, pltpu.roll(u, k, axis=0))
        y = y + sh * conv_w[3 - k:4 - k, :]
    return y * jax.nn.sigmoid(y)


def _gate_columns(sm, smt, bg_row, bg_col, m_vec, seg, n_valid):
    G = sm + bg_row
    lf = _log_sigmoid(G)
    ipre = G
    if n_valid < seg:
        rowc = lax.broadcasted_iota(jnp.int32, G.shape, 0) % seg
        lf = jnp.where(rowc < n_valid, lf, 0.0)
        ipre = jnp.where(rowc < n_valid, G, NEG)
    F = _seg_scan(lf, 0, seg, jnp.add, 0.0)
    F_al = pltpu.roll(F, 128 - M_HEADS, axis=1)
    a_col = ipre - F_al
    g_col = jnp.maximum(m_vec, _seg_scan(a_col, 0, seg, jnp.maximum, NEG))
    mt_col = F_al + g_col
    Gr = smt + bg_col
    lfr = _log_sigmoid(Gr)
    ir = Gr
    if n_valid < seg:
        lanec = lax.broadcasted_iota(jnp.int32, Gr.shape, 1) % seg
        lfr = jnp.where(lanec < n_valid, lfr, 0.0)
        ir = jnp.where(lanec < n_valid, Gr, NEG)
    Fr = _seg_scan(lfr, 1, seg, jnp.add, 0.0)
    a_row = ir - pltpu.roll(Fr, M_HEADS, axis=0)
    return a_col, g_col, mt_col, a_row


def _head_norm_gate(hh, ng, o):
    return hh * lax.rsqrt(jnp.mean(hh * hh, axis=-1, keepdims=True) + EPS) * ng * jax.nn.sigmoid(o)


def _mlstm_prompt_body(xm_ref, sm_ref, smt_ref, cw_ref, cb_ref, wq_ref, wk_ref, bgr_ref, bgc_ref, ng_ref,
                       ha_ref, c_out, n_out, m_out, conv_out, c_scr, n_scr, m_scr, tail_scr, *, L):
    c_idx = pl.program_id(1)

    @pl.when(c_idx == 0)
    def _():
        c_scr[...] = jnp.zeros_like(c_scr)
        n_scr[...] = jnp.zeros_like(n_scr)
        m_scr[...] = jnp.zeros_like(m_scr)
        tail_scr[...] = jnp.zeros_like(tail_scr)

    u = xm_ref[:, 0:M_WIDTH]
    tail = jnp.concatenate([tail_scr[...], jnp.zeros((L - 8, M_WIDTH), f32)], axis=0)
    cact = _conv_silu(u, tail, cw_ref[...], cb_ref[...], L)
    tail_scr[...] = pltpu.roll(u[L - 8:L, :], 3, axis=0)
    conv_out[0] = u[L - 8:L, :]

    m_vec = m_scr[0:1, :]
    a_col, g_col, mt_col, a_row = _gate_columns(sm_ref[...], smt_ref[...], bgr_ref[...], bgc_ref[...], m_vec, L, L)
    inter_col = jnp.exp(m_vec - g_col)
    floor_col = jnp.exp(-mt_col)
    g_last = g_col[L - 1:L, :]
    w_col = jnp.exp(a_col - g_last)
    decay = jnp.exp(m_vec - g_last)
    m_new = mt_col[L - 1:L, :]
    ti = lax.broadcasted_iota(jnp.int32, (L, L), 0)
    si = lax.broadcasted_iota(jnp.int32, (L, L), 1)
    causal = si <= ti
    scale = HD ** -0.5
    for h in range(M_HEADS):
        sl = slice(h * HD, (h + 1) * HD)
        ch = cact[:, sl].astype(bf16)
        qh = jnp.dot(ch, wq_ref[h], preferred_element_type=f32)
        kh = jnp.dot(ch, wk_ref[h], preferred_element_type=f32) * scale
        vh = xm_ref[:, M_WIDTH + h * HD:M_WIDTH + (h + 1) * HD].astype(bf16)
        qb = qh.astype(bf16)
        D = jnp.exp(jnp.where(causal, a_row[h:h + 1, :] - g_col[:, h:h + 1], NEG))
        S = lax.dot_general(qb, kh.astype(bf16), NT, preferred_element_type=f32) * D
        C = c_scr[h]
        nrow = n_scr[h:h + 1, :]
        ic = inter_col[:, h:h + 1]
        num = jnp.dot(S.astype(bf16), vh, preferred_element_type=f32) + ic * jnp.dot(
            qb, C.astype(bf16), preferred_element_type=f32)
        den = jnp.sum(S, axis=-1, keepdims=True) + ic * jnp.sum(qh * nrow, axis=-1, keepdims=True)
        hh = num / jnp.maximum(jnp.abs(den), floor_col[:, h:h + 1])
        o = xm_ref[:, 2 * M_WIDTH + h * HD:2 * M_WIDTH + (h + 1) * HD]
        ha_ref[:, sl] = _head_norm_gate(hh, ng_ref[:, sl], o)
        kw = kh * w_col[:, h:h + 1]
        dh = decay[:, h:h + 1]
        c_scr[h] = dh * C + lax.dot_general(kw.astype(bf16), vh, (((0,), (0,)), ((), ())), preferred_element_type=f32)
        n_scr[h:h + 1, :] = dh * nrow + jnp.sum(kw, axis=0, keepdims=True)
    m_scr[0:1, :] = m_new

    @pl.when(c_idx == pl.num_programs(1) - 1)
    def _():
        c_out[0] = c_scr[...]
        n_out[0] = n_scr[...]
        m_out[0] = m_scr[...]


def _mlstm_prompt(xm, sm, smt, conv_w, conv_b, wq, wk, b_gates, norm_g, *, nb, T, L):
    nc = T // L
    bg_row = jnp.zeros((1, 128), f32).at[0, :8].set(b_gates)
    bg_col = b_gates.reshape(8, 1)
    const = lambda s: pl.BlockSpec(s, lambda b, c: (0,) * len(s))
    return pl.pallas_call(
        functools.partial(_mlstm_prompt_body, L=L), grid=(nb, nc),
        in_specs=[pl.BlockSpec((L, 3 * M_WIDTH), lambda b, c: (b * nc + c, 0)),
                  pl.BlockSpec((L, 128), lambda b, c: (b * nc + c, 0)),
                  pl.BlockSpec((8, L), lambda b, c: (0, b * nc + c)),
                  const((4, M_WIDTH)), const((1, M_WIDTH)), const((M_HEADS, HD, HD)), const((M_HEADS, HD, HD)),
                  const((1, 128)), const((8, 1)), const((1, M_WIDTH))],
        out_specs=[pl.BlockSpec((L, M_WIDTH), lambda b, c: (b * nc + c, 0)),
                   pl.BlockSpec((1, M_HEADS, HD, HD), lambda b, c: (b, 0, 0, 0)),
                   pl.BlockSpec((1, 8, HD), lambda b, c: (b, 0, 0)),
                   pl.BlockSpec((1, 8, 128), lambda b, c: (b, 0, 0)),
                   pl.BlockSpec((1, 8, M_WIDTH), lambda b, c: (b, 0, 0))],
        out_shape=[jax.ShapeDtypeStruct((nb * T, M_WIDTH), f32),
                   jax.ShapeDtypeStruct((nb, M_HEADS, HD, HD), f32),
                   jax.ShapeDtypeStruct((nb, 8, HD), f32),
                   jax.ShapeDtypeStruct((nb, 8, 128), f32),
                   jax.ShapeDtypeStruct((nb, 8, M_WIDTH), f32)],
        scratch_shapes=[pltpu.VMEM((M_HEADS, HD, HD), f32), pltpu.VMEM((8, HD), f32), pltpu.VMEM((8, 128), f32),
                        pltpu.VMEM((8, M_WIDTH), f32)],
        compiler_params=pltpu.CompilerParams(dimension_semantics=("parallel", "arbitrary")),
        name="mlstm_prompt",
    )(xm, sm, smt, conv_w, conv_b.reshape(1, M_WIDTH), wq, wk, bg_row, bg_col, norm_g.reshape(1, M_WIDTH))


def _mlstm_sample_body(xm_ref, sm_ref, smt_ref, tail_ref, c_in, n_in, m_in, cw_ref, cb_ref, wq_ref, wk_ref,
                       bgr_ref, bgc_ref, ng_ref, *out_refs, n_valid):
    ha_ref, c_stack, n_out, m_out = out_refs[-4:]
    c_out = c_stack.at[0]
    L = SEG * SEQ_PER_STEP
    u = xm_ref[:, 0:M_WIDTH]
    cact = _conv_silu(u, tail_ref[...], cw_ref[...], cb_ref[...], SEG)
    m_rows = m_in[...]
    a_col, g_col, mt_col, a_row = _gate_columns(sm_ref[...], smt_ref[...], bgr_ref[...], bgc_ref[...], m_rows, SEG,
                                                n_valid)
    inter_col = jnp.exp(m_rows - g_col)
    floor_col = jnp.exp(-mt_col)
    ti = lax.broadcasted_iota(jnp.int32, (L, L), 0)
    si = lax.broadcasted_iota(jnp.int32, (L, L), 1)
    mask = (si <= ti) & ((si // SEG) == (ti // SEG))
    scale = HD ** -0.5
    for h in range(M_HEADS):
        sl = slice(h * HD, (h + 1) * HD)
        ch = cact[:, sl].astype(bf16)
        qh = jnp.dot(ch, wq_ref[h], preferred_element_type=f32)
        kh = jnp.dot(ch, wk_ref[h], preferred_element_type=f32) * scale
        vh = xm_ref[:, M_WIDTH + h * HD:M_WIDTH + (h + 1) * HD].astype(bf16)
        qb = qh.astype(bf16)
        D = jnp.exp(jnp.where(mask, a_row[h:h + 1, :] - g_col[:, h:h + 1], NEG))
        S = lax.dot_general(qb, kh.astype(bf16), NT, preferred_element_type=f32) * D
        num = jnp.dot(S.astype(bf16), vh, preferred_element_type=f32)
        den = jnp.sum(S, axis=-1, keepdims=True)
        inter_num, inter_den = [], []
        for s in range(SEQ_PER_STEP):
            rs = slice(s * SEG, (s + 1) * SEG)
            C = c_in[s, h]
            nrow = n_in[s, h:h + 1, :]
            inter_num.append(jnp.dot(qb[rs], C.astype(bf16), preferred_element_type=f32))
            inter_den.append(jnp.sum(qh[rs] * nrow, axis=-1, keepdims=True))
            g_last = g_col[s * SEG + SEG - 1:s * SEG + SEG, h:h + 1]
            m_prev = m_rows[s * SEG:s * SEG + 1, h:h + 1]
            w = jnp.exp(a_col[rs, h:h + 1] - g_last)
            dh = jnp.exp(m_prev - g_last)
            kw = kh[rs] * w
            c_out[s, h] = dh * C + lax.dot_general(kw.astype(bf16), vh[rs], (((0,), (0,)), ((), ())),
                                                    preferred_element_type=f32)
            n_out[s, h:h + 1, :] = dh * nrow + jnp.sum(kw, axis=0, keepdims=True)
        ic = inter_col[:, h:h + 1]
        num = num + ic * jnp.concatenate(inter_num, axis=0)
        den = den + ic * jnp.concatenate(inter_den, axis=0)
        hh = num / jnp.maximum(jnp.abs(den), floor_col[:, h:h + 1])
        o = xm_ref[:, 2 * M_WIDTH + h * HD:2 * M_WIDTH + (h + 1) * HD]
        ha_ref[:, sl] = _head_norm_gate(hh, ng_ref[:, sl], o)
    for s in range(SEQ_PER_STEP):
        n_out[s, M_HEADS:8, :] = jnp.zeros((8 - M_HEADS, HD), f32)
    m_out[...] = mt_col


def _stacked_out(prev, depth, shape):
    extra_in = [] if prev is None else [prev]
    extra_spec = [] if prev is None else [pl.BlockSpec(memory_space=pl.ANY)]
    return extra_in, extra_spec, jax.ShapeDtypeStruct((depth,) + shape, f32)


def _mlstm_sample(xm, sm, smt, tail, c0, n0, m0, conv_w, conv_b, wq, wk, b_gates, norm_g, c_stack, *, layer, depth,
                  row0, nseq, n_valid):
    L = SEG * SEQ_PER_STEP
    b0 = row0 // L
    bg_row = jnp.zeros((1, 128), f32).at[0, :8].set(b_gates)
    bg_col = b_gates.reshape(8, 1)
    const = lambda s: pl.BlockSpec(s, lambda i: (0,) * len(s))
    extra_in, extra_spec, c_shape = _stacked_out(c_stack, depth, (nseq, M_HEADS, HD, HD))
    n_in = 14
    return pl.pallas_call(
        functools.partial(_mlstm_sample_body, n_valid=n_valid), grid=(nseq // SEQ_PER_STEP,),
        in_specs=[pl.BlockSpec((L, 3 * M_WIDTH), lambda i: (b0 + i, 0)),
                  pl.BlockSpec((L, 128), lambda i: (b0 + i, 0)),
                  pl.BlockSpec((8, L), lambda i: (0, b0 + i)),
                  pl.BlockSpec((L, M_WIDTH), lambda i: (i, 0)),
                  pl.BlockSpec((SEQ_PER_STEP, M_HEADS, HD, HD), lambda i: (i, 0, 0, 0)),
                  pl.BlockSpec((SEQ_PER_STEP, 8, HD), lambda i: (i, 0, 0)),
                  pl.BlockSpec((L, 128), lambda i: (i, 0)),
                  const((4, M_WIDTH)), const((1, M_WIDTH)), const((M_HEADS, HD, HD)), const((M_HEADS, HD, HD)),
                  const((1, 128)), const((8, 1)), const((1, M_WIDTH))] + extra_spec,
        out_specs=[pl.BlockSpec((L, M_WIDTH), lambda i: (i, 0)),
                   pl.BlockSpec((1, SEQ_PER_STEP, M_HEADS, HD, HD), lambda i: (layer, i, 0, 0, 0)),
                   pl.BlockSpec((SEQ_PER_STEP, 8, HD), lambda i: (i, 0, 0)),
                   pl.BlockSpec((L, 128), lambda i: (i, 0))],
        out_shape=[jax.ShapeDtypeStruct((nseq * SEG, M_WIDTH), f32), c_shape,
                   jax.ShapeDtypeStruct((nseq, 8, HD), f32),
                   jax.ShapeDtypeStruct((nseq * SEG, 128), f32)],
        input_output_aliases={n_in: 1} if extra_in else {},
        compiler_params=pltpu.CompilerParams(dimension_semantics=("parallel",), vmem_limit_bytes=VMEM_LIMIT),
        name="mlstm_sample",
    )(xm, sm, smt, tail, c0, n0, m0, conv_w, conv_b.reshape(1, M_WIDTH), wq, wk, bg_row, bg_col,
      norm_g.reshape(1, M_WIDTH), *extra_in)


def _sel_overlap_t(n_chunk, n_blk):
    n_cmp = n_chunk - 1
    start = np.arange(n_cmp) * CMP_STRIDE
    bs = np.arange(n_blk) * SEL_BLOCK
    ov = np.minimum(start[:, None] + CMP_LEN, bs[None, :] + SEL_BLOCK) - np.maximum(start[:, None], bs[None, :])
    ov = np.clip(ov, 0, None) / CMP_LEN
    out = np.zeros((NBLK_PAD, n_chunk), np.float32)
    out[:n_blk, :n_cmp] = ov.T
    return jnp.asarray(out)


def _block_expand(n_keys):
    e = (np.arange(n_keys)[None, :] // SEL_BLOCK) == np.arange(128)[:, None]
    return jnp.asarray(e, dtype=bf16)


def _compress_weights(pe, w1, w2):
    W = jnp.zeros((16, 2, 64, 2, 2, CMP_HID), f32)
    for g in range(N_KV):
        W = W.at[:, g, :, g, 0, :].set(w1[:16])
        W = W.at[:, g, :, g, 1, :].set(w1[16:])
    W = W.reshape(2048, 512)
    peA = jnp.broadcast_to(pe[:16, None, :], (16, 2, 64)).reshape(1, 2048)
    peB = jnp.broadcast_to(pe[16:, None, :], (16, 2, 64)).reshape(1, 2048)
    PE = jnp.concatenate([peA, peB, jnp.zeros((14, 2048), f32)], axis=0)
    W2 = jnp.zeros((2, CMP_HID, 128), f32).at[0, :, 0:64].set(w2).at[1, :, 64:128].set(w2)
    return W.astype(bf16), PE.astype(bf16), W2.astype(bf16)


def _compress_rows(flat, w_ref, pe_ref, w2_ref):
    n = flat.shape[0]
    y = jnp.dot(jnp.concatenate([flat, pe_ref[...]], axis=0), w_ref[...], preferred_element_type=f32)
    c = y[n:n + 8]
    out = jnp.zeros((n, w2_ref.shape[2]), f32)
    for g in range(N_KV):
        a = y[0:n, (2 * g) * CMP_HID:(2 * g + 1) * CMP_HID] + c[0:1, (2 * g) * CMP_HID:(2 * g + 1) * CMP_HID]
        b = (y[0:n, (2 * g + 1) * CMP_HID:(2 * g + 2) * CMP_HID]
             + c[1:2, (2 * g + 1) * CMP_HID:(2 * g + 2) * CMP_HID])
        hid = a + pltpu.roll(b, n - 1, axis=0)
        out = out + jnp.dot(jax.nn.gelu(hid, approximate=True).astype(bf16), w2_ref[g], preferred_element_type=f32)
    return out


def _compress_prompt_body(xk_ref, xv_ref, wk_ref, pek_ref, w2k_ref, wv_ref, pev_ref, w2v_ref, kc_ref, vc_ref, *,
                          n_chunk):
    def flat(x_ref):
        return jnp.concatenate(
            [x_ref[pl.ds(l, n_chunk, stride=CMP_STRIDE), :].astype(bf16) for l in range(CMP_STRIDE)], axis=1)
    kc_ref[0] = _compress_rows(flat(xk_ref), wk_ref, pek_ref, w2k_ref).astype(bf16)
    vc_ref[0] = _compress_rows(flat(xv_ref), wv_ref, pev_ref, w2v_ref).T.astype(bf16)


def _compress_prompt(kvrow, cwk, cwv, *, nb, T):
    n_chunk = T // CMP_STRIDE
    const = lambda a: pl.BlockSpec(a.shape, lambda b: (0,) * a.ndim)
    return pl.pallas_call(
        functools.partial(_compress_prompt_body, n_chunk=n_chunk), grid=(nb,),
        in_specs=[pl.BlockSpec((T, 128), lambda b: (b, 0)), pl.BlockSpec((T, 128), lambda b: (b, 1))]
        + [const(a) for a in (*cwk, *cwv)],
        out_specs=[pl.BlockSpec((1, n_chunk, 128), lambda b: (b, 0, 0)),
                   pl.BlockSpec((1, 128, n_chunk), lambda b: (b, 0, 0))],
        out_shape=[jax.ShapeDtypeStruct((nb, n_chunk, 128), bf16), jax.ShapeDtypeStruct((nb, 128, n_chunk), bf16)],
        compiler_params=pltpu.CompilerParams(dimension_semantics=("parallel",), vmem_limit_bytes=VMEM_LIMIT),
        name="nsa_compress",
    )(kvrow, kvrow, *cwk, *cwv)


def _softmax_rows(s, mask):
    s = jnp.where(mask, s, NEG)
    e = jnp.exp2(s - jnp.max(s, axis=-1, keepdims=True))
    e = jnp.where(mask, e, 0.0)
    return e / jnp.maximum(jnp.sum(e, axis=-1, keepdims=True), TINY)


def _select_blocks(imp_t, qpos_row, n_blk):
    n = imp_t.shape[1]
    j = lax.broadcasted_iota(jnp.int32, imp_t.shape, 0)
    cur = qpos_row // SEL_BLOCK
    forced = (j == 0) | (j == cur) | (j == cur - 1)
    score = jnp.where(j <= cur, jnp.where(forced, FORCE_SCORE, imp_t), -1.0)
    n_tiles = -(-n_blk // 8)
    sub = lax.broadcasted_iota(jnp.int32, (8, n), 0)
    tiles = [score[8 * v:8 * v + 8] for v in range(n_tiles)]
    if n_blk % 8:
        tiles[-1] = jnp.where(sub < n_blk % 8, tiles[-1], -2.0)
    cnts = [jnp.zeros((8, n), f32)] * n_tiles
    for jp in range(n_blk):
        sj = tiles[jp // 8][jp % 8:jp % 8 + 1, :]
        for v in range(n_tiles):
            if v > jp // 8:
                beat = jnp.where(sj >= tiles[v], 1.0, 0.0)
            elif v < jp // 8:
                beat = jnp.where(sj > tiles[v], 1.0, 0.0)
            else:
                beat = jnp.where(sub > jp % 8, jnp.where(sj >= tiles[v], 1.0, 0.0), jnp.where(sj > tiles[v], 1.0, 0.0))
            cnts[v] = cnts[v] + beat
    keep = [jnp.where(c < float(min(SEL_TOP, n_blk)), 1.0, 0.0) for c in cnts]
    if n_blk % 8:
        keep[-1] = jnp.where(sub < n_blk % 8, keep[-1], 0.0)
    return jnp.concatenate(keep + [jnp.zeros((NBLK_PAD - 8 * n_tiles, n), f32)] * (NBLK_PAD > 8 * n_tiles), axis=0)


def _transpose_sel(sel_t):
    return jnp.concatenate([sel_t, jnp.zeros((128 - NBLK_PAD, sel_t.shape[1]), f32)], axis=0).T


def _softmax_cols(s, mask):
    s = jnp.where(mask, s, NEG)
    e = jnp.exp2(s - jnp.max(s, axis=0, keepdims=True))
    e = jnp.where(mask, e, 0.0)
    return e * (1.0 / jnp.maximum(jnp.sum(e, axis=0, keepdims=True), TINY))


def _lane_tile(x, n):
    return jnp.concatenate([x] * n, axis=1)


def _nsa_prompt_body(qct_ref, qrt_ref, gt_ref, kk_ref, vt_ref, kc_ref, vct_ref, ovt_ref, hbt_ref, sel_scr, *, T):
    i = pl.program_id(1)
    n_chunk = T // CMP_STRIDE
    n_blk = T // SEL_BLOCK
    q0 = i * TQ
    qpos_row = q0 + lax.broadcasted_iota(jnp.int32, (1, TQ), 1)
    gsig = jax.nn.sigmoid(gt_ref[...])
    n_kt = (q0 + TQ + TK - 1) // TK
    w0 = jnp.maximum(i - WINDOW // TQ, 0) * TQ
    WK = WINDOW + TQ
    blk_per_tile = TK // SEL_BLOCK
    ones_lhs = jnp.ones((16, TK), bf16)
    zeros_q = jnp.zeros((HEAD_DIM, TQ), bf16)

    def group_queries(ref, g):
        cols = []
        for r in range(R):
            q = ref[(g * R + r) * HEAD_DIM:(g * R + r + 1) * HEAD_DIM, :]
            cols.append(jnp.concatenate([q, zeros_q] if g == 0 else [zeros_q, q], axis=0))
        return jnp.concatenate(cols, axis=1)

    qrts, o_cs = [], []
    for g in range(N_KV):
        qct = group_queries(qct_ref, g)
        qrts.append(group_queries(qrt_ref, g))
        s = jnp.dot(kc_ref[0], qct, preferred_element_type=f32)
        cmp_end = lax.broadcasted_iota(jnp.int32, (n_chunk, 1), 0) * CMP_STRIDE + (CMP_LEN - 1)
        p = _softmax_cols(s, _lane_tile(cmp_end <= qpos_row, R))
        o_cs.append(jnp.dot(vct_ref[0], p.astype(bf16), preferred_element_type=f32))
        psum = p[:, 0:TQ] + p[:, TQ:2 * TQ] + p[:, 2 * TQ:3 * TQ] + p[:, 3 * TQ:4 * TQ]
        imp_t = jnp.dot(ovt_ref[...], psum, preferred_element_type=f32, precision=HIGHEST)
        sel_scr[g] = jnp.where(_select_blocks(imp_t, qpos_row, n_blk) > 0.5, 0.0, NEG)

    def kv_step(kt, carry):
        k0 = pl.multiple_of(kt * TK, TK)
        b0 = pl.multiple_of(kt * blk_per_tile, blk_per_tile)
        causal = (k0 + lax.broadcasted_iota(jnp.int32, (TK, 1), 0)) <= qpos_row
        m_i, acc = carry
        sc = jnp.dot(kk_ref[pl.ds(k0, TK), 0:128], qrt_all, preferred_element_type=f32)
        biases = []
        for g in range(N_KV):
            sel8 = sel_scr[g, pl.ds(b0, blk_per_tile), :]
            bias = jnp.broadcast_to(sel8[:, None, :], (blk_per_tile, SEL_BLOCK, TQ)).reshape(TK, TQ)
            biases.append(_lane_tile(jnp.where(causal, bias, NEG), R))
        sc = sc + jnp.concatenate(biases, axis=1)
        m_new = jnp.maximum(m_i, jnp.max(sc, axis=0, keepdims=True))
        alpha = jnp.exp2(m_i - m_new)
        pp = jnp.exp2(sc - m_new).astype(bf16)
        v_ext = jnp.concatenate([vt_ref[0:128, pl.ds(k0, TK)], ones_lhs], axis=0)
        return m_new, alpha * acc + jnp.dot(v_ext, pp, preferred_element_type=f32)

    qrt_all = jnp.concatenate(qrts, axis=1)
    init = (jnp.full((1, N_KV * R * TQ), NEG, f32), jnp.zeros((128 + 16, N_KV * R * TQ), f32))
    _, acc_all = lax.fori_loop(0, n_kt, kv_step, init)
    sel_out = [(None, acc_all[:, g * R * TQ:(g + 1) * R * TQ]) for g in range(N_KV)]

    ones_w = jnp.ones((16, WINDOW + TQ), bf16)
    for g in range(N_KV):
        qrt, o_c = qrts[g], o_cs[g]
        acc = sel_out[g][1]
        o_s = acc[0:128] * (1.0 / acc[128:129])
        w0a = pl.multiple_of(w0, TQ)
        sw = jnp.dot(kk_ref[pl.ds(w0a, WK), 128:256], qrt, preferred_element_type=f32)
        dpos = qpos_row - (w0 + lax.broadcasted_iota(jnp.int32, (WK, 1), 0))
        biasw = jnp.where((dpos >= 0) & (dpos < WINDOW), 0.0, NEG)
        sw = sw + _lane_tile(biasw, R)
        pw = jnp.exp2(sw - jnp.max(sw, axis=0, keepdims=True)).astype(bf16)
        ow = jnp.dot(jnp.concatenate([vt_ref[128:256, pl.ds(w0a, WK)], ones_w], axis=0), pw,
                     preferred_element_type=f32)
        o_w = ow[0:128] * (1.0 / ow[128:129])
        rows = slice(g * HEAD_DIM, (g + 1) * HEAD_DIM)
        for r in range(R):
            h = g * R + r
            cs = slice(r * TQ, (r + 1) * TQ)
            c0 = 8 + 3 * h
            out = (gsig[c0:c0 + 1, :] * o_c[rows, cs] + gsig[c0 + 1:c0 + 2, :] * o_s[rows, cs]
                   + gsig[c0 + 2:c0 + 3, :] * o_w[rows, cs])
            hbt_ref[h * 64:(h + 1) * 64, :] = out.astype(bf16)


def _nsa_prompt(qct, qrt, gt, kk, vt, kc, vct, *, nb, T):
    nq = T // TQ
    n_chunk = T // CMP_STRIDE
    ovt = _sel_overlap_t(n_chunk, T // SEL_BLOCK)
    col = lambda h: pl.BlockSpec((h, TQ), lambda b, i: (0, b * nq + i))
    const = lambda a: pl.BlockSpec(a.shape, lambda b, i: (0,) * a.ndim)
    return pl.pallas_call(
        functools.partial(_nsa_prompt_body, T=T), grid=(nb, nq),
        in_specs=[col(512), col(512), col(32),
                  pl.BlockSpec((T, 256), lambda b, i: (b, 0)),
                  pl.BlockSpec((256, T), lambda b, i: (0, b)),
                  pl.BlockSpec((1, n_chunk, 128), lambda b, i: (b, 0, 0)),
                  pl.BlockSpec((1, 128, n_chunk), lambda b, i: (b, 0, 0)),
                  const(ovt)],
        out_specs=col(512),
        out_shape=jax.ShapeDtypeStruct((512, nb * T), bf16),
        scratch_shapes=[pltpu.VMEM((N_KV, NBLK_PAD, TQ), f32)],
        compiler_params=pltpu.CompilerParams(dimension_semantics=("parallel", "arbitrary"),
                                             vmem_limit_bytes=VMEM_LIMIT),
        name="nsa_prompt",
    )(qct, qrt, gt, kk, vt, kc, vct, ovt)


def _nsa_sample_body(pt_ref, *refs, n_pages, ts):
    n_pg = GB * n_pages
    pages = refs[:n_pg]
    (win_ref, qc_ref, qr_ref, sm_ref, kvn_ref, wn_ref, wk_ref, pek_ref, w2k_ref, wv_ref, pev_ref, w2v_ref,
     ovt_ref, e_ref, perm_ref) = refs[n_pg:n_pg + 15]
    hb_ref, wout_stack = refs[-2:]
    wout_ref = wout_stack.at[0]
    P = n_pages * PAGE
    n_chunk = P // CMP_STRIDE
    n_blk = -(-(P + ts) // SEL_BLOCK)
    WB = win_ref.shape[4]
    chunks_per_page = PAGE // CMP_STRIDE

    xt_stack = jnp.concatenate([pages[j][0, 0, c].astype(bf16) for j in range(n_pg) for c in (0, 1)], axis=0)
    y_all = lax.dot_general(perm_ref[...], xt_stack, NT, preferred_element_type=f32)

    def flat(c):
        return jnp.concatenate(
            [jnp.concatenate([y_all[l * chunks_per_page:(l + 1) * chunks_per_page, (2 * j + c) * 128:(2 * j + c + 1) * 128]
                              for j in range(n_pg)], axis=0)
             for l in range(CMP_STRIDE)], axis=1).astype(bf16)

    kc_all = _compress_rows(flat(0), wk_ref, pek_ref, w2k_ref).astype(bf16)
    vc_all = _compress_rows(flat(1), wv_ref, pev_ref, w2v_ref).astype(bf16)

    qc_all = qc_ref[...].astype(f32)
    qr_all = qr_ref[...].astype(f32)
    gsig = jax.nn.sigmoid(sm_ref[...])
    rows = N_KV * R * SEG
    qpos_col = P + lax.broadcasted_iota(jnp.int32, (rows, 1), 0) % SEG
    qpos_row = P + lax.broadcasted_iota(jnp.int32, (1, 128), 1) % SEG
    new_lane = lax.broadcasted_iota(jnp.int32, (1, 128), 1)
    zpad = jnp.zeros((128 - SEG, 128), f32)

    def stack(q_all, bi):
        parts = []
        for h in range(N_HEADS):
            g = h // R
            q = q_all[bi * SEG:(bi + 1) * SEG, (h // 2) * 128:(h // 2 + 1) * 128]
            if h % 2 != g:
                q = pltpu.roll(q, HEAD_DIM, axis=1)
            parts.append(jnp.where(new_lane // HEAD_DIM == g, q, 0.0))
        return jnp.concatenate(parts, axis=0).astype(bf16)

    cmp_end = lax.broadcasted_iota(jnp.int32, (1, n_chunk), 1) * CMP_STRIDE + (CMP_LEN - 1)
    o_cs, psums = [], []
    for bi in range(GB):
        s = lax.dot_general(stack(qc_all, bi), kc_all[bi * n_chunk:(bi + 1) * n_chunk], NT,
                            preferred_element_type=f32)
        p = _softmax_rows(s, cmp_end <= qpos_col)
        o_cs.append(jnp.dot(p.astype(bf16), vc_all[bi * n_chunk:(bi + 1) * n_chunk], preferred_element_type=f32))
        for g in range(N_KV):
            pg = p[g * R * SEG:(g + 1) * R * SEG]
            psums.append(pg[0:SEG] + pg[SEG:2 * SEG] + pg[2 * SEG:3 * SEG] + pg[3 * SEG:4 * SEG])
    psum_all = jnp.concatenate(psums + [jnp.zeros((128 - SEG * GB * N_KV, n_chunk), f32)], axis=0)
    imp_t = lax.dot_general(ovt_ref[...], psum_all, NT, preferred_element_type=f32, precision=HIGHEST)
    sel_all = _transpose_sel(_select_blocks(imp_t, qpos_row, n_blk)).astype(bf16)
    mk_all = jnp.dot(sel_all, e_ref[...], preferred_element_type=f32)

    out_rows = []
    for bi in range(GB):
        rs8 = slice(bi * SEG, (bi + 1) * SEG)
        ksel_t = jnp.concatenate([pages[bi * n_pages + pp][0, 0, 2].astype(bf16) for pp in range(n_pages)], axis=1)
        vsel_t = jnp.concatenate([pages[bi * n_pages + pp][0, 0, 3].astype(bf16) for pp in range(n_pages)], axis=1)
        knew = jnp.concatenate([kvn_ref[rs8, 256:384], zpad], axis=0).astype(bf16)
        vnew = jnp.concatenate([kvn_ref[rs8, 384:512], zpad], axis=0).astype(bf16)
        kwnew_f = jnp.concatenate([wn_ref[rs8, 0:128], zpad], axis=0)
        vwnew_f = jnp.concatenate([wn_ref[rs8, 128:256], zpad], axis=0)
        kwnew, vwnew = kwnew_f.astype(bf16), vwnew_f.astype(bf16)
        kwin_t = win_ref[0, bi, 0].astype(bf16)
        vwin_t = win_ref[0, bi, 1].astype(bf16)
        qr = stack(qr_all, bi)
        o_c = o_cs[bi]
        mk = jnp.concatenate(
            [mk_all[(bi * N_KV + g) * SEG:(bi * N_KV + g + 1) * SEG] for g in range(N_KV) for _ in range(R)], axis=0) > 0.5
        s_sel = jnp.concatenate([jnp.dot(qr, ksel_t, preferred_element_type=f32),
                                 lax.dot_general(qr, knew, NT, preferred_element_type=f32)], axis=1)
        kpos = jnp.concatenate([lax.broadcasted_iota(jnp.int32, (1, P), 1), P + new_lane], axis=1)
        valid = jnp.concatenate([jnp.full((1, P), True), new_lane < SEG], axis=1)
        p_s = _softmax_rows(s_sel, mk & (kpos <= qpos_col) & valid).astype(bf16)
        o_s = (lax.dot_general(p_s[:, 0:P], vsel_t, NT, preferred_element_type=f32)
               + jnp.dot(p_s[:, P:P + 128], vnew, preferred_element_type=f32))
        s_w = jnp.concatenate([jnp.dot(qr, kwin_t, preferred_element_type=f32),
                               lax.dot_general(qr, kwnew, NT, preferred_element_type=f32)], axis=1)
        kwpos = jnp.concatenate([P - WB + lax.broadcasted_iota(jnp.int32, (1, WB), 1), P + new_lane], axis=1)
        validw = jnp.concatenate([jnp.full((1, WB), True), new_lane < SEG], axis=1)
        dpos = qpos_col - kwpos
        p_w = _softmax_rows(s_w, (dpos >= 0) & (dpos < WINDOW) & (kwpos >= 0) & validw).astype(bf16)
        o_w = (lax.dot_general(p_w[:, 0:WB], vwin_t, NT, preferred_element_type=f32)
               + jnp.dot(p_w[:, WB:WB + 128], vwnew, preferred_element_type=f32))
        gs = gsig[rs8]
        heads = [None] * N_HEADS
        for h in range(N_HEADS):
            rr = slice(h * SEG, (h + 1) * SEG)
            c0 = 8 + 3 * h
            out = gs[:, c0:c0 + 1] * o_c[rr] + gs[:, c0 + 1:c0 + 2] * o_s[rr] + gs[:, c0 + 2:c0 + 3] * o_w[rr]
            if h % 2 != h // R:
                out = pltpu.roll(out, HEAD_DIM, axis=1)
            heads[h] = out
        out_rows.append(jnp.concatenate(
            [jnp.where(new_lane < HEAD_DIM, heads[2 * j], heads[2 * j + 1]) for j in range(N_HEADS // 2)], axis=1))
        for kv, new_f in ((0, kwnew_f), (1, vwnew_f)):
            old = pltpu.roll(win_ref[0, bi, kv], WB - ts, axis=1)
            new_t = pltpu.roll(new_f.T, 128 - ts, axis=1)
            wout_ref[bi, kv, :, 0:WB - 128] = old[:, 0:WB - 128]
            wout_ref[bi, kv, :, WB - 128:WB] = jnp.where(new_lane >= 128 - ts, new_t, old[:, WB - 128:WB])
    hb_ref[...] = jnp.concatenate(out_rows, axis=0).astype(bf16)


def _nsa_sample(page_table, cache_t, cwin_t, qc, qr, sm, kvrow, winrow, cwk, cwv, win_stack, *, layer, row0, ts):
    nseq, n_pages = page_table.shape
    depth = cwin_t.shape[0]
    P = n_pages * PAGE
    WB = cwin_t.shape[4]
    n_chunk = P // CMP_STRIDE
    ovt = _sel_overlap_t(n_chunk, -(-(P + ts) // SEL_BLOCK))
    e = _block_expand(P + 128)
    r = np.arange(PAGE)
    perm = jnp.asarray(np.arange(PAGE)[None, :] == (CMP_STRIDE * (r % (PAGE // CMP_STRIDE)) + r // (PAGE // CMP_STRIDE))[:, None],
                       dtype=bf16)
    b0 = row0 // (GB * SEG)

    def page_map(i, pt, *, bi, p):
        return (layer, pt[i * GB + bi, p], 0, 0, 0)

    page_specs = [pl.BlockSpec((1, 1, 4, 128, PAGE), functools.partial(page_map, bi=bi, p=p))
                  for bi in range(GB) for p in range(n_pages)]
    row = lambda w: pl.BlockSpec((GB * SEG, w), lambda i, pt: (b0 + i, 0))
    const = lambda a: pl.BlockSpec(a.shape, lambda i, pt: (0,) * a.ndim)
    extra_in, extra_spec, win_shape = _stacked_out(win_stack, depth, (nseq, 2, 128, WB))
    args = (page_table, *([cache_t] * (GB * n_pages)), cwin_t, qc, qr, sm, kvrow, winrow, *cwk, *cwv, ovt, e, perm)
    grid_spec = pltpu.PrefetchScalarGridSpec(
        num_scalar_prefetch=1, grid=(nseq // GB,),
        in_specs=page_specs + [pl.BlockSpec((1, GB, 2, 128, WB), lambda i, pt: (layer, i, 0, 0, 0)),
                               row(N_WIDTH), row(N_WIDTH), row(128), row(512), row(256)]
        + [const(a) for a in (*cwk, *cwv, ovt, e, perm)] + extra_spec,
        out_specs=[pl.BlockSpec((GB * SEG, N_WIDTH), lambda i, pt: (i, 0)),
                   pl.BlockSpec((1, GB, 2, 128, WB), lambda i, pt: (layer, i, 0, 0, 0))])
    return pl.pallas_call(
        functools.partial(_nsa_sample_body, n_pages=n_pages, ts=ts), grid_spec=grid_spec,
        out_shape=[jax.ShapeDtypeStruct((nseq * SEG, N_WIDTH), bf16), win_shape],
        input_output_aliases={len(args): 1} if extra_in else {},
        compiler_params=pltpu.CompilerParams(dimension_semantics=("parallel",), vmem_limit_bytes=VMEM_LIMIT),
        name="nsa_sample",
    )(*args, *extra_in)


def kernel(x_prompt, x_sample, cache_kv_pages, page_table, cache_win, state_mlstm_C, state_mlstm_n, state_mlstm_m, state_mlstm_conv, norm1_g, w_in, b_gates, conv_w, conv_b, w_mq, w_mk, mlstm_norm_g, cmp_pe_k, cmp_pe_v, cmp_w1_k, cmp_w2_k, cmp_w1_v, cmp_w2_v, w_branch_a, w_branch_b, w_out, norm2_g, w_ffn_up, w_ffn_down, final_norm_g):
    Bp, Tp, _ = x_prompt.shape
    Bs, Ts, _ = x_sample.shape
    depth = w_in.shape[0]
    n_pool, page = cache_kv_pages.shape[1:3]
    n_pages = page_table.shape[1]
    past_len = n_pages * page
    wb_len = cache_win.shape[2]
    mp = Bp * Tp
    ms = Bs * SAMPLE_PAD
    assert page == PAGE and M_CONV - 1 <= Ts <= SAMPLE_PAD and Tp % MLSTM_CHUNK == 0 and Tp >= WINDOW + TQ
    assert mp % (SEG * SEQ_PER_STEP) == 0 and Bs % SEQ_PER_STEP == 0 and Bs % GB == 0

    pad_seq = lambda a: jnp.pad(a, ((0, 0), (0, SAMPLE_PAD - a.shape[1]), (0, 0)))
    x_all = jnp.concatenate([x_prompt.reshape(mp, D_MODEL), pad_seq(x_sample).reshape(ms, D_MODEL)], axis=0)
    tabs = _rope_tables(Tp, past_len, ROW_TILE * (ms // ROW_TILE))
    col_idx, col_scale = _in_proj_columns()
    cache_t = jnp.transpose(cache_kv_pages, (0, 1, 3, 4, 5, 2)).reshape(depth, n_pool, 4, KV_W, PAGE)
    cwin_t = jnp.transpose(cache_win, (0, 1, 3, 4, 5, 2)).reshape(depth, Bs, 2, KV_W, wb_len)

    outs = {k: [] for k in ("p_win", "p_C", "p_n", "p_m", "p_conv", "s_kv", "s_n", "s_m", "s_conv")}
    c_stack = win_stack = kvt_stack = None
    y_parts = None
    for l in range(depth):
        w_ext = jnp.concatenate([w_in[l], jnp.zeros((D_MODEL, 1), f32)], axis=1)
        w_all = (w_ext[:, col_idx] * col_scale).astype(bf16)
        w_t = jnp.concatenate([w_in[l][:, O_IG:O_IG + 8], w_in[l][:, O_NG:O_NG + 24]], axis=1).T.astype(bf16)
        xm, sm, smt, qc, qr, qct, qrt, kvrow, winrow, kk, vt, gates, kvt_stack = _in_proj(
            x_all, norm1_g[l].reshape(1, D_MODEL), w_all, w_t, tabs, Tp, kvt_stack, layer=l, depth=depth)

        wq, wk = w_mq[l].astype(bf16), w_mk[l].astype(bf16)
        ha_p, c_p, n_p, m_p, cv_p = _mlstm_prompt(xm, sm, smt, conv_w[l], conv_b[l], wq, wk, b_gates[l], mlstm_norm_g[l],
                                                   nb=Bp, T=Tp, L=MLSTM_CHUNK)
        tail = jnp.pad(state_mlstm_conv[l], ((0, 0), (0, SEG - (M_CONV - 1)), (0, 0))).reshape(ms, M_WIDTH)
        n0 = jnp.pad(state_mlstm_n[l], ((0, 0), (0, 8 - M_HEADS), (0, 0)))
        m0 = jnp.broadcast_to(jnp.pad(state_mlstm_m[l], ((0, 0), (0, 128 - M_HEADS)))[:, None, :], (Bs, SEG, 128))
        ha_s, c_stack, n_s, mt_s = _mlstm_sample(
            xm, sm, smt, tail, state_mlstm_C[l], n0, m0.reshape(ms, 128), conv_w[l], conv_b[l], wq, wk, b_gates[l],
            mlstm_norm_g[l], c_stack, layer=l, depth=depth, row0=mp, nseq=Bs, n_valid=Ts)

        cwk = _compress_weights(cmp_pe_k[l], cmp_w1_k[l], cmp_w2_k[l])
        cwv = _compress_weights(cmp_pe_v[l], cmp_w1_v[l], cmp_w2_v[l])
        kc, vct = _compress_prompt(kvrow, cwk, cwv, nb=Bp, T=Tp)
        hbt_p = _nsa_prompt(qct, qrt, smt, kk, vt, kc, vct, nb=Bp, T=Tp)
        hb_s, win_stack = _nsa_sample(page_table, cache_t, cwin_t, qc, qr, sm, kvrow, winrow, cwk, cwv, win_stack,
                                      layer=l, row0=mp, ts=Ts)

        h_a = jnp.concatenate([ha_p, ha_s], axis=0)
        h_bt = jnp.concatenate([hbt_p, hb_s.T], axis=1)
        x_all, *y_parts = _mix_ffn(
            x_all, h_a, h_bt, gates, w_branch_a[l].astype(bf16), w_branch_b[l].astype(bf16), w_out[l].astype(bf16),
            norm2_g[l].reshape(1, D_MODEL), w_ffn_up[l].astype(bf16), w_ffn_down[l].astype(bf16),
            final_norm_g.reshape(1, D_MODEL), n_prompt_rows=mp, final=l == depth - 1)

        sample_rows = lambda a: a[mp:].reshape(Bs, SAMPLE_PAD, a.shape[1])
        wp = min(WINDOW, Tp)
        outs["p_win"].append(winrow[:mp].reshape(Bp, Tp, 2 * KV_W)[:, -wp:].reshape(Bp, wp, 2, N_KV, HEAD_DIM))
        outs["p_C"].append(c_p)
        outs["p_n"].append(n_p[:, :M_HEADS])
        outs["p_m"].append(m_p[:, 0, :M_HEADS])
        outs["p_conv"].append(cv_p[:, 8 - (M_CONV - 1):])
        outs["s_kv"].append(sample_rows(kvrow)[:, :Ts].reshape(Bs, Ts, 4, N_KV, HEAD_DIM))
        outs["s_n"].append(n_s[:, :M_HEADS])
        outs["s_m"].append(mt_s.reshape(Bs, SEG, 128)[:, SEG - 1, :M_HEADS])
        outs["s_conv"].append(sample_rows(xm)[:, Ts - (M_CONV - 1):Ts, :M_WIDTH])

    s_win = jnp.transpose(win_stack.reshape(depth, Bs, 2, N_KV, HEAD_DIM, wb_len), (0, 1, 5, 2, 3, 4))
    y_prompt = y_parts[0].reshape(Bp, Tp, D_MODEL)
    y_sample = y_parts[1].reshape(Bs, SAMPLE_PAD, D_MODEL)[:, :Ts]
    st = lambda k: jnp.stack(outs[k])
    p_kv = jnp.transpose(kvt_stack.reshape(depth, Bp, 4, N_KV, HEAD_DIM, Tp), (0, 1, 5, 2, 3, 4))
    return (y_prompt, y_sample, p_kv, st("p_win"), st("p_C"), st("p_n"), st("p_m"), st("p_conv"),
            st("s_kv"), s_win, c_stack, st("s_n"), st("s_m"), st("s_conv"))
```

```python
import functools

import jax
import jax.numpy as jnp
import numpy as np
from jax import lax
from jax.experimental import pallas as pl
from jax.experimental.pallas import tpu as pltpu

f32 = jnp.float32
bf16 = jnp.bfloat16

D_MODEL = 1024
M_HEADS = 4
M_WIDTH = 512
M_HEAD_DIM = 128
M_CONV = 4
MLSTM_CHUNK = 256
HEAD_DIM = 64
N_WIDTH = 512
N_HEADS = 8
N_KV = 2
KV_W = 128
CMP_STRIDE = 16
CMP_LEN = 32
SEL_BLOCK = 64
SEL_TOP = 16
WINDOW = 512
ROT_DIM = 16
ROPE_THETA = 500000.0
D_FF = 2816
EPS = 1e-6
NEG = -1e30
TINY = 1e-30
FORCE_SCORE = 1e9

SAMPLE_PAD = 8
SEG = SAMPLE_PAD
SEQ_PER_STEP = 16
GB = 2
PAGE = 128
HD = M_HEAD_DIM
R = N_HEADS // N_KV
CMP_HID = 128
TQ = 128
TK = 1024
NBLK_PAD = 64
HIGHEST = lax.Precision.HIGHEST
NT = (((1,), (1,)), ((), ()))
LOG2E = 1.4426950408889634
ROW_TILE = 256
FF_CHUNK = 1408
VMEM_LIMIT = 56 * 1024 * 1024

C_XM, C_Q, C_KV, C_WIN, C_GATE, C_SM = 0, 1536, 2048, 2560, 2816, 4864
C_TOTAL = 4992
O_U, O_V, O_O, O_IG, O_FG, O_Q, O_KV, O_NG, O_GA, O_GB = 0, 512, 1024, 1536, 1540, 1544, 2056, 2824, 2848, 3872
IN_WIDTH = 4896


def _in_proj_weights(w):
    lead = w.shape[:-1]
    parts = [w[..., O_U:O_IG], w[..., O_Q:O_Q + N_WIDTH] * (HEAD_DIM ** -0.5), w[..., O_KV:O_KV + 6 * KV_W],
             w[..., O_GA:O_GA + 2 * D_MODEL], w[..., O_IG:O_IG + 8], w[..., O_NG:O_NG + 24],
             jnp.zeros(lead + (128 - 32,), f32)]
    out = jnp.concatenate(parts, axis=-1).astype(bf16)
    assert out.shape[-1] == C_TOTAL
    return out


def _rope_tables(seq, past_len, n_sample_rows):
    half = ROT_DIM // 2
    inv = ROPE_THETA ** (-jnp.arange(half, dtype=f32) / half)
    pos = jnp.concatenate([jnp.arange(seq), past_len + (jnp.arange(n_sample_rows) % SAMPLE_PAD)]).astype(f32)
    ang = pos[:, None] * inv[None, :]
    cos8, sin8 = jnp.cos(ang), jnp.sin(ang)
    n = pos.shape[0]
    one = jnp.ones((n, 64 - ROT_DIM), f32)
    zero = jnp.zeros((n, 64 - ROT_DIM), f32)
    z8 = jnp.zeros((n, half), f32)
    cos = jnp.concatenate([cos8, cos8, one], axis=1)
    sa = jnp.concatenate([-sin8, z8, zero], axis=1)
    sb = jnp.concatenate([z8, sin8, zero], axis=1)
    tile2 = lambda a: jnp.concatenate([a, a], axis=1)
    return tile2(cos), tile2(sa), tile2(sb)


def _rms(x, g):
    return x * lax.rsqrt(jnp.mean(x * x, axis=-1, keepdims=True) + EPS) * g


def _in_proj_body(*refs, n_x, n_sample_tiles):
    x_refs = refs[:n_x]
    g_ref, w_ref, wt_ref, cos_ref, sa_ref, sb_ref = refs[n_x:n_x + 6]
    (xm_ref, sm_ref, smt_ref, qc_ref, qr_ref, qct_ref, qrt_ref, kv_ref, win_ref, kk_ref, vt_ref, gate_ref,
     kvt_stack) = refs[-13:]
    x = x_refs[0][...]
    if n_x == 2:
        x = jnp.where(pl.program_id(0) < n_sample_tiles, x_refs[1][...], x)
    hb = _rms(x, g_ref[...]).astype(bf16)
    cos, sa, sb = cos_ref[...], sa_ref[...], sb_ref[...]

    def rope(v):
        return v * cos + pltpu.roll(v, 128 - ROT_DIM // 2, axis=1) * sa + pltpu.roll(v, ROT_DIM // 2, axis=1) * sb

    def proj(c0, n):
        return jnp.dot(hb, w_ref[0, :, c0:c0 + n], preferred_element_type=f32)

    xm_ref[...] = proj(C_XM, 1536)
    sm_ref[...] = proj(C_SM, 128)
    smt_ref[...] = lax.dot_general(wt_ref[...], hb, NT, preferred_element_type=f32)
    qf = proj(C_Q, N_WIDTH) * LOG2E
    for j in range(N_HEADS // 2):
        sl = slice(j * 128, (j + 1) * 128)
        q = qf[:, sl]
        qrot = rope(q)
        qc_ref[:, sl] = q.astype(bf16)
        qr_ref[:, sl] = qrot.astype(bf16)
        qct_ref[sl, :] = q.T.astype(bf16)
        qrt_ref[sl, :] = qrot.T.astype(bf16)
    kv = proj(C_KV, 512)
    ksel = rope(kv[:, 256:384])
    kv_ref[:, 0:256] = kv[:, 0:256]
    kv_ref[:, 256:384] = ksel
    kv_ref[:, 384:512] = kv[:, 384:512]
    kvt_stack[0, 0, 0:256, :] = kv[:, 0:256].T
    kvt_stack[0, 0, 256:384, :] = ksel.T
    kvt_stack[0, 0, 384:512, :] = kv[:, 384:512].T
    win = proj(C_WIN, 256)
    kwin = rope(win[:, 0:128])
    win_ref[:, 0:128] = kwin
    win_ref[:, 128:256] = win[:, 128:256]
    kk_ref[:, 0:128] = ksel.astype(bf16)
    kk_ref[:, 128:256] = kwin.astype(bf16)
    vt_ref[0:128, :] = kv[:, 384:512].T.astype(bf16)
    vt_ref[128:256, :] = win[:, 128:256].T.astype(bf16)
    gate_ref[...] = proj(C_GATE, 2048)


def _in_proj(xs, norm_g, w_all, w_t, tabs, seq, kvt_stack, *, layer, depth):
    m = sum(x.shape[0] for x in xs)
    n_prompt_tiles_per_seq = seq // ROW_TILE
    n_prompt_tiles = (m - (tabs[0].shape[0] - seq)) // ROW_TILE
    n_seq = n_prompt_tiles // n_prompt_tiles_per_seq
    n_tiles = m // ROW_TILE
    n_sample_tiles = n_tiles - n_prompt_tiles

    def tile(i):
        return (i + n_prompt_tiles) % n_tiles

    def kvt_map(i):
        j = jnp.maximum(i - n_sample_tiles, 0)
        return (layer, j // n_prompt_tiles_per_seq, 0, j % n_prompt_tiles_per_seq)

    def tab_map(i):
        t = tile(i)
        return (jnp.where(t < n_prompt_tiles, t % n_prompt_tiles_per_seq, n_prompt_tiles_per_seq + t - n_prompt_tiles), 0)

    row = lambda w: pl.BlockSpec((ROW_TILE, w), lambda i: (tile(i), 0))
    const = lambda s: pl.BlockSpec(s, lambda i: (0, 0), pipeline_mode=pl.Buffered(1))
    tab = pl.BlockSpec((ROW_TILE, 128), tab_map)
    widths = (1536, 128, -32, 512, 512, -512, -512, 512, 256, 256, -256, 2048)
    dts = (f32, f32, f32, bf16, bf16, bf16, bf16, f32, f32, bf16, bf16, f32)
    out_shape = [jax.ShapeDtypeStruct((-w, m) if w < 0 else (m, w), d) for w, d in zip(widths, dts)]
    out_specs = [pl.BlockSpec((-w, ROW_TILE), lambda i: (0, tile(i))) if w < 0 else row(w) for w in widths]
    extra_in, extra_spec, kvt_shape = _stacked_out(kvt_stack, depth, (n_seq, 4 * KV_W, seq))
    if len(xs) == 1:
        x_specs = [row(D_MODEL)]
    else:
        x_specs = [pl.BlockSpec((ROW_TILE, D_MODEL), lambda i: (jnp.maximum(i - n_sample_tiles, 0), 0)),
                   pl.BlockSpec((ROW_TILE, D_MODEL), lambda i: (jnp.minimum(i, n_sample_tiles - 1), 0))]
    args = (*xs, norm_g, w_all, w_t, *tabs)
    return pl.pallas_call(
        functools.partial(_in_proj_body, n_x=len(xs), n_sample_tiles=n_sample_tiles), grid=(n_tiles,),
        in_specs=x_specs + [const((1, D_MODEL)),
                            pl.BlockSpec((1, D_MODEL, C_TOTAL), lambda i: (layer, 0, 0), pipeline_mode=pl.Buffered(1)),
                            const((32, D_MODEL)), tab, tab, tab]
        + extra_spec,
        out_specs=out_specs + [pl.BlockSpec((1, 1, 4 * KV_W, ROW_TILE), kvt_map)],
        out_shape=out_shape + [kvt_shape], name="in_proj",
        input_output_aliases={len(args): len(out_shape)} if extra_in else {},
        compiler_params=pltpu.CompilerParams(dimension_semantics=("arbitrary",), vmem_limit_bytes=VMEM_LIMIT),
    )(*args, *extra_in)


def _mix_ffn_body(*refs, n_x, n_prompt_tiles, final):
    x_refs = refs[:n_x]
    hap_ref, has_ref, hbp_ref, hbs_ref, gate_ref, wa_ref, wb_ref, wo_ref, g2_ref, wup_ref, wdn_ref, gf_ref = (
        refs[n_x:n_x + 12])
    xo_ref, *y_refs = refs[n_x + 12:]
    in_prompt = pl.program_id(0) < n_prompt_tiles
    pick = lambda p_ref, s_ref: jnp.where(in_prompt, p_ref[...], s_ref[...])
    x = x_refs[0][...] if n_x == 1 else pick(*x_refs)
    a = jnp.dot(pick(hap_ref, has_ref).astype(bf16), wa_ref[...], preferred_element_type=f32)
    b = lax.dot_general(pick(hbp_ref, hbs_ref), wb_ref[...], (((0,), (0,)), ((), ())),
                        preferred_element_type=f32)
    merged = jax.nn.sigmoid(gate_ref[:, 0:D_MODEL]) * a + jax.nn.sigmoid(gate_ref[:, D_MODEL:2 * D_MODEL]) * b
    x1 = x + jnp.dot(merged.astype(bf16), wo_ref[...], preferred_element_type=f32)
    hn = _rms(x1, g2_ref[...]).astype(bf16)
    acc = x1
    for c in range(D_FF // FF_CHUNK):
        lo = c * FF_CHUNK
        g = jnp.dot(hn, wup_ref[0, :, lo:lo + FF_CHUNK], preferred_element_type=f32)
        u = jnp.dot(hn, wup_ref[0, :, D_FF + lo:D_FF + lo + FF_CHUNK], preferred_element_type=f32)
        act = (g * jax.nn.sigmoid(g) * u).astype(bf16)
        acc = acc + jnp.dot(act, wdn_ref[0, lo:lo + FF_CHUNK, :], preferred_element_type=f32)
    xo_ref[...] = acc
    if final:
        y = _rms(acc, gf_ref[...])
        yp_ref, ys_ref = y_refs

        @pl.when(in_prompt)
        def _():
            yp_ref[...] = y

        @pl.when(jnp.logical_not(in_prompt))
        def _():
            ys_ref[...] = y


def _mix_ffn(xs, h_a, h_bt, gates, wa, wb, wo, g2, wup, wdn, gf, *, layer, n_prompt_rows, final):
    m = gates.shape[0]
    npt = n_prompt_rows // ROW_TILE
    row = lambda w: pl.BlockSpec((ROW_TILE, w), lambda i: (i, 0))
    p_row = lambda w: pl.BlockSpec((ROW_TILE, w), lambda i: (jnp.minimum(i, npt - 1), 0))
    s_row = lambda w: pl.BlockSpec((ROW_TILE, w), lambda i: (jnp.maximum(i - npt, 0), 0))
    p_col = lambda h: pl.BlockSpec((h, ROW_TILE), lambda i: (0, jnp.minimum(i, npt - 1)))
    s_col = lambda h: pl.BlockSpec((h, ROW_TILE), lambda i: (0, jnp.maximum(i - npt, 0)))
    const = lambda s: pl.BlockSpec(s, lambda i: (0, 0), pipeline_mode=pl.Buffered(1))
    layer_w = lambda a: pl.BlockSpec((1,) + a.shape[1:], lambda i: (layer, 0, 0), pipeline_mode=pl.Buffered(1))
    out_specs = [row(D_MODEL)]
    out_shape = [jax.ShapeDtypeStruct((m, D_MODEL), f32)]
    if final:
        out_specs += [p_row(D_MODEL), s_row(D_MODEL)]
        out_shape += [jax.ShapeDtypeStruct((n_prompt_rows, D_MODEL), f32),
                      jax.ShapeDtypeStruct((m - n_prompt_rows, D_MODEL), f32)]
    x_specs = [row(D_MODEL)] if len(xs) == 1 else [p_row(D_MODEL), s_row(D_MODEL)]
    return pl.pallas_call(
        functools.partial(_mix_ffn_body, n_x=len(xs), n_prompt_tiles=npt, final=final), grid=(m // ROW_TILE,),
        in_specs=x_specs + [p_row(M_WIDTH), s_row(M_WIDTH), p_col(N_WIDTH), s_col(N_WIDTH), row(2 * D_MODEL),
                            const(wa.shape), const(wb.shape), const(wo.shape), const((1, D_MODEL)),
                            layer_w(wup), layer_w(wdn), const((1, D_MODEL))],
        out_specs=out_specs, out_shape=out_shape, name="mix_ffn",
        compiler_params=pltpu.CompilerParams(dimension_semantics=("arbitrary",), vmem_limit_bytes=VMEM_LIMIT),
    )(*xs, *h_a, *h_bt, gates, wa, wb, wo, g2, wup, wdn, gf)


def _log_sigmoid(x):
    return jnp.minimum(x, 0.0) - jnp.log1p(jnp.exp(-jnp.abs(x)))


def _seg_scan(x, axis, seg, op, fill):
    idx = lax.broadcasted_iota(jnp.int32, x.shape, axis) % seg
    d = 1
    while d < seg:
        x = op(x, jnp.where(idx >= d, pltpu.roll(x, d, axis=axis), fill))
        d *= 2
    return x


def _conv_silu(u, tail, conv_w, conv_b, seg):
    L = u.shape[0]
    row = lax.broadcasted_iota(jnp.int32, u.shape, 0) % seg
    y = conv_b + u * conv_w[3:4, :]
    for k in (1, 2, 3):
        prev = tail if k == 3 else pltpu.roll(tail, L - (3 - k), axis=0)
        sh = jnp.where(row < k, prev, pltpu.roll(u, k, axis=0))
        y = y + sh * conv_w[3 - k:4 - k, :]
    return y * jax.nn.sigmoid(y)


def _gate_columns(sm, smt, bg_row, bg_col, m_vec, seg, n_valid):
    G = sm + bg_row
    lf = _log_sigmoid(G)
    ipre = G
    if n_valid < seg:
        rowc = lax.broadcasted_iota(jnp.int32, G.shape, 0) % seg
        lf = jnp.where(rowc < n_valid, lf, 0.0)
        ipre = jnp.where(rowc < n_valid, G, NEG)
    F = _seg_scan(lf, 0, seg, jnp.add, 0.0)
    F_al = pltpu.roll(F, 128 - M_HEADS, axis=1)
    a_col = ipre - F_al
    g_col = jnp.maximum(m_vec, _seg_scan(a_col, 0, seg, jnp.maximum, NEG))
    mt_col = F_al + g_col
    Gr = smt + bg_col
    lfr = _log_sigmoid(Gr)
    ir = Gr
    if n_valid < seg:
        lanec = lax.broadcasted_iota(jnp.int32, Gr.shape, 1) % seg
        lfr = jnp.where(lanec < n_valid, lfr, 0.0)
        ir = jnp.where(lanec < n_valid, Gr, NEG)
    Fr = _seg_scan(lfr, 1, seg, jnp.add, 0.0)
    a_row = ir - pltpu.roll(Fr, M_HEADS, axis=0)
    return a_col, g_col, mt_col, a_row


def _head_norm_gate(hh, ng, o):
    return hh * lax.rsqrt(jnp.mean(hh * hh, axis=-1, keepdims=True) + EPS) * ng * jax.nn.sigmoid(o)


def _mlstm_prompt_body(xm_ref, sm_ref, smt_ref, cw_ref, cb_ref, wq_ref, wk_ref, bgr_ref, bgc_ref, ng_ref,
                       ha_ref, c_out, n_out, m_out, conv_out, c_scr, n_scr, m_scr, tail_scr, *, L):
    c_idx = pl.program_id(1)

    @pl.when(c_idx == 0)
    def _():
        c_scr[...] = jnp.zeros_like(c_scr)
        n_scr[...] = jnp.zeros_like(n_scr)
        m_scr[...] = jnp.zeros_like(m_scr)
        tail_scr[...] = jnp.zeros_like(tail_scr)

    u = xm_ref[:, 0:M_WIDTH]
    tail = jnp.concatenate([tail_scr[...], jnp.zeros((L - 8, M_WIDTH), f32)], axis=0)
    cact = _conv_silu(u, tail, cw_ref[...], cb_ref[...], L)
    tail_scr[...] = pltpu.roll(u[L - 8:L, :], 3, axis=0)
    conv_out[0] = u[L - 8:L, :]

    m_vec = m_scr[0:1, :]
    a_col, g_col, mt_col, a_row = _gate_columns(sm_ref[...], smt_ref[...], bgr_ref[...], bgc_ref[...], m_vec, L, L)
    inter_col = jnp.exp(m_vec - g_col)
    floor_col = jnp.exp(-mt_col)
    g_last = g_col[L - 1:L, :]
    w_col = jnp.exp(a_col - g_last)
    decay = jnp.exp(m_vec - g_last)
    m_new = mt_col[L - 1:L, :]
    ti = lax.broadcasted_iota(jnp.int32, (L, L), 0)
    si = lax.broadcasted_iota(jnp.int32, (L, L), 1)
    causal = si <= ti
    scale = HD ** -0.5
    for h in range(M_HEADS):
        sl = slice(h * HD, (h + 1) * HD)
        ch = cact[:, sl].astype(bf16)
        qh = jnp.dot(ch, wq_ref[h], preferred_element_type=f32)
        kh = jnp.dot(ch, wk_ref[h], preferred_element_type=f32) * scale
        vh = xm_ref[:, M_WIDTH + h * HD:M_WIDTH + (h + 1) * HD].astype(bf16)
        qb = qh.astype(bf16)
        D = jnp.exp(jnp.where(causal, a_row[h:h + 1, :] - g_col[:, h:h + 1], NEG))
        S = lax.dot_general(qb, kh.astype(bf16), NT, preferred_element_type=f32) * D
        C = c_scr[h]
        nrow = n_scr[h:h + 1, :]
        ic = inter_col[:, h:h + 1]
        num = jnp.dot(S.astype(bf16), vh, preferred_element_type=f32) + ic * jnp.dot(
            qb, C.astype(bf16), preferred_element_type=f32)
        den = jnp.sum(S, axis=-1, keepdims=True) + ic * jnp.sum(qh * nrow, axis=-1, keepdims=True)
        hh = num / jnp.maximum(jnp.abs(den), floor_col[:, h:h + 1])
        o = xm_ref[:, 2 * M_WIDTH + h * HD:2 * M_WIDTH + (h + 1) * HD]
        ha_ref[:, sl] = _head_norm_gate(hh, ng_ref[:, sl], o)
        kw = kh * w_col[:, h:h + 1]
        dh = decay[:, h:h + 1]
        c_scr[h] = dh * C + lax.dot_general(kw.astype(bf16), vh, (((0,), (0,)), ((), ())), preferred_element_type=f32)
        n_scr[h:h + 1, :] = dh * nrow + jnp.sum(kw, axis=0, keepdims=True)
    m_scr[0:1, :] = m_new

    @pl.when(c_idx == pl.num_programs(1) - 1)
    def _():
        c_out[0] = c_scr[...]
        n_out[0] = n_scr[...]
        m_out[0] = m_scr[...]


def _mlstm_prompt(xm, sm, smt, conv_w, conv_b, wq, wk, b_gates, norm_g, *, nb, T, L):
    nc = T // L
    bg_row = jnp.zeros((1, 128), f32).at[0, :8].set(b_gates)
    bg_col = b_gates.reshape(8, 1)
    const = lambda s: pl.BlockSpec(s, lambda b, c: (0,) * len(s))
    return pl.pallas_call(
        functools.partial(_mlstm_prompt_body, L=L), grid=(nb, nc),
        in_specs=[pl.BlockSpec((L, 3 * M_WIDTH), lambda b, c: (b * nc + c, 0)),
                  pl.BlockSpec((L, 128), lambda b, c: (b * nc + c, 0)),
                  pl.BlockSpec((8, L), lambda b, c: (0, b * nc + c)),
                  const((4, M_WIDTH)), const((1, M_WIDTH)), const((M_HEADS, HD, HD)), const((M_HEADS, HD, HD)),
                  const((1, 128)), const((8, 1)), const((1, M_WIDTH))],
        out_specs=[pl.BlockSpec((L, M_WIDTH), lambda b, c: (b * nc + c, 0)),
                   pl.BlockSpec((1, M_HEADS, HD, HD), lambda b, c: (b, 0, 0, 0)),
                   pl.BlockSpec((1, 8, HD), lambda b, c: (b, 0, 0)),
                   pl.BlockSpec((1, 8, 128), lambda b, c: (b, 0, 0)),
                   pl.BlockSpec((1, 8, M_WIDTH), lambda b, c: (b, 0, 0))],
        out_shape=[jax.ShapeDtypeStruct((nb * T, M_WIDTH), f32),
                   jax.ShapeDtypeStruct((nb, M_HEADS, HD, HD), f32),
                   jax.ShapeDtypeStruct((nb, 8, HD), f32),
                   jax.ShapeDtypeStruct((nb, 8, 128), f32),
                   jax.ShapeDtypeStruct((nb, 8, M_WIDTH), f32)],
        scratch_shapes=[pltpu.VMEM((M_HEADS, HD, HD), f32), pltpu.VMEM((8, HD), f32), pltpu.VMEM((8, 128), f32),
                        pltpu.VMEM((8, M_WIDTH), f32)],
        compiler_params=pltpu.CompilerParams(dimension_semantics=("parallel", "arbitrary")),
        name="mlstm_prompt",
    )(xm, sm, smt, conv_w, conv_b.reshape(1, M_WIDTH), wq, wk, bg_row, bg_col, norm_g.reshape(1, M_WIDTH))


def _mlstm_sample_body(xm_ref, sm_ref, smt_ref, tail_ref, c_in, n_in, m_in, cw_ref, cb_ref, wq_ref, wk_ref,
                       bgr_ref, bgc_ref, ng_ref, *out_refs, n_valid):
    ha_ref, c_stack, n_out, m_out = out_refs[-4:]
    c_out = c_stack.at[0]
    L = SEG * SEQ_PER_STEP
    u = xm_ref[:, 0:M_WIDTH]
    cact = _conv_silu(u, tail_ref[...], cw_ref[...], cb_ref[...], SEG)
    m_rows = m_in[...]
    a_col, g_col, mt_col, a_row = _gate_columns(sm_ref[...], smt_ref[...], bgr_ref[...], bgc_ref[...], m_rows, SEG,
                                                n_valid)
    inter_col = jnp.exp(m_rows - g_col)
    floor_col = jnp.exp(-mt_col)
    ti = lax.broadcasted_iota(jnp.int32, (L, L), 0)
    si = lax.broadcasted_iota(jnp.int32, (L, L), 1)
    mask = (si <= ti) & ((si // SEG) == (ti // SEG))
    scale = HD ** -0.5
    for h in range(M_HEADS):
        sl = slice(h * HD, (h + 1) * HD)
        ch = cact[:, sl].astype(bf16)
        qh = jnp.dot(ch, wq_ref[h], preferred_element_type=f32)
        kh = jnp.dot(ch, wk_ref[h], preferred_element_type=f32) * scale
        vh = xm_ref[:, M_WIDTH + h * HD:M_WIDTH + (h + 1) * HD].astype(bf16)
        qb = qh.astype(bf16)
        D = jnp.exp(jnp.where(mask, a_row[h:h + 1, :] - g_col[:, h:h + 1], NEG))
        S = lax.dot_general(qb, kh.astype(bf16), NT, preferred_element_type=f32) * D
        num = jnp.dot(S.astype(bf16), vh, preferred_element_type=f32)
        den = jnp.sum(S, axis=-1, keepdims=True)
        inter_num, inter_den = [], []
        for s in range(SEQ_PER_STEP):
            rs = slice(s * SEG, (s + 1) * SEG)
            C = c_in[0, s, h]
            nrow = n_in[s, h:h + 1, :]
            inter_num.append(jnp.dot(qb[rs], C.astype(bf16), preferred_element_type=f32))
            inter_den.append(jnp.sum(qh[rs] * nrow, axis=-1, keepdims=True))
            g_last = g_col[s * SEG + SEG - 1:s * SEG + SEG, h:h + 1]
            m_prev = m_rows[s * SEG:s * SEG + 1, h:h + 1]
            w = jnp.exp(a_col[rs, h:h + 1] - g_last)
            dh = jnp.exp(m_prev - g_last)
            kw = kh[rs] * w
            c_out[s, h] = dh * C + lax.dot_general(kw.astype(bf16), vh[rs], (((0,), (0,)), ((), ())),
                                                    preferred_element_type=f32)
            n_out[s, h:h + 1, :] = dh * nrow + jnp.sum(kw, axis=0, keepdims=True)
        ic = inter_col[:, h:h + 1]
        num = num + ic * jnp.concatenate(inter_num, axis=0)
        den = den + ic * jnp.concatenate(inter_den, axis=0)
        hh = num / jnp.maximum(jnp.abs(den), floor_col[:, h:h + 1])
        o = xm_ref[:, 2 * M_WIDTH + h * HD:2 * M_WIDTH + (h + 1) * HD]
        ha_ref[:, sl] = _head_norm_gate(hh, ng_ref[:, sl], o)
    for s in range(SEQ_PER_STEP):
        n_out[s, M_HEADS:8, :] = jnp.zeros((8 - M_HEADS, HD), f32)
    m_out[...] = mt_col


def _stacked_out(prev, depth, shape):
    extra_in = [] if prev is None else [prev]
    extra_spec = [] if prev is None else [pl.BlockSpec(memory_space=pl.ANY)]
    return extra_in, extra_spec, jax.ShapeDtypeStruct((depth,) + shape, f32)


def _mlstm_sample(xm, sm, smt, tail, c0, n0, m0, conv_w, conv_b, wq, wk, b_gates, norm_g, c_stack, *, layer, depth,
                  row0, nseq, n_valid):
    L = SEG * SEQ_PER_STEP
    b0 = row0 // L
    bg_row = jnp.zeros((1, 128), f32).at[0, :8].set(b_gates)
    bg_col = b_gates.reshape(8, 1)
    const = lambda s: pl.BlockSpec(s, lambda i: (0,) * len(s))
    extra_in, extra_spec, c_shape = _stacked_out(c_stack, depth, (nseq, M_HEADS, HD, HD))
    n_in = 14
    return pl.pallas_call(
        functools.partial(_mlstm_sample_body, n_valid=n_valid), grid=(nseq // SEQ_PER_STEP,),
        in_specs=[pl.BlockSpec((L, 3 * M_WIDTH), lambda i: (b0 + i, 0)),
                  pl.BlockSpec((L, 128), lambda i: (b0 + i, 0)),
                  pl.BlockSpec((8, L), lambda i: (0, b0 + i)),
                  pl.BlockSpec((L, M_WIDTH), lambda i: (i, 0)),
                  pl.BlockSpec((1, SEQ_PER_STEP, M_HEADS, HD, HD), lambda i: (layer, i, 0, 0, 0)),
                  pl.BlockSpec((SEQ_PER_STEP, 8, HD), lambda i: (i, 0, 0)),
                  pl.BlockSpec((L, 128), lambda i: (i, 0)),
                  const((4, M_WIDTH)), const((1, M_WIDTH)), const((M_HEADS, HD, HD)), const((M_HEADS, HD, HD)),
                  const((1, 128)), const((8, 1)), const((1, M_WIDTH))] + extra_spec,
        out_specs=[pl.BlockSpec((L, M_WIDTH), lambda i: (i, 0)),
                   pl.BlockSpec((1, SEQ_PER_STEP, M_HEADS, HD, HD), lambda i: (layer, i, 0, 0, 0)),
                   pl.BlockSpec((SEQ_PER_STEP, 8, HD), lambda i: (i, 0, 0)),
                   pl.BlockSpec((L, 128), lambda i: (i, 0))],
        out_shape=[jax.ShapeDtypeStruct((nseq * SEG, M_WIDTH), f32), c_shape,
                   jax.ShapeDtypeStruct((nseq, 8, HD), f32),
                   jax.ShapeDtypeStruct((nseq * SEG, 128), f32)],
        input_output_aliases={n_in: 1} if extra_in else {},
        compiler_params=pltpu.CompilerParams(dimension_semantics=("parallel",), vmem_limit_bytes=VMEM_LIMIT),
        name="mlstm_sample",
    )(xm, sm, smt, tail, c0, n0, m0, conv_w, conv_b.reshape(1, M_WIDTH), wq, wk, bg_row, bg_col,
      norm_g.reshape(1, M_WIDTH), *extra_in)


def _sel_overlap_t(n_chunk, n_blk):
    n_cmp = n_chunk - 1
    start = np.arange(n_cmp) * CMP_STRIDE
    bs = np.arange(n_blk) * SEL_BLOCK
    ov = np.minimum(start[:, None] + CMP_LEN, bs[None, :] + SEL_BLOCK) - np.maximum(start[:, None], bs[None, :])
    ov = np.clip(ov, 0, None) / CMP_LEN
    out = np.zeros((NBLK_PAD, n_chunk), np.float32)
    out[:n_blk, :n_cmp] = ov.T
    return jnp.asarray(out)


def _block_expand(n_keys):
    e = (np.arange(n_keys)[None, :] // SEL_BLOCK) == np.arange(128)[:, None]
    return jnp.asarray(e, dtype=bf16)


def _compress_weights(pe, w1, w2):
    W = jnp.zeros((16, 2, 64, 2, 2, CMP_HID), f32)
    for g in range(N_KV):
        W = W.at[:, g, :, g, 0, :].set(w1[:16])
        W = W.at[:, g, :, g, 1, :].set(w1[16:])
    W = W.reshape(2048, 512)
    peA = jnp.broadcast_to(pe[:16, None, :], (16, 2, 64)).reshape(1, 2048)
    peB = jnp.broadcast_to(pe[16:, None, :], (16, 2, 64)).reshape(1, 2048)
    PE = jnp.concatenate([peA, peB, jnp.zeros((14, 2048), f32)], axis=0)
    W2 = jnp.zeros((2, CMP_HID, 128), f32).at[0, :, 0:64].set(w2).at[1, :, 64:128].set(w2)
    return W.astype(bf16), PE.astype(bf16), W2.astype(bf16)


def _compress_rows(flat, w_ref, pe_ref, w2_ref):
    n = flat.shape[0]
    y = jnp.dot(jnp.concatenate([flat, pe_ref[...]], axis=0), w_ref[...], preferred_element_type=f32)
    c = y[n:n + 8]
    out = jnp.zeros((n, w2_ref.shape[2]), f32)
    for g in range(N_KV):
        a = y[0:n, (2 * g) * CMP_HID:(2 * g + 1) * CMP_HID] + c[0:1, (2 * g) * CMP_HID:(2 * g + 1) * CMP_HID]
        b = (y[0:n, (2 * g + 1) * CMP_HID:(2 * g + 2) * CMP_HID]
             + c[1:2, (2 * g + 1) * CMP_HID:(2 * g + 2) * CMP_HID])
        hid = a + pltpu.roll(b, n - 1, axis=0)
        out = out + jnp.dot(jax.nn.gelu(hid, approximate=True).astype(bf16), w2_ref[g], preferred_element_type=f32)
    return out


def _compress_prompt_body(xk_ref, xv_ref, wk_ref, pek_ref, w2k_ref, wv_ref, pev_ref, w2v_ref, kc_ref, vc_ref, *,
                          n_chunk):
    def flat(x_ref):
        return jnp.concatenate(
            [x_ref[pl.ds(l, n_chunk, stride=CMP_STRIDE), :].astype(bf16) for l in range(CMP_STRIDE)], axis=1)
    kc_ref[0] = _compress_rows(flat(xk_ref), wk_ref, pek_ref, w2k_ref).astype(bf16)
    vc_ref[0] = _compress_rows(flat(xv_ref), wv_ref, pev_ref, w2v_ref).T.astype(bf16)


def _compress_prompt(kvrow, cwk, cwv, *, nb, T):
    n_chunk = T // CMP_STRIDE
    const = lambda a: pl.BlockSpec(a.shape, lambda b: (0,) * a.ndim)
    return pl.pallas_call(
        functools.partial(_compress_prompt_body, n_chunk=n_chunk), grid=(nb,),
        in_specs=[pl.BlockSpec((T, 128), lambda b: (b, 0)), pl.BlockSpec((T, 128), lambda b: (b, 1))]
        + [const(a) for a in (*cwk, *cwv)],
        out_specs=[pl.BlockSpec((1, n_chunk, 128), lambda b: (b, 0, 0)),
                   pl.BlockSpec((1, 128, n_chunk), lambda b: (b, 0, 0))],
        out_shape=[jax.ShapeDtypeStruct((nb, n_chunk, 128), bf16), jax.ShapeDtypeStruct((nb, 128, n_chunk), bf16)],
        compiler_params=pltpu.CompilerParams(dimension_semantics=("parallel",), vmem_limit_bytes=VMEM_LIMIT),
        name="nsa_compress",
    )(kvrow, kvrow, *cwk, *cwv)


def _softmax_rows(s, mask):
    s = jnp.where(mask, s, NEG)
    e = jnp.exp2(s - jnp.max(s, axis=-1, keepdims=True))
    e = jnp.where(mask, e, 0.0)
    return e / jnp.maximum(jnp.sum(e, axis=-1, keepdims=True), TINY)


def _select_blocks(imp_t, qpos_row, n_blk):
    n = imp_t.shape[1]
    j = lax.broadcasted_iota(jnp.int32, imp_t.shape, 0)
    cur = qpos_row // SEL_BLOCK
    forced = (j == 0) | (j == cur) | (j == cur - 1)
    score = jnp.where(j <= cur, jnp.where(forced, FORCE_SCORE, imp_t), -1.0)
    n_tiles = -(-n_blk // 8)
    sub = lax.broadcasted_iota(jnp.int32, (8, n), 0)
    tiles = [score[8 * v:8 * v + 8] for v in range(n_tiles)]
    if n_blk % 8:
        tiles[-1] = jnp.where(sub < n_blk % 8, tiles[-1], -2.0)
    cnts = [jnp.zeros((8, n), f32)] * n_tiles
    for jp in range(n_blk):
        sj = tiles[jp // 8][jp % 8:jp % 8 + 1, :]
        for v in range(n_tiles):
            if v > jp // 8:
                beat = jnp.where(sj >= tiles[v], 1.0, 0.0)
            elif v < jp // 8:
                beat = jnp.where(sj > tiles[v], 1.0, 0.0)
            else:
                beat = jnp.where(sub > jp % 8, jnp.where(sj >= tiles[v], 1.0, 0.0), jnp.where(sj > tiles[v], 1.0, 0.0))
            cnts[v] = cnts[v] + beat
    keep = [jnp.where(c < float(min(SEL_TOP, n_blk)), 1.0, 0.0) for c in cnts]
    if n_blk % 8:
        keep[-1] = jnp.where(sub < n_blk % 8, keep[-1], 0.0)
    return jnp.concatenate(keep + [jnp.zeros((NBLK_PAD - 8 * n_tiles, n), f32)] * (NBLK_PAD > 8 * n_tiles), axis=0)


def _transpose_sel(sel_t):
    return jnp.concatenate([sel_t, jnp.zeros((128 - NBLK_PAD, sel_t.shape[1]), f32)], axis=0).T


def _softmax_cols(s, mask):
    s = jnp.where(mask, s, NEG)
    e = jnp.exp2(s - jnp.max(s, axis=0, keepdims=True))
    e = jnp.where(mask, e, 0.0)
    return e * (1.0 / jnp.maximum(jnp.sum(e, axis=0, keepdims=True), TINY))


def _lane_tile(x, n):
    return jnp.concatenate([x] * n, axis=1)


def _nsa_prompt_body(qct_ref, qrt_ref, gt_ref, kk_ref, vt_ref, kc_ref, vct_ref, ovt_ref, hbt_ref, sel_scr, *, T):
    i = pl.program_id(1)
    n_chunk = T // CMP_STRIDE
    n_blk = T // SEL_BLOCK
    q0 = i * TQ
    qpos_row = q0 + lax.broadcasted_iota(jnp.int32, (1, TQ), 1)
    gsig = jax.nn.sigmoid(gt_ref[...])
    n_kt = (q0 + TQ + TK - 1) // TK
    w0 = jnp.maximum(i - WINDOW // TQ, 0) * TQ
    WK = WINDOW + TQ
    blk_per_tile = TK // SEL_BLOCK
    ones_lhs = jnp.ones((16, TK), bf16)
    zeros_q = jnp.zeros((HEAD_DIM, TQ), bf16)

    def group_queries(ref, g):
        cols = []
        for r in range(R):
            q = ref[(g * R + r) * HEAD_DIM:(g * R + r + 1) * HEAD_DIM, :]
            cols.append(jnp.concatenate([q, zeros_q] if g == 0 else [zeros_q, q], axis=0))
        return jnp.concatenate(cols, axis=1)

    qct_all = jnp.concatenate([group_queries(qct_ref, g) for g in range(N_KV)], axis=1)
    qrt_all = jnp.concatenate([group_queries(qrt_ref, g) for g in range(N_KV)], axis=1)
    s = jnp.dot(kc_ref[0], qct_all, preferred_element_type=f32)
    cmp_end = lax.broadcasted_iota(jnp.int32, (n_chunk, 1), 0) * CMP_STRIDE + (CMP_LEN - 1)
    p = _softmax_cols(s, _lane_tile(cmp_end <= qpos_row, N_KV * R))
    o_c_all = jnp.dot(vct_ref[0], p.astype(bf16), preferred_element_type=f32)
    for g in range(N_KV):
        base = g * R * TQ
        psum = (p[:, base:base + TQ] + p[:, base + TQ:base + 2 * TQ] + p[:, base + 2 * TQ:base + 3 * TQ]
                + p[:, base + 3 * TQ:base + 4 * TQ])
        imp_t = jnp.dot(ovt_ref[...], psum, preferred_element_type=f32, precision=HIGHEST)
        sel_scr[g] = jnp.where(_select_blocks(imp_t, qpos_row, n_blk) > 0.5, 0.0, NEG)

    def kv_step(kt, carry):
        k0 = pl.multiple_of(kt * TK, TK)
        b0 = pl.multiple_of(kt * blk_per_tile, blk_per_tile)
        causal = (k0 + lax.broadcasted_iota(jnp.int32, (TK, 1), 0)) <= qpos_row
        m_i, acc = carry
        sc = jnp.dot(kk_ref[pl.ds(k0, TK), 0:128], qrt_all, preferred_element_type=f32)
        biases = []
        for g in range(N_KV):
            sel8 = sel_scr[g, pl.ds(b0, blk_per_tile), :]
            bias = jnp.broadcast_to(sel8[:, None, :], (blk_per_tile, SEL_BLOCK, TQ)).reshape(TK, TQ)
            biases.append(_lane_tile(jnp.where(causal, bias, NEG), R))
        sc = sc + jnp.concatenate(biases, axis=1)
        m_new = jnp.maximum(m_i, jnp.max(sc, axis=0, keepdims=True))
        alpha = jnp.exp2(m_i - m_new)
        pp = jnp.exp2(sc - m_new).astype(bf16)
        v_ext = jnp.concatenate([vt_ref[0:128, pl.ds(k0, TK)], ones_lhs], axis=0)
        return m_new, alpha * acc + jnp.dot(v_ext, pp, preferred_element_type=f32)

    init = (jnp.full((1, N_KV * R * TQ), NEG, f32), jnp.zeros((128 + 16, N_KV * R * TQ), f32))
    _, acc_all = lax.fori_loop(0, n_kt, kv_step, init)

    o_s_all = acc_all[0:128] * (1.0 / acc_all[128:129])
    w0a = pl.multiple_of(w0, TQ)
    sw = jnp.dot(kk_ref[pl.ds(w0a, WK), 128:256], qrt_all, preferred_element_type=f32)
    dpos = qpos_row - (w0 + lax.broadcasted_iota(jnp.int32, (WK, 1), 0))
    biasw = jnp.where((dpos >= 0) & (dpos < WINDOW), 0.0, NEG)
    sw = sw + _lane_tile(biasw, N_KV * R)
    pw = jnp.exp2(sw - jnp.max(sw, axis=0, keepdims=True)).astype(bf16)
    ones_w = jnp.ones((16, WK), bf16)
    ow = jnp.dot(jnp.concatenate([vt_ref[128:256, pl.ds(w0a, WK)], ones_w], axis=0), pw, preferred_element_type=f32)
    o_w_all = ow[0:128] * (1.0 / ow[128:129])
    for h in range(N_HEADS):
        rows = slice((h // R) * HEAD_DIM, (h // R + 1) * HEAD_DIM)
        cs = slice(h * TQ, (h + 1) * TQ)
        c0 = 8 + 3 * h
        out = (gsig[c0:c0 + 1, :] * o_c_all[rows, cs] + gsig[c0 + 1:c0 + 2, :] * o_s_all[rows, cs]
               + gsig[c0 + 2:c0 + 3, :] * o_w_all[rows, cs])
        hbt_ref[h * 64:(h + 1) * 64, :] = out.astype(bf16)


def _nsa_prompt(qct, qrt, gt, kk, vt, kc, vct, *, nb, T):
    nq = T // TQ
    n_chunk = T // CMP_STRIDE
    ovt = _sel_overlap_t(n_chunk, T // SEL_BLOCK)
    col = lambda h: pl.BlockSpec((h, TQ), lambda b, i: (0, b * nq + i))
    const = lambda a: pl.BlockSpec(a.shape, lambda b, i: (0,) * a.ndim)
    return pl.pallas_call(
        functools.partial(_nsa_prompt_body, T=T), grid=(nb, nq),
        in_specs=[col(512), col(512), col(32),
                  pl.BlockSpec((T, 256), lambda b, i: (b, 0)),
                  pl.BlockSpec((256, T), lambda b, i: (0, b)),
                  pl.BlockSpec((1, n_chunk, 128), lambda b, i: (b, 0, 0)),
                  pl.BlockSpec((1, 128, n_chunk), lambda b, i: (b, 0, 0)),
                  const(ovt)],
        out_specs=col(512),
        out_shape=jax.ShapeDtypeStruct((512, nb * T), bf16),
        scratch_shapes=[pltpu.VMEM((N_KV, NBLK_PAD, TQ), f32)],
        compiler_params=pltpu.CompilerParams(dimension_semantics=("parallel", "arbitrary"),
                                             vmem_limit_bytes=VMEM_LIMIT),
        name="nsa_prompt",
    )(qct, qrt, gt, kk, vt, kc, vct, ovt)


def _nsa_sample_body(pt_ref, *refs, n_pages, ts):
    n_pg = GB * n_pages
    pages = refs[:n_pg]
    (win_ref, qc_ref, qr_ref, sm_ref, kvn_ref, wn_ref, wk_ref, pek_ref, w2k_ref, wv_ref, pev_ref, w2v_ref,
     ovt_ref, e_ref, perm_ref) = refs[n_pg:n_pg + 15]
    hb_ref, wout_stack = refs[-2:]
    wout_ref = wout_stack.at[0]
    P = n_pages * PAGE
    n_chunk = P // CMP_STRIDE
    n_blk = -(-(P + ts) // SEL_BLOCK)
    WB = win_ref.shape[4]
    chunks_per_page = PAGE // CMP_STRIDE

    xt_stack = jnp.concatenate([pages[j][0, 0, c].astype(bf16) for j in range(n_pg) for c in (0, 1)], axis=0)
    y_all = lax.dot_general(perm_ref[...], xt_stack, NT, preferred_element_type=f32)

    def flat(c):
        return jnp.concatenate(
            [jnp.concatenate([y_all[l * chunks_per_page:(l + 1) * chunks_per_page, (2 * j + c) * 128:(2 * j + c + 1) * 128]
                              for j in range(n_pg)], axis=0)
             for l in range(CMP_STRIDE)], axis=1).astype(bf16)

    kc_all = _compress_rows(flat(0), wk_ref, pek_ref, w2k_ref).astype(bf16)
    vc_all = _compress_rows(flat(1), wv_ref, pev_ref, w2v_ref).astype(bf16)

    qc_all = qc_ref[...].astype(f32)
    qr_all = qr_ref[...].astype(f32)
    gsig = jax.nn.sigmoid(sm_ref[...])
    rows = N_KV * R * SEG
    qpos_col = P + lax.broadcasted_iota(jnp.int32, (rows, 1), 0) % SEG
    qpos_row = P + lax.broadcasted_iota(jnp.int32, (1, 128), 1) % SEG
    new_lane = lax.broadcasted_iota(jnp.int32, (1, 128), 1)
    zpad = jnp.zeros((128 - SEG, 128), f32)

    def stack(q_all, bi):
        parts = []
        for h in range(N_HEADS):
            g = h // R
            q = q_all[bi * SEG:(bi + 1) * SEG, (h // 2) * 128:(h // 2 + 1) * 128]
            if h % 2 != g:
                q = pltpu.roll(q, HEAD_DIM, axis=1)
            parts.append(jnp.where(new_lane // HEAD_DIM == g, q, 0.0))
        return jnp.concatenate(parts, axis=0).astype(bf16)

    cmp_end = lax.broadcasted_iota(jnp.int32, (1, n_chunk), 1) * CMP_STRIDE + (CMP_LEN - 1)
    o_cs, psums = [], []
    for bi in range(GB):
        s = lax.dot_general(stack(qc_all, bi), kc_all[bi * n_chunk:(bi + 1) * n_chunk], NT,
                            preferred_element_type=f32)
        p = _softmax_rows(s, cmp_end <= qpos_col)
        o_cs.append(jnp.dot(p.astype(bf16), vc_all[bi * n_chunk:(bi + 1) * n_chunk], preferred_element_type=f32))
        for g in range(N_KV):
            pg = p[g * R * SEG:(g + 1) * R * SEG]
            psums.append(pg[0:SEG] + pg[SEG:2 * SEG] + pg[2 * SEG:3 * SEG] + pg[3 * SEG:4 * SEG])
    psum_all = jnp.concatenate(psums + [jnp.zeros((128 - SEG * GB * N_KV, n_chunk), f32)], axis=0)
    imp_t = lax.dot_general(ovt_ref[...], psum_all, NT, preferred_element_type=f32, precision=HIGHEST)
    sel_all = _transpose_sel(_select_blocks(imp_t, qpos_row, n_blk)).astype(bf16)
    mk_all = jnp.dot(sel_all, e_ref[...], preferred_element_type=f32)

    out_rows = []
    for bi in range(GB):
        rs8 = slice(bi * SEG, (bi + 1) * SEG)
        ksel_t = jnp.concatenate([pages[bi * n_pages + pp][0, 0, 2].astype(bf16) for pp in range(n_pages)], axis=1)
        vsel_t = jnp.concatenate([pages[bi * n_pages + pp][0, 0, 3].astype(bf16) for pp in range(n_pages)], axis=1)
        knew = jnp.concatenate([kvn_ref[rs8, 256:384], zpad], axis=0).astype(bf16)
        vnew = jnp.concatenate([kvn_ref[rs8, 384:512], zpad], axis=0).astype(bf16)
        kwnew_f = jnp.concatenate([wn_ref[rs8, 0:128], zpad], axis=0)
        vwnew_f = jnp.concatenate([wn_ref[rs8, 128:256], zpad], axis=0)
        kwnew, vwnew = kwnew_f.astype(bf16), vwnew_f.astype(bf16)
        kwin_t = win_ref[0, bi, 0].astype(bf16)
        vwin_t = win_ref[0, bi, 1].astype(bf16)
        qr = stack(qr_all, bi)
        o_c = o_cs[bi]
        mk = jnp.concatenate(
            [mk_all[(bi * N_KV + g) * SEG:(bi * N_KV + g + 1) * SEG] for g in range(N_KV) for _ in range(R)], axis=0) > 0.5
        s_sel = jnp.concatenate([jnp.dot(qr, ksel_t, preferred_element_type=f32),
                                 lax.dot_general(qr, knew, NT, preferred_element_type=f32)], axis=1)
        kpos = jnp.concatenate([lax.broadcasted_iota(jnp.int32, (1, P), 1), P + new_lane], axis=1)
        valid = jnp.concatenate([jnp.full((1, P), True), new_lane < SEG], axis=1)
        p_s = _softmax_rows(s_sel, mk & (kpos <= qpos_col) & valid).astype(bf16)
        o_s = (lax.dot_general(p_s[:, 0:P], vsel_t, NT, preferred_element_type=f32)
               + jnp.dot(p_s[:, P:P + 128], vnew, preferred_element_type=f32))
        s_w = jnp.concatenate([jnp.dot(qr, kwin_t, preferred_element_type=f32),
                               lax.dot_general(qr, kwnew, NT, preferred_element_type=f32)], axis=1)
        kwpos = jnp.concatenate([P - WB + lax.broadcasted_iota(jnp.int32, (1, WB), 1), P + new_lane], axis=1)
        validw = jnp.concatenate([jnp.full((1, WB), True), new_lane < SEG], axis=1)
        dpos = qpos_col - kwpos
        p_w = _softmax_rows(s_w, (dpos >= 0) & (dpos < WINDOW) & (kwpos >= 0) & validw).astype(bf16)
        o_w = (lax.dot_general(p_w[:, 0:WB], vwin_t, NT, preferred_element_type=f32)
               + jnp.dot(p_w[:, WB:WB + 128], vwnew, preferred_element_type=f32))
        gs = gsig[rs8]
        heads = [None] * N_HEADS
        for h in range(N_HEADS):
            rr = slice(h * SEG, (h + 1) * SEG)
            c0 = 8 + 3 * h
            out = gs[:, c0:c0 + 1] * o_c[rr] + gs[:, c0 + 1:c0 + 2] * o_s[rr] + gs[:, c0 + 2:c0 + 3] * o_w[rr]
            if h % 2 != h // R:
                out = pltpu.roll(out, HEAD_DIM, axis=1)
            heads[h] = out
        out_rows.append(jnp.concatenate(
            [jnp.where(new_lane < HEAD_DIM, heads[2 * j], heads[2 * j + 1]) for j in range(N_HEADS // 2)], axis=1))
        for kv, new_f in ((0, kwnew_f), (1, vwnew_f)):
            old = pltpu.roll(win_ref[0, bi, kv], WB - ts, axis=1)
            new_t = pltpu.roll(new_f.T, 128 - ts, axis=1)
            wout_ref[bi, kv, :, 0:WB - 128] = old[:, 0:WB - 128]
            wout_ref[bi, kv, :, WB - 128:WB] = jnp.where(new_lane >= 128 - ts, new_t, old[:, WB - 128:WB])
    hb_ref[...] = jnp.concatenate(out_rows, axis=0).astype(bf16)


def _nsa_sample(page_table, cache_t, cwin_t, qc, qr, sm, kvrow, winrow, cwk, cwv, win_stack, *, layer, row0, ts):
    nseq, n_pages = page_table.shape
    depth = cwin_t.shape[0]
    P = n_pages * PAGE
    WB = cwin_t.shape[4]
    n_chunk = P // CMP_STRIDE
    ovt = _sel_overlap_t(n_chunk, -(-(P + ts) // SEL_BLOCK))
    e = _block_expand(P + 128)
    r = np.arange(PAGE)
    perm = jnp.asarray(np.arange(PAGE)[None, :] == (CMP_STRIDE * (r % (PAGE // CMP_STRIDE)) + r // (PAGE // CMP_STRIDE))[:, None],
                       dtype=bf16)
    b0 = row0 // (GB * SEG)

    def page_map(i, pt, *, bi, p):
        return (layer, pt[i * GB + bi, p], 0, 0, 0)

    page_specs = [pl.BlockSpec((1, 1, 4, 128, PAGE), functools.partial(page_map, bi=bi, p=p))
                  for bi in range(GB) for p in range(n_pages)]
    row = lambda w: pl.BlockSpec((GB * SEG, w), lambda i, pt: (b0 + i, 0))
    const = lambda a: pl.BlockSpec(a.shape, lambda i, pt: (0,) * a.ndim)
    extra_in, extra_spec, win_shape = _stacked_out(win_stack, depth, (nseq, 2, 128, WB))
    args = (page_table, *([cache_t] * (GB * n_pages)), cwin_t, qc, qr, sm, kvrow, winrow, *cwk, *cwv, ovt, e, perm)
    grid_spec = pltpu.PrefetchScalarGridSpec(
        num_scalar_prefetch=1, grid=(nseq // GB,),
        in_specs=page_specs + [pl.BlockSpec((1, GB, 2, 128, WB), lambda i, pt: (layer, i, 0, 0, 0)),
                               row(N_WIDTH), row(N_WIDTH), row(128), row(512), row(256)]
        + [const(a) for a in (*cwk, *cwv, ovt, e, perm)] + extra_spec,
        out_specs=[pl.BlockSpec((GB * SEG, N_WIDTH), lambda i, pt: (i, 0)),
                   pl.BlockSpec((1, GB, 2, 128, WB), lambda i, pt: (layer, i, 0, 0, 0))])
    return pl.pallas_call(
        functools.partial(_nsa_sample_body, n_pages=n_pages, ts=ts), grid_spec=grid_spec,
        out_shape=[jax.ShapeDtypeStruct((nseq * SEG, N_WIDTH), bf16), win_shape],
        input_output_aliases={len(args): 1} if extra_in else {},
        compiler_params=pltpu.CompilerParams(dimension_semantics=("parallel",), vmem_limit_bytes=VMEM_LIMIT),
        name="nsa_sample",
    )(*args, *extra_in)


def kernel(x_prompt, x_sample, cache_kv_pages, page_table, cache_win, state_mlstm_C, state_mlstm_n, state_mlstm_m, state_mlstm_conv, norm1_g, w_in, b_gates, conv_w, conv_b, w_mq, w_mk, mlstm_norm_g, cmp_pe_k, cmp_pe_v, cmp_w1_k, cmp_w2_k, cmp_w1_v, cmp_w2_v, w_branch_a, w_branch_b, w_out, norm2_g, w_ffn_up, w_ffn_down, final_norm_g):
    Bp, Tp, _ = x_prompt.shape
    Bs, Ts, _ = x_sample.shape
    depth = w_in.shape[0]
    n_pool, page = cache_kv_pages.shape[1:3]
    n_pages = page_table.shape[1]
    past_len = n_pages * page
    wb_len = cache_win.shape[2]
    mp = Bp * Tp
    ms = Bs * SAMPLE_PAD
    assert page == PAGE and M_CONV - 1 <= Ts <= SAMPLE_PAD and Tp % MLSTM_CHUNK == 0 and Tp >= WINDOW + TQ
    assert mp % (SEG * SEQ_PER_STEP) == 0 and Bs % SEQ_PER_STEP == 0 and Bs % GB == 0

    pad_seq = lambda a: jnp.pad(a, ((0, 0), (0, SAMPLE_PAD - a.shape[1]), (0, 0)))
    xs = (x_prompt.reshape(mp, D_MODEL), pad_seq(x_sample).reshape(ms, D_MODEL))
    tabs = _rope_tables(Tp, past_len, ROW_TILE * (ms // ROW_TILE))
    w_all = _in_proj_weights(w_in)
    w_up, w_down = w_ffn_up.astype(bf16), w_ffn_down.astype(bf16)
    cache_t = jnp.transpose(cache_kv_pages, (0, 1, 3, 4, 5, 2)).reshape(depth, n_pool, 4, KV_W, PAGE)
    cwin_t = jnp.transpose(cache_win, (0, 1, 3, 4, 5, 2)).reshape(depth, Bs, 2, KV_W, wb_len)

    outs = {k: [] for k in ("p_win", "p_C", "p_n", "p_m", "p_conv", "s_kv", "s_n", "s_m", "s_conv")}
    c_stack = win_stack = kvt_stack = None
    y_parts = None
    for l in range(depth):
        w_t = jnp.concatenate([w_in[l][:, O_IG:O_IG + 8], w_in[l][:, O_NG:O_NG + 24]], axis=1).T.astype(bf16)
        xm, sm, smt, qc, qr, qct, qrt, kvrow, winrow, kk, vt, gates, kvt_stack = _in_proj(
            xs, norm1_g[l].reshape(1, D_MODEL), w_all, w_t, tabs, Tp, kvt_stack, layer=l, depth=depth)

        wq, wk = w_mq[l].astype(bf16), w_mk[l].astype(bf16)
        ha_p, c_p, n_p, m_p, cv_p = _mlstm_prompt(xm, sm, smt, conv_w[l], conv_b[l], wq, wk, b_gates[l], mlstm_norm_g[l],
                                                   nb=Bp, T=Tp, L=MLSTM_CHUNK)
        tail = jnp.pad(state_mlstm_conv[l], ((0, 0), (0, SEG - (M_CONV - 1)), (0, 0))).reshape(ms, M_WIDTH)
        n0 = jnp.pad(state_mlstm_n[l], ((0, 0), (0, 8 - M_HEADS), (0, 0)))
        m0 = jnp.broadcast_to(jnp.pad(state_mlstm_m[l], ((0, 0), (0, 128 - M_HEADS)))[:, None, :], (Bs, SEG, 128))
        ha_s, c_stack, n_s, mt_s = _mlstm_sample(
            xm, sm, smt, tail, state_mlstm_C, n0, m0.reshape(ms, 128), conv_w[l], conv_b[l], wq, wk, b_gates[l],
            mlstm_norm_g[l], c_stack, layer=l, depth=depth, row0=mp, nseq=Bs, n_valid=Ts)

        cwk = _compress_weights(cmp_pe_k[l], cmp_w1_k[l], cmp_w2_k[l])
        cwv = _compress_weights(cmp_pe_v[l], cmp_w1_v[l], cmp_w2_v[l])
        kc, vct = _compress_prompt(kvrow, cwk, cwv, nb=Bp, T=Tp)
        hbt_p = _nsa_prompt(qct, qrt, smt, kk, vt, kc, vct, nb=Bp, T=Tp)
        hb_s, win_stack = _nsa_sample(page_table, cache_t, cwin_t, qc, qr, sm, kvrow, winrow, cwk, cwv, win_stack,
                                      layer=l, row0=mp, ts=Ts)

        x_all, *y_parts = _mix_ffn(
            xs, (ha_p, ha_s), (hbt_p, hb_s.T), gates, w_branch_a[l].astype(bf16), w_branch_b[l].astype(bf16),
            w_out[l].astype(bf16), norm2_g[l].reshape(1, D_MODEL), w_up, w_down, final_norm_g.reshape(1, D_MODEL),
            layer=l, n_prompt_rows=mp, final=l == depth - 1)
        xs = (x_all,)

        sample_rows = lambda a: a[mp:].reshape(Bs, SAMPLE_PAD, a.shape[1])
        wp = min(WINDOW, Tp)
        outs["p_win"].append(winrow[:mp].reshape(Bp, Tp, 2 * KV_W)[:, -wp:].reshape(Bp, wp, 2, N_KV, HEAD_DIM))
        outs["p_C"].append(c_p)
        outs["p_n"].append(n_p[:, :M_HEADS])
        outs["p_m"].append(m_p[:, 0, :M_HEADS])
        outs["p_conv"].append(cv_p[:, 8 - (M_CONV - 1):])
        outs["s_kv"].append(sample_rows(kvrow)[:, :Ts].reshape(Bs, Ts, 4, N_KV, HEAD_DIM))
        outs["s_n"].append(n_s[:, :M_HEADS])
        outs["s_m"].append(mt_s.reshape(Bs, SEG, 128)[:, SEG - 1, :M_HEADS])
        outs["s_conv"].append(sample_rows(xm)[:, Ts - (M_CONV - 1):Ts, :M_WIDTH])

    s_win = jnp.transpose(win_stack.reshape(depth, Bs, 2, N_KV, HEAD_DIM, wb_len), (0, 1, 5, 2, 3, 4))
    y_prompt = y_parts[0].reshape(Bp, Tp, D_MODEL)
    y_sample = y_parts[1].reshape(Bs, SAMPLE_PAD, D_MODEL)[:, :Ts]
    st = lambda k: jnp.stack(outs[k])
    p_kv = jnp.transpose(kvt_stack.reshape(depth, Bp, 4, N_KV, HEAD_DIM, Tp), (0, 1, 5, 2, 3, 4))
    return (y_prompt, y_sample, p_kv, st("p_win"), st("p_C"), st("p_n"), st("p_m"), st("p_conv"),
            st("s_kv"), s_win, c_stack, st("s_n"), st("s_m"), st("s_conv"))
```

```python
import functools

import jax
import jax.numpy as jnp
import numpy as np
from jax import lax
from jax.experimental import pallas as pl
from jax.experimental.pallas import tpu as pltpu

f32 = jnp.float32
bf16 = jnp.bfloat16

D_MODEL = 1024
M_HEADS = 4
M_WIDTH = 512
M_HEAD_DIM = 128
M_CONV = 4
MLSTM_CHUNK = 256
HEAD_DIM = 64
N_WIDTH = 512
N_HEADS = 8
N_KV = 2
KV_W = 128
CMP_STRIDE = 16
CMP_LEN = 32
SEL_BLOCK = 64
SEL_TOP = 16
WINDOW = 512
ROT_DIM = 16
ROPE_THETA = 500000.0
D_FF = 2816
EPS = 1e-6
NEG = -1e30
TINY = 1e-30
FORCE_SCORE = 1e9

SAMPLE_PAD = 8
SEG = SAMPLE_PAD
SEQ_PER_STEP = 16
GB = 2
PAGE = 128
HD = M_HEAD_DIM
R = N_HEADS // N_KV
CMP_HID = 128
TQ = 128
TK = 1024
NBLK_PAD = 64
HIGHEST = lax.Precision.HIGHEST
NT = (((1,), (1,)), ((), ()))
LOG2E = 1.4426950408889634
ROW_TILE = 256
FF_CHUNK = 2816
VMEM_LIMIT = 56 * 1024 * 1024

C_XM, C_Q, C_KV, C_WIN, C_GATE, C_SM = 0, 1536, 2048, 2560, 2816, 4864
C_TOTAL = 4992
O_U, O_V, O_O, O_IG, O_FG, O_Q, O_KV, O_NG, O_GA, O_GB = 0, 512, 1024, 1536, 1540, 1544, 2056, 2824, 2848, 3872
IN_WIDTH = 4896


def _in_proj_weights(w):
    lead = w.shape[:-1]
    parts = [w[..., O_U:O_IG], w[..., O_Q:O_Q + N_WIDTH] * (HEAD_DIM ** -0.5), w[..., O_KV:O_KV + 6 * KV_W],
             w[..., O_GA:O_GA + 2 * D_MODEL], w[..., O_IG:O_IG + 8], w[..., O_NG:O_NG + 24],
             jnp.zeros(lead + (128 - 32,), f32)]
    out = jnp.concatenate(parts, axis=-1).astype(bf16)
    assert out.shape[-1] == C_TOTAL
    return out


def _rope_tables(seq, past_len, n_sample_rows):
    half = ROT_DIM // 2
    inv = ROPE_THETA ** (-jnp.arange(half, dtype=f32) / half)
    pos = jnp.concatenate([jnp.arange(seq), past_len + (jnp.arange(n_sample_rows) % SAMPLE_PAD)]).astype(f32)
    ang = pos[:, None] * inv[None, :]
    cos8, sin8 = jnp.cos(ang), jnp.sin(ang)
    n = pos.shape[0]
    one = jnp.ones((n, 64 - ROT_DIM), f32)
    zero = jnp.zeros((n, 64 - ROT_DIM), f32)
    z8 = jnp.zeros((n, half), f32)
    cos = jnp.concatenate([cos8, cos8, one], axis=1)
    sa = jnp.concatenate([-sin8, z8, zero], axis=1)
    sb = jnp.concatenate([z8, sin8, zero], axis=1)
    tile2 = lambda a: jnp.concatenate([a, a], axis=1)
    return tile2(cos), tile2(sa), tile2(sb)


def _rms(x, g):
    return x * lax.rsqrt(jnp.mean(x * x, axis=-1, keepdims=True) + EPS) * g


def _in_proj_body(*refs, n_x, n_sample_tiles):
    x_refs = refs[:n_x]
    g_ref, w_ref, wt_ref, cos_ref, sa_ref, sb_ref = refs[n_x:n_x + 6]
    (xm_ref, sm_ref, smt_ref, qc_ref, qr_ref, qct_ref, qrt_ref, kv_ref, win_ref, kk_ref, vt_ref, gate_ref,
     kvt_stack) = refs[-13:]
    x = x_refs[0][...]
    if n_x == 2:
        x = jnp.where(pl.program_id(0) < n_sample_tiles, x_refs[1][...], x)
    hb = _rms(x, g_ref[...]).astype(bf16)
    cos, sa, sb = cos_ref[...], sa_ref[...], sb_ref[...]

    def rope(v):
        return v * cos + pltpu.roll(v, 128 - ROT_DIM // 2, axis=1) * sa + pltpu.roll(v, ROT_DIM // 2, axis=1) * sb

    def proj(c0, n):
        return jnp.dot(hb, w_ref[0, :, c0:c0 + n], preferred_element_type=f32)

    xm_ref[...] = proj(C_XM, 1536)
    sm_ref[...] = proj(C_SM, 128)
    smt_ref[...] = lax.dot_general(wt_ref[...], hb, NT, preferred_element_type=f32)
    qf = proj(C_Q, N_WIDTH) * LOG2E
    for j in range(N_HEADS // 2):
        sl = slice(j * 128, (j + 1) * 128)
        q = qf[:, sl]
        qrot = rope(q)
        qc_ref[:, sl] = q.astype(bf16)
        qr_ref[:, sl] = qrot.astype(bf16)
        qct_ref[sl, :] = q.T.astype(bf16)
        qrt_ref[sl, :] = qrot.T.astype(bf16)
    kv = proj(C_KV, 512)
    ksel = rope(kv[:, 256:384])
    kv_ref[:, 0:256] = kv[:, 0:256]
    kv_ref[:, 256:384] = ksel
    kv_ref[:, 384:512] = kv[:, 384:512]
    kvt_stack[0, 0, 0:256, :] = kv[:, 0:256].T
    kvt_stack[0, 0, 256:384, :] = ksel.T
    kvt_stack[0, 0, 384:512, :] = kv[:, 384:512].T
    win = proj(C_WIN, 256)
    kwin = rope(win[:, 0:128])
    win_ref[:, 0:128] = kwin
    win_ref[:, 128:256] = win[:, 128:256]
    kk_ref[:, 0:128] = ksel.astype(bf16)
    kk_ref[:, 128:256] = kwin.astype(bf16)
    vt_ref[0:128, :] = kv[:, 384:512].T.astype(bf16)
    vt_ref[128:256, :] = win[:, 128:256].T.astype(bf16)
    gate_ref[...] = proj(C_GATE, 2048)


def _in_proj(xs, norm_g, w_all, w_t, tabs, seq, kvt_stack, *, layer, depth):
    m = sum(x.shape[0] for x in xs)
    n_prompt_tiles_per_seq = seq // ROW_TILE
    n_prompt_tiles = (m - (tabs[0].shape[0] - seq)) // ROW_TILE
    n_seq = n_prompt_tiles // n_prompt_tiles_per_seq
    n_tiles = m // ROW_TILE
    n_sample_tiles = n_tiles - n_prompt_tiles

    def tile(i):
        return (i + n_prompt_tiles) % n_tiles

    def kvt_map(i):
        j = jnp.maximum(i - n_sample_tiles, 0)
        return (layer, j // n_prompt_tiles_per_seq, 0, j % n_prompt_tiles_per_seq)

    def tab_map(i):
        t = tile(i)
        return (jnp.where(t < n_prompt_tiles, t % n_prompt_tiles_per_seq, n_prompt_tiles_per_seq + t - n_prompt_tiles), 0)

    row = lambda w: pl.BlockSpec((ROW_TILE, w), lambda i: (tile(i), 0))
    const = lambda s: pl.BlockSpec(s, lambda i: (0, 0), pipeline_mode=pl.Buffered(1))
    tab = pl.BlockSpec((ROW_TILE, 128), tab_map)
    widths = (1536, 128, -32, 512, 512, -512, -512, 512, 256, 256, -256, 2048)
    dts = (f32, f32, f32, bf16, bf16, bf16, bf16, f32, f32, bf16, bf16, f32)
    out_shape = [jax.ShapeDtypeStruct((-w, m) if w < 0 else (m, w), d) for w, d in zip(widths, dts)]
    out_specs = [pl.BlockSpec((-w, ROW_TILE), lambda i: (0, tile(i))) if w < 0 else row(w) for w in widths]
    extra_in, extra_spec, kvt_shape = _stacked_out(kvt_stack, depth, (n_seq, 4 * KV_W, seq))
    if len(xs) == 1:
        x_specs = [row(D_MODEL)]
    else:
        x_specs = [pl.BlockSpec((ROW_TILE, D_MODEL), lambda i: (jnp.maximum(i - n_sample_tiles, 0), 0)),
                   pl.BlockSpec((ROW_TILE, D_MODEL), lambda i: (jnp.minimum(i, n_sample_tiles - 1), 0))]
    args = (*xs, norm_g, w_all, w_t, *tabs)
    return pl.pallas_call(
        functools.partial(_in_proj_body, n_x=len(xs), n_sample_tiles=n_sample_tiles), grid=(n_tiles,),
        in_specs=x_specs + [const((1, D_MODEL)),
                            pl.BlockSpec((1, D_MODEL, C_TOTAL), lambda i: (layer, 0, 0), pipeline_mode=pl.Buffered(1)),
                            const((32, D_MODEL)), tab, tab, tab]
        + extra_spec,
        out_specs=out_specs + [pl.BlockSpec((1, 1, 4 * KV_W, ROW_TILE), kvt_map)],
        out_shape=out_shape + [kvt_shape], name="in_proj",
        input_output_aliases={len(args): len(out_shape)} if extra_in else {},
        compiler_params=pltpu.CompilerParams(dimension_semantics=("arbitrary",), vmem_limit_bytes=VMEM_LIMIT),
    )(*args, *extra_in)


def _mix_ffn_body(*refs, n_x, n_prompt_tiles, final):
    x_refs = refs[:n_x]
    hap_ref, has_ref, hbp_ref, hbs_ref, gate_ref, wa_ref, wb_ref, wo_ref, g2_ref, wup_ref, wdn_ref, gf_ref = (
        refs[n_x:n_x + 12])
    xo_ref, *y_refs = refs[n_x + 12:]
    in_prompt = pl.program_id(0) < n_prompt_tiles
    pick = lambda p_ref, s_ref: jnp.where(in_prompt, p_ref[...], s_ref[...])
    x = x_refs[0][...] if n_x == 1 else pick(*x_refs)
    a = jnp.dot(pick(hap_ref, has_ref).astype(bf16), wa_ref[...], preferred_element_type=f32)
    b = lax.dot_general(pick(hbp_ref, hbs_ref), wb_ref[...], (((0,), (0,)), ((), ())),
                        preferred_element_type=f32)
    merged = jax.nn.sigmoid(gate_ref[:, 0:D_MODEL]) * a + jax.nn.sigmoid(gate_ref[:, D_MODEL:2 * D_MODEL]) * b
    x1 = x + jnp.dot(merged.astype(bf16), wo_ref[...], preferred_element_type=f32)
    hn = _rms(x1, g2_ref[...]).astype(bf16)
    acc = x1
    for c in range(D_FF // FF_CHUNK):
        lo = c * FF_CHUNK
        g = jnp.dot(hn, wup_ref[0, :, lo:lo + FF_CHUNK], preferred_element_type=f32)
        u = jnp.dot(hn, wup_ref[0, :, D_FF + lo:D_FF + lo + FF_CHUNK], preferred_element_type=f32)
        act = (g * jax.nn.sigmoid(g) * u).astype(bf16)
        acc = acc + jnp.dot(act, wdn_ref[0, lo:lo + FF_CHUNK, :], preferred_element_type=f32)
    xo_ref[...] = acc
    if final:
        y = _rms(acc, gf_ref[...])
        yp_ref, ys_ref = y_refs

        @pl.when(in_prompt)
        def _():
            yp_ref[...] = y

        @pl.when(jnp.logical_not(in_prompt))
        def _():
            ys_ref[...] = y


def _mix_ffn(xs, h_a, h_bt, gates, wa, wb, wo, g2, wup, wdn, gf, *, layer, n_prompt_rows, final):
    m = gates.shape[0]
    npt = n_prompt_rows // ROW_TILE
    row = lambda w: pl.BlockSpec((ROW_TILE, w), lambda i: (i, 0))
    p_row = lambda w: pl.BlockSpec((ROW_TILE, w), lambda i: (jnp.minimum(i, npt - 1), 0))
    s_row = lambda w: pl.BlockSpec((ROW_TILE, w), lambda i: (jnp.maximum(i - npt, 0), 0))
    p_col = lambda h: pl.BlockSpec((h, ROW_TILE), lambda i: (0, jnp.minimum(i, npt - 1)))
    s_col = lambda h: pl.BlockSpec((h, ROW_TILE), lambda i: (0, jnp.maximum(i - npt, 0)))
    const = lambda s: pl.BlockSpec(s, lambda i: (0, 0), pipeline_mode=pl.Buffered(1))
    layer_w = lambda a: pl.BlockSpec((1,) + a.shape[1:], lambda i: (layer, 0, 0), pipeline_mode=pl.Buffered(1))
    out_specs = [row(D_MODEL)]
    out_shape = [jax.ShapeDtypeStruct((m, D_MODEL), f32)]
    if final:
        out_specs += [p_row(D_MODEL), s_row(D_MODEL)]
        out_shape += [jax.ShapeDtypeStruct((n_prompt_rows, D_MODEL), f32),
                      jax.ShapeDtypeStruct((m - n_prompt_rows, D_MODEL), f32)]
    x_specs = [row(D_MODEL)] if len(xs) == 1 else [p_row(D_MODEL), s_row(D_MODEL)]
    return pl.pallas_call(
        functools.partial(_mix_ffn_body, n_x=len(xs), n_prompt_tiles=npt, final=final), grid=(m // ROW_TILE,),
        in_specs=x_specs + [p_row(M_WIDTH), s_row(M_WIDTH), p_col(N_WIDTH), s_col(N_WIDTH), row(2 * D_MODEL),
                            const(wa.shape), const(wb.shape), const(wo.shape), const((1, D_MODEL)),
                            layer_w(wup), layer_w(wdn), const((1, D_MODEL))],
        out_specs=out_specs, out_shape=out_shape, name="mix_ffn",
        compiler_params=pltpu.CompilerParams(dimension_semantics=("arbitrary",), vmem_limit_bytes=VMEM_LIMIT),
    )(*xs, *h_a, *h_bt, gates, wa, wb, wo, g2, wup, wdn, gf)


def _log_sigmoid(x):
    return jnp.minimum(x, 0.0) - jnp.log1p(jnp.exp(-jnp.abs(x)))


def _seg_scan(x, axis, seg, op, fill):
    idx = lax.broadcasted_iota(jnp.int32, x.shape, axis) % seg
    d = 1
    while d < seg:
        x = op(x, jnp.where(idx >= d, pltpu.roll(x, d, axis=axis), fill))
        d *= 2
    return x


def _conv_silu(u, tail, conv_w, conv_b, seg):
    L = u.shape[0]
    row = lax.broadcasted_iota(jnp.int32, u.shape, 0) % seg
    y = conv_b + u * conv_w[3:4, :]
    for k in (1, 2, 3):
        prev = tail if k == 3 else pltpu.roll(tail, L - (3 - k), axis=0)
        sh = jnp.where(row < k, prev, pltpu.roll(u, k, axis=0))
        y = y + sh * conv_w[3 - k:4 - k, :]
    return y * jax.nn.sigmoid(y)


def _gate_columns(sm, smt, bg_row, bg_col, m_vec, seg, n_valid):
    G = sm + bg_row
    lf = _log_sigmoid(G)
    ipre = G
    if n_valid < seg:
        rowc = lax.broadcasted_iota(jnp.int32, G.shape, 0) % seg
        lf = jnp.where(rowc < n_valid, lf, 0.0)
        ipre = jnp.where(rowc < n_valid, G, NEG)
    F = _seg_scan(lf, 0, seg, jnp.add, 0.0)
    F_al = pltpu.roll(F, 128 - M_HEADS, axis=1)
    a_col = ipre - F_al
    g_col = jnp.maximum(m_vec, _seg_scan(a_col, 0, seg, jnp.maximum, NEG))
    mt_col = F_al + g_col
    Gr = smt + bg_col
    lfr = _log_sigmoid(Gr)
    ir = Gr
    if n_valid < seg:
        lanec = lax.broadcasted_iota(jnp.int32, Gr.shape, 1) % seg
        lfr = jnp.where(lanec < n_valid, lfr, 0.0)
        ir = jnp.where(lanec < n_valid, Gr, NEG)
    Fr = _seg_scan(lfr, 1, seg, jnp.add, 0.0)
    a_row = ir - pltpu.roll(Fr, M_HEADS, axis=0)
    return a_col, g_col, mt_col, a_row


def _head_norm_gate(hh, ng, o):
    return hh * lax.rsqrt(jnp.mean(hh * hh, axis=-1, keepdims=True) + EPS) * ng * jax.nn.sigmoid(o)


def _mlstm_prompt_body(xm_ref, sm_ref, smt_ref, cw_ref, cb_ref, wq_ref, wk_ref, bgr_ref, bgc_ref, ng_ref,
                       ha_ref, c_out, n_out, m_out, conv_out, c_scr, n_scr, m_scr, tail_scr, *, L):
    c_idx = pl.program_id(1)

    @pl.when(c_idx == 0)
    def _():
        c_scr[...] = jnp.zeros_like(c_scr)
        n_scr[...] = jnp.zeros_like(n_scr)
        m_scr[...] = jnp.zeros_like(m_scr)
        tail_scr[...] = jnp.zeros_like(tail_scr)

    u = xm_ref[:, 0:M_WIDTH]
    tail = jnp.concatenate([tail_scr[...], jnp.zeros((L - 8, M_WIDTH), f32)], axis=0)
    cact = _conv_silu(u, tail, cw_ref[...], cb_ref[...], L)
    tail_scr[...] = pltpu.roll(u[L - 8:L, :], 3, axis=0)
    conv_out[0] = u[L - 8:L, :]

    m_vec = m_scr[0:1, :]
    a_col, g_col, mt_col, a_row = _gate_columns(sm_ref[...], smt_ref[...], bgr_ref[...], bgc_ref[...], m_vec, L, L)
    inter_col = jnp.exp(m_vec - g_col)
    floor_col = jnp.exp(-mt_col)
    g_last = g_col[L - 1:L, :]
    w_col = jnp.exp(a_col - g_last)
    decay = jnp.exp(m_vec - g_last)
    m_new = mt_col[L - 1:L, :]
    ti = lax.broadcasted_iota(jnp.int32, (L, L), 0)
    si = lax.broadcasted_iota(jnp.int32, (L, L), 1)
    causal = si <= ti
    scale = HD ** -0.5
    for h in range(M_HEADS):
        sl = slice(h * HD, (h + 1) * HD)
        ch = cact[:, sl].astype(bf16)
        qh = jnp.dot(ch, wq_ref[h], preferred_element_type=f32)
        kh = jnp.dot(ch, wk_ref[h], preferred_element_type=f32) * scale
        vh = xm_ref[:, M_WIDTH + h * HD:M_WIDTH + (h + 1) * HD].astype(bf16)
        qb = qh.astype(bf16)
        D = jnp.exp(jnp.where(causal, a_row[h:h + 1, :] - g_col[:, h:h + 1], NEG))
        S = lax.dot_general(qb, kh.astype(bf16), NT, preferred_element_type=f32) * D
        C = c_scr[h]
        nrow = n_scr[h:h + 1, :]
        ic = inter_col[:, h:h + 1]
        num = jnp.dot(S.astype(bf16), vh, preferred_element_type=f32) + ic * jnp.dot(
            qb, C.astype(bf16), preferred_element_type=f32)
        den = jnp.sum(S, axis=-1, keepdims=True) + ic * jnp.sum(qh * nrow, axis=-1, keepdims=True)
        hh = num / jnp.maximum(jnp.abs(den), floor_col[:, h:h + 1])
        o = xm_ref[:, 2 * M_WIDTH + h * HD:2 * M_WIDTH + (h + 1) * HD]
        ha_ref[:, sl] = _head_norm_gate(hh, ng_ref[:, sl], o)
        kw = kh * w_col[:, h:h + 1]
        dh = decay[:, h:h + 1]
        c_scr[h] = dh * C + lax.dot_general(kw.astype(bf16), vh, (((0,), (0,)), ((), ())), preferred_element_type=f32)
        n_scr[h:h + 1, :] = dh * nrow + jnp.sum(kw, axis=0, keepdims=True)
    m_scr[0:1, :] = m_new

    @pl.when(c_idx == pl.num_programs(1) - 1)
    def _():
        c_out[0] = c_scr[...]
        n_out[0] = n_scr[...]
        m_out[0] = m_scr[...]


def _mlstm_prompt(xm, sm, smt, conv_w, conv_b, wq, wk, b_gates, norm_g, *, nb, T, L):
    nc = T // L
    bg_row = jnp.zeros((1, 128), f32).at[0, :8].set(b_gates)
    bg_col = b_gates.reshape(8, 1)
    const = lambda s: pl.BlockSpec(s, lambda b, c: (0,) * len(s))
    return pl.pallas_call(
        functools.partial(_mlstm_prompt_body, L=L), grid=(nb, nc),
        in_specs=[pl.BlockSpec((L, 3 * M_WIDTH), lambda b, c: (b * nc + c, 0)),
                  pl.BlockSpec((L, 128), lambda b, c: (b * nc + c, 0)),
                  pl.BlockSpec((8, L), lambda b, c: (0, b * nc + c)),
                  const((4, M_WIDTH)), const((1, M_WIDTH)), const((M_HEADS, HD, HD)), const((M_HEADS, HD, HD)),
                  const((1, 128)), const((8, 1)), const((1, M_WIDTH))],
        out_specs=[pl.BlockSpec((L, M_WIDTH), lambda b, c: (b * nc + c, 0)),
                   pl.BlockSpec((1, M_HEADS, HD, HD), lambda b, c: (b, 0, 0, 0)),
                   pl.BlockSpec((1, 8, HD), lambda b, c: (b, 0, 0)),
                   pl.BlockSpec((1, 8, 128), lambda b, c: (b, 0, 0)),
                   pl.BlockSpec((1, 8, M_WIDTH), lambda b, c: (b, 0, 0))],
        out_shape=[jax.ShapeDtypeStruct((nb * T, M_WIDTH), f32),
                   jax.ShapeDtypeStruct((nb, M_HEADS, HD, HD), f32),
                   jax.ShapeDtypeStruct((nb, 8, HD), f32),
                   jax.ShapeDtypeStruct((nb, 8, 128), f32),
                   jax.ShapeDtypeStruct((nb, 8, M_WIDTH), f32)],
        scratch_shapes=[pltpu.VMEM((M_HEADS, HD, HD), f32), pltpu.VMEM((8, HD), f32), pltpu.VMEM((8, 128), f32),
                        pltpu.VMEM((8, M_WIDTH), f32)],
        compiler_params=pltpu.CompilerParams(dimension_semantics=("parallel", "arbitrary")),
        name="mlstm_prompt",
    )(xm, sm, smt, conv_w, conv_b.reshape(1, M_WIDTH), wq, wk, bg_row, bg_col, norm_g.reshape(1, M_WIDTH))


def _mlstm_sample_body(xm_ref, sm_ref, smt_ref, tail_ref, c_in, n_in, m_in, cw_ref, cb_ref, wq_ref, wk_ref,
                       bgr_ref, bgc_ref, ng_ref, *out_refs, n_valid):
    ha_ref, c_stack, n_out, m_out = out_refs[-4:]
    c_out = c_stack.at[0]
    L = SEG * SEQ_PER_STEP
    u = xm_ref[:, 0:M_WIDTH]
    cact = _conv_silu(u, tail_ref[...], cw_ref[...], cb_ref[...], SEG)
    m_rows = m_in[...]
    a_col, g_col, mt_col, a_row = _gate_columns(sm_ref[...], smt_ref[...], bgr_ref[...], bgc_ref[...], m_rows, SEG,
                                                n_valid)
    inter_col = jnp.exp(m_rows - g_col)
    floor_col = jnp.exp(-mt_col)
    ti = lax.broadcasted_iota(jnp.int32, (L, L), 0)
    si = lax.broadcasted_iota(jnp.int32, (L, L), 1)
    mask = (si <= ti) & ((si // SEG) == (ti // SEG))
    scale = HD ** -0.5
    for h in range(M_HEADS):
        sl = slice(h * HD, (h + 1) * HD)
        ch = cact[:, sl].astype(bf16)
        qh = jnp.dot(ch, wq_ref[h], preferred_element_type=f32)
        kh = jnp.dot(ch, wk_ref[h], preferred_element_type=f32) * scale
        vh = xm_ref[:, M_WIDTH + h * HD:M_WIDTH + (h + 1) * HD].astype(bf16)
        qb = qh.astype(bf16)
        D = jnp.exp(jnp.where(mask, a_row[h:h + 1, :] - g_col[:, h:h + 1], NEG))
        S = lax.dot_general(qb, kh.astype(bf16), NT, preferred_element_type=f32) * D
        num = jnp.dot(S.astype(bf16), vh, preferred_element_type=f32)
        den = jnp.sum(S, axis=-1, keepdims=True)
        inter_num, inter_den = [], []
        for s in range(SEQ_PER_STEP):
            rs = slice(s * SEG, (s + 1) * SEG)
            C = c_in[0, s, h]
            nrow = n_in[s, h:h + 1, :]
            inter_num.append(jnp.dot(qb[rs], C.astype(bf16), preferred_element_type=f32))
            inter_den.append(jnp.sum(qh[rs] * nrow, axis=-1, keepdims=True))
            g_last = g_col[s * SEG + SEG - 1:s * SEG + SEG, h:h + 1]
            m_prev = m_rows[s * SEG:s * SEG + 1, h:h + 1]
            w = jnp.exp(a_col[rs, h:h + 1] - g_last)
            dh = jnp.exp(m_prev - g_last)
            kw = kh[rs] * w
            c_out[s, h] = dh * C + lax.dot_general(kw.astype(bf16), vh[rs], (((0,), (0,)), ((), ())),
                                                    preferred_element_type=f32)
            n_out[s, h:h + 1, :] = dh * nrow + jnp.sum(kw, axis=0, keepdims=True)
        ic = inter_col[:, h:h + 1]
        num = num + ic * jnp.concatenate(inter_num, axis=0)
        den = den + ic * jnp.concatenate(inter_den, axis=0)
        hh = num / jnp.maximum(jnp.abs(den), floor_col[:, h:h + 1])
        o = xm_ref[:, 2 * M_WIDTH + h * HD:2 * M_WIDTH + (h + 1) * HD]
        ha_ref[:, sl] = _head_norm_gate(hh, ng_ref[:, sl], o)
    for s in range(SEQ_PER_STEP):
        n_out[s, M_HEADS:8, :] = jnp.zeros((8 - M_HEADS, HD), f32)
    m_out[...] = mt_col


def _stacked_out(prev, depth, shape):
    extra_in = [] if prev is None else [prev]
    extra_spec = [] if prev is None else [pl.BlockSpec(memory_space=pl.ANY)]
    return extra_in, extra_spec, jax.ShapeDtypeStruct((depth,) + shape, f32)


def _mlstm_sample(xm, sm, smt, tail, c0, n0, m0, conv_w, conv_b, wq, wk, b_gates, norm_g, c_stack, *, layer, depth,
                  row0, nseq, n_valid):
    L = SEG * SEQ_PER_STEP
    b0 = row0 // L
    bg_row = jnp.zeros((1, 128), f32).at[0, :8].set(b_gates)
    bg_col = b_gates.reshape(8, 1)
    const = lambda s: pl.BlockSpec(s, lambda i: (0,) * len(s))
    extra_in, extra_spec, c_shape = _stacked_out(c_stack, depth, (nseq, M_HEADS, HD, HD))
    n_in = 14
    return pl.pallas_call(
        functools.partial(_mlstm_sample_body, n_valid=n_valid), grid=(nseq // SEQ_PER_STEP,),
        in_specs=[pl.BlockSpec((L, 3 * M_WIDTH), lambda i: (b0 + i, 0)),
                  pl.BlockSpec((L, 128), lambda i: (b0 + i, 0)),
                  pl.BlockSpec((8, L), lambda i: (0, b0 + i)),
                  pl.BlockSpec((L, M_WIDTH), lambda i: (i, 0)),
                  pl.BlockSpec((1, SEQ_PER_STEP, M_HEADS, HD, HD), lambda i: (layer, i, 0, 0, 0)),
                  pl.BlockSpec((SEQ_PER_STEP, 8, HD), lambda i: (i, 0, 0)),
                  pl.BlockSpec((L, 128), lambda i: (i, 0)),
                  const((4, M_WIDTH)), const((1, M_WIDTH)), const((M_HEADS, HD, HD)), const((M_HEADS, HD, HD)),
                  const((1, 128)), const((8, 1)), const((1, M_WIDTH))] + extra_spec,
        out_specs=[pl.BlockSpec((L, M_WIDTH), lambda i: (i, 0)),
                   pl.BlockSpec((1, SEQ_PER_STEP, M_HEADS, HD, HD), lambda i: (layer, i, 0, 0, 0)),
                   pl.BlockSpec((SEQ_PER_STEP, 8, HD), lambda i: (i, 0, 0)),
                   pl.BlockSpec((L, 128), lambda i: (i, 0))],
        out_shape=[jax.ShapeDtypeStruct((nseq * SEG, M_WIDTH), f32), c_shape,
                   jax.ShapeDtypeStruct((nseq, 8, HD), f32),
                   jax.ShapeDtypeStruct((nseq * SEG, 128), f32)],
        input_output_aliases={n_in: 1} if extra_in else {},
        compiler_params=pltpu.CompilerParams(dimension_semantics=("parallel",), vmem_limit_bytes=VMEM_LIMIT),
        name="mlstm_sample",
    )(xm, sm, smt, tail, c0, n0, m0, conv_w, conv_b.reshape(1, M_WIDTH), wq, wk, bg_row, bg_col,
      norm_g.reshape(1, M_WIDTH), *extra_in)


def _sel_overlap_t(n_chunk, n_blk):
    n_cmp = n_chunk - 1
    start = np.arange(n_cmp) * CMP_STRIDE
    bs = np.arange(n_blk) * SEL_BLOCK
    ov = np.minimum(start[:, None] + CMP_LEN, bs[None, :] + SEL_BLOCK) - np.maximum(start[:, None], bs[None, :])
    ov = np.clip(ov, 0, None) / CMP_LEN
    out = np.zeros((NBLK_PAD, n_chunk), np.float32)
    out[:n_blk, :n_cmp] = ov.T
    return jnp.asarray(out)


def _block_expand(n_keys):
    e = (np.arange(n_keys)[None, :] // SEL_BLOCK) == np.arange(128)[:, None]
    return jnp.asarray(e, dtype=bf16)


def _compress_weights(pe, w1, w2):
    W = jnp.zeros((16, 2, 64, 2, 2, CMP_HID), f32)
    for g in range(N_KV):
        W = W.at[:, g, :, g, 0, :].set(w1[:16])
        W = W.at[:, g, :, g, 1, :].set(w1[16:])
    W = W.reshape(2048, 512)
    peA = jnp.broadcast_to(pe[:16, None, :], (16, 2, 64)).reshape(1, 2048)
    peB = jnp.broadcast_to(pe[16:, None, :], (16, 2, 64)).reshape(1, 2048)
    PE = jnp.concatenate([peA, peB, jnp.zeros((14, 2048), f32)], axis=0)
    W2 = jnp.zeros((2, CMP_HID, 128), f32).at[0, :, 0:64].set(w2).at[1, :, 64:128].set(w2)
    return W.astype(bf16), PE.astype(bf16), W2.astype(bf16)


def _compress_rows(flat, w_ref, pe_ref, w2_ref):
    n = flat.shape[0]
    y = jnp.dot(jnp.concatenate([flat, pe_ref[...]], axis=0), w_ref[...], preferred_element_type=f32)
    c = y[n:n + 8]
    out = jnp.zeros((n, w2_ref.shape[2]), f32)
    for g in range(N_KV):
        a = y[0:n, (2 * g) * CMP_HID:(2 * g + 1) * CMP_HID] + c[0:1, (2 * g) * CMP_HID:(2 * g + 1) * CMP_HID]
        b = (y[0:n, (2 * g + 1) * CMP_HID:(2 * g + 2) * CMP_HID]
             + c[1:2, (2 * g + 1) * CMP_HID:(2 * g + 2) * CMP_HID])
        hid = a + pltpu.roll(b, n - 1, axis=0)
        out = out + jnp.dot(jax.nn.gelu(hid, approximate=True).astype(bf16), w2_ref[g], preferred_element_type=f32)
    return out


def _compress_prompt_body(xk_ref, xv_ref, wk_ref, pek_ref, w2k_ref, wv_ref, pev_ref, w2v_ref, kc_ref, vc_ref, *,
                          n_chunk):
    def flat(x_ref):
        return jnp.concatenate(
            [x_ref[pl.ds(l, n_chunk, stride=CMP_STRIDE), :].astype(bf16) for l in range(CMP_STRIDE)], axis=1)
    kc_ref[0] = _compress_rows(flat(xk_ref), wk_ref, pek_ref, w2k_ref).astype(bf16)
    vc_ref[0] = _compress_rows(flat(xv_ref), wv_ref, pev_ref, w2v_ref).T.astype(bf16)


def _compress_prompt(kvrow, cwk, cwv, *, nb, T):
    n_chunk = T // CMP_STRIDE
    const = lambda a: pl.BlockSpec(a.shape, lambda b: (0,) * a.ndim)
    return pl.pallas_call(
        functools.partial(_compress_prompt_body, n_chunk=n_chunk), grid=(nb,),
        in_specs=[pl.BlockSpec((T, 128), lambda b: (b, 0)), pl.BlockSpec((T, 128), lambda b: (b, 1))]
        + [const(a) for a in (*cwk, *cwv)],
        out_specs=[pl.BlockSpec((1, n_chunk, 128), lambda b: (b, 0, 0)),
                   pl.BlockSpec((1, 128, n_chunk), lambda b: (b, 0, 0))],
        out_shape=[jax.ShapeDtypeStruct((nb, n_chunk, 128), bf16), jax.ShapeDtypeStruct((nb, 128, n_chunk), bf16)],
        compiler_params=pltpu.CompilerParams(dimension_semantics=("parallel",), vmem_limit_bytes=VMEM_LIMIT),
        name="nsa_compress",
    )(kvrow, kvrow, *cwk, *cwv)


def _softmax_rows(s, mask):
    s = jnp.where(mask, s, NEG)
    e = jnp.exp2(s - jnp.max(s, axis=-1, keepdims=True))
    e = jnp.where(mask, e, 0.0)
    return e / jnp.maximum(jnp.sum(e, axis=-1, keepdims=True), TINY)


def _select_blocks(imp_t, qpos_row, n_blk):
    n = imp_t.shape[1]
    j = lax.broadcasted_iota(jnp.int32, imp_t.shape, 0)
    cur = qpos_row // SEL_BLOCK
    forced = (j == 0) | (j == cur) | (j == cur - 1)
    score = jnp.where(j <= cur, jnp.where(forced, FORCE_SCORE, imp_t), -1.0)
    n_tiles = -(-n_blk // 8)
    sub = lax.broadcasted_iota(jnp.int32, (8, n), 0)
    tiles = [score[8 * v:8 * v + 8] for v in range(n_tiles)]
    if n_blk % 8:
        tiles[-1] = jnp.where(sub < n_blk % 8, tiles[-1], -2.0)
    cnts = [jnp.zeros((8, n), f32)] * n_tiles
    for jp in range(n_blk):
        sj = tiles[jp // 8][jp % 8:jp % 8 + 1, :]
        for v in range(n_tiles):
            if v > jp // 8:
                beat = jnp.where(sj >= tiles[v], 1.0, 0.0)
            elif v < jp // 8:
                beat = jnp.where(sj > tiles[v], 1.0, 0.0)
            else:
                beat = jnp.where(sub > jp % 8, jnp.where(sj >= tiles[v], 1.0, 0.0), jnp.where(sj > tiles[v], 1.0, 0.0))
            cnts[v] = cnts[v] + beat
    keep = [jnp.where(c < float(min(SEL_TOP, n_blk)), 1.0, 0.0) for c in cnts]
    if n_blk % 8:
        keep[-1] = jnp.where(sub < n_blk % 8, keep[-1], 0.0)
    return jnp.concatenate(keep + [jnp.zeros((NBLK_PAD - 8 * n_tiles, n), f32)] * (NBLK_PAD > 8 * n_tiles), axis=0)


def _transpose_sel(sel_t):
    return jnp.concatenate([sel_t, jnp.zeros((128 - NBLK_PAD, sel_t.shape[1]), f32)], axis=0).T


def _softmax_cols(s, mask):
    s = jnp.where(mask, s, NEG)
    e = jnp.exp2(s - jnp.max(s, axis=0, keepdims=True))
    e = jnp.where(mask, e, 0.0)
    return e * (1.0 / jnp.maximum(jnp.sum(e, axis=0, keepdims=True), TINY))


def _lane_tile(x, n):
    return jnp.concatenate([x] * n, axis=1)


def _nsa_prompt_body(qct_ref, qrt_ref, gt_ref, kk_ref, vt_ref, kc_ref, vct_ref, ovt_ref, hbt_ref, sel_scr, *, T):
    i = pl.program_id(1)
    n_chunk = T // CMP_STRIDE
    n_blk = T // SEL_BLOCK
    q0 = i * TQ
    qpos_row = q0 + lax.broadcasted_iota(jnp.int32, (1, TQ), 1)
    gsig = jax.nn.sigmoid(gt_ref[...])
    n_kt = (q0 + TQ + TK - 1) // TK
    w0 = jnp.maximum(i - WINDOW // TQ, 0) * TQ
    WK = WINDOW + TQ
    blk_per_tile = TK // SEL_BLOCK
    ones_lhs = jnp.ones((16, TK), bf16)
    zeros_q = jnp.zeros((HEAD_DIM, TQ), bf16)

    def group_queries(ref, g):
        cols = []
        for r in range(R):
            q = ref[(g * R + r) * HEAD_DIM:(g * R + r + 1) * HEAD_DIM, :]
            cols.append(jnp.concatenate([q, zeros_q] if g == 0 else [zeros_q, q], axis=0))
        return jnp.concatenate(cols, axis=1)

    qct_all = jnp.concatenate([group_queries(qct_ref, g) for g in range(N_KV)], axis=1)
    qrt_all = jnp.concatenate([group_queries(qrt_ref, g) for g in range(N_KV)], axis=1)
    s = jnp.dot(kc_ref[0], qct_all, preferred_element_type=f32)
    cmp_end = lax.broadcasted_iota(jnp.int32, (n_chunk, 1), 0) * CMP_STRIDE + (CMP_LEN - 1)
    p = _softmax_cols(s, _lane_tile(cmp_end <= qpos_row, N_KV * R))
    o_c_all = jnp.dot(vct_ref[0], p.astype(bf16), preferred_element_type=f32)
    for g in range(N_KV):
        base = g * R * TQ
        psum = (p[:, base:base + TQ] + p[:, base + TQ:base + 2 * TQ] + p[:, base + 2 * TQ:base + 3 * TQ]
                + p[:, base + 3 * TQ:base + 4 * TQ])
        imp_t = jnp.dot(ovt_ref[...], psum, preferred_element_type=f32, precision=HIGHEST)
        sel_scr[g] = jnp.where(_select_blocks(imp_t, qpos_row, n_blk) > 0.5, 0.0, NEG)

    def kv_step(kt, carry):
        k0 = pl.multiple_of(kt * TK, TK)
        b0 = pl.multiple_of(kt * blk_per_tile, blk_per_tile)
        causal = (k0 + lax.broadcasted_iota(jnp.int32, (TK, 1), 0)) <= qpos_row
        m_i, acc = carry
        sc = jnp.dot(kk_ref[pl.ds(k0, TK), 0:128], qrt_all, preferred_element_type=f32)
        biases = []
        for g in range(N_KV):
            sel8 = sel_scr[g, pl.ds(b0, blk_per_tile), :]
            bias = jnp.broadcast_to(sel8[:, None, :], (blk_per_tile, SEL_BLOCK, TQ)).reshape(TK, TQ)
            biases.append(_lane_tile(jnp.where(causal, bias, NEG), R))
        sc = sc + jnp.concatenate(biases, axis=1)
        m_new = jnp.maximum(m_i, jnp.max(sc, axis=0, keepdims=True))
        alpha = jnp.exp2(m_i - m_new)
        pp = jnp.exp2(sc - m_new).astype(bf16)
        v_ext = jnp.concatenate([vt_ref[0:128, pl.ds(k0, TK)], ones_lhs], axis=0)
        return m_new, alpha * acc + jnp.dot(v_ext, pp, preferred_element_type=f32)

    init = (jnp.full((1, N_KV * R * TQ), NEG, f32), jnp.zeros((128 + 16, N_KV * R * TQ), f32))
    _, acc_all = lax.fori_loop(0, n_kt, kv_step, init)

    o_s_all = acc_all[0:128] * (1.0 / acc_all[128:129])
    w0a = pl.multiple_of(w0, TQ)
    sw = jnp.dot(kk_ref[pl.ds(w0a, WK), 128:256], qrt_all, preferred_element_type=f32)
    dpos = qpos_row - (w0 + lax.broadcasted_iota(jnp.int32, (WK, 1), 0))
    biasw = jnp.where((dpos >= 0) & (dpos < WINDOW), 0.0, NEG)
    sw = sw + _lane_tile(biasw, N_KV * R)
    pw = jnp.exp2(sw - jnp.max(sw, axis=0, keepdims=True)).astype(bf16)
    ones_w = jnp.ones((16, WK), bf16)
    ow = jnp.dot(jnp.concatenate([vt_ref[128:256, pl.ds(w0a, WK)], ones_w], axis=0), pw, preferred_element_type=f32)
    o_w_all = ow[0:128] * (1.0 / ow[128:129])
    for h in range(N_HEADS):
        rows = slice((h // R) * HEAD_DIM, (h // R + 1) * HEAD_DIM)
        cs = slice(h * TQ, (h + 1) * TQ)
        c0 = 8 + 3 * h
        out = (gsig[c0:c0 + 1, :] * o_c_all[rows, cs] + gsig[c0 + 1:c0 + 2, :] * o_s_all[rows, cs]
               + gsig[c0 + 2:c0 + 3, :] * o_w_all[rows, cs])
        hbt_ref[h * 64:(h + 1) * 64, :] = out.astype(bf16)


def _nsa_prompt(qct, qrt, gt, kk, vt, kc, vct, *, nb, T):
    nq = T // TQ
    n_chunk = T // CMP_STRIDE
    ovt = _sel_overlap_t(n_chunk, T // SEL_BLOCK)
    col = lambda h: pl.BlockSpec((h, TQ), lambda b, i: (0, b * nq + i))
    const = lambda a: pl.BlockSpec(a.shape, lambda b, i: (0,) * a.ndim)
    return pl.pallas_call(
        functools.partial(_nsa_prompt_body, T=T), grid=(nb, nq),
        in_specs=[col(512), col(512), col(32),
                  pl.BlockSpec((T, 256), lambda b, i: (b, 0)),
                  pl.BlockSpec((256, T), lambda b, i: (0, b)),
                  pl.BlockSpec((1, n_chunk, 128), lambda b, i: (b, 0, 0)),
                  pl.BlockSpec((1, 128, n_chunk), lambda b, i: (b, 0, 0)),
                  const(ovt)],
        out_specs=col(512),
        out_shape=jax.ShapeDtypeStruct((512, nb * T), bf16),
        scratch_shapes=[pltpu.VMEM((N_KV, NBLK_PAD, TQ), f32)],
        compiler_params=pltpu.CompilerParams(dimension_semantics=("parallel", "arbitrary"),
                                             vmem_limit_bytes=VMEM_LIMIT),
        name="nsa_prompt",
    )(qct, qrt, gt, kk, vt, kc, vct, ovt)


def _nsa_sample_body(pt_ref, *refs, n_pages, ts):
    n_pg = GB * n_pages
    pages = refs[:n_pg]
    (win_ref, qc_ref, qr_ref, sm_ref, kvn_ref, wn_ref, wk_ref, pek_ref, w2k_ref, wv_ref, pev_ref, w2v_ref,
     ovt_ref, e_ref, perm_ref) = refs[n_pg:n_pg + 15]
    hb_ref, wout_stack = refs[-2:]
    wout_ref = wout_stack.at[0]
    P = n_pages * PAGE
    n_chunk = P // CMP_STRIDE
    n_blk = -(-(P + ts) // SEL_BLOCK)
    WB = win_ref.shape[4]
    chunks_per_page = PAGE // CMP_STRIDE

    qc_all = qc_ref[...].astype(f32)
    qr_all = qr_ref[...].astype(f32)
    gsig = jax.nn.sigmoid(sm_ref[...])
    rows = N_KV * R * SEG
    qpos_col = P + lax.broadcasted_iota(jnp.int32, (rows, 1), 0) % SEG
    qpos_row = P + lax.broadcasted_iota(jnp.int32, (1, 128), 1) % SEG
    new_lane = lax.broadcasted_iota(jnp.int32, (1, 128), 1)
    zpad = jnp.zeros((128 - SEG, 128), f32)

    def stack(q_all, bi):
        parts = []
        for h in range(N_HEADS):
            g = h // R
            q = q_all[bi * SEG:(bi + 1) * SEG, (h // 2) * 128:(h // 2 + 1) * 128]
            if h % 2 != g:
                q = pltpu.roll(q, HEAD_DIM, axis=1)
            parts.append(jnp.where(new_lane // HEAD_DIM == g, q, 0.0))
        return jnp.concatenate(parts, axis=0).astype(bf16)

    kwpos = jnp.concatenate([P - WB + lax.broadcasted_iota(jnp.int32, (1, WB), 1), P + new_lane], axis=1)
    validw = jnp.concatenate([jnp.full((1, WB), True), new_lane < SEG], axis=1)
    dpos = qpos_col - kwpos
    maskw = (dpos >= 0) & (dpos < WINDOW) & (kwpos >= 0) & validw
    pre = []
    for bi in range(GB):
        rs8 = slice(bi * SEG, (bi + 1) * SEG)
        ksel_t = jnp.concatenate([pages[bi * n_pages + pp][0, 0, 2].astype(bf16) for pp in range(n_pages)], axis=1)
        vsel_t = jnp.concatenate([pages[bi * n_pages + pp][0, 0, 3].astype(bf16) for pp in range(n_pages)], axis=1)
        knew = jnp.concatenate([kvn_ref[rs8, 256:384], zpad], axis=0).astype(bf16)
        vnew = jnp.concatenate([kvn_ref[rs8, 384:512], zpad], axis=0).astype(bf16)
        kwnew_f = jnp.concatenate([wn_ref[rs8, 0:128], zpad], axis=0)
        vwnew_f = jnp.concatenate([wn_ref[rs8, 128:256], zpad], axis=0)
        qr = stack(qr_all, bi)
        s_sel = jnp.concatenate([jnp.dot(qr, ksel_t, preferred_element_type=f32),
                                 lax.dot_general(qr, knew, NT, preferred_element_type=f32)], axis=1)
        s_w = jnp.concatenate([jnp.dot(qr, win_ref[0, bi, 0].astype(bf16), preferred_element_type=f32),
                               lax.dot_general(qr, kwnew_f.astype(bf16), NT, preferred_element_type=f32)], axis=1)
        p_w = _softmax_rows(s_w, maskw).astype(bf16)
        o_w = (lax.dot_general(p_w[:, 0:WB], win_ref[0, bi, 1].astype(bf16), NT, preferred_element_type=f32)
               + jnp.dot(p_w[:, WB:WB + 128], vwnew_f.astype(bf16), preferred_element_type=f32))
        pre.append((s_sel, o_w, vsel_t, vnew, kwnew_f, vwnew_f))

    xt_stack = jnp.concatenate([pages[j][0, 0, c].astype(bf16) for j in range(n_pg) for c in (0, 1)], axis=0)
    y_all = lax.dot_general(perm_ref[...], xt_stack, NT, preferred_element_type=f32)

    def flat(c):
        return jnp.concatenate(
            [jnp.concatenate([y_all[l * chunks_per_page:(l + 1) * chunks_per_page, (2 * j + c) * 128:(2 * j + c + 1) * 128]
                              for j in range(n_pg)], axis=0)
             for l in range(CMP_STRIDE)], axis=1).astype(bf16)

    kc_all = _compress_rows(flat(0), wk_ref, pek_ref, w2k_ref).astype(bf16)
    vc_all = _compress_rows(flat(1), wv_ref, pev_ref, w2v_ref).astype(bf16)

    cmp_end = lax.broadcasted_iota(jnp.int32, (1, n_chunk), 1) * CMP_STRIDE + (CMP_LEN - 1)
    o_cs, psums = [], []
    for bi in range(GB):
        s = lax.dot_general(stack(qc_all, bi), kc_all[bi * n_chunk:(bi + 1) * n_chunk], NT,
                            preferred_element_type=f32)
        p = _softmax_rows(s, cmp_end <= qpos_col)
        o_cs.append(jnp.dot(p.astype(bf16), vc_all[bi * n_chunk:(bi + 1) * n_chunk], preferred_element_type=f32))
        for g in range(N_KV):
            pg = p[g * R * SEG:(g + 1) * R * SEG]
            psums.append(pg[0:SEG] + pg[SEG:2 * SEG] + pg[2 * SEG:3 * SEG] + pg[3 * SEG:4 * SEG])
    psum_all = jnp.concatenate(psums + [jnp.zeros((128 - SEG * GB * N_KV, n_chunk), f32)], axis=0)
    imp_t = lax.dot_general(ovt_ref[...], psum_all, NT, preferred_element_type=f32, precision=HIGHEST)
    sel_all = _transpose_sel(_select_blocks(imp_t, qpos_row, n_blk)).astype(bf16)
    mk_all = jnp.dot(sel_all, e_ref[...], preferred_element_type=f32)

    out_rows = []
    for bi in range(GB):
        rs8 = slice(bi * SEG, (bi + 1) * SEG)
        s_sel, o_w, vsel_t, vnew, kwnew_f, vwnew_f = pre[bi]
        o_c = o_cs[bi]
        mk = jnp.concatenate(
            [mk_all[(bi * N_KV + g) * SEG:(bi * N_KV + g + 1) * SEG] for g in range(N_KV) for _ in range(R)], axis=0) > 0.5
        kpos = jnp.concatenate([lax.broadcasted_iota(jnp.int32, (1, P), 1), P + new_lane], axis=1)
        valid = jnp.concatenate([jnp.full((1, P), True), new_lane < SEG], axis=1)
        p_s = _softmax_rows(s_sel, mk & (kpos <= qpos_col) & valid).astype(bf16)
        o_s = (lax.dot_general(p_s[:, 0:P], vsel_t, NT, preferred_element_type=f32)
               + jnp.dot(p_s[:, P:P + 128], vnew, preferred_element_type=f32))
        gs = gsig[rs8]
        heads = [None] * N_HEADS
        for h in range(N_HEADS):
            rr = slice(h * SEG, (h + 1) * SEG)
            c0 = 8 + 3 * h
            out = gs[:, c0:c0 + 1] * o_c[rr] + gs[:, c0 + 1:c0 + 2] * o_s[rr] + gs[:, c0 + 2:c0 + 3] * o_w[rr]
            if h % 2 != h // R:
                out = pltpu.roll(out, HEAD_DIM, axis=1)
            heads[h] = out
        out_rows.append(jnp.concatenate(
            [jnp.where(new_lane < HEAD_DIM, heads[2 * j], heads[2 * j + 1]) for j in range(N_HEADS // 2)], axis=1))
        for kv, new_f in ((0, kwnew_f), (1, vwnew_f)):
            old = pltpu.roll(win_ref[0, bi, kv], WB - ts, axis=1)
            new_t = pltpu.roll(new_f.T, 128 - ts, axis=1)
            wout_ref[bi, kv, :, 0:WB - 128] = old[:, 0:WB - 128]
            wout_ref[bi, kv, :, WB - 128:WB] = jnp.where(new_lane >= 128 - ts, new_t, old[:, WB - 128:WB])
    hb_ref[...] = jnp.concatenate(out_rows, axis=0).astype(bf16)


def _nsa_sample(page_table, cache_t, cwin_t, qc, qr, sm, kvrow, winrow, cwk, cwv, win_stack, *, layer, row0, ts):
    nseq, n_pages = page_table.shape
    depth = cwin_t.shape[0]
    P = n_pages * PAGE
    WB = cwin_t.shape[4]
    n_chunk = P // CMP_STRIDE
    ovt = _sel_overlap_t(n_chunk, -(-(P + ts) // SEL_BLOCK))
    e = _block_expand(P + 128)
    r = np.arange(PAGE)
    perm = jnp.asarray(np.arange(PAGE)[None, :] == (CMP_STRIDE * (r % (PAGE // CMP_STRIDE)) + r // (PAGE // CMP_STRIDE))[:, None],
                       dtype=bf16)
    b0 = row0 // (GB * SEG)

    def page_map(i, pt, *, bi, p):
        return (layer, pt[i * GB + bi, p], 0, 0, 0)

    page_specs = [pl.BlockSpec((1, 1, 4, 128, PAGE), functools.partial(page_map, bi=bi, p=p))
                  for bi in range(GB) for p in range(n_pages)]
    row = lambda w: pl.BlockSpec((GB * SEG, w), lambda i, pt: (b0 + i, 0))
    const = lambda a: pl.BlockSpec(a.shape, lambda i, pt: (0,) * a.ndim)
    extra_in, extra_spec, win_shape = _stacked_out(win_stack, depth, (nseq, 2, 128, WB))
    args = (page_table, *([cache_t] * (GB * n_pages)), cwin_t, qc, qr, sm, kvrow, winrow, *cwk, *cwv, ovt, e, perm)
    grid_spec = pltpu.PrefetchScalarGridSpec(
        num_scalar_prefetch=1, grid=(nseq // GB,),
        in_specs=page_specs + [pl.BlockSpec((1, GB, 2, 128, WB), lambda i, pt: (layer, i, 0, 0, 0)),
                               row(N_WIDTH), row(N_WIDTH), row(128), row(512), row(256)]
        + [const(a) for a in (*cwk, *cwv, ovt, e, perm)] + extra_spec,
        out_specs=[pl.BlockSpec((GB * SEG, N_WIDTH), lambda i, pt: (i, 0)),
                   pl.BlockSpec((1, GB, 2, 128, WB), lambda i, pt: (layer, i, 0, 0, 0))])
    return pl.pallas_call(
        functools.partial(_nsa_sample_body, n_pages=n_pages, ts=ts), grid_spec=grid_spec,
        out_shape=[jax.ShapeDtypeStruct((nseq * SEG, N_WIDTH), bf16), win_shape],
        input_output_aliases={len(args): 1} if extra_in else {},
        compiler_params=pltpu.CompilerParams(dimension_semantics=("parallel",), vmem_limit_bytes=VMEM_LIMIT),
        name="nsa_sample",
    )(*args, *extra_in)


def kernel(x_prompt, x_sample, cache_kv_pages, page_table, cache_win, state_mlstm_C, state_mlstm_n, state_mlstm_m, state_mlstm_conv, norm1_g, w_in, b_gates, conv_w, conv_b, w_mq, w_mk, mlstm_norm_g, cmp_pe_k, cmp_pe_v, cmp_w1_k, cmp_w2_k, cmp_w1_v, cmp_w2_v, w_branch_a, w_branch_b, w_out, norm2_g, w_ffn_up, w_ffn_down, final_norm_g):
    Bp, Tp, _ = x_prompt.shape
    Bs, Ts, _ = x_sample.shape
    depth = w_in.shape[0]
    n_pool, page = cache_kv_pages.shape[1:3]
    n_pages = page_table.shape[1]
    past_len = n_pages * page
    wb_len = cache_win.shape[2]
    mp = Bp * Tp
    ms = Bs * SAMPLE_PAD
    assert page == PAGE and M_CONV - 1 <= Ts <= SAMPLE_PAD and Tp % MLSTM_CHUNK == 0 and Tp >= WINDOW + TQ
    assert mp % (SEG * SEQ_PER_STEP) == 0 and Bs % SEQ_PER_STEP == 0 and Bs % GB == 0

    pad_seq = lambda a: jnp.pad(a, ((0, 0), (0, SAMPLE_PAD - a.shape[1]), (0, 0)))
    xs = (x_prompt.reshape(mp, D_MODEL), pad_seq(x_sample).reshape(ms, D_MODEL))
    tabs = _rope_tables(Tp, past_len, ROW_TILE * (ms // ROW_TILE))
    w_all = _in_proj_weights(w_in)
    w_up, w_down = w_ffn_up.astype(bf16), w_ffn_down.astype(bf16)
    cache_t = jnp.transpose(cache_kv_pages, (0, 1, 3, 4, 5, 2)).reshape(depth, n_pool, 4, KV_W, PAGE)
    cwin_t = jnp.transpose(cache_win, (0, 1, 3, 4, 5, 2)).reshape(depth, Bs, 2, KV_W, wb_len)

    outs = {k: [] for k in ("p_win", "p_C", "p_n", "p_m", "p_conv", "s_kv", "s_n", "s_m", "s_conv")}
    c_stack = win_stack = kvt_stack = None
    y_parts = None
    for l in range(depth):
        w_t = jnp.concatenate([w_in[l][:, O_IG:O_IG + 8], w_in[l][:, O_NG:O_NG + 24]], axis=1).T.astype(bf16)
        xm, sm, smt, qc, qr, qct, qrt, kvrow, winrow, kk, vt, gates, kvt_stack = _in_proj(
            xs, norm1_g[l].reshape(1, D_MODEL), w_all, w_t, tabs, Tp, kvt_stack, layer=l, depth=depth)

        wq, wk = w_mq[l].astype(bf16), w_mk[l].astype(bf16)
        ha_p, c_p, n_p, m_p, cv_p = _mlstm_prompt(xm, sm, smt, conv_w[l], conv_b[l], wq, wk, b_gates[l], mlstm_norm_g[l],
                                                   nb=Bp, T=Tp, L=MLSTM_CHUNK)
        tail = jnp.pad(state_mlstm_conv[l], ((0, 0), (0, SEG - (M_CONV - 1)), (0, 0))).reshape(ms, M_WIDTH)
        n0 = jnp.pad(state_mlstm_n[l], ((0, 0), (0, 8 - M_HEADS), (0, 0)))
        m0 = jnp.broadcast_to(jnp.pad(state_mlstm_m[l], ((0, 0), (0, 128 - M_HEADS)))[:, None, :], (Bs, SEG, 128))
        ha_s, c_stack, n_s, mt_s = _mlstm_sample(
            xm, sm, smt, tail, state_mlstm_C, n0, m0.reshape(ms, 128), conv_w[l], conv_b[l], wq, wk, b_gates[l],
            mlstm_norm_g[l], c_stack, layer=l, depth=depth, row0=mp, nseq=Bs, n_valid=Ts)

        cwk = _compress_weights(cmp_pe_k[l], cmp_w1_k[l], cmp_w2_k[l])
        cwv = _compress_weights(cmp_pe_v[l], cmp_w1_v[l], cmp_w2_v[l])
        kc, vct = _compress_prompt(kvrow, cwk, cwv, nb=Bp, T=Tp)
        hbt_p = _nsa_prompt(qct, qrt, smt, kk, vt, kc, vct, nb=Bp, T=Tp)
        hb_s, win_stack = _nsa_sample(page_table, cache_t, cwin_t, qc, qr, sm, kvrow, winrow, cwk, cwv, win_stack,
                                      layer=l, row0=mp, ts=Ts)

        x_all, *y_parts = _mix_ffn(
            xs, (ha_p, ha_s), (hbt_p, hb_s.T), gates, w_branch_a[l].astype(bf16), w_branch_b[l].astype(bf16),
            w_out[l].astype(bf16), norm2_g[l].reshape(1, D_MODEL), w_up, w_down, final_norm_g.reshape(1, D_MODEL),
            layer=l, n_prompt_rows=mp, final=l == depth - 1)
        xs = (x_all,)

        sample_rows = lambda a: a[mp:].reshape(Bs, SAMPLE_PAD, a.shape[1])
        wp = min(WINDOW, Tp)
        outs["p_win"].append(winrow[:mp].reshape(Bp, Tp, 2 * KV_W)[:, -wp:].reshape(Bp, wp, 2, N_KV, HEAD_DIM))
        outs["p_C"].append(c_p)
        outs["p_n"].append(n_p[:, :M_HEADS])
        outs["p_m"].append(m_p[:, 0, :M_HEADS])
        outs["p_conv"].append(cv_p[:, 8 - (M_CONV - 1):])
        outs["s_kv"].append(sample_rows(kvrow)[:, :Ts].reshape(Bs, Ts, 4, N_KV, HEAD_DIM))
        outs["s_n"].append(n_s[:, :M_HEADS])
        outs["s_m"].append(mt_s.reshape(Bs, SEG, 128)[:, SEG - 1, :M_HEADS])
        outs["s_conv"].append(sample_rows(xm)[:, Ts - (M_CONV - 1):Ts, :M_WIDTH])

    s_win = jnp.transpose(win_stack.reshape(depth, Bs, 2, N_KV, HEAD_DIM, wb_len), (0, 1, 5, 2, 3, 4))
    y_prompt = y_parts[0].reshape(Bp, Tp, D_MODEL)
    y_sample = y_parts[1].reshape(Bs, SAMPLE_PAD, D_MODEL)[:, :Ts]
    st = lambda k: jnp.stack(outs[k])
    p_kv = jnp.transpose(kvt_stack.reshape(depth, Bp, 4, N_KV, HEAD_DIM, Tp), (0, 1, 5, 2, 3, 4))
    return (y_prompt, y_sample, p_kv, st("p_win"), st("p_C"), st("p_n"), st("p_m"), st("p_conv"),
            st("s_kv"), s_win, c_stack, st("s_n"), st("s_m"), st("s_conv"))
```

```python
import functools

import jax
import jax.numpy as jnp
import numpy as np
from jax import lax
from jax.experimental import pallas as pl
from jax.experimental.pallas import tpu as pltpu

f32 = jnp.float32
bf16 = jnp.bfloat16

D_MODEL = 1024
M_HEADS = 4
M_WIDTH = 512
M_HEAD_DIM = 128
M_CONV = 4
MLSTM_CHUNK = 256
HEAD_DIM = 64
N_WIDTH = 512
N_HEADS = 8
N_KV = 2
KV_W = 128
CMP_STRIDE = 16
CMP_LEN = 32
SEL_BLOCK = 64
SEL_TOP = 16
WINDOW = 512
ROT_DIM = 16
ROPE_THETA = 500000.0
D_FF = 2816
EPS = 1e-6
NEG = -1e30
TINY = 1e-30
FORCE_SCORE = 1e9

SAMPLE_PAD = 8
SEG = SAMPLE_PAD
SEQ_PER_STEP = 16
GB = 2
PAGE = 128
HD = M_HEAD_DIM
R = N_HEADS // N_KV
CMP_HID = 128
TQ = 128
TK = 1024
KV_SUB = 1
NBLK_PAD = 64
HIGHEST = lax.Precision.HIGHEST
NT = (((1,), (1,)), ((), ()))
LOG2E = 1.4426950408889634
ROW_TILE = 256
FF_CHUNK = 2816
VMEM_LIMIT = 56 * 1024 * 1024

C_XM, C_Q, C_KV, C_WIN, C_GATE, C_SM = 0, 1536, 2048, 2560, 2816, 4864
C_TOTAL = 4992
O_U, O_V, O_O, O_IG, O_FG, O_Q, O_KV, O_NG, O_GA, O_GB = 0, 512, 1024, 1536, 1540, 1544, 2056, 2824, 2848, 3872
IN_WIDTH = 4896


def _in_proj_weights(w):
    lead = w.shape[:-1]
    parts = [w[..., O_U:O_IG], w[..., O_Q:O_Q + N_WIDTH] * (HEAD_DIM ** -0.5), w[..., O_KV:O_KV + 6 * KV_W],
             w[..., O_GA:O_GA + 2 * D_MODEL], w[..., O_IG:O_IG + 8], w[..., O_NG:O_NG + 24],
             jnp.zeros(lead + (128 - 32,), f32)]
    out = jnp.concatenate(parts, axis=-1).astype(bf16)
    assert out.shape[-1] == C_TOTAL
    return out


def _rope_tables(seq, past_len, n_sample_rows):
    half = ROT_DIM // 2
    inv = ROPE_THETA ** (-jnp.arange(half, dtype=f32) / half)
    pos = jnp.concatenate([jnp.arange(seq), past_len + (jnp.arange(n_sample_rows) % SAMPLE_PAD)]).astype(f32)
    ang = pos[:, None] * inv[None, :]
    cos8, sin8 = jnp.cos(ang), jnp.sin(ang)
    n = pos.shape[0]
    one = jnp.ones((n, 64 - ROT_DIM), f32)
    zero = jnp.zeros((n, 64 - ROT_DIM), f32)
    z8 = jnp.zeros((n, half), f32)
    cos = jnp.concatenate([cos8, cos8, one], axis=1)
    sa = jnp.concatenate([-sin8, z8, zero], axis=1)
    sb = jnp.concatenate([z8, sin8, zero], axis=1)
    tile2 = lambda a: jnp.concatenate([a, a], axis=1)
    return tile2(cos), tile2(sa), tile2(sb)


def _rms(x, g):
    return x * lax.rsqrt(jnp.mean(x * x, axis=-1, keepdims=True) + EPS) * g


def _in_proj_body(*refs, n_x, n_sample_tiles):
    x_refs = refs[:n_x]
    g_ref, w_ref, wt_ref, cos_ref, sa_ref, sb_ref = refs[n_x:n_x + 6]
    (xm_ref, sm_ref, smt_ref, qc_ref, qr_ref, qct_ref, qrt_ref, kv_ref, win_ref, kk_ref, vt_ref, gate_ref,
     kvt_stack) = refs[-13:]
    x = x_refs[0][...]
    if n_x == 2:
        x = jnp.where(pl.program_id(0) < n_sample_tiles, x_refs[1][...], x)
    hb = _rms(x, g_ref[...]).astype(bf16)
    cos, sa, sb = cos_ref[...], sa_ref[...], sb_ref[...]

    def rope(v):
        return v * cos + pltpu.roll(v, 128 - ROT_DIM // 2, axis=1) * sa + pltpu.roll(v, ROT_DIM // 2, axis=1) * sb

    def proj(c0, n):
        return jnp.dot(hb, w_ref[0, :, c0:c0 + n], preferred_element_type=f32)

    xm_ref[...] = proj(C_XM, 1536)
    sm_ref[...] = proj(C_SM, 128)
    smt_ref[...] = lax.dot_general(wt_ref[...], hb, NT, preferred_element_type=f32)
    qf = proj(C_Q, N_WIDTH) * LOG2E
    for j in range(N_HEADS // 2):
        sl = slice(j * 128, (j + 1) * 128)
        q = qf[:, sl]
        qrot = rope(q)
        qc_ref[:, sl] = q.astype(bf16)
        qr_ref[:, sl] = qrot.astype(bf16)
        qct_ref[sl, :] = q.T.astype(bf16)
        qrt_ref[sl, :] = qrot.T.astype(bf16)
    kv = proj(C_KV, 512)
    ksel = rope(kv[:, 256:384])
    kv_ref[:, 0:256] = kv[:, 0:256]
    kv_ref[:, 256:384] = ksel
    kv_ref[:, 384:512] = kv[:, 384:512]
    kvt_stack[0, 0, 0:256, :] = kv[:, 0:256].T
    kvt_stack[0, 0, 256:384, :] = ksel.T
    kvt_stack[0, 0, 384:512, :] = kv[:, 384:512].T
    win = proj(C_WIN, 256)
    kwin = rope(win[:, 0:128])
    win_ref[:, 0:128] = kwin
    win_ref[:, 128:256] = win[:, 128:256]
    kk_ref[:, 0:128] = ksel.astype(bf16)
    kk_ref[:, 128:256] = kwin.astype(bf16)
    vt_ref[0:128, :] = kv[:, 384:512].T.astype(bf16)
    vt_ref[128:256, :] = win[:, 128:256].T.astype(bf16)
    gate_ref[...] = proj(C_GATE, 2048)


def _in_proj(xs, norm_g, w_all, w_t, tabs, seq, kvt_stack, *, layer, depth):
    m = sum(x.shape[0] for x in xs)
    n_prompt_tiles_per_seq = seq // ROW_TILE
    n_prompt_tiles = (m - (tabs[0].shape[0] - seq)) // ROW_TILE
    n_seq = n_prompt_tiles // n_prompt_tiles_per_seq
    n_tiles = m // ROW_TILE
    n_sample_tiles = n_tiles - n_prompt_tiles

    def tile(i):
        return (i + n_prompt_tiles) % n_tiles

    def kvt_map(i):
        j = jnp.maximum(i - n_sample_tiles, 0)
        return (layer, j // n_prompt_tiles_per_seq, 0, j % n_prompt_tiles_per_seq)

    def tab_map(i):
        t = tile(i)
        return (jnp.where(t < n_prompt_tiles, t % n_prompt_tiles_per_seq, n_prompt_tiles_per_seq + t - n_prompt_tiles), 0)

    row = lambda w: pl.BlockSpec((ROW_TILE, w), lambda i: (tile(i), 0))
    const = lambda s: pl.BlockSpec(s, lambda i: (0, 0), pipeline_mode=pl.Buffered(1))
    tab = pl.BlockSpec((ROW_TILE, 128), tab_map)
    widths = (1536, 128, -32, 512, 512, -512, -512, 512, 256, 256, -256, 2048)
    dts = (f32, f32, f32, bf16, bf16, bf16, bf16, f32, f32, bf16, bf16, f32)
    out_shape = [jax.ShapeDtypeStruct((-w, m) if w < 0 else (m, w), d) for w, d in zip(widths, dts)]
    out_specs = [pl.BlockSpec((-w, ROW_TILE), lambda i: (0, tile(i))) if w < 0 else row(w) for w in widths]
    extra_in, extra_spec, kvt_shape = _stacked_out(kvt_stack, depth, (n_seq, 4 * KV_W, seq))
    if len(xs) == 1:
        x_specs = [row(D_MODEL)]
    else:
        x_specs = [pl.BlockSpec((ROW_TILE, D_MODEL), lambda i: (jnp.maximum(i - n_sample_tiles, 0), 0)),
                   pl.BlockSpec((ROW_TILE, D_MODEL), lambda i: (jnp.minimum(i, n_sample_tiles - 1), 0))]
    args = (*xs, norm_g, w_all, w_t, *tabs)
    return pl.pallas_call(
        functools.partial(_in_proj_body, n_x=len(xs), n_sample_tiles=n_sample_tiles), grid=(n_tiles,),
        in_specs=x_specs + [const((1, D_MODEL)),
                            pl.BlockSpec((1, D_MODEL, C_TOTAL), lambda i: (layer, 0, 0), pipeline_mode=pl.Buffered(1)),
                            const((32, D_MODEL)), tab, tab, tab]
        + extra_spec,
        out_specs=out_specs + [pl.BlockSpec((1, 1, 4 * KV_W, ROW_TILE), kvt_map)],
        out_shape=out_shape + [kvt_shape], name="in_proj",
        input_output_aliases={len(args): len(out_shape)} if extra_in else {},
        compiler_params=pltpu.CompilerParams(dimension_semantics=("arbitrary",), vmem_limit_bytes=VMEM_LIMIT),
    )(*args, *extra_in)


def _mix_ffn_body(*refs, n_x, n_prompt_tiles, final):
    x_refs = refs[:n_x]
    hap_ref, has_ref, hbp_ref, hbs_ref, gate_ref, wa_ref, wb_ref, wo_ref, g2_ref, wup_ref, wdn_ref, gf_ref = (
        refs[n_x:n_x + 12])
    xo_ref, *y_refs = refs[n_x + 12:]
    in_prompt = pl.program_id(0) < n_prompt_tiles
    pick = lambda p_ref, s_ref: jnp.where(in_prompt, p_ref[...], s_ref[...])
    x = x_refs[0][...] if n_x == 1 else pick(*x_refs)
    a = jnp.dot(pick(hap_ref, has_ref).astype(bf16), wa_ref[...], preferred_element_type=f32)
    b = lax.dot_general(pick(hbp_ref, hbs_ref), wb_ref[...], (((0,), (0,)), ((), ())),
                        preferred_element_type=f32)
    merged = jax.nn.sigmoid(gate_ref[:, 0:D_MODEL]) * a + jax.nn.sigmoid(gate_ref[:, D_MODEL:2 * D_MODEL]) * b
    x1 = x + jnp.dot(merged.astype(bf16), wo_ref[...], preferred_element_type=f32)
    hn = _rms(x1, g2_ref[...]).astype(bf16)
    acc = x1
    for c in range(D_FF // FF_CHUNK):
        lo = c * FF_CHUNK
        g = jnp.dot(hn, wup_ref[0, :, lo:lo + FF_CHUNK], preferred_element_type=f32)
        u = jnp.dot(hn, wup_ref[0, :, D_FF + lo:D_FF + lo + FF_CHUNK], preferred_element_type=f32)
        act = (g * jax.nn.sigmoid(g) * u).astype(bf16)
        acc = acc + jnp.dot(act, wdn_ref[0, lo:lo + FF_CHUNK, :], preferred_element_type=f32)
    xo_ref[...] = acc
    if final:
        y = _rms(acc, gf_ref[...])
        yp_ref, ys_ref = y_refs

        @pl.when(in_prompt)
        def _():
            yp_ref[...] = y

        @pl.when(jnp.logical_not(in_prompt))
        def _():
            ys_ref[...] = y


def _mix_ffn(xs, h_a, h_bt, gates, wa, wb, wo, g2, wup, wdn, gf, *, layer, n_prompt_rows, final):
    m = gates.shape[0]
    npt = n_prompt_rows // ROW_TILE
    row = lambda w: pl.BlockSpec((ROW_TILE, w), lambda i: (i, 0))
    p_row = lambda w: pl.BlockSpec((ROW_TILE, w), lambda i: (jnp.minimum(i, npt - 1), 0))
    s_row = lambda w: pl.BlockSpec((ROW_TILE, w), lambda i: (jnp.maximum(i - npt, 0), 0))
    p_col = lambda h: pl.BlockSpec((h, ROW_TILE), lambda i: (0, jnp.minimum(i, npt - 1)))
    s_col = lambda h: pl.BlockSpec((h, ROW_TILE), lambda i: (0, jnp.maximum(i - npt, 0)))
    const = lambda s: pl.BlockSpec(s, lambda i: (0, 0), pipeline_mode=pl.Buffered(1))
    layer_w = lambda a: pl.BlockSpec((1,) + a.shape[1:], lambda i: (layer, 0, 0), pipeline_mode=pl.Buffered(1))
    out_specs = [row(D_MODEL)]
    out_shape = [jax.ShapeDtypeStruct((m, D_MODEL), f32)]
    if final:
        out_specs += [p_row(D_MODEL), s_row(D_MODEL)]
        out_shape += [jax.ShapeDtypeStruct((n_prompt_rows, D_MODEL), f32),
                      jax.ShapeDtypeStruct((m - n_prompt_rows, D_MODEL), f32)]
    x_specs = [row(D_MODEL)] if len(xs) == 1 else [p_row(D_MODEL), s_row(D_MODEL)]
    return pl.pallas_call(
        functools.partial(_mix_ffn_body, n_x=len(xs), n_prompt_tiles=npt, final=final), grid=(m // ROW_TILE,),
        in_specs=x_specs + [p_row(M_WIDTH), s_row(M_WIDTH), p_col(N_WIDTH), s_col(N_WIDTH), row(2 * D_MODEL),
                            const(wa.shape), const(wb.shape), const(wo.shape), const((1, D_MODEL)),
                            layer_w(wup), layer_w(wdn), const((1, D_MODEL))],
        out_specs=out_specs, out_shape=out_shape, name="mix_ffn",
        compiler_params=pltpu.CompilerParams(dimension_semantics=("arbitrary",), vmem_limit_bytes=VMEM_LIMIT),
    )(*xs, *h_a, *h_bt, gates, wa, wb, wo, g2, wup, wdn, gf)


def _log_sigmoid(x):
    return jnp.minimum(x, 0.0) - jnp.log1p(jnp.exp(-jnp.abs(x)))


def _seg_scan(x, axis, seg, op, fill):
    idx = lax.broadcasted_iota(jnp.int32, x.shape, axis) % seg
    d = 1
    while d < seg:
        x = op(x, jnp.where(idx >= d, pltpu.roll(x, d, axis=axis), fill))
        d *= 2
    return x


def _conv_silu(u, tail, conv_w, conv_b, seg):
    L = u.shape[0]
    row = lax.broadcasted_iota(jnp.int32, u.shape, 0) % seg
    y = conv_b + u * conv_w[3:4, :]
    for k in (1, 2, 3):
        prev = tail if k == 3 else pltpu.roll(tail, L - (3 - k), axis=0)
        sh = jnp.where(row < k, prev, pltpu.roll(u, k, axis=0))
        y = y + sh * conv_w[3 - k:4 - k, :]
    return y * jax.nn.sigmoid(y)


def _gate_columns(sm, smt, bg_row, bg_col, m_vec, seg, n_valid):
    G = sm + bg_row
    lf = _log_sigmoid(G)
    ipre = G
    if n_valid < seg:
        rowc = lax.broadcasted_iota(jnp.int32, G.shape, 0) % seg
        lf = jnp.where(rowc < n_valid, lf, 0.0)
        ipre = jnp.where(rowc < n_valid, G, NEG)
    F = _seg_scan(lf, 0, seg, jnp.add, 0.0)
    F_al = pltpu.roll(F, 128 - M_HEADS, axis=1)
    a_col = ipre - F_al
    g_col = jnp.maximum(m_vec, _seg_scan(a_col, 0, seg, jnp.maximum, NEG))
    mt_col = F_al + g_col
    Gr = smt + bg_col
    lfr = _log_sigmoid(Gr)
    ir = Gr
    if n_valid < seg:
        lanec = lax.broadcasted_iota(jnp.int32, Gr.shape, 1) % seg
        lfr = jnp.where(lanec < n_valid, lfr, 0.0)
        ir = jnp.where(lanec < n_valid, Gr, NEG)
    Fr = _seg_scan(lfr, 1, seg, jnp.add, 0.0)
    a_row = ir - pltpu.roll(Fr, M_HEADS, axis=0)
    return a_col, g_col, mt_col, a_row


def _head_norm_gate(hh, ng, o):
    return hh * lax.rsqrt(jnp.mean(hh * hh, axis=-1, keepdims=True) + EPS) * ng * jax.nn.sigmoid(o)


def _mlstm_prompt_body(xm_ref, sm_ref, smt_ref, cw_ref, cb_ref, wq_ref, wk_ref, bgr_ref, bgc_ref, ng_ref,
                       ha_ref, c_out, n_out, m_out, conv_out, c_scr, m_scr, tail_scr, *, L):
    c_idx = pl.program_id(1)

    @pl.when(c_idx == 0)
    def _():
        c_scr[...] = jnp.zeros_like(c_scr)
        m_scr[...] = jnp.zeros_like(m_scr)
        tail_scr[...] = jnp.zeros_like(tail_scr)

    u = xm_ref[:, 0:M_WIDTH]
    tail = jnp.concatenate([tail_scr[...], jnp.zeros((L - 8, M_WIDTH), f32)], axis=0)
    cact = _conv_silu(u, tail, cw_ref[...], cb_ref[...], L)
    tail_scr[...] = pltpu.roll(u[L - 8:L, :], 3, axis=0)
    conv_out[0] = u[L - 8:L, :]

    m_vec = m_scr[0:1, :]
    a_col, g_col, mt_col, a_row = _gate_columns(sm_ref[...], smt_ref[...], bgr_ref[...], bgc_ref[...], m_vec, L, L)
    inter_col = jnp.exp(m_vec - g_col)
    floor_col = jnp.exp(-mt_col)
    g_last = g_col[L - 1:L, :]
    w_col = jnp.exp(a_col - g_last)
    decay = jnp.exp(m_vec - g_last)
    m_new = mt_col[L - 1:L, :]
    ti = lax.broadcasted_iota(jnp.int32, (L, L), 0)
    si = lax.broadcasted_iota(jnp.int32, (L, L), 1)
    causal = si <= ti
    scale = HD ** -0.5
    ones_v = jnp.ones((L, HD), bf16)
    for h in range(M_HEADS):
        sl = slice(h * HD, (h + 1) * HD)
        ch = cact[:, sl].astype(bf16)
        qh = jnp.dot(ch, wq_ref[h], preferred_element_type=f32)
        kh = jnp.dot(ch, wk_ref[h], preferred_element_type=f32) * scale
        vh = jnp.concatenate([xm_ref[:, M_WIDTH + h * HD:M_WIDTH + (h + 1) * HD].astype(bf16), ones_v], axis=1)
        qb = qh.astype(bf16)
        D = jnp.exp(jnp.where(causal, a_row[h:h + 1, :] - g_col[:, h:h + 1], NEG))
        S = lax.dot_general(qb, kh.astype(bf16), NT, preferred_element_type=f32) * D
        C = c_scr[h]
        ic = inter_col[:, h:h + 1]
        both = jnp.dot(S.astype(bf16), vh, preferred_element_type=f32) + ic * jnp.dot(
            qb, C.astype(bf16), preferred_element_type=f32)
        hh = both[:, 0:HD] / jnp.maximum(jnp.abs(both[:, HD:2 * HD]), floor_col[:, h:h + 1])
        o = xm_ref[:, 2 * M_WIDTH + h * HD:2 * M_WIDTH + (h + 1) * HD]
        ha_ref[:, sl] = _head_norm_gate(hh, ng_ref[:, sl], o)
        kw = kh * w_col[:, h:h + 1]
        c_scr[h] = decay[:, h:h + 1] * C + lax.dot_general(kw.astype(bf16), vh, (((0,), (0,)), ((), ())),
                                                           preferred_element_type=f32)
    m_scr[0:1, :] = m_new

    @pl.when(c_idx == pl.num_programs(1) - 1)
    def _():
        c_out[0] = c_scr[:, :, 0:HD]
        for h in range(M_HEADS):
            n_out[0, h:h + 1, :] = c_scr[h][:, HD:2 * HD].T[0:1, :]
        n_out[0, M_HEADS:8, :] = jnp.zeros((8 - M_HEADS, HD), f32)
        m_out[0] = m_scr[...]


def _mlstm_prompt(xm, sm, smt, conv_w, conv_b, wq, wk, b_gates, norm_g, *, nb, T, L):
    nc = T // L
    bg_row = jnp.zeros((1, 128), f32).at[0, :8].set(b_gates)
    bg_col = b_gates.reshape(8, 1)
    const = lambda s: pl.BlockSpec(s, lambda b, c: (0,) * len(s))
    return pl.pallas_call(
        functools.partial(_mlstm_prompt_body, L=L), grid=(nb, nc),
        in_specs=[pl.BlockSpec((L, 3 * M_WIDTH), lambda b, c: (b * nc + c, 0)),
                  pl.BlockSpec((L, 128), lambda b, c: (b * nc + c, 0)),
                  pl.BlockSpec((8, L), lambda b, c: (0, b * nc + c)),
                  const((4, M_WIDTH)), const((1, M_WIDTH)), const((M_HEADS, HD, HD)), const((M_HEADS, HD, HD)),
                  const((1, 128)), const((8, 1)), const((1, M_WIDTH))],
        out_specs=[pl.BlockSpec((L, M_WIDTH), lambda b, c: (b * nc + c, 0)),
                   pl.BlockSpec((1, M_HEADS, HD, HD), lambda b, c: (b, 0, 0, 0)),
                   pl.BlockSpec((1, 8, HD), lambda b, c: (b, 0, 0)),
                   pl.BlockSpec((1, 8, 128), lambda b, c: (b, 0, 0)),
                   pl.BlockSpec((1, 8, M_WIDTH), lambda b, c: (b, 0, 0))],
        out_shape=[jax.ShapeDtypeStruct((nb * T, M_WIDTH), f32),
                   jax.ShapeDtypeStruct((nb, M_HEADS, HD, HD), f32),
                   jax.ShapeDtypeStruct((nb, 8, HD), f32),
                   jax.ShapeDtypeStruct((nb, 8, 128), f32),
                   jax.ShapeDtypeStruct((nb, 8, M_WIDTH), f32)],
        scratch_shapes=[pltpu.VMEM((M_HEADS, HD, 2 * HD), f32), pltpu.VMEM((8, 128), f32),
                        pltpu.VMEM((8, M_WIDTH), f32)],
        compiler_params=pltpu.CompilerParams(dimension_semantics=("parallel", "arbitrary")),
        name="mlstm_prompt",
    )(xm, sm, smt, conv_w, conv_b.reshape(1, M_WIDTH), wq, wk, bg_row, bg_col, norm_g.reshape(1, M_WIDTH))


def _mlstm_sample_body(xm_ref, sm_ref, smt_ref, tail_ref, c_in, n_in, m_in, cw_ref, cb_ref, wq_ref, wk_ref,
                       bgr_ref, bgc_ref, ng_ref, *out_refs, n_valid):
    ha_ref, c_stack, n_out, m_out = out_refs[-4:]
    c_out = c_stack.at[0]
    L = SEG * SEQ_PER_STEP
    u = xm_ref[:, 0:M_WIDTH]
    cact = _conv_silu(u, tail_ref[...], cw_ref[...], cb_ref[...], SEG)
    m_rows = m_in[...]
    a_col, g_col, mt_col, a_row = _gate_columns(sm_ref[...], smt_ref[...], bgr_ref[...], bgc_ref[...], m_rows, SEG,
                                                n_valid)
    inter_col = jnp.exp(m_rows - g_col)
    floor_col = jnp.exp(-mt_col)
    ti = lax.broadcasted_iota(jnp.int32, (L, L), 0)
    si = lax.broadcasted_iota(jnp.int32, (L, L), 1)
    mask = (si <= ti) & ((si // SEG) == (ti // SEG))
    scale = HD ** -0.5
    for h in range(M_HEADS):
        sl = slice(h * HD, (h + 1) * HD)
        ch = cact[:, sl].astype(bf16)
        qh = jnp.dot(ch, wq_ref[h], preferred_element_type=f32)
        kh = jnp.dot(ch, wk_ref[h], preferred_element_type=f32) * scale
        vh = xm_ref[:, M_WIDTH + h * HD:M_WIDTH + (h + 1) * HD].astype(bf16)
        qb = qh.astype(bf16)
        D = jnp.exp(jnp.where(mask, a_row[h:h + 1, :] - g_col[:, h:h + 1], NEG))
        S = lax.dot_general(qb, kh.astype(bf16), NT, preferred_element_type=f32) * D
        num = jnp.dot(S.astype(bf16), vh, preferred_element_type=f32)
        den = jnp.sum(S, axis=-1, keepdims=True)
        inter_num, inter_den = [], []
        for s in range(SEQ_PER_STEP):
            rs = slice(s * SEG, (s + 1) * SEG)
            C = c_in[0, s, h]
            nrow = n_in[s, h:h + 1, :]
            inter_num.append(jnp.dot(qb[rs], C.astype(bf16), preferred_element_type=f32))
            inter_den.append(jnp.sum(qh[rs] * nrow, axis=-1, keepdims=True))
            g_last = g_col[s * SEG + SEG - 1:s * SEG + SEG, h:h + 1]
            m_prev = m_rows[s * SEG:s * SEG + 1, h:h + 1]
            w = jnp.exp(a_col[rs, h:h + 1] - g_last)
            dh = jnp.exp(m_prev - g_last)
            kw = kh[rs] * w
            c_out[s, h] = dh * C + lax.dot_general(kw.astype(bf16), vh[rs], (((0,), (0,)), ((), ())),
                                                    preferred_element_type=f32)
            n_out[s, h:h + 1, :] = dh * nrow + jnp.sum(kw, axis=0, keepdims=True)
        ic = inter_col[:, h:h + 1]
        num = num + ic * jnp.concatenate(inter_num, axis=0)
        den = den + ic * jnp.concatenate(inter_den, axis=0)
        hh = num / jnp.maximum(jnp.abs(den), floor_col[:, h:h + 1])
        o = xm_ref[:, 2 * M_WIDTH + h * HD:2 * M_WIDTH + (h + 1) * HD]
        ha_ref[:, sl] = _head_norm_gate(hh, ng_ref[:, sl], o)
    for s in range(SEQ_PER_STEP):
        n_out[s, M_HEADS:8, :] = jnp.zeros((8 - M_HEADS, HD), f32)
    m_out[...] = mt_col


def _stacked_out(prev, depth, shape):
    extra_in = [] if prev is None else [prev]
    extra_spec = [] if prev is None else [pl.BlockSpec(memory_space=pl.ANY)]
    return extra_in, extra_spec, jax.ShapeDtypeStruct((depth,) + shape, f32)


def _mlstm_sample(xm, sm, smt, tail, c0, n0, m0, conv_w, conv_b, wq, wk, b_gates, norm_g, c_stack, *, layer, depth,
                  row0, nseq, n_valid):
    L = SEG * SEQ_PER_STEP
    b0 = row0 // L
    bg_row = jnp.zeros((1, 128), f32).at[0, :8].set(b_gates)
    bg_col = b_gates.reshape(8, 1)
    const = lambda s: pl.BlockSpec(s, lambda i: (0,) * len(s))
    extra_in, extra_spec, c_shape = _stacked_out(c_stack, depth, (nseq, M_HEADS, HD, HD))
    n_in = 14
    return pl.pallas_call(
        functools.partial(_mlstm_sample_body, n_valid=n_valid), grid=(nseq // SEQ_PER_STEP,),
        in_specs=[pl.BlockSpec((L, 3 * M_WIDTH), lambda i: (b0 + i, 0)),
                  pl.BlockSpec((L, 128), lambda i: (b0 + i, 0)),
                  pl.BlockSpec((8, L), lambda i: (0, b0 + i)),
                  pl.BlockSpec((L, M_WIDTH), lambda i: (i, 0)),
                  pl.BlockSpec((1, SEQ_PER_STEP, M_HEADS, HD, HD), lambda i: (layer, i, 0, 0, 0)),
                  pl.BlockSpec((SEQ_PER_STEP, 8, HD), lambda i: (i, 0, 0)),
                  pl.BlockSpec((L, 128), lambda i: (i, 0)),
                  const((4, M_WIDTH)), const((1, M_WIDTH)), const((M_HEADS, HD, HD)), const((M_HEADS, HD, HD)),
                  const((1, 128)), const((8, 1)), const((1, M_WIDTH))] + extra_spec,
        out_specs=[pl.BlockSpec((L, M_WIDTH), lambda i: (i, 0)),
                   pl.BlockSpec((1, SEQ_PER_STEP, M_HEADS, HD, HD), lambda i: (layer, i, 0, 0, 0)),
                   pl.BlockSpec((SEQ_PER_STEP, 8, HD), lambda i: (i, 0, 0)),
                   pl.BlockSpec((L, 128), lambda i: (i, 0))],
        out_shape=[jax.ShapeDtypeStruct((nseq * SEG, M_WIDTH), f32), c_shape,
                   jax.ShapeDtypeStruct((nseq, 8, HD), f32),
                   jax.ShapeDtypeStruct((nseq * SEG, 128), f32)],
        input_output_aliases={n_in: 1} if extra_in else {},
        compiler_params=pltpu.CompilerParams(dimension_semantics=("parallel",), vmem_limit_bytes=VMEM_LIMIT),
        name="mlstm_sample",
    )(xm, sm, smt, tail, c0, n0, m0, conv_w, conv_b.reshape(1, M_WIDTH), wq, wk, bg_row, bg_col,
      norm_g.reshape(1, M_WIDTH), *extra_in)


def _sel_overlap_t(n_chunk, n_blk):
    n_cmp = n_chunk - 1
    start = np.arange(n_cmp) * CMP_STRIDE
    bs = np.arange(n_blk) * SEL_BLOCK
    ov = np.minimum(start[:, None] + CMP_LEN, bs[None, :] + SEL_BLOCK) - np.maximum(start[:, None], bs[None, :])
    ov = np.clip(ov, 0, None) / CMP_LEN
    out = np.zeros((NBLK_PAD, n_chunk), np.float32)
    out[:n_blk, :n_cmp] = ov.T
    return jnp.asarray(out)


def _block_expand(n_keys):
    e = (np.arange(n_keys)[None, :] // SEL_BLOCK) == np.arange(128)[:, None]
    return jnp.asarray(e, dtype=bf16)


def _compress_weights(pe, w1, w2):
    W = jnp.zeros((16, 2, 64, 2, 2, CMP_HID), f32)
    for g in range(N_KV):
        W = W.at[:, g, :, g, 0, :].set(w1[:16])
        W = W.at[:, g, :, g, 1, :].set(w1[16:])
    W = W.reshape(2048, 512)
    peA = jnp.broadcast_to(pe[:16, None, :], (16, 2, 64)).reshape(1, 2048)
    peB = jnp.broadcast_to(pe[16:, None, :], (16, 2, 64)).reshape(1, 2048)
    PE = jnp.concatenate([peA, peB, jnp.zeros((14, 2048), f32)], axis=0)
    W2 = jnp.zeros((2, CMP_HID, 128), f32).at[0, :, 0:64].set(w2).at[1, :, 64:128].set(w2)
    return W.astype(bf16), PE.astype(bf16), W2.astype(bf16)


def _compress_rows(flat, w_ref, pe_ref, w2_ref):
    n = flat.shape[0]
    y = jnp.dot(jnp.concatenate([flat, pe_ref[...]], axis=0), w_ref[...], preferred_element_type=f32)
    c = y[n:n + 8]
    out = jnp.zeros((n, w2_ref.shape[2]), f32)
    for g in range(N_KV):
        a = y[0:n, (2 * g) * CMP_HID:(2 * g + 1) * CMP_HID] + c[0:1, (2 * g) * CMP_HID:(2 * g + 1) * CMP_HID]
        b = (y[0:n, (2 * g + 1) * CMP_HID:(2 * g + 2) * CMP_HID]
             + c[1:2, (2 * g + 1) * CMP_HID:(2 * g + 2) * CMP_HID])
        hid = a + pltpu.roll(b, n - 1, axis=0)
        out = out + jnp.dot(jax.nn.gelu(hid, approximate=True).astype(bf16), w2_ref[g], preferred_element_type=f32)
    return out


def _compress_prompt_body(xk_ref, xv_ref, wk_ref, pek_ref, w2k_ref, wv_ref, pev_ref, w2v_ref, kc_ref, vc_ref, *,
                          n_chunk):
    def flat(x_ref):
        return jnp.concatenate(
            [x_ref[pl.ds(l, n_chunk, stride=CMP_STRIDE), :].astype(bf16) for l in range(CMP_STRIDE)], axis=1)
    kc_ref[0] = _compress_rows(flat(xk_ref), wk_ref, pek_ref, w2k_ref).astype(bf16)
    vc_ref[0] = _compress_rows(flat(xv_ref), wv_ref, pev_ref, w2v_ref).T.astype(bf16)


def _compress_prompt(kvrow, cwk, cwv, *, nb, T):
    n_chunk = T // CMP_STRIDE
    const = lambda a: pl.BlockSpec(a.shape, lambda b: (0,) * a.ndim)
    return pl.pallas_call(
        functools.partial(_compress_prompt_body, n_chunk=n_chunk), grid=(nb,),
        in_specs=[pl.BlockSpec((T, 128), lambda b: (b, 0)), pl.BlockSpec((T, 128), lambda b: (b, 1))]
        + [const(a) for a in (*cwk, *cwv)],
        out_specs=[pl.BlockSpec((1, n_chunk, 128), lambda b: (b, 0, 0)),
                   pl.BlockSpec((1, 128, n_chunk), lambda b: (b, 0, 0))],
        out_shape=[jax.ShapeDtypeStruct((nb, n_chunk, 128), bf16), jax.ShapeDtypeStruct((nb, 128, n_chunk), bf16)],
        compiler_params=pltpu.CompilerParams(dimension_semantics=("parallel",), vmem_limit_bytes=VMEM_LIMIT),
        name="nsa_compress",
    )(kvrow, kvrow, *cwk, *cwv)


def _softmax_rows(s, mask):
    s = jnp.where(mask, s, NEG)
    e = jnp.exp2(s - jnp.max(s, axis=-1, keepdims=True))
    e = jnp.where(mask, e, 0.0)
    return e / jnp.maximum(jnp.sum(e, axis=-1, keepdims=True), TINY)


def _select_blocks(imp_t, qpos_row, n_blk):
    n = imp_t.shape[1]
    j = lax.broadcasted_iota(jnp.int32, imp_t.shape, 0)
    cur = qpos_row // SEL_BLOCK
    forced = (j == 0) | (j == cur) | (j == cur - 1)
    score = jnp.where(j <= cur, jnp.where(forced, FORCE_SCORE, imp_t), -1.0)
    n_tiles = -(-n_blk // 8)
    sub = lax.broadcasted_iota(jnp.int32, (8, n), 0)
    tiles = [score[8 * v:8 * v + 8] for v in range(n_tiles)]
    if n_blk % 8:
        tiles[-1] = jnp.where(sub < n_blk % 8, tiles[-1], -2.0)
    cnts = [jnp.zeros((8, n), f32)] * n_tiles
    for jp in range(n_blk):
        sj = tiles[jp // 8][jp % 8:jp % 8 + 1, :]
        for v in range(n_tiles):
            if v > jp // 8:
                beat = jnp.where(sj >= tiles[v], 1.0, 0.0)
            elif v < jp // 8:
                beat = jnp.where(sj > tiles[v], 1.0, 0.0)
            else:
                beat = jnp.where(sub > jp % 8, jnp.where(sj >= tiles[v], 1.0, 0.0), jnp.where(sj > tiles[v], 1.0, 0.0))
            cnts[v] = cnts[v] + beat
    keep = [jnp.where(c < float(min(SEL_TOP, n_blk)), 1.0, 0.0) for c in cnts]
    if n_blk % 8:
        keep[-1] = jnp.where(sub < n_blk % 8, keep[-1], 0.0)
    return jnp.concatenate(keep + [jnp.zeros((NBLK_PAD - 8 * n_tiles, n), f32)] * (NBLK_PAD > 8 * n_tiles), axis=0)


def _transpose_sel(sel_t):
    return jnp.concatenate([sel_t, jnp.zeros((128 - NBLK_PAD, sel_t.shape[1]), f32)], axis=0).T


def _softmax_cols(s, mask):
    s = jnp.where(mask, s, NEG)
    e = jnp.exp2(s - jnp.max(s, axis=0, keepdims=True))
    e = jnp.where(mask, e, 0.0)
    return e * (1.0 / jnp.maximum(jnp.sum(e, axis=0, keepdims=True), TINY))


def _lane_tile(x, n):
    return jnp.concatenate([x] * n, axis=1)


def _nsa_prompt_body(qct_ref, qrt_ref, gt_ref, kk_ref, vt_ref, kc_ref, vct_ref, ovt_ref, hbt_ref, sel_scr, *, T):
    i = pl.program_id(1)
    n_chunk = T // CMP_STRIDE
    n_blk = T // SEL_BLOCK
    q0 = i * TQ
    qpos_row = q0 + lax.broadcasted_iota(jnp.int32, (1, TQ), 1)
    gsig = jax.nn.sigmoid(gt_ref[...])
    n_kt = (q0 + TQ + TK - 1) // TK
    w0 = jnp.maximum(i - WINDOW // TQ, 0) * TQ
    WK = WINDOW + TQ
    blk_per_tile = TK // SEL_BLOCK
    sub_keys = TK // KV_SUB
    sub_blk = sub_keys // SEL_BLOCK
    ones_lhs = jnp.ones((16, sub_keys), bf16)
    zeros_q = jnp.zeros((HEAD_DIM, TQ), bf16)

    def group_queries(ref, g):
        cols = []
        for r in range(R):
            q = ref[(g * R + r) * HEAD_DIM:(g * R + r + 1) * HEAD_DIM, :]
            cols.append(jnp.concatenate([q, zeros_q] if g == 0 else [zeros_q, q], axis=0))
        return jnp.concatenate(cols, axis=1)

    qct_all = jnp.concatenate([group_queries(qct_ref, g) for g in range(N_KV)], axis=1)
    qrt_all = jnp.concatenate([group_queries(qrt_ref, g) for g in range(N_KV)], axis=1)
    s = jnp.dot(kc_ref[0], qct_all, preferred_element_type=f32)
    cmp_end = lax.broadcasted_iota(jnp.int32, (n_chunk, 1), 0) * CMP_STRIDE + (CMP_LEN - 1)
    p = _softmax_cols(s, _lane_tile(cmp_end <= qpos_row, N_KV * R))
    o_c_all = jnp.dot(vct_ref[0], p.astype(bf16), preferred_element_type=f32)
    for g in range(N_KV):
        base = g * R * TQ
        psum = (p[:, base:base + TQ] + p[:, base + TQ:base + 2 * TQ] + p[:, base + 2 * TQ:base + 3 * TQ]
                + p[:, base + 3 * TQ:base + 4 * TQ])
        imp_t = jnp.dot(ovt_ref[...], psum, preferred_element_type=f32, precision=HIGHEST)
        sel_scr[g] = jnp.where(_select_blocks(imp_t, qpos_row, n_blk) > 0.5, 0.0, NEG)

    def kv_step(kt, carry):
        b0 = pl.multiple_of(kt * blk_per_tile, blk_per_tile)
        sel_rows = [sel_scr[g, pl.ds(b0, blk_per_tile), :] for g in range(N_KV)]
        m_i, acc = carry
        for sub in range(KV_SUB):
            k0 = pl.multiple_of(kt * TK + sub * sub_keys, sub_keys)
            causal = (k0 + lax.broadcasted_iota(jnp.int32, (sub_keys, 1), 0)) <= qpos_row
            sc = jnp.dot(kk_ref[pl.ds(k0, sub_keys), 0:128], qrt_all, preferred_element_type=f32)
            biases = []
            for g in range(N_KV):
                rows = sel_rows[g][sub * sub_blk:(sub + 1) * sub_blk]
                bias = jnp.broadcast_to(rows[:, None, :], (sub_blk, SEL_BLOCK, TQ)).reshape(sub_keys, TQ)
                biases.append(_lane_tile(jnp.where(causal, bias, NEG), R))
            sc = sc + jnp.concatenate(biases, axis=1)
            m_new = jnp.maximum(m_i, jnp.max(sc, axis=0, keepdims=True))
            alpha = jnp.exp2(m_i - m_new)
            pp = jnp.exp2(sc - m_new).astype(bf16)
            v_ext = jnp.concatenate([vt_ref[0:128, pl.ds(k0, sub_keys)], ones_lhs], axis=0)
            m_i, acc = m_new, alpha * acc + jnp.dot(v_ext, pp, preferred_element_type=f32)
        return m_i, acc

    init = (jnp.full((1, N_KV * R * TQ), NEG, f32), jnp.zeros((128 + 16, N_KV * R * TQ), f32))
    _, acc_all = lax.fori_loop(0, n_kt, kv_step, init)

    o_s_all = acc_all[0:128] * (1.0 / acc_all[128:129])
    w0a = pl.multiple_of(w0, TQ)
    sw = jnp.dot(kk_ref[pl.ds(w0a, WK), 128:256], qrt_all, preferred_element_type=f32)
    dpos = qpos_row - (w0 + lax.broadcasted_iota(jnp.int32, (WK, 1), 0))
    biasw = jnp.where((dpos >= 0) & (dpos < WINDOW), 0.0, NEG)
    sw = sw + _lane_tile(biasw, N_KV * R)
    pw = jnp.exp2(sw - jnp.max(sw, axis=0, keepdims=True)).astype(bf16)
    ones_w = jnp.ones((16, WK), bf16)
    ow = jnp.dot(jnp.concatenate([vt_ref[128:256, pl.ds(w0a, WK)], ones_w], axis=0), pw, preferred_element_type=f32)
    o_w_all = ow[0:128] * (1.0 / ow[128:129])
    for h in range(N_HEADS):
        rows = slice((h // R) * HEAD_DIM, (h // R + 1) * HEAD_DIM)
        cs = slice(h * TQ, (h + 1) * TQ)
        c0 = 8 + 3 * h
        out = (gsig[c0:c0 + 1, :] * o_c_all[rows, cs] + gsig[c0 + 1:c0 + 2, :] * o_s_all[rows, cs]
               + gsig[c0 + 2:c0 + 3, :] * o_w_all[rows, cs])
        hbt_ref[h * 64:(h + 1) * 64, :] = out.astype(bf16)


def _nsa_prompt(qct, qrt, gt, kk, vt, kc, vct, *, nb, T):
    nq = T // TQ
    n_chunk = T // CMP_STRIDE
    ovt = _sel_overlap_t(n_chunk, T // SEL_BLOCK)
    col = lambda h: pl.BlockSpec((h, TQ), lambda b, i: (0, b * nq + i))
    const = lambda a: pl.BlockSpec(a.shape, lambda b, i: (0,) * a.ndim)
    return pl.pallas_call(
        functools.partial(_nsa_prompt_body, T=T), grid=(nb, nq),
        in_specs=[col(512), col(512), col(32),
                  pl.BlockSpec((T, 256), lambda b, i: (b, 0)),
                  pl.BlockSpec((256, T), lambda b, i: (0, b)),
                  pl.BlockSpec((1, n_chunk, 128), lambda b, i: (b, 0, 0)),
                  pl.BlockSpec((1, 128, n_chunk), lambda b, i: (b, 0, 0)),
                  const(ovt)],
        out_specs=col(512),
        out_shape=jax.ShapeDtypeStruct((512, nb * T), bf16),
        scratch_shapes=[pltpu.VMEM((N_KV, NBLK_PAD, TQ), f32)],
        compiler_params=pltpu.CompilerParams(dimension_semantics=("parallel", "arbitrary"),
                                             vmem_limit_bytes=VMEM_LIMIT),
        name="nsa_prompt",
    )(qct, qrt, gt, kk, vt, kc, vct, ovt)


def _nsa_sample_body(pt_ref, *refs, n_pages, ts):
    n_pg = GB * n_pages
    pages = refs[:n_pg]
    (win_ref, qc_ref, qr_ref, sm_ref, kvn_ref, wn_ref, wk_ref, pek_ref, w2k_ref, wv_ref, pev_ref, w2v_ref,
     ovt_ref, e_ref, perm_ref) = refs[n_pg:n_pg + 15]
    hb_ref, wout_stack = refs[-2:]
    wout_ref = wout_stack.at[0]
    P = n_pages * PAGE
    n_chunk = P // CMP_STRIDE
    n_blk = -(-(P + ts) // SEL_BLOCK)
    WB = win_ref.shape[4]
    chunks_per_page = PAGE // CMP_STRIDE

    qc_all = qc_ref[...].astype(f32)
    qr_all = qr_ref[...].astype(f32)
    gsig = jax.nn.sigmoid(sm_ref[...])
    rows = N_KV * R * SEG
    qpos_col = P + lax.broadcasted_iota(jnp.int32, (rows, 1), 0) % SEG
    qpos_row = P + lax.broadcasted_iota(jnp.int32, (1, 128), 1) % SEG
    new_lane = lax.broadcasted_iota(jnp.int32, (1, 128), 1)
    zpad = jnp.zeros((128 - SEG, 128), f32)

    def stack(q_all, bi):
        parts = []
        for h in range(N_HEADS):
            g = h // R
            q = q_all[bi * SEG:(bi + 1) * SEG, (h // 2) * 128:(h // 2 + 1) * 128]
            if h % 2 != g:
                q = pltpu.roll(q, HEAD_DIM, axis=1)
            parts.append(jnp.where(new_lane // HEAD_DIM == g, q, 0.0))
        return jnp.concatenate(parts, axis=0).astype(bf16)

    kwpos = jnp.concatenate([P - WB + lax.broadcasted_iota(jnp.int32, (1, WB), 1), P + new_lane], axis=1)
    validw = jnp.concatenate([jnp.full((1, WB), True), new_lane < SEG], axis=1)
    dpos = qpos_col - kwpos
    maskw = (dpos >= 0) & (dpos < WINDOW) & (kwpos >= 0) & validw
    pre = []
    for bi in range(GB):
        rs8 = slice(bi * SEG, (bi + 1) * SEG)
        ksel_t = jnp.concatenate([pages[bi * n_pages + pp][0, 0, 2].astype(bf16) for pp in range(n_pages)], axis=1)
        vsel_t = jnp.concatenate([pages[bi * n_pages + pp][0, 0, 3].astype(bf16) for pp in range(n_pages)], axis=1)
        knew = jnp.concatenate([kvn_ref[rs8, 256:384], zpad], axis=0).astype(bf16)
        vnew = jnp.concatenate([kvn_ref[rs8, 384:512], zpad], axis=0).astype(bf16)
        kwnew_f = jnp.concatenate([wn_ref[rs8, 0:128], zpad], axis=0)
        vwnew_f = jnp.concatenate([wn_ref[rs8, 128:256], zpad], axis=0)
        qr = stack(qr_all, bi)
        s_sel = jnp.concatenate([jnp.dot(qr, ksel_t, preferred_element_type=f32),
                                 lax.dot_general(qr, knew, NT, preferred_element_type=f32)], axis=1)
        s_w = jnp.concatenate([jnp.dot(qr, win_ref[0, bi, 0].astype(bf16), preferred_element_type=f32),
                               lax.dot_general(qr, kwnew_f.astype(bf16), NT, preferred_element_type=f32)], axis=1)
        p_w = _softmax_rows(s_w, maskw).astype(bf16)
        o_w = (lax.dot_general(p_w[:, 0:WB], win_ref[0, bi, 1].astype(bf16), NT, preferred_element_type=f32)
               + jnp.dot(p_w[:, WB:WB + 128], vwnew_f.astype(bf16), preferred_element_type=f32))
        pre.append((s_sel, o_w, vsel_t, vnew, kwnew_f, vwnew_f))

    xt_stack = jnp.concatenate([pages[j][0, 0, c].astype(bf16) for j in range(n_pg) for c in (0, 1)], axis=0)
    y_all = lax.dot_general(perm_ref[...], xt_stack, NT, preferred_element_type=f32)

    def flat(c):
        return jnp.concatenate(
            [jnp.concatenate([y_all[l * chunks_per_page:(l + 1) * chunks_per_page, (2 * j + c) * 128:(2 * j + c + 1) * 128]
                              for j in range(n_pg)], axis=0)
             for l in range(CMP_STRIDE)], axis=1).astype(bf16)

    kc_all = _compress_rows(flat(0), wk_ref, pek_ref, w2k_ref).astype(bf16)
    vc_all = _compress_rows(flat(1), wv_ref, pev_ref, w2v_ref).astype(bf16)

    cmp_end = lax.broadcasted_iota(jnp.int32, (1, n_chunk), 1) * CMP_STRIDE + (CMP_LEN - 1)
    o_cs, psums = [], []
    for bi in range(GB):
        s = lax.dot_general(stack(qc_all, bi), kc_all[bi * n_chunk:(bi + 1) * n_chunk], NT,
                            preferred_element_type=f32)
        p = _softmax_rows(s, cmp_end <= qpos_col)
        o_cs.append(jnp.dot(p.astype(bf16), vc_all[bi * n_chunk:(bi + 1) * n_chunk], preferred_element_type=f32))
        for g in range(N_KV):
            pg = p[g * R * SEG:(g + 1) * R * SEG]
            psums.append(pg[0:SEG] + pg[SEG:2 * SEG] + pg[2 * SEG:3 * SEG] + pg[3 * SEG:4 * SEG])
    psum_all = jnp.concatenate(psums + [jnp.zeros((128 - SEG * GB * N_KV, n_chunk), f32)], axis=0)
    imp_t = lax.dot_general(ovt_ref[...], psum_all, NT, preferred_element_type=f32, precision=HIGHEST)
    sel_all = _transpose_sel(_select_blocks(imp_t, qpos_row, n_blk)).astype(bf16)
    mk_all = jnp.dot(sel_all, e_ref[...], preferred_element_type=f32)

    out_rows = []
    for bi in range(GB):
        rs8 = slice(bi * SEG, (bi + 1) * SEG)
        s_sel, o_w, vsel_t, vnew, kwnew_f, vwnew_f = pre[bi]
        o_c = o_cs[bi]
        mk = jnp.concatenate(
            [mk_all[(bi * N_KV + g) * SEG:(bi * N_KV + g + 1) * SEG] for g in range(N_KV) for _ in range(R)], axis=0) > 0.5
        kpos = jnp.concatenate([lax.broadcasted_iota(jnp.int32, (1, P), 1), P + new_lane], axis=1)
        valid = jnp.concatenate([jnp.full((1, P), True), new_lane < SEG], axis=1)
        p_s = _softmax_rows(s_sel, mk & (kpos <= qpos_col) & valid).astype(bf16)
        o_s = (lax.dot_general(p_s[:, 0:P], vsel_t, NT, preferred_element_type=f32)
               + jnp.dot(p_s[:, P:P + 128], vnew, preferred_element_type=f32))
        gs = gsig[rs8]
        heads = [None] * N_HEADS
        for h in range(N_HEADS):
            rr = slice(h * SEG, (h + 1) * SEG)
            c0 = 8 + 3 * h
            out = gs[:, c0:c0 + 1] * o_c[rr] + gs[:, c0 + 1:c0 + 2] * o_s[rr] + gs[:, c0 + 2:c0 + 3] * o_w[rr]
            if h % 2 != h // R:
                out = pltpu.roll(out, HEAD_DIM, axis=1)
            heads[h] = out
        out_rows.append(jnp.concatenate(
            [jnp.where(new_lane < HEAD_DIM, heads[2 * j], heads[2 * j + 1]) for j in range(N_HEADS // 2)], axis=1))
        for kv, new_f in ((0, kwnew_f), (1, vwnew_f)):
            old = pltpu.roll(win_ref[0, bi, kv], WB - ts, axis=1)
            new_t = pltpu.roll(new_f.T, 128 - ts, axis=1)
            wout_ref[bi, kv, :, 0:WB - 128] = old[:, 0:WB - 128]
            wout_ref[bi, kv, :, WB - 128:WB] = jnp.where(new_lane >= 128 - ts, new_t, old[:, WB - 128:WB])
    hb_ref[...] = jnp.concatenate(out_rows, axis=0).astype(bf16)


def _nsa_sample(page_table, cache_t, cwin_t, qc, qr, sm, kvrow, winrow, cwk, cwv, win_stack, *, layer, row0, ts):
    nseq, n_pages = page_table.shape
    depth = cwin_t.shape[0]
    P = n_pages * PAGE
    WB = cwin_t.shape[4]
    n_chunk = P // CMP_STRIDE
    ovt = _sel_overlap_t(n_chunk, -(-(P + ts) // SEL_BLOCK))
    e = _block_expand(P + 128)
    r = np.arange(PAGE)
    perm = jnp.asarray(np.arange(PAGE)[None, :] == (CMP_STRIDE * (r % (PAGE // CMP_STRIDE)) + r // (PAGE // CMP_STRIDE))[:, None],
                       dtype=bf16)
    b0 = row0 // (GB * SEG)

    def page_map(i, pt, *, bi, p):
        return (layer, pt[i * GB + bi, p], 0, 0, 0)

    page_specs = [pl.BlockSpec((1, 1, 4, 128, PAGE), functools.partial(page_map, bi=bi, p=p))
                  for bi in range(GB) for p in range(n_pages)]
    row = lambda w: pl.BlockSpec((GB * SEG, w), lambda i, pt: (b0 + i, 0))
    const = lambda a: pl.BlockSpec(a.shape, lambda i, pt: (0,) * a.ndim)
    extra_in, extra_spec, win_shape = _stacked_out(win_stack, depth, (nseq, 2, 128, WB))
    args = (page_table, *([cache_t] * (GB * n_pages)), cwin_t, qc, qr, sm, kvrow, winrow, *cwk, *cwv, ovt, e, perm)
    grid_spec = pltpu.PrefetchScalarGridSpec(
        num_scalar_prefetch=1, grid=(nseq // GB,),
        in_specs=page_specs + [pl.BlockSpec((1, GB, 2, 128, WB), lambda i, pt: (layer, i, 0, 0, 0)),
                               row(N_WIDTH), row(N_WIDTH), row(128), row(512), row(256)]
        + [const(a) for a in (*cwk, *cwv, ovt, e, perm)] + extra_spec,
        out_specs=[pl.BlockSpec((GB * SEG, N_WIDTH), lambda i, pt: (i, 0)),
                   pl.BlockSpec((1, GB, 2, 128, WB), lambda i, pt: (layer, i, 0, 0, 0))])
    return pl.pallas_call(
        functools.partial(_nsa_sample_body, n_pages=n_pages, ts=ts), grid_spec=grid_spec,
        out_shape=[jax.ShapeDtypeStruct((nseq * SEG, N_WIDTH), bf16), win_shape],
        input_output_aliases={len(args): 1} if extra_in else {},
        compiler_params=pltpu.CompilerParams(dimension_semantics=("parallel",), vmem_limit_bytes=VMEM_LIMIT),
        name="nsa_sample",
    )(*args, *extra_in)


def kernel(x_prompt, x_sample, cache_kv_pages, page_table, cache_win, state_mlstm_C, state_mlstm_n, state_mlstm_m, state_mlstm_conv, norm1_g, w_in, b_gates, conv_w, conv_b, w_mq, w_mk, mlstm_norm_g, cmp_pe_k, cmp_pe_v, cmp_w1_k, cmp_w2_k, cmp_w1_v, cmp_w2_v, w_branch_a, w_branch_b, w_out, norm2_g, w_ffn_up, w_ffn_down, final_norm_g):
    Bp, Tp, _ = x_prompt.shape
    Bs, Ts, _ = x_sample.shape
    depth = w_in.shape[0]
    n_pool, page = cache_kv_pages.shape[1:3]
    n_pages = page_table.shape[1]
    past_len = n_pages * page
    wb_len = cache_win.shape[2]
    mp = Bp * Tp
    ms = Bs * SAMPLE_PAD
    assert page == PAGE and M_CONV - 1 <= Ts <= SAMPLE_PAD and Tp % MLSTM_CHUNK == 0 and Tp >= WINDOW + TQ
    assert mp % (SEG * SEQ_PER_STEP) == 0 and Bs % SEQ_PER_STEP == 0 and Bs % GB == 0

    pad_seq = lambda a: jnp.pad(a, ((0, 0), (0, SAMPLE_PAD - a.shape[1]), (0, 0)))
    xs = (x_prompt.reshape(mp, D_MODEL), pad_seq(x_sample).reshape(ms, D_MODEL))
    tabs = _rope_tables(Tp, past_len, ROW_TILE * (ms // ROW_TILE))
    w_all = _in_proj_weights(w_in)
    w_up, w_down = w_ffn_up.astype(bf16), w_ffn_down.astype(bf16)
    cache_t = jnp.transpose(cache_kv_pages, (0, 1, 3, 4, 5, 2)).reshape(depth, n_pool, 4, KV_W, PAGE)
    cwin_t = jnp.transpose(cache_win, (0, 1, 3, 4, 5, 2)).reshape(depth, Bs, 2, KV_W, wb_len)

    outs = {k: [] for k in ("p_win", "p_C", "p_n", "p_m", "p_conv", "s_kv", "s_n", "s_m", "s_conv")}
    c_stack = win_stack = kvt_stack = None
    y_parts = None
    for l in range(depth):
        w_t = jnp.concatenate([w_in[l][:, O_IG:O_IG + 8], w_in[l][:, O_NG:O_NG + 24]], axis=1).T.astype(bf16)
        xm, sm, smt, qc, qr, qct, qrt, kvrow, winrow, kk, vt, gates, kvt_stack = _in_proj(
            xs, norm1_g[l].reshape(1, D_MODEL), w_all, w_t, tabs, Tp, kvt_stack, layer=l, depth=depth)

        wq, wk = w_mq[l].astype(bf16), w_mk[l].astype(bf16)
        ha_p, c_p, n_p, m_p, cv_p = _mlstm_prompt(xm, sm, smt, conv_w[l], conv_b[l], wq, wk, b_gates[l], mlstm_norm_g[l],
                                                   nb=Bp, T=Tp, L=MLSTM_CHUNK)
        tail = jnp.pad(state_mlstm_conv[l], ((0, 0), (0, SEG - (M_CONV - 1)), (0, 0))).reshape(ms, M_WIDTH)
        n0 = jnp.pad(state_mlstm_n[l], ((0, 0), (0, 8 - M_HEADS), (0, 0)))
        m0 = jnp.broadcast_to(jnp.pad(state_mlstm_m[l], ((0, 0), (0, 128 - M_HEADS)))[:, None, :], (Bs, SEG, 128))
        ha_s, c_stack, n_s, mt_s = _mlstm_sample(
            xm, sm, smt, tail, state_mlstm_C, n0, m0.reshape(ms, 128), conv_w[l], conv_b[l], wq, wk, b_gates[l],
            mlstm_norm_g[l], c_stack, layer=l, depth=depth, row0=mp, nseq=Bs, n_valid=Ts)

        cwk = _compress_weights(cmp_pe_k[l], cmp_w1_k[l], cmp_w2_k[l])
        cwv = _compress_weights(cmp_pe_v[l], cmp_w1_v[l], cmp_w2_v[l])
        kc, vct = _compress_prompt(kvrow, cwk, cwv, nb=Bp, T=Tp)
        hbt_p = _nsa_prompt(qct, qrt, smt, kk, vt, kc, vct, nb=Bp, T=Tp)
        hb_s, win_stack = _nsa_sample(page_table, cache_t, cwin_t, qc, qr, sm, kvrow, winrow, cwk, cwv, win_stack,
                                      layer=l, row0=mp, ts=Ts)

        x_all, *y_parts = _mix_ffn(
            xs, (ha_p, ha_s), (hbt_p, hb_s.T), gates, w_branch_a[l].astype(bf16), w_branch_b[l].astype(bf16),
            w_out[l].astype(bf16), norm2_g[l].reshape(1, D_MODEL), w_up, w_down, final_norm_g.reshape(1, D_MODEL),
            layer=l, n_prompt_rows=mp, final=l == depth - 1)
        xs = (x_all,)

        sample_rows = lambda a: a[mp:].reshape(Bs, SAMPLE_PAD, a.shape[1])
        wp = min(WINDOW, Tp)
        outs["p_win"].append(winrow[:mp].reshape(Bp, Tp, 2 * KV_W)[:, -wp:].reshape(Bp, wp, 2, N_KV, HEAD_DIM))
        outs["p_C"].append(c_p)
        outs["p_n"].append(n_p[:, :M_HEADS])
        outs["p_m"].append(m_p[:, 0, :M_HEADS])
        outs["p_conv"].append(cv_p[:, 8 - (M_CONV - 1):])
        outs["s_kv"].append(sample_rows(kvrow)[:, :Ts].reshape(Bs, Ts, 4, N_KV, HEAD_DIM))
        outs["s_n"].append(n_s[:, :M_HEADS])
        outs["s_m"].append(mt_s.reshape(Bs, SEG, 128)[:, SEG - 1, :M_HEADS])
        outs["s_conv"].append(sample_rows(xm)[:, Ts - (M_CONV - 1):Ts, :M_WIDTH])

    s_win = jnp.transpose(win_stack.reshape(depth, Bs, 2, N_KV, HEAD_DIM, wb_len), (0, 1, 5, 2, 3, 4))
    y_prompt = y_parts[0].reshape(Bp, Tp, D_MODEL)
    y_sample = y_parts[1].reshape(Bs, SAMPLE_PAD, D_MODEL)[:, :Ts]
    st = lambda k: jnp.stack(outs[k])
    p_kv = jnp.transpose(kvt_stack.reshape(depth, Bp, 4, N_KV, HEAD_DIM, Tp), (0, 1, 5, 2, 3, 4))
    return (y_prompt, y_sample, p_kv, st("p_win"), st("p_C"), st("p_n"), st("p_m"), st("p_conv"),
            st("s_kv"), s_win, c_stack, st("s_n"), st("s_m"), st("s_conv"))
```

```python
import functools

import jax
import jax.numpy as jnp
import numpy as np
from jax import lax
from jax.experimental import pallas as pl
from jax.experimental.pallas import tpu as pltpu

f32 = jnp.float32
bf16 = jnp.bfloat16

D_MODEL = 1024
M_HEADS = 4
M_WIDTH = 512
M_HEAD_DIM = 128
M_CONV = 4
MLSTM_CHUNK = 256
HEAD_DIM = 64
N_WIDTH = 512
N_HEADS = 8
N_KV = 2
KV_W = 128
CMP_STRIDE = 16
CMP_LEN = 32
SEL_BLOCK = 64
SEL_TOP = 16
WINDOW = 512
ROT_DIM = 16
ROPE_THETA = 500000.0
D_FF = 2816
EPS = 1e-6
NEG = -1e30
TINY = 1e-30
FORCE_SCORE = 1e9

SAMPLE_PAD = 8
SEG = SAMPLE_PAD
SEQ_PER_STEP = 16
GB = 2
PAGE = 128
HD = M_HEAD_DIM
R = N_HEADS // N_KV
CMP_HID = 128
TQ = 128
TK = 1024
KV_SUB = 1
NBLK_PAD = 64
HIGHEST = lax.Precision.HIGHEST
NT = (((1,), (1,)), ((), ()))
LOG2E = 1.4426950408889634
ROW_TILE = 256
FF_CHUNK = 2816
VMEM_LIMIT = 56 * 1024 * 1024

C_XM, C_Q, C_KV, C_WIN, C_GATE, C_SM = 0, 1536, 2048, 2560, 2816, 4864
C_TOTAL = 4992
O_U, O_V, O_O, O_IG, O_FG, O_Q, O_KV, O_NG, O_GA, O_GB = 0, 512, 1024, 1536, 1540, 1544, 2056, 2824, 2848, 3872
IN_WIDTH = 4896


def _in_proj_weights(w):
    lead = w.shape[:-1]
    parts = [w[..., O_U:O_IG], w[..., O_Q:O_Q + N_WIDTH] * (HEAD_DIM ** -0.5), w[..., O_KV:O_KV + 6 * KV_W],
             w[..., O_GA:O_GA + 2 * D_MODEL], w[..., O_IG:O_IG + 8], w[..., O_NG:O_NG + 24],
             jnp.zeros(lead + (128 - 32,), f32)]
    out = jnp.concatenate(parts, axis=-1).astype(bf16)
    assert out.shape[-1] == C_TOTAL
    return out


def _rope_tables(seq, past_len, n_sample_rows):
    half = ROT_DIM // 2
    inv = ROPE_THETA ** (-jnp.arange(half, dtype=f32) / half)
    pos = jnp.concatenate([jnp.arange(seq), past_len + (jnp.arange(n_sample_rows) % SAMPLE_PAD)]).astype(f32)
    ang = pos[:, None] * inv[None, :]
    cos8, sin8 = jnp.cos(ang), jnp.sin(ang)
    n = pos.shape[0]
    one = jnp.ones((n, 64 - ROT_DIM), f32)
    zero = jnp.zeros((n, 64 - ROT_DIM), f32)
    z8 = jnp.zeros((n, half), f32)
    cos = jnp.concatenate([cos8, cos8, one], axis=1)
    sa = jnp.concatenate([-sin8, z8, zero], axis=1)
    sb = jnp.concatenate([z8, sin8, zero], axis=1)
    tile2 = lambda a: jnp.concatenate([a, a], axis=1)
    return tile2(cos), tile2(sa), tile2(sb)


def _rms(x, g):
    return x * lax.rsqrt(jnp.mean(x * x, axis=-1, keepdims=True) + EPS) * g


def _in_proj_body(*refs, n_x, n_sample_tiles):
    x_refs = refs[:n_x]
    g_ref, w_ref, wt_ref, cos_ref, sa_ref, sb_ref = refs[n_x:n_x + 6]
    (xm_ref, sm_ref, smt_ref, qc_ref, qr_ref, qct_ref, qrt_ref, kv_ref, win_ref, kk_ref, vt_ref, gate_ref,
     kvt_stack) = refs[-13:]
    x = x_refs[0][...]
    if n_x == 2:
        x = jnp.where(pl.program_id(0) < n_sample_tiles, x_refs[1][...], x)
    hb = _rms(x, g_ref[...]).astype(bf16)
    cos, sa, sb = cos_ref[...], sa_ref[...], sb_ref[...]

    def rope(v):
        return v * cos + pltpu.roll(v, 128 - ROT_DIM // 2, axis=1) * sa + pltpu.roll(v, ROT_DIM // 2, axis=1) * sb

    def proj(c0, n):
        return jnp.dot(hb, w_ref[0, :, c0:c0 + n], preferred_element_type=f32)

    xm_ref[...] = proj(C_XM, 1536)
    sm_ref[...] = proj(C_SM, 128)
    smt_ref[...] = lax.dot_general(wt_ref[...], hb, NT, preferred_element_type=f32)
    qf = proj(C_Q, N_WIDTH) * LOG2E
    for j in range(N_HEADS // 2):
        sl = slice(j * 128, (j + 1) * 128)
        q = qf[:, sl]
        qrot = rope(q)
        qc_ref[:, sl] = q.astype(bf16)
        qr_ref[:, sl] = qrot.astype(bf16)
        qct_ref[sl, :] = q.T.astype(bf16)
        qrt_ref[sl, :] = qrot.T.astype(bf16)
    kv = proj(C_KV, 512)
    ksel = rope(kv[:, 256:384])
    kv_ref[:, 0:256] = kv[:, 0:256]
    kv_ref[:, 256:384] = ksel
    kv_ref[:, 384:512] = kv[:, 384:512]
    kvt_stack[0, 0, 0:256, :] = kv[:, 0:256].T
    kvt_stack[0, 0, 256:384, :] = ksel.T
    kvt_stack[0, 0, 384:512, :] = kv[:, 384:512].T
    win = proj(C_WIN, 256)
    kwin = rope(win[:, 0:128])
    win_ref[:, 0:128] = kwin
    win_ref[:, 128:256] = win[:, 128:256]
    kk_ref[:, 0:128] = ksel.astype(bf16)
    kk_ref[:, 128:256] = kwin.astype(bf16)
    vt_ref[0:128, :] = kv[:, 384:512].T.astype(bf16)
    vt_ref[128:256, :] = win[:, 128:256].T.astype(bf16)
    gate_ref[...] = proj(C_GATE, 2048)


def _in_proj(xs, norm_g, w_all, w_t, tabs, seq, kvt_stack, *, layer, depth):
    m = sum(x.shape[0] for x in xs)
    n_prompt_tiles_per_seq = seq // ROW_TILE
    n_prompt_tiles = (m - (tabs[0].shape[0] - seq)) // ROW_TILE
    n_seq = n_prompt_tiles // n_prompt_tiles_per_seq
    n_tiles = m // ROW_TILE
    n_sample_tiles = n_tiles - n_prompt_tiles

    def tile(i):
        return (i + n_prompt_tiles) % n_tiles

    def kvt_map(i):
        j = jnp.maximum(i - n_sample_tiles, 0)
        return (layer, j // n_prompt_tiles_per_seq, 0, j % n_prompt_tiles_per_seq)

    def tab_map(i):
        t = tile(i)
        return (jnp.where(t < n_prompt_tiles, t % n_prompt_tiles_per_seq, n_prompt_tiles_per_seq + t - n_prompt_tiles), 0)

    row = lambda w: pl.BlockSpec((ROW_TILE, w), lambda i: (tile(i), 0))
    const = lambda s: pl.BlockSpec(s, lambda i: (0, 0), pipeline_mode=pl.Buffered(1))
    tab = pl.BlockSpec((ROW_TILE, 128), tab_map)
    widths = (1536, 128, -32, 512, 512, -512, -512, 512, 256, 256, -256, 2048)
    dts = (f32, f32, f32, bf16, bf16, bf16, bf16, f32, f32, bf16, bf16, f32)
    out_shape = [jax.ShapeDtypeStruct((-w, m) if w < 0 else (m, w), d) for w, d in zip(widths, dts)]
    out_specs = [pl.BlockSpec((-w, ROW_TILE), lambda i: (0, tile(i))) if w < 0 else row(w) for w in widths]
    extra_in, extra_spec, kvt_shape = _stacked_out(kvt_stack, depth, (n_seq, 4 * KV_W, seq))
    if len(xs) == 1:
        x_specs = [row(D_MODEL)]
    else:
        x_specs = [pl.BlockSpec((ROW_TILE, D_MODEL), lambda i: (jnp.maximum(i - n_sample_tiles, 0), 0)),
                   pl.BlockSpec((ROW_TILE, D_MODEL), lambda i: (jnp.minimum(i, n_sample_tiles - 1), 0))]
    args = (*xs, norm_g, w_all, w_t, *tabs)
    return pl.pallas_call(
        functools.partial(_in_proj_body, n_x=len(xs), n_sample_tiles=n_sample_tiles), grid=(n_tiles,),
        in_specs=x_specs + [const((1, D_MODEL)),
                            pl.BlockSpec((1, D_MODEL, C_TOTAL), lambda i: (layer, 0, 0), pipeline_mode=pl.Buffered(1)),
                            const((32, D_MODEL)), tab, tab, tab]
        + extra_spec,
        out_specs=out_specs + [pl.BlockSpec((1, 1, 4 * KV_W, ROW_TILE), kvt_map)],
        out_shape=out_shape + [kvt_shape], name="in_proj",
        input_output_aliases={len(args): len(out_shape)} if extra_in else {},
        compiler_params=pltpu.CompilerParams(dimension_semantics=("arbitrary",), vmem_limit_bytes=VMEM_LIMIT),
    )(*args, *extra_in)


def _mix_ffn_body(*refs, n_x, n_prompt_tiles, final):
    x_refs = refs[:n_x]
    hap_ref, has_ref, hbp_ref, hbs_ref, gate_ref, wa_ref, wb_ref, wo_ref, g2_ref, wup_ref, wdn_ref, gf_ref = (
        refs[n_x:n_x + 12])
    xo_ref, *y_refs = refs[n_x + 12:]
    in_prompt = pl.program_id(0) < n_prompt_tiles
    pick = lambda p_ref, s_ref: jnp.where(in_prompt, p_ref[...], s_ref[...])
    x = x_refs[0][...] if n_x == 1 else pick(*x_refs)
    a = jnp.dot(pick(hap_ref, has_ref).astype(bf16), wa_ref[...], preferred_element_type=f32)
    b = lax.dot_general(pick(hbp_ref, hbs_ref), wb_ref[...], (((0,), (0,)), ((), ())),
                        preferred_element_type=f32)
    merged = jax.nn.sigmoid(gate_ref[:, 0:D_MODEL]) * a + jax.nn.sigmoid(gate_ref[:, D_MODEL:2 * D_MODEL]) * b
    x1 = x + jnp.dot(merged.astype(bf16), wo_ref[...], preferred_element_type=f32)
    hn = _rms(x1, g2_ref[...]).astype(bf16)
    acc = x1
    for c in range(D_FF // FF_CHUNK):
        lo = c * FF_CHUNK
        g = jnp.dot(hn, wup_ref[0, :, lo:lo + FF_CHUNK], preferred_element_type=f32)
        u = jnp.dot(hn, wup_ref[0, :, D_FF + lo:D_FF + lo + FF_CHUNK], preferred_element_type=f32)
        act = (g * jax.nn.sigmoid(g) * u).astype(bf16)
        acc = acc + jnp.dot(act, wdn_ref[0, lo:lo + FF_CHUNK, :], preferred_element_type=f32)
    xo_ref[...] = acc
    if final:
        y = _rms(acc, gf_ref[...])
        yp_ref, ys_ref = y_refs

        @pl.when(in_prompt)
        def _():
            yp_ref[...] = y

        @pl.when(jnp.logical_not(in_prompt))
        def _():
            ys_ref[...] = y


def _mix_ffn(xs, h_a, h_bt, gates, wa, wb, wo, g2, wup, wdn, gf, *, layer, n_prompt_rows, final):
    m = gates.shape[0]
    npt = n_prompt_rows // ROW_TILE
    row = lambda w: pl.BlockSpec((ROW_TILE, w), lambda i: (i, 0))
    p_row = lambda w: pl.BlockSpec((ROW_TILE, w), lambda i: (jnp.minimum(i, npt - 1), 0))
    s_row = lambda w: pl.BlockSpec((ROW_TILE, w), lambda i: (jnp.maximum(i - npt, 0), 0))
    p_col = lambda h: pl.BlockSpec((h, ROW_TILE), lambda i: (0, jnp.minimum(i, npt - 1)))
    s_col = lambda h: pl.BlockSpec((h, ROW_TILE), lambda i: (0, jnp.maximum(i - npt, 0)))
    const = lambda s: pl.BlockSpec(s, lambda i: (0, 0), pipeline_mode=pl.Buffered(1))
    layer_w = lambda a: pl.BlockSpec((1,) + a.shape[1:], lambda i: (layer, 0, 0), pipeline_mode=pl.Buffered(1))
    out_specs = [row(D_MODEL)]
    out_shape = [jax.ShapeDtypeStruct((m, D_MODEL), f32)]
    if final:
        out_specs += [p_row(D_MODEL), s_row(D_MODEL)]
        out_shape += [jax.ShapeDtypeStruct((n_prompt_rows, D_MODEL), f32),
                      jax.ShapeDtypeStruct((m - n_prompt_rows, D_MODEL), f32)]
    x_specs = [row(D_MODEL)] if len(xs) == 1 else [p_row(D_MODEL), s_row(D_MODEL)]
    return pl.pallas_call(
        functools.partial(_mix_ffn_body, n_x=len(xs), n_prompt_tiles=npt, final=final), grid=(m // ROW_TILE,),
        in_specs=x_specs + [p_row(M_WIDTH), s_row(M_WIDTH), p_col(N_WIDTH), s_col(N_WIDTH), row(2 * D_MODEL),
                            const(wa.shape), const(wb.shape), const(wo.shape), const((1, D_MODEL)),
                            layer_w(wup), layer_w(wdn), const((1, D_MODEL))],
        out_specs=out_specs, out_shape=out_shape, name="mix_ffn",
        compiler_params=pltpu.CompilerParams(dimension_semantics=("arbitrary",), vmem_limit_bytes=VMEM_LIMIT),
    )(*xs, *h_a, *h_bt, gates, wa, wb, wo, g2, wup, wdn, gf)


def _log_sigmoid(x):
    return jnp.minimum(x, 0.0) - jnp.log1p(jnp.exp(-jnp.abs(x)))


def _seg_scan(x, axis, seg, op, fill):
    idx = lax.broadcasted_iota(jnp.int32, x.shape, axis) % seg
    d = 1
    while d < seg:
        x = op(x, jnp.where(idx >= d, pltpu.roll(x, d, axis=axis), fill))
        d *= 2
    return x


def _conv_silu(u, tail, conv_w, conv_b, seg):
    L = u.shape[0]
    row = lax.broadcasted_iota(jnp.int32, u.shape, 0) % seg
    y = conv_b + u * conv_w[3:4, :]
    for k in (1, 2, 3):
        prev = tail if k == 3 else pltpu.roll(tail, L - (3 - k), axis=0)
        sh = jnp.where(row < k, prev, pltpu.roll(u, k, axis=0))
        y = y + sh * conv_w[3 - k:4 - k, :]
    return y * jax.nn.sigmoid(y)


def _gate_columns(sm, smt, bg_row, bg_col, m_vec, seg, n_valid):
    G = sm + bg_row
    lf = _log_sigmoid(G)
    ipre = G
    if n_valid < seg:
        rowc = lax.broadcasted_iota(jnp.int32, G.shape, 0) % seg
        lf = jnp.where(rowc < n_valid, lf, 0.0)
        ipre = jnp.where(rowc < n_valid, G, NEG)
    F = _seg_scan(lf, 0, seg, jnp.add, 0.0)
    F_al = pltpu.roll(F, 128 - M_HEADS, axis=1)
    a_col = ipre - F_al
    g_col = jnp.maximum(m_vec, _seg_scan(a_col, 0, seg, jnp.maximum, NEG))
    mt_col = F_al + g_col
    Gr = smt + bg_col
    lfr = _log_sigmoid(Gr)
    ir = Gr
    if n_valid < seg:
        lanec = lax.broadcasted_iota(jnp.int32, Gr.shape, 1) % seg
        lfr = jnp.where(lanec < n_valid, lfr, 0.0)
        ir = jnp.where(lanec < n_valid, Gr, NEG)
    Fr = _seg_scan(lfr, 1, seg, jnp.add, 0.0)
    a_row = ir - pltpu.roll(Fr, M_HEADS, axis=0)
    return a_col, g_col, mt_col, a_row


def _head_norm_gate(hh, ng, o):
    return hh * lax.rsqrt(jnp.mean(hh * hh, axis=-1, keepdims=True) + EPS) * ng * jax.nn.sigmoid(o)


def _mlstm_prompt_body(xm_ref, sm_ref, smt_ref, cw_ref, cb_ref, wq_ref, wk_ref, bgr_ref, bgc_ref, ng_ref,
                       ha_ref, c_out, n_out, m_out, conv_out, c_scr, m_scr, tail_scr, *, L):
    c_idx = pl.program_id(1)

    @pl.when(c_idx == 0)
    def _():
        c_scr[...] = jnp.zeros_like(c_scr)
        m_scr[...] = jnp.zeros_like(m_scr)
        tail_scr[...] = jnp.zeros_like(tail_scr)

    u = xm_ref[:, 0:M_WIDTH]
    tail = jnp.concatenate([tail_scr[...], jnp.zeros((L - 8, M_WIDTH), f32)], axis=0)
    cact = _conv_silu(u, tail, cw_ref[...], cb_ref[...], L)
    tail_scr[...] = pltpu.roll(u[L - 8:L, :], 3, axis=0)
    conv_out[0] = u[L - 8:L, :]

    m_vec = m_scr[0:1, :]
    a_col, g_col, mt_col, a_row = _gate_columns(sm_ref[...], smt_ref[...], bgr_ref[...], bgc_ref[...], m_vec, L, L)
    inter_col = jnp.exp(m_vec - g_col)
    floor_col = jnp.exp(-mt_col)
    g_last = g_col[L - 1:L, :]
    w_col = jnp.exp(a_col - g_last)
    decay = jnp.exp(m_vec - g_last)
    m_new = mt_col[L - 1:L, :]
    ti = lax.broadcasted_iota(jnp.int32, (L, L), 0)
    si = lax.broadcasted_iota(jnp.int32, (L, L), 1)
    causal = si <= ti
    scale = HD ** -0.5
    ones_v = jnp.ones((L, HD), bf16)
    for h in range(M_HEADS):
        sl = slice(h * HD, (h + 1) * HD)
        ch = cact[:, sl].astype(bf16)
        qh = jnp.dot(ch, wq_ref[h], preferred_element_type=f32)
        kh = jnp.dot(ch, wk_ref[h], preferred_element_type=f32) * scale
        vh = jnp.concatenate([xm_ref[:, M_WIDTH + h * HD:M_WIDTH + (h + 1) * HD].astype(bf16), ones_v], axis=1)
        qb = qh.astype(bf16)
        D = jnp.exp(jnp.where(causal, a_row[h:h + 1, :] - g_col[:, h:h + 1], NEG))
        S = lax.dot_general(qb, kh.astype(bf16), NT, preferred_element_type=f32) * D
        C = c_scr[h]
        ic = inter_col[:, h:h + 1]
        both = jnp.dot(S.astype(bf16), vh, preferred_element_type=f32) + ic * jnp.dot(
            qb, C.astype(bf16), preferred_element_type=f32)
        hh = both[:, 0:HD] / jnp.maximum(jnp.abs(both[:, HD:2 * HD]), floor_col[:, h:h + 1])
        o = xm_ref[:, 2 * M_WIDTH + h * HD:2 * M_WIDTH + (h + 1) * HD]
        ha_ref[:, sl] = _head_norm_gate(hh, ng_ref[:, sl], o)
        kw = kh * w_col[:, h:h + 1]
        c_scr[h] = decay[:, h:h + 1] * C + lax.dot_general(kw.astype(bf16), vh, (((0,), (0,)), ((), ())),
                                                           preferred_element_type=f32)
    m_scr[0:1, :] = m_new

    @pl.when(c_idx == pl.num_programs(1) - 1)
    def _():
        c_out[0] = c_scr[:, :, 0:HD]
        for h in range(M_HEADS):
            n_out[0, h:h + 1, :] = c_scr[h][:, HD:2 * HD].T[0:1, :]
        n_out[0, M_HEADS:8, :] = jnp.zeros((8 - M_HEADS, HD), f32)
        m_out[0] = m_scr[...]


def _mlstm_prompt(xm, sm, smt, conv_w, conv_b, wq, wk, b_gates, norm_g, *, nb, T, L):
    nc = T // L
    bg_row = jnp.zeros((1, 128), f32).at[0, :8].set(b_gates)
    bg_col = b_gates.reshape(8, 1)
    const = lambda s: pl.BlockSpec(s, lambda b, c: (0,) * len(s))
    return pl.pallas_call(
        functools.partial(_mlstm_prompt_body, L=L), grid=(nb, nc),
        in_specs=[pl.BlockSpec((L, 3 * M_WIDTH), lambda b, c: (b * nc + c, 0)),
                  pl.BlockSpec((L, 128), lambda b, c: (b * nc + c, 0)),
                  pl.BlockSpec((8, L), lambda b, c: (0, b * nc + c)),
                  const((4, M_WIDTH)), const((1, M_WIDTH)), const((M_HEADS, HD, HD)), const((M_HEADS, HD, HD)),
                  const((1, 128)), const((8, 1)), const((1, M_WIDTH))],
        out_specs=[pl.BlockSpec((L, M_WIDTH), lambda b, c: (b * nc + c, 0)),
                   pl.BlockSpec((1, M_HEADS, HD, HD), lambda b, c: (b, 0, 0, 0)),
                   pl.BlockSpec((1, 8, HD), lambda b, c: (b, 0, 0)),
                   pl.BlockSpec((1, 8, 128), lambda b, c: (b, 0, 0)),
                   pl.BlockSpec((1, 8, M_WIDTH), lambda b, c: (b, 0, 0))],
        out_shape=[jax.ShapeDtypeStruct((nb * T, M_WIDTH), f32),
                   jax.ShapeDtypeStruct((nb, M_HEADS, HD, HD), f32),
                   jax.ShapeDtypeStruct((nb, 8, HD), f32),
                   jax.ShapeDtypeStruct((nb, 8, 128), f32),
                   jax.ShapeDtypeStruct((nb, 8, M_WIDTH), f32)],
        scratch_shapes=[pltpu.VMEM((M_HEADS, HD, 2 * HD), f32), pltpu.VMEM((8, 128), f32),
                        pltpu.VMEM((8, M_WIDTH), f32)],
        compiler_params=pltpu.CompilerParams(dimension_semantics=("parallel", "arbitrary")),
        name="mlstm_prompt",
    )(xm, sm, smt, conv_w, conv_b.reshape(1, M_WIDTH), wq, wk, bg_row, bg_col, norm_g.reshape(1, M_WIDTH))


def _mlstm_sample_body(xm_ref, sm_ref, smt_ref, tail_ref, c_in, n_in, m_in, cw_ref, cb_ref, wq_ref, wk_ref,
                       bgr_ref, bgc_ref, ng_ref, *out_refs, n_valid):
    ha_ref, c_stack, n_out, m_out = out_refs[-4:]
    c_out = c_stack.at[0]
    L = SEG * SEQ_PER_STEP
    u = xm_ref[:, 0:M_WIDTH]
    cact = _conv_silu(u, tail_ref[...], cw_ref[...], cb_ref[...], SEG)
    m_rows = m_in[...]
    a_col, g_col, mt_col, a_row = _gate_columns(sm_ref[...], smt_ref[...], bgr_ref[...], bgc_ref[...], m_rows, SEG,
                                                n_valid)
    inter_col = jnp.exp(m_rows - g_col)
    floor_col = jnp.exp(-mt_col)
    ti = lax.broadcasted_iota(jnp.int32, (L, L), 0)
    si = lax.broadcasted_iota(jnp.int32, (L, L), 1)
    mask = (si <= ti) & ((si // SEG) == (ti // SEG))
    scale = HD ** -0.5
    for h in range(M_HEADS):
        sl = slice(h * HD, (h + 1) * HD)
        ch = cact[:, sl].astype(bf16)
        qh = jnp.dot(ch, wq_ref[h], preferred_element_type=f32)
        kh = jnp.dot(ch, wk_ref[h], preferred_element_type=f32) * scale
        vh = xm_ref[:, M_WIDTH + h * HD:M_WIDTH + (h + 1) * HD].astype(bf16)
        qb = qh.astype(bf16)
        D = jnp.exp(jnp.where(mask, a_row[h:h + 1, :] - g_col[:, h:h + 1], NEG))
        S = lax.dot_general(qb, kh.astype(bf16), NT, preferred_element_type=f32) * D
        num = jnp.dot(S.astype(bf16), vh, preferred_element_type=f32)
        den = jnp.sum(S, axis=-1, keepdims=True)
        inter_num, inter_den = [], []
        for s in range(SEQ_PER_STEP):
            rs = slice(s * SEG, (s + 1) * SEG)
            C = c_in[0, s, h]
            nrow = n_in[s, h:h + 1, :]
            inter_num.append(jnp.dot(qb[rs], C.astype(bf16), preferred_element_type=f32))
            inter_den.append(jnp.sum(qh[rs] * nrow, axis=-1, keepdims=True))
            g_last = g_col[s * SEG + SEG - 1:s * SEG + SEG, h:h + 1]
            m_prev = m_rows[s * SEG:s * SEG + 1, h:h + 1]
            w = jnp.exp(a_col[rs, h:h + 1] - g_last)
            dh = jnp.exp(m_prev - g_last)
            kw = kh[rs] * w
            c_out[s, h] = dh * C + lax.dot_general(kw.astype(bf16), vh[rs], (((0,), (0,)), ((), ())),
                                                    preferred_element_type=f32)
            n_out[s, h:h + 1, :] = dh * nrow + jnp.sum(kw, axis=0, keepdims=True)
        ic = inter_col[:, h:h + 1]
        num = num + ic * jnp.concatenate(inter_num, axis=0)
        den = den + ic * jnp.concatenate(inter_den, axis=0)
        hh = num / jnp.maximum(jnp.abs(den), floor_col[:, h:h + 1])
        o = xm_ref[:, 2 * M_WIDTH + h * HD:2 * M_WIDTH + (h + 1) * HD]
        ha_ref[:, sl] = _head_norm_gate(hh, ng_ref[:, sl], o)
    for s in range(SEQ_PER_STEP):
        n_out[s, M_HEADS:8, :] = jnp.zeros((8 - M_HEADS, HD), f32)
    m_out[...] = mt_col


def _stacked_out(prev, depth, shape):
    extra_in = [] if prev is None else [prev]
    extra_spec = [] if prev is None else [pl.BlockSpec(memory_space=pl.ANY)]
    return extra_in, extra_spec, jax.ShapeDtypeStruct((depth,) + shape, f32)


def _mlstm_sample(xm, sm, smt, tail, c0, n0, m0, conv_w, conv_b, wq, wk, b_gates, norm_g, c_stack, *, layer, depth,
                  row0, nseq, n_valid):
    L = SEG * SEQ_PER_STEP
    b0 = row0 // L
    bg_row = jnp.zeros((1, 128), f32).at[0, :8].set(b_gates)
    bg_col = b_gates.reshape(8, 1)
    const = lambda s: pl.BlockSpec(s, lambda i: (0,) * len(s))
    extra_in, extra_spec, c_shape = _stacked_out(c_stack, depth, (nseq, M_HEADS, HD, HD))
    n_in = 14
    return pl.pallas_call(
        functools.partial(_mlstm_sample_body, n_valid=n_valid), grid=(nseq // SEQ_PER_STEP,),
        in_specs=[pl.BlockSpec((L, 3 * M_WIDTH), lambda i: (b0 + i, 0)),
                  pl.BlockSpec((L, 128), lambda i: (b0 + i, 0)),
                  pl.BlockSpec((8, L), lambda i: (0, b0 + i)),
                  pl.BlockSpec((L, M_WIDTH), lambda i: (i, 0)),
                  pl.BlockSpec((1, SEQ_PER_STEP, M_HEADS, HD, HD), lambda i: (layer, i, 0, 0, 0)),
                  pl.BlockSpec((SEQ_PER_STEP, 8, HD), lambda i: (i, 0, 0)),
                  pl.BlockSpec((L, 128), lambda i: (i, 0)),
                  const((4, M_WIDTH)), const((1, M_WIDTH)), const((M_HEADS, HD, HD)), const((M_HEADS, HD, HD)),
                  const((1, 128)), const((8, 1)), const((1, M_WIDTH))] + extra_spec,
        out_specs=[pl.BlockSpec((L, M_WIDTH), lambda i: (i, 0)),
                   pl.BlockSpec((1, SEQ_PER_STEP, M_HEADS, HD, HD), lambda i: (layer, i, 0, 0, 0)),
                   pl.BlockSpec((SEQ_PER_STEP, 8, HD), lambda i: (i, 0, 0)),
                   pl.BlockSpec((L, 128), lambda i: (i, 0))],
        out_shape=[jax.ShapeDtypeStruct((nseq * SEG, M_WIDTH), f32), c_shape,
                   jax.ShapeDtypeStruct((nseq, 8, HD), f32),
                   jax.ShapeDtypeStruct((nseq * SEG, 128), f32)],
        input_output_aliases={n_in: 1} if extra_in else {},
        compiler_params=pltpu.CompilerParams(dimension_semantics=("parallel",), vmem_limit_bytes=VMEM_LIMIT),
        name="mlstm_sample",
    )(xm, sm, smt, tail, c0, n0, m0, conv_w, conv_b.reshape(1, M_WIDTH), wq, wk, bg_row, bg_col,
      norm_g.reshape(1, M_WIDTH), *extra_in)


def _sel_overlap_t(n_chunk, n_blk):
    n_cmp = n_chunk - 1
    start = np.arange(n_cmp) * CMP_STRIDE
    bs = np.arange(n_blk) * SEL_BLOCK
    ov = np.minimum(start[:, None] + CMP_LEN, bs[None, :] + SEL_BLOCK) - np.maximum(start[:, None], bs[None, :])
    ov = np.clip(ov, 0, None) / CMP_LEN
    out = np.zeros((NBLK_PAD, n_chunk), np.float32)
    out[:n_blk, :n_cmp] = ov.T
    return jnp.asarray(out)


def _block_expand(n_keys):
    e = (np.arange(n_keys)[None, :] // SEL_BLOCK) == np.arange(128)[:, None]
    return jnp.asarray(e, dtype=bf16)


def _compress_weights(pe, w1, w2):
    W = jnp.zeros((16, 2, 64, 2, 2, CMP_HID), f32)
    for g in range(N_KV):
        W = W.at[:, g, :, g, 0, :].set(w1[:16])
        W = W.at[:, g, :, g, 1, :].set(w1[16:])
    W = W.reshape(2048, 512)
    peA = jnp.broadcast_to(pe[:16, None, :], (16, 2, 64)).reshape(1, 2048)
    peB = jnp.broadcast_to(pe[16:, None, :], (16, 2, 64)).reshape(1, 2048)
    PE = jnp.concatenate([peA, peB, jnp.zeros((14, 2048), f32)], axis=0)
    W2 = jnp.zeros((2, CMP_HID, 128), f32).at[0, :, 0:64].set(w2).at[1, :, 64:128].set(w2)
    return W.astype(bf16), PE.astype(bf16), W2.astype(bf16)


def _compress_rows(flat, w_ref, pe_ref, w2_ref):
    n = flat.shape[0]
    y = jnp.dot(jnp.concatenate([flat, pe_ref[...]], axis=0), w_ref[...], preferred_element_type=f32)
    c = y[n:n + 8]
    out = jnp.zeros((n, w2_ref.shape[2]), f32)
    for g in range(N_KV):
        a = y[0:n, (2 * g) * CMP_HID:(2 * g + 1) * CMP_HID] + c[0:1, (2 * g) * CMP_HID:(2 * g + 1) * CMP_HID]
        b = (y[0:n, (2 * g + 1) * CMP_HID:(2 * g + 2) * CMP_HID]
             + c[1:2, (2 * g + 1) * CMP_HID:(2 * g + 2) * CMP_HID])
        hid = a + pltpu.roll(b, n - 1, axis=0)
        out = out + jnp.dot(jax.nn.gelu(hid, approximate=True).astype(bf16), w2_ref[g], preferred_element_type=f32)
    return out


def _compress_prompt_body(xk_ref, xv_ref, wk_ref, pek_ref, w2k_ref, wv_ref, pev_ref, w2v_ref, kc_ref, vc_ref, *,
                          n_chunk):
    def flat(x_ref):
        return jnp.concatenate(
            [x_ref[pl.ds(l, n_chunk, stride=CMP_STRIDE), :].astype(bf16) for l in range(CMP_STRIDE)], axis=1)
    kc_ref[0] = _compress_rows(flat(xk_ref), wk_ref, pek_ref, w2k_ref).astype(bf16)
    vc_ref[0] = _compress_rows(flat(xv_ref), wv_ref, pev_ref, w2v_ref).T.astype(bf16)


def _compress_prompt(kvrow, cwk, cwv, *, nb, T):
    n_chunk = T // CMP_STRIDE
    const = lambda a: pl.BlockSpec(a.shape, lambda b: (0,) * a.ndim)
    return pl.pallas_call(
        functools.partial(_compress_prompt_body, n_chunk=n_chunk), grid=(nb,),
        in_specs=[pl.BlockSpec((T, 128), lambda b: (b, 0)), pl.BlockSpec((T, 128), lambda b: (b, 1))]
        + [const(a) for a in (*cwk, *cwv)],
        out_specs=[pl.BlockSpec((1, n_chunk, 128), lambda b: (b, 0, 0)),
                   pl.BlockSpec((1, 128, n_chunk), lambda b: (b, 0, 0))],
        out_shape=[jax.ShapeDtypeStruct((nb, n_chunk, 128), bf16), jax.ShapeDtypeStruct((nb, 128, n_chunk), bf16)],
        compiler_params=pltpu.CompilerParams(dimension_semantics=("parallel",), vmem_limit_bytes=VMEM_LIMIT),
        name="nsa_compress",
    )(kvrow, kvrow, *cwk, *cwv)


def _softmax_rows(s, mask):
    s = jnp.where(mask, s, NEG)
    e = jnp.exp2(s - jnp.max(s, axis=-1, keepdims=True))
    e = jnp.where(mask, e, 0.0)
    return e / jnp.maximum(jnp.sum(e, axis=-1, keepdims=True), TINY)


def _select_blocks(imp_t, qpos_row, n_blk):
    n = imp_t.shape[1]
    j = lax.broadcasted_iota(jnp.int32, imp_t.shape, 0)
    cur = qpos_row // SEL_BLOCK
    forced = (j == 0) | (j == cur) | (j == cur - 1)
    score = jnp.where(j <= cur, jnp.where(forced, FORCE_SCORE, imp_t), -1.0)
    n_tiles = -(-n_blk // 8)
    sub = lax.broadcasted_iota(jnp.int32, (8, n), 0)
    tiles = [score[8 * v:8 * v + 8] for v in range(n_tiles)]
    if n_blk % 8:
        tiles[-1] = jnp.where(sub < n_blk % 8, tiles[-1], -2.0)
    cnts = [jnp.zeros((8, n), f32)] * n_tiles
    for jp in range(n_blk):
        sj = tiles[jp // 8][jp % 8:jp % 8 + 1, :]
        for v in range(n_tiles):
            if v > jp // 8:
                beat = jnp.where(sj >= tiles[v], 1.0, 0.0)
            elif v < jp // 8:
                beat = jnp.where(sj > tiles[v], 1.0, 0.0)
            else:
                beat = jnp.where(sub > jp % 8, jnp.where(sj >= tiles[v], 1.0, 0.0), jnp.where(sj > tiles[v], 1.0, 0.0))
            cnts[v] = cnts[v] + beat
    keep = [jnp.where(c < float(min(SEL_TOP, n_blk)), 1.0, 0.0) for c in cnts]
    if n_blk % 8:
        keep[-1] = jnp.where(sub < n_blk % 8, keep[-1], 0.0)
    return jnp.concatenate(keep + [jnp.zeros((NBLK_PAD - 8 * n_tiles, n), f32)] * (NBLK_PAD > 8 * n_tiles), axis=0)


def _transpose_sel(sel_t):
    return jnp.concatenate([sel_t, jnp.zeros((128 - NBLK_PAD, sel_t.shape[1]), f32)], axis=0).T


def _softmax_cols(s, mask):
    s = jnp.where(mask, s, NEG)
    e = jnp.exp2(s - jnp.max(s, axis=0, keepdims=True))
    e = jnp.where(mask, e, 0.0)
    return e * (1.0 / jnp.maximum(jnp.sum(e, axis=0, keepdims=True), TINY))


def _lane_tile(x, n):
    return jnp.concatenate([x] * n, axis=1)


def _nsa_prompt_body(qct_ref, qrt_ref, gt_ref, kk_ref, vt_ref, kc_ref, vct_ref, ovt_ref, hbt_ref, sel_scr, *, T):
    i = pl.program_id(1)
    n_chunk = T // CMP_STRIDE
    n_blk = T // SEL_BLOCK
    q0 = i * TQ
    qpos_row = q0 + lax.broadcasted_iota(jnp.int32, (1, TQ), 1)
    gsig = jax.nn.sigmoid(gt_ref[...])
    n_kt = (q0 + TQ + TK - 1) // TK
    w0 = jnp.maximum(i - WINDOW // TQ, 0) * TQ
    WK = WINDOW + TQ
    blk_per_tile = TK // SEL_BLOCK
    sub_keys = TK // KV_SUB
    sub_blk = sub_keys // SEL_BLOCK
    ones_lhs = jnp.ones((16, sub_keys), bf16)
    zeros_q = jnp.zeros((HEAD_DIM, TQ), bf16)

    def group_queries(ref, g):
        cols = []
        for r in range(R):
            q = ref[(g * R + r) * HEAD_DIM:(g * R + r + 1) * HEAD_DIM, :]
            cols.append(jnp.concatenate([q, zeros_q] if g == 0 else [zeros_q, q], axis=0))
        return jnp.concatenate(cols, axis=1)

    qct_all = jnp.concatenate([group_queries(qct_ref, g) for g in range(N_KV)], axis=1)
    qrt_all = jnp.concatenate([group_queries(qrt_ref, g) for g in range(N_KV)], axis=1)
    s = jnp.dot(kc_ref[0], qct_all, preferred_element_type=f32)
    cmp_end = lax.broadcasted_iota(jnp.int32, (n_chunk, 1), 0) * CMP_STRIDE + (CMP_LEN - 1)
    p = _softmax_cols(s, _lane_tile(cmp_end <= qpos_row, N_KV * R))
    o_c_all = jnp.dot(vct_ref[0], p.astype(bf16), preferred_element_type=f32)
    for g in range(N_KV):
        base = g * R * TQ
        psum = (p[:, base:base + TQ] + p[:, base + TQ:base + 2 * TQ] + p[:, base + 2 * TQ:base + 3 * TQ]
                + p[:, base + 3 * TQ:base + 4 * TQ])
        imp_t = jnp.dot(ovt_ref[...], psum, preferred_element_type=f32, precision=HIGHEST)
        sel_scr[g] = jnp.where(_select_blocks(imp_t, qpos_row, n_blk) > 0.5, 0.0, NEG)

    def kv_step(kt, carry):
        b0 = pl.multiple_of(kt * blk_per_tile, blk_per_tile)
        sel_rows = [sel_scr[g, pl.ds(b0, blk_per_tile), :] for g in range(N_KV)]
        m_i, acc = carry
        for sub in range(KV_SUB):
            k0 = pl.multiple_of(kt * TK + sub * sub_keys, sub_keys)
            causal = (k0 + lax.broadcasted_iota(jnp.int32, (sub_keys, 1), 0)) <= qpos_row
            sc = jnp.dot(kk_ref[pl.ds(k0, sub_keys), 0:128], qrt_all, preferred_element_type=f32)
            biases = []
            for g in range(N_KV):
                rows = sel_rows[g][sub * sub_blk:(sub + 1) * sub_blk]
                bias = jnp.broadcast_to(rows[:, None, :], (sub_blk, SEL_BLOCK, TQ)).reshape(sub_keys, TQ)
                biases.append(_lane_tile(jnp.where(causal, bias, NEG), R))
            sc = sc + jnp.concatenate(biases, axis=1)
            m_new = jnp.maximum(m_i, jnp.max(sc, axis=0, keepdims=True))
            alpha = jnp.exp2(m_i - m_new)
            pp = jnp.exp2(sc - m_new).astype(bf16)
            v_ext = jnp.concatenate([vt_ref[0:128, pl.ds(k0, sub_keys)], ones_lhs], axis=0)
            m_i, acc = m_new, alpha * acc + jnp.dot(v_ext, pp, preferred_element_type=f32)
        return m_i, acc

    init = (jnp.full((1, N_KV * R * TQ), NEG, f32), jnp.zeros((128 + 16, N_KV * R * TQ), f32))
    _, acc_all = lax.fori_loop(0, n_kt, kv_step, init)

    o_s_all = acc_all[0:128] * (1.0 / acc_all[128:129])
    w0a = pl.multiple_of(w0, TQ)
    sw = jnp.dot(kk_ref[pl.ds(w0a, WK), 128:256], qrt_all, preferred_element_type=f32)
    dpos = qpos_row - (w0 + lax.broadcasted_iota(jnp.int32, (WK, 1), 0))
    biasw = jnp.where((dpos >= 0) & (dpos < WINDOW), 0.0, NEG)
    sw = sw + _lane_tile(biasw, N_KV * R)
    pw = jnp.exp2(sw - jnp.max(sw, axis=0, keepdims=True)).astype(bf16)
    ones_w = jnp.ones((16, WK), bf16)
    ow = jnp.dot(jnp.concatenate([vt_ref[128:256, pl.ds(w0a, WK)], ones_w], axis=0), pw, preferred_element_type=f32)
    o_w_all = ow[0:128] * (1.0 / ow[128:129])
    for h in range(N_HEADS):
        rows = slice((h // R) * HEAD_DIM, (h // R + 1) * HEAD_DIM)
        cs = slice(h * TQ, (h + 1) * TQ)
        c0 = 8 + 3 * h
        out = (gsig[c0:c0 + 1, :] * o_c_all[rows, cs] + gsig[c0 + 1:c0 + 2, :] * o_s_all[rows, cs]
               + gsig[c0 + 2:c0 + 3, :] * o_w_all[rows, cs])
        hbt_ref[h * 64:(h + 1) * 64, :] = out.astype(bf16)


def _nsa_prompt(qct, qrt, gt, kk, vt, kc, vct, *, nb, T):
    nq = T // TQ
    n_chunk = T // CMP_STRIDE
    ovt = _sel_overlap_t(n_chunk, T // SEL_BLOCK)
    col = lambda h: pl.BlockSpec((h, TQ), lambda b, i: (0, b * nq + i))
    const = lambda a: pl.BlockSpec(a.shape, lambda b, i: (0,) * a.ndim)
    return pl.pallas_call(
        functools.partial(_nsa_prompt_body, T=T), grid=(nb, nq),
        in_specs=[col(512), col(512), col(32),
                  pl.BlockSpec((T, 256), lambda b, i: (b, 0)),
                  pl.BlockSpec((256, T), lambda b, i: (0, b)),
                  pl.BlockSpec((1, n_chunk, 128), lambda b, i: (b, 0, 0)),
                  pl.BlockSpec((1, 128, n_chunk), lambda b, i: (b, 0, 0)),
                  const(ovt)],
        out_specs=col(512),
        out_shape=jax.ShapeDtypeStruct((512, nb * T), bf16),
        scratch_shapes=[pltpu.VMEM((N_KV, NBLK_PAD, TQ), f32)],
        compiler_params=pltpu.CompilerParams(dimension_semantics=("parallel", "arbitrary"),
                                             vmem_limit_bytes=VMEM_LIMIT),
        name="nsa_prompt",
    )(qct, qrt, gt, kk, vt, kc, vct, ovt)


def _nsa_sample_body(pt_ref, *refs, n_pages, ts):
    n_pg = GB * n_pages
    pages = refs[:n_pg]
    (win_ref, qc_ref, qr_ref, sm_ref, kvn_ref, wn_ref, wk_ref, pek_ref, w2k_ref, wv_ref, pev_ref, w2v_ref,
     ovt_ref, e_ref, perm_ref) = refs[n_pg:n_pg + 15]
    hb_ref, wout_stack = refs[-2:]
    wout_ref = wout_stack.at[0]
    P = n_pages * PAGE
    n_chunk = P // CMP_STRIDE
    n_blk = -(-(P + ts) // SEL_BLOCK)
    WB = win_ref.shape[4]
    chunks_per_page = PAGE // CMP_STRIDE

    qc_all = qc_ref[...].astype(f32)
    qr_all = qr_ref[...].astype(f32)
    gsig = jax.nn.sigmoid(sm_ref[...])
    rows = N_KV * R * SEG
    qpos_col = P + lax.broadcasted_iota(jnp.int32, (rows, 1), 0) % SEG
    qpos_row = P + lax.broadcasted_iota(jnp.int32, (1, 128), 1) % SEG
    new_lane = lax.broadcasted_iota(jnp.int32, (1, 128), 1)
    zpad = jnp.zeros((128 - SEG, 128), f32)

    def stack(q_all, bi):
        parts = []
        for h in range(N_HEADS):
            g = h // R
            q = q_all[bi * SEG:(bi + 1) * SEG, (h // 2) * 128:(h // 2 + 1) * 128]
            if h % 2 != g:
                q = pltpu.roll(q, HEAD_DIM, axis=1)
            parts.append(jnp.where(new_lane // HEAD_DIM == g, q, 0.0))
        return jnp.concatenate(parts, axis=0).astype(bf16)

    kwpos = jnp.concatenate([P - WB + lax.broadcasted_iota(jnp.int32, (1, WB), 1), P + new_lane], axis=1)
    validw = jnp.concatenate([jnp.full((1, WB), True), new_lane < SEG], axis=1)
    dpos = qpos_col - kwpos
    maskw = (dpos >= 0) & (dpos < WINDOW) & (kwpos >= 0) & validw
    pre = []
    for bi in range(GB):
        rs8 = slice(bi * SEG, (bi + 1) * SEG)
        ksel_t = jnp.concatenate([pages[bi * n_pages + pp][0, 0, 2].astype(bf16) for pp in range(n_pages)], axis=1)
        vsel_t = jnp.concatenate([pages[bi * n_pages + pp][0, 0, 3].astype(bf16) for pp in range(n_pages)], axis=1)
        knew = jnp.concatenate([kvn_ref[rs8, 256:384], zpad], axis=0).astype(bf16)
        vnew = jnp.concatenate([kvn_ref[rs8, 384:512], zpad], axis=0).astype(bf16)
        kwnew_f = jnp.concatenate([wn_ref[rs8, 0:128], zpad], axis=0)
        vwnew_f = jnp.concatenate([wn_ref[rs8, 128:256], zpad], axis=0)
        qr = stack(qr_all, bi)
        s_sel = jnp.concatenate([jnp.dot(qr, ksel_t, preferred_element_type=f32),
                                 lax.dot_general(qr, knew, NT, preferred_element_type=f32)], axis=1)
        s_w = jnp.concatenate([jnp.dot(qr, win_ref[0, bi, 0].astype(bf16), preferred_element_type=f32),
                               lax.dot_general(qr, kwnew_f.astype(bf16), NT, preferred_element_type=f32)], axis=1)
        p_w = _softmax_rows(s_w, maskw).astype(bf16)
        o_w = (lax.dot_general(p_w[:, 0:WB], win_ref[0, bi, 1].astype(bf16), NT, preferred_element_type=f32)
               + jnp.dot(p_w[:, WB:WB + 128], vwnew_f.astype(bf16), preferred_element_type=f32))
        pre.append((s_sel, o_w, vsel_t, vnew, kwnew_f, vwnew_f))

    xt_stack = jnp.concatenate([pages[j][0, 0, c].astype(bf16) for j in range(n_pg) for c in (0, 1)], axis=0)
    y_all = lax.dot_general(perm_ref[...], xt_stack, NT, preferred_element_type=f32)

    def flat(c):
        return jnp.concatenate(
            [jnp.concatenate([y_all[l * chunks_per_page:(l + 1) * chunks_per_page, (2 * j + c) * 128:(2 * j + c + 1) * 128]
                              for j in range(n_pg)], axis=0)
             for l in range(CMP_STRIDE)], axis=1).astype(bf16)

    kc_all = _compress_rows(flat(0), wk_ref, pek_ref, w2k_ref).astype(bf16)
    vc_all = _compress_rows(flat(1), wv_ref, pev_ref, w2v_ref).astype(bf16)

    cmp_end = lax.broadcasted_iota(jnp.int32, (1, n_chunk), 1) * CMP_STRIDE + (CMP_LEN - 1)
    o_cs, psums = [], []
    for bi in range(GB):
        s = lax.dot_general(stack(qc_all, bi), kc_all[bi * n_chunk:(bi + 1) * n_chunk], NT,
                            preferred_element_type=f32)
        p = _softmax_rows(s, cmp_end <= qpos_col)
        o_cs.append(jnp.dot(p.astype(bf16), vc_all[bi * n_chunk:(bi + 1) * n_chunk], preferred_element_type=f32))
        for g in range(N_KV):
            pg = p[g * R * SEG:(g + 1) * R * SEG]
            psums.append(pg[0:SEG] + pg[SEG:2 * SEG] + pg[2 * SEG:3 * SEG] + pg[3 * SEG:4 * SEG])
    psum_all = jnp.concatenate(psums + [jnp.zeros((128 - SEG * GB * N_KV, n_chunk), f32)], axis=0)
    imp_t = lax.dot_general(ovt_ref[...], psum_all, NT, preferred_element_type=f32, precision=HIGHEST)
    sel_all = _transpose_sel(_select_blocks(imp_t, qpos_row, n_blk)).astype(bf16)
    mk_all = jnp.dot(sel_all, e_ref[...], preferred_element_type=f32)

    out_rows = []
    for bi in range(GB):
        rs8 = slice(bi * SEG, (bi + 1) * SEG)
        s_sel, o_w, vsel_t, vnew, kwnew_f, vwnew_f = pre[bi]
        o_c = o_cs[bi]
        mk = jnp.concatenate(
            [mk_all[(bi * N_KV + g) * SEG:(bi * N_KV + g + 1) * SEG] for g in range(N_KV) for _ in range(R)], axis=0) > 0.5
        new_ok = (P + new_lane <= qpos_col) & (new_lane < SEG)
        p_s = _softmax_rows(s_sel, jnp.concatenate([mk[:, 0:P], mk[:, P:P + 128] & new_ok], axis=1)).astype(bf16)
        o_s = (lax.dot_general(p_s[:, 0:P], vsel_t, NT, preferred_element_type=f32)
               + jnp.dot(p_s[:, P:P + 128], vnew, preferred_element_type=f32))
        gs = gsig[rs8]
        heads = [None] * N_HEADS
        for h in range(N_HEADS):
            rr = slice(h * SEG, (h + 1) * SEG)
            c0 = 8 + 3 * h
            out = gs[:, c0:c0 + 1] * o_c[rr] + gs[:, c0 + 1:c0 + 2] * o_s[rr] + gs[:, c0 + 2:c0 + 3] * o_w[rr]
            if h % 2 != h // R:
                out = pltpu.roll(out, HEAD_DIM, axis=1)
            heads[h] = out
        out_rows.append(jnp.concatenate(
            [jnp.where(new_lane < HEAD_DIM, heads[2 * j], heads[2 * j + 1]) for j in range(N_HEADS // 2)], axis=1))
        for kv, new_f in ((0, kwnew_f), (1, vwnew_f)):
            old = pltpu.roll(win_ref[0, bi, kv], WB - ts, axis=1)
            new_t = pltpu.roll(new_f.T, 128 - ts, axis=1)
            wout_ref[bi, kv, :, 0:WB - 128] = old[:, 0:WB - 128]
            wout_ref[bi, kv, :, WB - 128:WB] = jnp.where(new_lane >= 128 - ts, new_t, old[:, WB - 128:WB])
    hb_ref[...] = jnp.concatenate(out_rows, axis=0).astype(bf16)


def _nsa_sample(page_table, cache_t, cwin_t, qc, qr, sm, kvrow, winrow, cwk, cwv, win_stack, *, layer, row0, ts):
    nseq, n_pages = page_table.shape
    depth = cwin_t.shape[0]
    P = n_pages * PAGE
    WB = cwin_t.shape[4]
    n_chunk = P // CMP_STRIDE
    ovt = _sel_overlap_t(n_chunk, -(-(P + ts) // SEL_BLOCK))
    e = _block_expand(P + 128)
    r = np.arange(PAGE)
    perm = jnp.asarray(np.arange(PAGE)[None, :] == (CMP_STRIDE * (r % (PAGE // CMP_STRIDE)) + r // (PAGE // CMP_STRIDE))[:, None],
                       dtype=bf16)
    b0 = row0 // (GB * SEG)

    def page_map(i, pt, *, bi, p):
        return (layer, pt[i * GB + bi, p], 0, 0, 0)

    page_specs = [pl.BlockSpec((1, 1, 4, 128, PAGE), functools.partial(page_map, bi=bi, p=p))
                  for bi in range(GB) for p in range(n_pages)]
    row = lambda w: pl.BlockSpec((GB * SEG, w), lambda i, pt: (b0 + i, 0))
    const = lambda a: pl.BlockSpec(a.shape, lambda i, pt: (0,) * a.ndim)
    extra_in, extra_spec, win_shape = _stacked_out(win_stack, depth, (nseq, 2, 128, WB))
    args = (page_table, *([cache_t] * (GB * n_pages)), cwin_t, qc, qr, sm, kvrow, winrow, *cwk, *cwv, ovt, e, perm)
    grid_spec = pltpu.PrefetchScalarGridSpec(
        num_scalar_prefetch=1, grid=(nseq // GB,),
        in_specs=page_specs + [pl.BlockSpec((1, GB, 2, 128, WB), lambda i, pt: (layer, i, 0, 0, 0)),
                               row(N_WIDTH), row(N_WIDTH), row(128), row(512), row(256)]
        + [const(a) for a in (*cwk, *cwv, ovt, e, perm)] + extra_spec,
        out_specs=[pl.BlockSpec((GB * SEG, N_WIDTH), lambda i, pt: (i, 0)),
                   pl.BlockSpec((1, GB, 2, 128, WB), lambda i, pt: (layer, i, 0, 0, 0))])
    return pl.pallas_call(
        functools.partial(_nsa_sample_body, n_pages=n_pages, ts=ts), grid_spec=grid_spec,
        out_shape=[jax.ShapeDtypeStruct((nseq * SEG, N_WIDTH), bf16), win_shape],
        input_output_aliases={len(args): 1} if extra_in else {},
        compiler_params=pltpu.CompilerParams(dimension_semantics=("parallel",), vmem_limit_bytes=VMEM_LIMIT),
        name="nsa_sample",
    )(*args, *extra_in)


def kernel(x_prompt, x_sample, cache_kv_pages, page_table, cache_win, state_mlstm_C, state_mlstm_n, state_mlstm_m, state_mlstm_conv, norm1_g, w_in, b_gates, conv_w, conv_b, w_mq, w_mk, mlstm_norm_g, cmp_pe_k, cmp_pe_v, cmp_w1_k, cmp_w2_k, cmp_w1_v, cmp_w2_v, w_branch_a, w_branch_b, w_out, norm2_g, w_ffn_up, w_ffn_down, final_norm_g):
    Bp, Tp, _ = x_prompt.shape
    Bs, Ts, _ = x_sample.shape
    depth = w_in.shape[0]
    n_pool, page = cache_kv_pages.shape[1:3]
    n_pages = page_table.shape[1]
    past_len = n_pages * page
    wb_len = cache_win.shape[2]
    mp = Bp * Tp
    ms = Bs * SAMPLE_PAD
    assert page == PAGE and M_CONV - 1 <= Ts <= SAMPLE_PAD and Tp % MLSTM_CHUNK == 0 and Tp >= WINDOW + TQ
    assert mp % (SEG * SEQ_PER_STEP) == 0 and Bs % SEQ_PER_STEP == 0 and Bs % GB == 0

    pad_seq = lambda a: jnp.pad(a, ((0, 0), (0, SAMPLE_PAD - a.shape[1]), (0, 0)))
    xs = (x_prompt.reshape(mp, D_MODEL), pad_seq(x_sample).reshape(ms, D_MODEL))
    tabs = _rope_tables(Tp, past_len, ROW_TILE * (ms // ROW_TILE))
    w_all = _in_proj_weights(w_in)
    w_up, w_down = w_ffn_up.astype(bf16), w_ffn_down.astype(bf16)
    cache_t = jnp.transpose(cache_kv_pages, (0, 1, 3, 4, 5, 2)).reshape(depth, n_pool, 4, KV_W, PAGE)
    cwin_t = jnp.transpose(cache_win, (0, 1, 3, 4, 5, 2)).reshape(depth, Bs, 2, KV_W, wb_len)

    outs = {k: [] for k in ("p_win", "p_C", "p_n", "p_m", "p_conv", "s_kv", "s_n", "s_m", "s_conv")}
    c_stack = win_stack = kvt_stack = None
    y_parts = None
    for l in range(depth):
        w_t = jnp.concatenate([w_in[l][:, O_IG:O_IG + 8], w_in[l][:, O_NG:O_NG + 24]], axis=1).T.astype(bf16)
        xm, sm, smt, qc, qr, qct, qrt, kvrow, winrow, kk, vt, gates, kvt_stack = _in_proj(
            xs, norm1_g[l].reshape(1, D_MODEL), w_all, w_t, tabs, Tp, kvt_stack, layer=l, depth=depth)

        wq, wk = w_mq[l].astype(bf16), w_mk[l].astype(bf16)
        ha_p, c_p, n_p, m_p, cv_p = _mlstm_prompt(xm, sm, smt, conv_w[l], conv_b[l], wq, wk, b_gates[l], mlstm_norm_g[l],
                                                   nb=Bp, T=Tp, L=MLSTM_CHUNK)
        tail = jnp.pad(state_mlstm_conv[l], ((0, 0), (0, SEG - (M_CONV - 1)), (0, 0))).reshape(ms, M_WIDTH)
        n0 = jnp.pad(state_mlstm_n[l], ((0, 0), (0, 8 - M_HEADS), (0, 0)))
        m0 = jnp.broadcast_to(jnp.pad(state_mlstm_m[l], ((0, 0), (0, 128 - M_HEADS)))[:, None, :], (Bs, SEG, 128))
        ha_s, c_stack, n_s, mt_s = _mlstm_sample(
            xm, sm, smt, tail, state_mlstm_C, n0, m0.reshape(ms, 128), conv_w[l], conv_b[l], wq, wk, b_gates[l],
            mlstm_norm_g[l], c_stack, layer=l, depth=depth, row0=mp, nseq=Bs, n_valid=Ts)

        cwk = _compress_weights(cmp_pe_k[l], cmp_w1_k[l], cmp_w2_k[l])
        cwv = _compress_weights(cmp_pe_v[l], cmp_w1_v[l], cmp_w2_v[l])
        kc, vct = _compress_prompt(kvrow, cwk, cwv, nb=Bp, T=Tp)
        hbt_p = _nsa_prompt(qct, qrt, smt, kk, vt, kc, vct, nb=Bp, T=Tp)
        hb_s, win_stack = _nsa_sample(page_table, cache_t, cwin_t, qc, qr, sm, kvrow, winrow, cwk, cwv, win_stack,
                                      layer=l, row0=mp, ts=Ts)

        x_all, *y_parts = _mix_ffn(
            xs, (ha_p, ha_s), (hbt_p, hb_s.T), gates, w_branch_a[l].astype(bf16), w_branch_b[l].astype(bf16),
            w_out[l].astype(bf16), norm2_g[l].reshape(1, D_MODEL), w_up, w_down, final_norm_g.reshape(1, D_MODEL),
            layer=l, n_prompt_rows=mp, final=l == depth - 1)
        xs = (x_all,)

        sample_rows = lambda a: a[mp:].reshape(Bs, SAMPLE_PAD, a.shape[1])
        wp = min(WINDOW, Tp)
        outs["p_win"].append(winrow[:mp].reshape(Bp, Tp, 2 * KV_W)[:, -wp:].reshape(Bp, wp, 2, N_KV, HEAD_DIM))
        outs["p_C"].append(c_p)
        outs["p_n"].append(n_p[:, :M_HEADS])
        outs["p_m"].append(m_p[:, 0, :M_HEADS])
        outs["p_conv"].append(cv_p[:, 8 - (M_CONV - 1):])
        outs["s_kv"].append(sample_rows(kvrow)[:, :Ts].reshape(Bs, Ts, 4, N_KV, HEAD_DIM))
        outs["s_n"].append(n_s[:, :M_HEADS])
        outs["s_m"].append(mt_s.reshape(Bs, SEG, 128)[:, SEG - 1, :M_HEADS])
        outs["s_conv"].append(sample_rows(xm)[:, Ts - (M_CONV - 1):Ts, :M_WIDTH])

    s_win = jnp.transpose(win_stack.reshape(depth, Bs, 2, N_KV, HEAD_DIM, wb_len), (0, 1, 5, 2, 3, 4))
    y_prompt = y_parts[0].reshape(Bp, Tp, D_MODEL)
    y_sample = y_parts[1].reshape(Bs, SAMPLE_PAD, D_MODEL)[:, :Ts]
    st = lambda k: jnp.stack(outs[k])
    p_kv = jnp.transpose(kvt_stack.reshape(depth, Bp, 4, N_KV, HEAD_DIM, Tp), (0, 1, 5, 2, 3, 4))
    return (y_prompt, y_sample, p_kv, st("p_win"), st("p_C"), st("p_n"), st("p_m"), st("p_conv"),
            st("s_kv"), s_win, c_stack, st("s_n"), st("s_m"), st("s_conv"))
```

```python
import functools

import jax
import jax.numpy as jnp
import numpy as np
from jax import lax
from jax.experimental import pallas as pl
from jax.experimental.pallas import tpu as pltpu

f32 = jnp.float32
bf16 = jnp.bfloat16

D_MODEL = 1024
M_HEADS = 4
M_WIDTH = 512
M_HEAD_DIM = 128
M_CONV = 4
MLSTM_CHUNK = 256
HEAD_DIM = 64
N_WIDTH = 512
N_HEADS = 8
N_KV = 2
KV_W = 128
CMP_STRIDE = 16
CMP_LEN = 32
SEL_BLOCK = 64
SEL_TOP = 16
WINDOW = 512
ROT_DIM = 16
ROPE_THETA = 500000.0
D_FF = 2816
EPS = 1e-6
NEG = -1e30
TINY = 1e-30
FORCE_SCORE = 1e9

SAMPLE_PAD = 8
SEG = SAMPLE_PAD
SEQ_PER_STEP = 16
GB = 2
PAGE = 128
HD = M_HEAD_DIM
R = N_HEADS // N_KV
CMP_HID = 128
TQ = 128
TK = 1024
KV_SUB = 1
NBLK_PAD = 64
HIGHEST = lax.Precision.HIGHEST
NT = (((1,), (1,)), ((), ()))
LOG2E = 1.4426950408889634
ROW_TILE = 256
FF_CHUNK = 2816
VMEM_LIMIT = 56 * 1024 * 1024

C_XM, C_Q, C_KV, C_WIN, C_GATE, C_SM = 0, 1536, 2048, 2560, 2816, 4864
C_TOTAL = 4992
O_U, O_V, O_O, O_IG, O_FG, O_Q, O_KV, O_NG, O_GA, O_GB = 0, 512, 1024, 1536, 1540, 1544, 2056, 2824, 2848, 3872
IN_WIDTH = 4896


def _in_proj_weights(w):
    lead = w.shape[:-1]
    parts = [w[..., O_U:O_IG], w[..., O_Q:O_Q + N_WIDTH] * (HEAD_DIM ** -0.5), w[..., O_KV:O_KV + 6 * KV_W],
             w[..., O_GA:O_GA + 2 * D_MODEL], w[..., O_IG:O_IG + 8], w[..., O_NG:O_NG + 24],
             jnp.zeros(lead + (128 - 32,), f32)]
    out = jnp.concatenate(parts, axis=-1).astype(bf16)
    assert out.shape[-1] == C_TOTAL
    return out


def _rope_tables(seq, past_len, n_sample_rows):
    half = ROT_DIM // 2
    inv = ROPE_THETA ** (-jnp.arange(half, dtype=f32) / half)
    pos = jnp.concatenate([jnp.arange(seq), past_len + (jnp.arange(n_sample_rows) % SAMPLE_PAD)]).astype(f32)
    ang = pos[:, None] * inv[None, :]
    cos8, sin8 = jnp.cos(ang), jnp.sin(ang)
    n = pos.shape[0]
    one = jnp.ones((n, 64 - ROT_DIM), f32)
    zero = jnp.zeros((n, 64 - ROT_DIM), f32)
    z8 = jnp.zeros((n, half), f32)
    cos = jnp.concatenate([cos8, cos8, one], axis=1)
    sa = jnp.concatenate([-sin8, z8, zero], axis=1)
    sb = jnp.concatenate([z8, sin8, zero], axis=1)
    tile2 = lambda a: jnp.concatenate([a, a], axis=1)
    return tile2(cos), tile2(sa), tile2(sb)


def _rms(x, g):
    return x * lax.rsqrt(jnp.mean(x * x, axis=-1, keepdims=True) + EPS) * g


def _in_proj_body(*refs, n_x, n_sample_tiles):
    x_refs = refs[:n_x]
    g_ref, w_ref, wt_ref, cos_ref, sa_ref, sb_ref = refs[n_x:n_x + 6]
    (xm_ref, sm_ref, smt_ref, qc_ref, qr_ref, qct_ref, qrt_ref, kv_ref, win_ref, kk_ref, vt_ref, gate_ref,
     kvt_stack) = refs[-13:]
    x = x_refs[0][...]
    if n_x == 2:
        x = jnp.where(pl.program_id(0) < n_sample_tiles, x_refs[1][...], x)
    hb = _rms(x, g_ref[...]).astype(bf16)
    cos, sa, sb = cos_ref[...], sa_ref[...], sb_ref[...]

    def rope(v):
        return v * cos + pltpu.roll(v, 128 - ROT_DIM // 2, axis=1) * sa + pltpu.roll(v, ROT_DIM // 2, axis=1) * sb

    def proj(c0, n):
        return jnp.dot(hb, w_ref[0, :, c0:c0 + n], preferred_element_type=f32)

    xm_ref[...] = proj(C_XM, 1536)
    sm_ref[...] = proj(C_SM, 128)
    smt_ref[...] = lax.dot_general(wt_ref[...], hb, NT, preferred_element_type=f32)
    qf = proj(C_Q, N_WIDTH) * LOG2E
    for j in range(N_HEADS // 2):
        sl = slice(j * 128, (j + 1) * 128)
        q = qf[:, sl]
        qrot = rope(q)
        qc_ref[:, sl] = q.astype(bf16)
        qr_ref[:, sl] = qrot.astype(bf16)
        qct_ref[sl, :] = q.T.astype(bf16)
        qrt_ref[sl, :] = qrot.T.astype(bf16)
    kv = proj(C_KV, 512)
    ksel = rope(kv[:, 256:384])
    kv_ref[:, 0:256] = kv[:, 0:256]
    kv_ref[:, 256:384] = ksel
    kv_ref[:, 384:512] = kv[:, 384:512]
    kvt_stack[0, 0, 0:256, :] = kv[:, 0:256].T
    kvt_stack[0, 0, 256:384, :] = ksel.T
    kvt_stack[0, 0, 384:512, :] = kv[:, 384:512].T
    win = proj(C_WIN, 256)
    kwin = rope(win[:, 0:128])
    win_ref[:, 0:128] = kwin
    win_ref[:, 128:256] = win[:, 128:256]
    kk_ref[:, 0:128] = ksel.astype(bf16)
    kk_ref[:, 128:256] = kwin.astype(bf16)
    vt_ref[0:128, :] = kv[:, 384:512].T.astype(bf16)
    vt_ref[128:256, :] = win[:, 128:256].T.astype(bf16)
    gate_ref[...] = proj(C_GATE, 2048)


def _in_proj(xs, norm_g, w_all, w_t, tabs, seq, kvt_stack, *, layer, depth):
    m = sum(x.shape[0] for x in xs)
    n_prompt_tiles_per_seq = seq // ROW_TILE
    n_prompt_tiles = (m - (tabs[0].shape[0] - seq)) // ROW_TILE
    n_seq = n_prompt_tiles // n_prompt_tiles_per_seq
    n_tiles = m // ROW_TILE
    n_sample_tiles = n_tiles - n_prompt_tiles

    def tile(i):
        return (i + n_prompt_tiles) % n_tiles

    def kvt_map(i):
        j = jnp.maximum(i - n_sample_tiles, 0)
        return (layer, j // n_prompt_tiles_per_seq, 0, j % n_prompt_tiles_per_seq)

    def tab_map(i):
        t = tile(i)
        return (jnp.where(t < n_prompt_tiles, t % n_prompt_tiles_per_seq, n_prompt_tiles_per_seq + t - n_prompt_tiles), 0)

    row = lambda w: pl.BlockSpec((ROW_TILE, w), lambda i: (tile(i), 0))
    const = lambda s: pl.BlockSpec(s, lambda i: (0, 0), pipeline_mode=pl.Buffered(1))
    tab = pl.BlockSpec((ROW_TILE, 128), tab_map)
    widths = (1536, 128, -32, 512, 512, -512, -512, 512, 256, 256, -256, 2048)
    dts = (f32, f32, f32, bf16, bf16, bf16, bf16, f32, f32, bf16, bf16, f32)
    out_shape = [jax.ShapeDtypeStruct((-w, m) if w < 0 else (m, w), d) for w, d in zip(widths, dts)]
    out_specs = [pl.BlockSpec((-w, ROW_TILE), lambda i: (0, tile(i))) if w < 0 else row(w) for w in widths]
    extra_in, extra_spec, kvt_shape = _stacked_out(kvt_stack, depth, (n_seq, 4 * KV_W, seq))
    if len(xs) == 1:
        x_specs = [row(D_MODEL)]
    else:
        x_specs = [pl.BlockSpec((ROW_TILE, D_MODEL), lambda i: (jnp.maximum(i - n_sample_tiles, 0), 0)),
                   pl.BlockSpec((ROW_TILE, D_MODEL), lambda i: (jnp.minimum(i, n_sample_tiles - 1), 0))]
    args = (*xs, norm_g, w_all, w_t, *tabs)
    return pl.pallas_call(
        functools.partial(_in_proj_body, n_x=len(xs), n_sample_tiles=n_sample_tiles), grid=(n_tiles,),
        in_specs=x_specs + [const((1, D_MODEL)),
                            pl.BlockSpec((1, D_MODEL, C_TOTAL), lambda i: (layer, 0, 0), pipeline_mode=pl.Buffered(1)),
                            const((32, D_MODEL)), tab, tab, tab]
        + extra_spec,
        out_specs=out_specs + [pl.BlockSpec((1, 1, 4 * KV_W, ROW_TILE), kvt_map)],
        out_shape=out_shape + [kvt_shape], name="in_proj",
        input_output_aliases={len(args): len(out_shape)} if extra_in else {},
        compiler_params=pltpu.CompilerParams(dimension_semantics=("arbitrary",), vmem_limit_bytes=VMEM_LIMIT),
    )(*args, *extra_in)


def _mix_ffn_body(*refs, n_x, n_prompt_tiles, final):
    x_refs = refs[:n_x]
    hap_ref, has_ref, hbp_ref, hbs_ref, gate_ref, wa_ref, wb_ref, wo_ref, g2_ref, wup_ref, wdn_ref, gf_ref = (
        refs[n_x:n_x + 12])
    xo_ref, *y_refs = refs[n_x + 12:]
    in_prompt = pl.program_id(0) < n_prompt_tiles
    pick = lambda p_ref, s_ref: jnp.where(in_prompt, p_ref[...], s_ref[...])
    x = x_refs[0][...] if n_x == 1 else pick(*x_refs)
    a = jnp.dot(pick(hap_ref, has_ref).astype(bf16), wa_ref[...], preferred_element_type=f32)
    b = lax.dot_general(pick(hbp_ref, hbs_ref), wb_ref[...], (((0,), (0,)), ((), ())),
                        preferred_element_type=f32)
    merged = jax.nn.sigmoid(gate_ref[:, 0:D_MODEL]) * a + jax.nn.sigmoid(gate_ref[:, D_MODEL:2 * D_MODEL]) * b
    x1 = x + jnp.dot(merged.astype(bf16), wo_ref[...], preferred_element_type=f32)
    hn = _rms(x1, g2_ref[...]).astype(bf16)
    acc = x1
    for c in range(D_FF // FF_CHUNK):
        lo = c * FF_CHUNK
        g = jnp.dot(hn, wup_ref[0, :, lo:lo + FF_CHUNK], preferred_element_type=f32)
        u = jnp.dot(hn, wup_ref[0, :, D_FF + lo:D_FF + lo + FF_CHUNK], preferred_element_type=f32)
        act = (g * jax.nn.sigmoid(g) * u).astype(bf16)
        acc = acc + jnp.dot(act, wdn_ref[0, lo:lo + FF_CHUNK, :], preferred_element_type=f32)
    xo_ref[...] = acc
    if final:
        y = _rms(acc, gf_ref[...])
        yp_ref, ys_ref = y_refs

        @pl.when(in_prompt)
        def _():
            yp_ref[...] = y

        @pl.when(jnp.logical_not(in_prompt))
        def _():
            ys_ref[...] = y


def _mix_ffn(xs, h_a, h_bt, gates, wa, wb, wo, g2, wup, wdn, gf, *, layer, n_prompt_rows, final):
    m = gates.shape[0]
    npt = n_prompt_rows // ROW_TILE
    row = lambda w: pl.BlockSpec((ROW_TILE, w), lambda i: (i, 0))
    p_row = lambda w: pl.BlockSpec((ROW_TILE, w), lambda i: (jnp.minimum(i, npt - 1), 0))
    s_row = lambda w: pl.BlockSpec((ROW_TILE, w), lambda i: (jnp.maximum(i - npt, 0), 0))
    p_col = lambda h: pl.BlockSpec((h, ROW_TILE), lambda i: (0, jnp.minimum(i, npt - 1)))
    s_col = lambda h: pl.BlockSpec((h, ROW_TILE), lambda i: (0, jnp.maximum(i - npt, 0)))
    const = lambda s: pl.BlockSpec(s, lambda i: (0, 0), pipeline_mode=pl.Buffered(1))
    layer_w = lambda a: pl.BlockSpec((1,) + a.shape[1:], lambda i: (layer, 0, 0), pipeline_mode=pl.Buffered(1))
    out_specs = [row(D_MODEL)]
    out_shape = [jax.ShapeDtypeStruct((m, D_MODEL), f32)]
    if final:
        out_specs += [p_row(D_MODEL), s_row(D_MODEL)]
        out_shape += [jax.ShapeDtypeStruct((n_prompt_rows, D_MODEL), f32),
                      jax.ShapeDtypeStruct((m - n_prompt_rows, D_MODEL), f32)]
    x_specs = [row(D_MODEL)] if len(xs) == 1 else [p_row(D_MODEL), s_row(D_MODEL)]
    return pl.pallas_call(
        functools.partial(_mix_ffn_body, n_x=len(xs), n_prompt_tiles=npt, final=final), grid=(m // ROW_TILE,),
        in_specs=x_specs + [p_row(M_WIDTH), s_row(M_WIDTH), p_col(N_WIDTH), s_col(N_WIDTH), row(2 * D_MODEL),
                            const(wa.shape), const(wb.shape), const(wo.shape), const((1, D_MODEL)),
                            layer_w(wup), layer_w(wdn), const((1, D_MODEL))],
        out_specs=out_specs, out_shape=out_shape, name="mix_ffn",
        compiler_params=pltpu.CompilerParams(dimension_semantics=("arbitrary",), vmem_limit_bytes=VMEM_LIMIT),
    )(*xs, *h_a, *h_bt, gates, wa, wb, wo, g2, wup, wdn, gf)


def _log_sigmoid(x):
    return jnp.minimum(x, 0.0) - jnp.log1p(jnp.exp(-jnp.abs(x)))


def _seg_scan(x, axis, seg, op, fill):
    idx = lax.broadcasted_iota(jnp.int32, x.shape, axis) % seg
    d = 1
    while d < seg:
        x = op(x, jnp.where(idx >= d, pltpu.roll(x, d, axis=axis), fill))
        d *= 2
    return x


def _conv_silu(u, tail, conv_w, conv_b, seg):
    L = u.shape[0]
    row = lax.broadcasted_iota(jnp.int32, u.shape, 0) % seg
    y = conv_b + u * conv_w[3:4, :]
    for k in (1, 2, 3):
        prev = tail if k == 3 else pltpu.roll(tail, L - (3 - k), axis=0)
        sh = jnp.where(row < k, prev, pltpu.roll(u, k, axis=0))
        y = y + sh * conv_w[3 - k:4 - k, :]
    return y * jax.nn.sigmoid(y)


def _gate_columns(sm, smt, bg_row, bg_col, m_vec, seg, n_valid):
    G = sm + bg_row
    lf = _log_sigmoid(G)
    ipre = G
    if n_valid < seg:
        rowc = lax.broadcasted_iota(jnp.int32, G.shape, 0) % seg
        lf = jnp.where(rowc < n_valid, lf, 0.0)
        ipre = jnp.where(rowc < n_valid, G, NEG)
    F = _seg_scan(lf, 0, seg, jnp.add, 0.0)
    F_al = pltpu.roll(F, 128 - M_HEADS, axis=1)
    a_col = ipre - F_al
    g_col = jnp.maximum(m_vec, _seg_scan(a_col, 0, seg, jnp.maximum, NEG))
    mt_col = F_al + g_col
    Gr = smt + bg_col
    lfr = _log_sigmoid(Gr)
    ir = Gr
    if n_valid < seg:
        lanec = lax.broadcasted_iota(jnp.int32, Gr.shape, 1) % seg
        lfr = jnp.where(lanec < n_valid, lfr, 0.0)
        ir = jnp.where(lanec < n_valid, Gr, NEG)
    Fr = _seg_scan(lfr, 1, seg, jnp.add, 0.0)
    a_row = ir - pltpu.roll(Fr, M_HEADS, axis=0)
    return a_col, g_col, mt_col, a_row


def _head_norm_gate(hh, ng, o):
    return hh * lax.rsqrt(jnp.mean(hh * hh, axis=-1, keepdims=True) + EPS) * ng * jax.nn.sigmoid(o)


def _mlstm_prompt_body(xm_ref, sm_ref, smt_ref, cw_ref, cb_ref, wq_ref, wk_ref, bgr_ref, bgc_ref, ng_ref,
                       ha_ref, c_out, n_out, m_out, conv_out, c_scr, m_scr, tail_scr, *, L):
    c_idx = pl.program_id(1)

    @pl.when(c_idx == 0)
    def _():
        c_scr[...] = jnp.zeros_like(c_scr)
        m_scr[...] = jnp.zeros_like(m_scr)
        tail_scr[...] = jnp.zeros_like(tail_scr)

    u = xm_ref[:, 0:M_WIDTH]
    tail = jnp.concatenate([tail_scr[...], jnp.zeros((L - 8, M_WIDTH), f32)], axis=0)
    cact = _conv_silu(u, tail, cw_ref[...], cb_ref[...], L)
    tail_scr[...] = pltpu.roll(u[L - 8:L, :], 3, axis=0)
    conv_out[0] = u[L - 8:L, :]

    m_vec = m_scr[0:1, :]
    a_col, g_col, mt_col, a_row = _gate_columns(sm_ref[...], smt_ref[...], bgr_ref[...], bgc_ref[...], m_vec, L, L)
    inter_col = jnp.exp(m_vec - g_col)
    floor_col = jnp.exp(-mt_col)
    g_last = g_col[L - 1:L, :]
    w_col = jnp.exp(a_col - g_last)
    decay = jnp.exp(m_vec - g_last)
    m_new = mt_col[L - 1:L, :]
    ti = lax.broadcasted_iota(jnp.int32, (L, L), 0)
    si = lax.broadcasted_iota(jnp.int32, (L, L), 1)
    causal = si <= ti
    scale = HD ** -0.5
    ones_v = jnp.ones((L, HD), bf16)
    for h in range(M_HEADS):
        sl = slice(h * HD, (h + 1) * HD)
        ch = cact[:, sl].astype(bf16)
        qh = jnp.dot(ch, wq_ref[h], preferred_element_type=f32)
        kh = jnp.dot(ch, wk_ref[h], preferred_element_type=f32) * scale
        vh = jnp.concatenate([xm_ref[:, M_WIDTH + h * HD:M_WIDTH + (h + 1) * HD].astype(bf16), ones_v], axis=1)
        qb = qh.astype(bf16)
        D = jnp.exp(jnp.where(causal, a_row[h:h + 1, :] - g_col[:, h:h + 1], NEG))
        S = lax.dot_general(qb, kh.astype(bf16), NT, preferred_element_type=f32) * D
        C = c_scr[h]
        ic = inter_col[:, h:h + 1]
        both = jnp.dot(S.astype(bf16), vh, preferred_element_type=f32) + ic * jnp.dot(
            qb, C.astype(bf16), preferred_element_type=f32)
        hh = both[:, 0:HD] / jnp.maximum(jnp.abs(both[:, HD:2 * HD]), floor_col[:, h:h + 1])
        o = xm_ref[:, 2 * M_WIDTH + h * HD:2 * M_WIDTH + (h + 1) * HD]
        ha_ref[:, sl] = _head_norm_gate(hh, ng_ref[:, sl], o)
        kw = kh * w_col[:, h:h + 1]
        c_scr[h] = decay[:, h:h + 1] * C + lax.dot_general(kw.astype(bf16), vh, (((0,), (0,)), ((), ())),
                                                           preferred_element_type=f32)
    m_scr[0:1, :] = m_new

    @pl.when(c_idx == pl.num_programs(1) - 1)
    def _():
        c_out[0] = c_scr[:, :, 0:HD]
        for h in range(M_HEADS):
            n_out[0, h:h + 1, :] = c_scr[h][:, HD:2 * HD].T[0:1, :]
        n_out[0, M_HEADS:8, :] = jnp.zeros((8 - M_HEADS, HD), f32)
        m_out[0] = m_scr[...]


def _mlstm_prompt(xm, sm, smt, conv_w, conv_b, wq, wk, b_gates, norm_g, *, nb, T, L):
    nc = T // L
    bg_row = jnp.zeros((1, 128), f32).at[0, :8].set(b_gates)
    bg_col = b_gates.reshape(8, 1)
    const = lambda s: pl.BlockSpec(s, lambda b, c: (0,) * len(s))
    return pl.pallas_call(
        functools.partial(_mlstm_prompt_body, L=L), grid=(nb, nc),
        in_specs=[pl.BlockSpec((L, 3 * M_WIDTH), lambda b, c: (b * nc + c, 0)),
                  pl.BlockSpec((L, 128), lambda b, c: (b * nc + c, 0)),
                  pl.BlockSpec((8, L), lambda b, c: (0, b * nc + c)),
                  const((4, M_WIDTH)), const((1, M_WIDTH)), const((M_HEADS, HD, HD)), const((M_HEADS, HD, HD)),
                  const((1, 128)), const((8, 1)), const((1, M_WIDTH))],
        out_specs=[pl.BlockSpec((L, M_WIDTH), lambda b, c: (b * nc + c, 0)),
                   pl.BlockSpec((1, M_HEADS, HD, HD), lambda b, c: (b, 0, 0, 0)),
                   pl.BlockSpec((1, 8, HD), lambda b, c: (b, 0, 0)),
                   pl.BlockSpec((1, 8, 128), lambda b, c: (b, 0, 0)),
                   pl.BlockSpec((1, 8, M_WIDTH), lambda b, c: (b, 0, 0))],
        out_shape=[jax.ShapeDtypeStruct((nb * T, M_WIDTH), f32),
                   jax.ShapeDtypeStruct((nb, M_HEADS, HD, HD), f32),
                   jax.ShapeDtypeStruct((nb, 8, HD), f32),
                   jax.ShapeDtypeStruct((nb, 8, 128), f32),
                   jax.ShapeDtypeStruct((nb, 8, M_WIDTH), f32)],
        scratch_shapes=[pltpu.VMEM((M_HEADS, HD, 2 * HD), f32), pltpu.VMEM((8, 128), f32),
                        pltpu.VMEM((8, M_WIDTH), f32)],
        compiler_params=pltpu.CompilerParams(dimension_semantics=("parallel", "arbitrary")),
        name="mlstm_prompt",
    )(xm, sm, smt, conv_w, conv_b.reshape(1, M_WIDTH), wq, wk, bg_row, bg_col, norm_g.reshape(1, M_WIDTH))


def _mlstm_sample_body(xm_ref, sm_ref, smt_ref, tail_ref, c_in, n_in, m_in, cw_ref, cb_ref, wq_ref, wk_ref,
                       bgr_ref, bgc_ref, ng_ref, *out_refs, n_valid):
    ha_ref, c_stack, n_out, m_out = out_refs[-4:]
    c_out = c_stack.at[0]
    L = SEG * SEQ_PER_STEP
    u = xm_ref[:, 0:M_WIDTH]
    cact = _conv_silu(u, tail_ref[...], cw_ref[...], cb_ref[...], SEG)
    m_rows = m_in[...]
    a_col, g_col, mt_col, a_row = _gate_columns(sm_ref[...], smt_ref[...], bgr_ref[...], bgc_ref[...], m_rows, SEG,
                                                n_valid)
    inter_col = jnp.exp(m_rows - g_col)
    floor_col = jnp.exp(-mt_col)
    ti = lax.broadcasted_iota(jnp.int32, (L, L), 0)
    si = lax.broadcasted_iota(jnp.int32, (L, L), 1)
    mask = (si <= ti) & ((si // SEG) == (ti // SEG))
    scale = HD ** -0.5
    for h in range(M_HEADS):
        sl = slice(h * HD, (h + 1) * HD)
        ch = cact[:, sl].astype(bf16)
        qh = jnp.dot(ch, wq_ref[h], preferred_element_type=f32)
        kh = jnp.dot(ch, wk_ref[h], preferred_element_type=f32) * scale
        vh = xm_ref[:, M_WIDTH + h * HD:M_WIDTH + (h + 1) * HD].astype(bf16)
        qb = qh.astype(bf16)
        D = jnp.exp(jnp.where(mask, a_row[h:h + 1, :] - g_col[:, h:h + 1], NEG))
        S = lax.dot_general(qb, kh.astype(bf16), NT, preferred_element_type=f32) * D
        num = jnp.dot(S.astype(bf16), vh, preferred_element_type=f32)
        den = jnp.sum(S, axis=-1, keepdims=True)
        inter_num, inter_den = [], []
        for s in range(SEQ_PER_STEP):
            rs = slice(s * SEG, (s + 1) * SEG)
            C = c_in[0, s, h]
            nrow = n_in[s, h:h + 1, :]
            inter_num.append(jnp.dot(qb[rs], C.astype(bf16), preferred_element_type=f32))
            inter_den.append(jnp.sum(qh[rs] * nrow, axis=-1, keepdims=True))
            g_last = g_col[s * SEG + SEG - 1:s * SEG + SEG, h:h + 1]
            m_prev = m_rows[s * SEG:s * SEG + 1, h:h + 1]
            w = jnp.exp(a_col[rs, h:h + 1] - g_last)
            dh = jnp.exp(m_prev - g_last)
            kw = kh[rs] * w
            c_out[s, h] = dh * C + lax.dot_general(kw.astype(bf16), vh[rs], (((0,), (0,)), ((), ())),
                                                    preferred_element_type=f32)
            n_out[s, h:h + 1, :] = dh * nrow + jnp.sum(kw, axis=0, keepdims=True)
        ic = inter_col[:, h:h + 1]
        num = num + ic * jnp.concatenate(inter_num, axis=0)
        den = den + ic * jnp.concatenate(inter_den, axis=0)
        hh = num / jnp.maximum(jnp.abs(den), floor_col[:, h:h + 1])
        o = xm_ref[:, 2 * M_WIDTH + h * HD:2 * M_WIDTH + (h + 1) * HD]
        ha_ref[:, sl] = _head_norm_gate(hh, ng_ref[:, sl], o)
    for s in range(SEQ_PER_STEP):
        n_out[s, M_HEADS:8, :] = jnp.zeros((8 - M_HEADS, HD), f32)
    m_out[...] = mt_col


def _stacked_out(prev, depth, shape):
    extra_in = [] if prev is None else [prev]
    extra_spec = [] if prev is None else [pl.BlockSpec(memory_space=pl.ANY)]
    return extra_in, extra_spec, jax.ShapeDtypeStruct((depth,) + shape, f32)


def _mlstm_sample(xm, sm, smt, tail, c0, n0, m0, conv_w, conv_b, wq, wk, b_gates, norm_g, c_stack, *, layer, depth,
                  row0, nseq, n_valid):
    L = SEG * SEQ_PER_STEP
    b0 = row0 // L
    bg_row = jnp.zeros((1, 128), f32).at[0, :8].set(b_gates)
    bg_col = b_gates.reshape(8, 1)
    const = lambda s: pl.BlockSpec(s, lambda i: (0,) * len(s))
    extra_in, extra_spec, c_shape = _stacked_out(c_stack, depth, (nseq, M_HEADS, HD, HD))
    n_in = 14
    return pl.pallas_call(
        functools.partial(_mlstm_sample_body, n_valid=n_valid), grid=(nseq // SEQ_PER_STEP,),
        in_specs=[pl.BlockSpec((L, 3 * M_WIDTH), lambda i: (b0 + i, 0)),
                  pl.BlockSpec((L, 128), lambda i: (b0 + i, 0)),
                  pl.BlockSpec((8, L), lambda i: (0, b0 + i)),
                  pl.BlockSpec((L, M_WIDTH), lambda i: (i, 0)),
                  pl.BlockSpec((1, SEQ_PER_STEP, M_HEADS, HD, HD), lambda i: (layer, i, 0, 0, 0)),
                  pl.BlockSpec((SEQ_PER_STEP, 8, HD), lambda i: (i, 0, 0)),
                  pl.BlockSpec((L, 128), lambda i: (i, 0)),
                  const((4, M_WIDTH)), const((1, M_WIDTH)), const((M_HEADS, HD, HD)), const((M_HEADS, HD, HD)),
                  const((1, 128)), const((8, 1)), const((1, M_WIDTH))] + extra_spec,
        out_specs=[pl.BlockSpec((L, M_WIDTH), lambda i: (i, 0)),
                   pl.BlockSpec((1, SEQ_PER_STEP, M_HEADS, HD, HD), lambda i: (layer, i, 0, 0, 0)),
                   pl.BlockSpec((SEQ_PER_STEP, 8, HD), lambda i: (i, 0, 0)),
                   pl.BlockSpec((L, 128), lambda i: (i, 0))],
        out_shape=[jax.ShapeDtypeStruct((nseq * SEG, M_WIDTH), f32), c_shape,
                   jax.ShapeDtypeStruct((nseq, 8, HD), f32),
                   jax.ShapeDtypeStruct((nseq * SEG, 128), f32)],
        input_output_aliases={n_in: 1} if extra_in else {},
        compiler_params=pltpu.CompilerParams(dimension_semantics=("parallel",), vmem_limit_bytes=VMEM_LIMIT),
        name="mlstm_sample",
    )(xm, sm, smt, tail, c0, n0, m0, conv_w, conv_b.reshape(1, M_WIDTH), wq, wk, bg_row, bg_col,
      norm_g.reshape(1, M_WIDTH), *extra_in)


def _sel_overlap_t(n_chunk, n_blk):
    n_cmp = n_chunk - 1
    start = np.arange(n_cmp) * CMP_STRIDE
    bs = np.arange(n_blk) * SEL_BLOCK
    ov = np.minimum(start[:, None] + CMP_LEN, bs[None, :] + SEL_BLOCK) - np.maximum(start[:, None], bs[None, :])
    ov = np.clip(ov, 0, None) / CMP_LEN
    out = np.zeros((NBLK_PAD, n_chunk), np.float32)
    out[:n_blk, :n_cmp] = ov.T
    return jnp.asarray(out)


def _block_expand(n_keys):
    e = (np.arange(n_keys)[None, :] // SEL_BLOCK) == np.arange(128)[:, None]
    return jnp.asarray(e, dtype=bf16)


def _compress_weights(pe, w1, w2):
    W = jnp.zeros((16, 2, 64, 2, 2, CMP_HID), f32)
    for g in range(N_KV):
        W = W.at[:, g, :, g, 0, :].set(w1[:16])
        W = W.at[:, g, :, g, 1, :].set(w1[16:])
    W = W.reshape(2048, 512)
    peA = jnp.broadcast_to(pe[:16, None, :], (16, 2, 64)).reshape(1, 2048)
    peB = jnp.broadcast_to(pe[16:, None, :], (16, 2, 64)).reshape(1, 2048)
    PE = jnp.concatenate([peA, peB, jnp.zeros((14, 2048), f32)], axis=0)
    W2 = jnp.zeros((2, CMP_HID, 128), f32).at[0, :, 0:64].set(w2).at[1, :, 64:128].set(w2)
    return W.astype(bf16), PE.astype(bf16), W2.astype(bf16)


def _compress_rows(flat, w_ref, pe_ref, w2_ref):
    n = flat.shape[0]
    y = jnp.dot(jnp.concatenate([flat, pe_ref[...]], axis=0), w_ref[...], preferred_element_type=f32)
    c = y[n:n + 8]
    out = jnp.zeros((n, w2_ref.shape[2]), f32)
    for g in range(N_KV):
        a = y[0:n, (2 * g) * CMP_HID:(2 * g + 1) * CMP_HID] + c[0:1, (2 * g) * CMP_HID:(2 * g + 1) * CMP_HID]
        b = (y[0:n, (2 * g + 1) * CMP_HID:(2 * g + 2) * CMP_HID]
             + c[1:2, (2 * g + 1) * CMP_HID:(2 * g + 2) * CMP_HID])
        hid = a + pltpu.roll(b, n - 1, axis=0)
        out = out + jnp.dot(jax.nn.gelu(hid, approximate=True).astype(bf16), w2_ref[g], preferred_element_type=f32)
    return out


def _compress_prompt_body(xk_ref, xv_ref, wk_ref, pek_ref, w2k_ref, wv_ref, pev_ref, w2v_ref, kc_ref, vc_ref, *,
                          n_chunk):
    def flat(x_ref):
        return jnp.concatenate(
            [x_ref[pl.ds(l, n_chunk, stride=CMP_STRIDE), :].astype(bf16) for l in range(CMP_STRIDE)], axis=1)
    kc_ref[0] = _compress_rows(flat(xk_ref), wk_ref, pek_ref, w2k_ref).astype(bf16)
    vc_ref[0] = _compress_rows(flat(xv_ref), wv_ref, pev_ref, w2v_ref).T.astype(bf16)


def _compress_prompt(kvrow, cwk, cwv, *, nb, T):
    n_chunk = T // CMP_STRIDE
    const = lambda a: pl.BlockSpec(a.shape, lambda b: (0,) * a.ndim)
    return pl.pallas_call(
        functools.partial(_compress_prompt_body, n_chunk=n_chunk), grid=(nb,),
        in_specs=[pl.BlockSpec((T, 128), lambda b: (b, 0)), pl.BlockSpec((T, 128), lambda b: (b, 1))]
        + [const(a) for a in (*cwk, *cwv)],
        out_specs=[pl.BlockSpec((1, n_chunk, 128), lambda b: (b, 0, 0)),
                   pl.BlockSpec((1, 128, n_chunk), lambda b: (b, 0, 0))],
        out_shape=[jax.ShapeDtypeStruct((nb, n_chunk, 128), bf16), jax.ShapeDtypeStruct((nb, 128, n_chunk), bf16)],
        compiler_params=pltpu.CompilerParams(dimension_semantics=("parallel",), vmem_limit_bytes=VMEM_LIMIT),
        name="nsa_compress",
    )(kvrow, kvrow, *cwk, *cwv)


def _softmax_rows(s, mask):
    s = jnp.where(mask, s, NEG)
    e = jnp.exp2(s - jnp.max(s, axis=-1, keepdims=True))
    e = jnp.where(mask, e, 0.0)
    return e / jnp.maximum(jnp.sum(e, axis=-1, keepdims=True), TINY)


def _select_blocks(imp_t, qpos_row, n_blk, cnt_scr=None, live_tile=None):
    n = imp_t.shape[1]
    j = lax.broadcasted_iota(jnp.int32, imp_t.shape, 0)
    cur = qpos_row // SEL_BLOCK
    forced = (j == 0) | (j == cur) | (j == cur - 1)
    score = jnp.where(j <= cur, jnp.where(forced, FORCE_SCORE, imp_t), -1.0)
    n_tiles = -(-n_blk // 8)
    sub = lax.broadcasted_iota(jnp.int32, (8, n), 0)
    tiles = [score[8 * v:8 * v + 8] for v in range(n_tiles)]
    if n_blk % 8:
        tiles[-1] = jnp.where(sub < n_blk % 8, tiles[-1], -2.0)
    def candidates(cnts, jps):
        for jp in jps:
            sj = tiles[jp // 8][jp % 8:jp % 8 + 1, :]
            for v in range(n_tiles):
                if v > jp // 8:
                    beat = jnp.where(sj >= tiles[v], 1.0, 0.0)
                elif v < jp // 8:
                    beat = jnp.where(sj > tiles[v], 1.0, 0.0)
                else:
                    beat = jnp.where(sub > jp % 8, jnp.where(sj >= tiles[v], 1.0, 0.0),
                                     jnp.where(sj > tiles[v], 1.0, 0.0))
                cnts[v] = cnts[v] + beat
        return cnts

    zeros = [jnp.zeros((8, n), f32)] * n_tiles
    if cnt_scr is None:
        cnts = candidates(list(zeros), range(n_blk))
    else:
        cnt_scr[...] = jnp.zeros_like(cnt_scr)
        for vj in range(n_tiles):
            @pl.when(vj <= live_tile)
            def _():
                part = candidates(list(zeros), range(8 * vj, min(8 * vj + 8, n_blk)))
                for v in range(n_tiles):
                    cnt_scr[8 * v:8 * v + 8, :] += part[v]
        cnts = [cnt_scr[8 * v:8 * v + 8, :] for v in range(n_tiles)]
    keep =[jnp.where(c < float(min(SEL_TOP, n_blk)), 1.0, 0.0) for c in cnts]
    if n_blk % 8:
        keep[-1] = jnp.where(sub < n_blk % 8, keep[-1], 0.0)
    return jnp.concatenate(keep + [jnp.zeros((NBLK_PAD - 8 * n_tiles, n), f32)] * (NBLK_PAD > 8 * n_tiles), axis=0)


def _transpose_sel(sel_t):
    return jnp.concatenate([sel_t, jnp.zeros((128 - NBLK_PAD, sel_t.shape[1]), f32)], axis=0).T


def _softmax_cols(s, mask):
    s = jnp.where(mask, s, NEG)
    e = jnp.exp2(s - jnp.max(s, axis=0, keepdims=True))
    e = jnp.where(mask, e, 0.0)
    return e * (1.0 / jnp.maximum(jnp.sum(e, axis=0, keepdims=True), TINY))


def _lane_tile(x, n):
    return jnp.concatenate([x] * n, axis=1)


def _nsa_prompt_body(qct_ref, qrt_ref, gt_ref, kk_ref, vt_ref, kc_ref, vct_ref, ovt_ref, hbt_ref, sel_scr, cnt_scr,
                     *, T):
    i = pl.program_id(1)
    n_chunk = T // CMP_STRIDE
    n_blk = T // SEL_BLOCK
    q0 = i * TQ
    qpos_row = q0 + lax.broadcasted_iota(jnp.int32, (1, TQ), 1)
    gsig = jax.nn.sigmoid(gt_ref[...])
    n_kt = (q0 + TQ + TK - 1) // TK
    w0 = jnp.maximum(i - WINDOW // TQ, 0) * TQ
    WK = WINDOW + TQ
    blk_per_tile = TK // SEL_BLOCK
    sub_keys = TK // KV_SUB
    sub_blk = sub_keys // SEL_BLOCK
    ones_lhs = jnp.ones((16, sub_keys), bf16)
    zeros_q = jnp.zeros((HEAD_DIM, TQ), bf16)

    def group_queries(ref, g):
        cols = []
        for r in range(R):
            q = ref[(g * R + r) * HEAD_DIM:(g * R + r + 1) * HEAD_DIM, :]
            cols.append(jnp.concatenate([q, zeros_q] if g == 0 else [zeros_q, q], axis=0))
        return jnp.concatenate(cols, axis=1)

    qct_all = jnp.concatenate([group_queries(qct_ref, g) for g in range(N_KV)], axis=1)
    qrt_all = jnp.concatenate([group_queries(qrt_ref, g) for g in range(N_KV)], axis=1)
    s = jnp.dot(kc_ref[0], qct_all, preferred_element_type=f32)
    cmp_end = lax.broadcasted_iota(jnp.int32, (n_chunk, 1), 0) * CMP_STRIDE + (CMP_LEN - 1)
    p = _softmax_cols(s, _lane_tile(cmp_end <= qpos_row, N_KV * R))
    o_c_all = jnp.dot(vct_ref[0], p.astype(bf16), preferred_element_type=f32)
    for g in range(N_KV):
        base = g * R * TQ
        psum = (p[:, base:base + TQ] + p[:, base + TQ:base + 2 * TQ] + p[:, base + 2 * TQ:base + 3 * TQ]
                + p[:, base + 3 * TQ:base + 4 * TQ])
        imp_t = jnp.dot(ovt_ref[...], psum, preferred_element_type=f32, precision=HIGHEST)
        live_tile = ((q0 + TQ - 1) // SEL_BLOCK) // 8
        sel_scr[g] = jnp.where(_select_blocks(imp_t, qpos_row, n_blk, cnt_scr, live_tile) > 0.5, 0.0, NEG)

    def kv_step(kt, carry):
        b0 = pl.multiple_of(kt * blk_per_tile, blk_per_tile)
        sel_rows = [sel_scr[g, pl.ds(b0, blk_per_tile), :] for g in range(N_KV)]
        m_i, acc = carry
        for sub in range(KV_SUB):
            k0 = pl.multiple_of(kt * TK + sub * sub_keys, sub_keys)
            causal = (k0 + lax.broadcasted_iota(jnp.int32, (sub_keys, 1), 0)) <= qpos_row
            sc = jnp.dot(kk_ref[pl.ds(k0, sub_keys), 0:128], qrt_all, preferred_element_type=f32)
            biases = []
            for g in range(N_KV):
                rows = sel_rows[g][sub * sub_blk:(sub + 1) * sub_blk]
                bias = jnp.broadcast_to(rows[:, None, :], (sub_blk, SEL_BLOCK, TQ)).reshape(sub_keys, TQ)
                biases.append(_lane_tile(jnp.where(causal, bias, NEG), R))
            sc = sc + jnp.concatenate(biases, axis=1)
            m_new = jnp.maximum(m_i, jnp.max(sc, axis=0, keepdims=True))
            alpha = jnp.exp2(m_i - m_new)
            pp = jnp.exp2(sc - m_new).astype(bf16)
            v_ext = jnp.concatenate([vt_ref[0:128, pl.ds(k0, sub_keys)], ones_lhs], axis=0)
            m_i, acc = m_new, alpha * acc + jnp.dot(v_ext, pp, preferred_element_type=f32)
        return m_i, acc

    init = (jnp.full((1, N_KV * R * TQ), NEG, f32), jnp.zeros((128 + 16, N_KV * R * TQ), f32))
    _, acc_all = lax.fori_loop(0, n_kt, kv_step, init)

    o_s_all = acc_all[0:128] * (1.0 / acc_all[128:129])
    w0a = pl.multiple_of(w0, TQ)
    sw = jnp.dot(kk_ref[pl.ds(w0a, WK), 128:256], qrt_all, preferred_element_type=f32)
    dpos = qpos_row - (w0 + lax.broadcasted_iota(jnp.int32, (WK, 1), 0))
    biasw = jnp.where((dpos >= 0) & (dpos < WINDOW), 0.0, NEG)
    sw = sw + _lane_tile(biasw, N_KV * R)
    pw = jnp.exp2(sw - jnp.max(sw, axis=0, keepdims=True)).astype(bf16)
    ones_w = jnp.ones((16, WK), bf16)
    ow = jnp.dot(jnp.concatenate([vt_ref[128:256, pl.ds(w0a, WK)], ones_w], axis=0), pw, preferred_element_type=f32)
    o_w_all = ow[0:128] * (1.0 / ow[128:129])
    for h in range(N_HEADS):
        rows = slice((h // R) * HEAD_DIM, (h // R + 1) * HEAD_DIM)
        cs = slice(h * TQ, (h + 1) * TQ)
        c0 = 8 + 3 * h
        out = (gsig[c0:c0 + 1, :] * o_c_all[rows, cs] + gsig[c0 + 1:c0 + 2, :] * o_s_all[rows, cs]
               + gsig[c0 + 2:c0 + 3, :] * o_w_all[rows, cs])
        hbt_ref[h * 64:(h + 1) * 64, :] = out.astype(bf16)


def _nsa_prompt(qct, qrt, gt, kk, vt, kc, vct, *, nb, T):
    nq = T // TQ
    n_chunk = T // CMP_STRIDE
    ovt = _sel_overlap_t(n_chunk, T // SEL_BLOCK)
    col = lambda h: pl.BlockSpec((h, TQ), lambda b, i: (0, b * nq + i))
    const = lambda a: pl.BlockSpec(a.shape, lambda b, i: (0,) * a.ndim)
    return pl.pallas_call(
        functools.partial(_nsa_prompt_body, T=T), grid=(nb, nq),
        in_specs=[col(512), col(512), col(32),
                  pl.BlockSpec((T, 256), lambda b, i: (b, 0)),
                  pl.BlockSpec((256, T), lambda b, i: (0, b)),
                  pl.BlockSpec((1, n_chunk, 128), lambda b, i: (b, 0, 0)),
                  pl.BlockSpec((1, 128, n_chunk), lambda b, i: (b, 0, 0)),
                  const(ovt)],
        out_specs=col(512),
        out_shape=jax.ShapeDtypeStruct((512, nb * T), bf16),
        scratch_shapes=[pltpu.VMEM((N_KV, NBLK_PAD, TQ), f32), pltpu.VMEM((NBLK_PAD, TQ), f32)],
        compiler_params=pltpu.CompilerParams(dimension_semantics=("parallel", "arbitrary"),
                                             vmem_limit_bytes=VMEM_LIMIT),
        name="nsa_prompt",
    )(qct, qrt, gt, kk, vt, kc, vct, ovt)


def _nsa_sample_body(pt_ref, *refs, n_pages, ts):
    n_pg = GB * n_pages
    pages = refs[:n_pg]
    (win_ref, qc_ref, qr_ref, sm_ref, kvn_ref, wn_ref, wk_ref, pek_ref, w2k_ref, wv_ref, pev_ref, w2v_ref,
     ovt_ref, e_ref, perm_ref) = refs[n_pg:n_pg + 15]
    hb_ref, wout_stack = refs[-2:]
    wout_ref = wout_stack.at[0]
    P = n_pages * PAGE
    n_chunk = P // CMP_STRIDE
    n_blk = -(-(P + ts) // SEL_BLOCK)
    WB = win_ref.shape[4]
    chunks_per_page = PAGE // CMP_STRIDE

    qc_all = qc_ref[...].astype(f32)
    qr_all = qr_ref[...].astype(f32)
    gsig = jax.nn.sigmoid(sm_ref[...])
    rows = N_KV * R * SEG
    qpos_col = P + lax.broadcasted_iota(jnp.int32, (rows, 1), 0) % SEG
    qpos_row = P + lax.broadcasted_iota(jnp.int32, (1, 128), 1) % SEG
    new_lane = lax.broadcasted_iota(jnp.int32, (1, 128), 1)
    zpad = jnp.zeros((128 - SEG, 128), f32)

    def stack(q_all, bi):
        parts = []
        for h in range(N_HEADS):
            g = h // R
            q = q_all[bi * SEG:(bi + 1) * SEG, (h // 2) * 128:(h // 2 + 1) * 128]
            if h % 2 != g:
                q = pltpu.roll(q, HEAD_DIM, axis=1)
            parts.append(jnp.where(new_lane // HEAD_DIM == g, q, 0.0))
        return jnp.concatenate(parts, axis=0).astype(bf16)

    kwpos = jnp.concatenate([P - WB + lax.broadcasted_iota(jnp.int32, (1, WB), 1), P + new_lane], axis=1)
    validw = jnp.concatenate([jnp.full((1, WB), True), new_lane < SEG], axis=1)
    dpos = qpos_col - kwpos
    maskw = (dpos >= 0) & (dpos < WINDOW) & (kwpos >= 0) & validw
    pre = []
    for bi in range(GB):
        rs8 = slice(bi * SEG, (bi + 1) * SEG)
        ksel_t = jnp.concatenate([pages[bi * n_pages + pp][0, 0, 2].astype(bf16) for pp in range(n_pages)], axis=1)
        vsel_t = jnp.concatenate([pages[bi * n_pages + pp][0, 0, 3].astype(bf16) for pp in range(n_pages)], axis=1)
        knew = jnp.concatenate([kvn_ref[rs8, 256:384], zpad], axis=0).astype(bf16)
        vnew = jnp.concatenate([kvn_ref[rs8, 384:512], zpad], axis=0).astype(bf16)
        kwnew_f = jnp.concatenate([wn_ref[rs8, 0:128], zpad], axis=0)
        vwnew_f = jnp.concatenate([wn_ref[rs8, 128:256], zpad], axis=0)
        qr = stack(qr_all, bi)
        s_sel = jnp.concatenate([jnp.dot(qr, ksel_t, preferred_element_type=f32),
                                 lax.dot_general(qr, knew, NT, preferred_element_type=f32)], axis=1)
        s_w = jnp.concatenate([jnp.dot(qr, win_ref[0, bi, 0].astype(bf16), preferred_element_type=f32),
                               lax.dot_general(qr, kwnew_f.astype(bf16), NT, preferred_element_type=f32)], axis=1)
        p_w = _softmax_rows(s_w, maskw).astype(bf16)
        o_w = (lax.dot_general(p_w[:, 0:WB], win_ref[0, bi, 1].astype(bf16), NT, preferred_element_type=f32)
               + jnp.dot(p_w[:, WB:WB + 128], vwnew_f.astype(bf16), preferred_element_type=f32))
        pre.append((s_sel, o_w, vsel_t, vnew, kwnew_f, vwnew_f))

    xt_stack = jnp.concatenate([pages[j][0, 0, c].astype(bf16) for j in range(n_pg) for c in (0, 1)], axis=0)
    y_all = lax.dot_general(perm_ref[...], xt_stack, NT, preferred_element_type=f32)

    def flat(c):
        return jnp.concatenate(
            [jnp.concatenate([y_all[l * chunks_per_page:(l + 1) * chunks_per_page, (2 * j + c) * 128:(2 * j + c + 1) * 128]
                              for j in range(n_pg)], axis=0)
             for l in range(CMP_STRIDE)], axis=1).astype(bf16)

    kc_all = _compress_rows(flat(0), wk_ref, pek_ref, w2k_ref).astype(bf16)
    vc_all = _compress_rows(flat(1), wv_ref, pev_ref, w2v_ref).astype(bf16)

    cmp_end = lax.broadcasted_iota(jnp.int32, (1, n_chunk), 1) * CMP_STRIDE + (CMP_LEN - 1)
    o_cs, psums = [], []
    for bi in range(GB):
        s = lax.dot_general(stack(qc_all, bi), kc_all[bi * n_chunk:(bi + 1) * n_chunk], NT,
                            preferred_element_type=f32)
        p = _softmax_rows(s, cmp_end <= qpos_col)
        o_cs.append(jnp.dot(p.astype(bf16), vc_all[bi * n_chunk:(bi + 1) * n_chunk], preferred_element_type=f32))
        for g in range(N_KV):
            pg = p[g * R * SEG:(g + 1) * R * SEG]
            psums.append(pg[0:SEG] + pg[SEG:2 * SEG] + pg[2 * SEG:3 * SEG] + pg[3 * SEG:4 * SEG])
    psum_all = jnp.concatenate(psums + [jnp.zeros((128 - SEG * GB * N_KV, n_chunk), f32)], axis=0)
    imp_t = lax.dot_general(ovt_ref[...], psum_all, NT, preferred_element_type=f32, precision=HIGHEST)
    sel_all = _transpose_sel(_select_blocks(imp_t, qpos_row, n_blk)).astype(bf16)
    mk_all = jnp.dot(sel_all, e_ref[...], preferred_element_type=f32)

    out_rows = []
    for bi in range(GB):
        rs8 = slice(bi * SEG, (bi + 1) * SEG)
        s_sel, o_w, vsel_t, vnew, kwnew_f, vwnew_f = pre[bi]
        o_c = o_cs[bi]
        mk = jnp.concatenate(
            [mk_all[(bi * N_KV + g) * SEG:(bi * N_KV + g + 1) * SEG] for g in range(N_KV) for _ in range(R)], axis=0) > 0.5
        new_ok = (P + new_lane <= qpos_col) & (new_lane < SEG)
        p_s = _softmax_rows(s_sel, jnp.concatenate([mk[:, 0:P], mk[:, P:P + 128] & new_ok], axis=1)).astype(bf16)
        o_s = (lax.dot_general(p_s[:, 0:P], vsel_t, NT, preferred_element_type=f32)
               + jnp.dot(p_s[:, P:P + 128], vnew, preferred_element_type=f32))
        gs = gsig[rs8]
        heads = [None] * N_HEADS
        for h in range(N_HEADS):
            rr = slice(h * SEG, (h + 1) * SEG)
            c0 = 8 + 3 * h
            out = gs[:, c0:c0 + 1] * o_c[rr] + gs[:, c0 + 1:c0 + 2] * o_s[rr] + gs[:, c0 + 2:c0 + 3] * o_w[rr]
            if h % 2 != h // R:
                out = pltpu.roll(out, HEAD_DIM, axis=1)
            heads[h] = out
        out_rows.append(jnp.concatenate(
            [jnp.where(new_lane < HEAD_DIM, heads[2 * j], heads[2 * j + 1]) for j in range(N_HEADS // 2)], axis=1))
        for kv, new_f in ((0, kwnew_f), (1, vwnew_f)):
            old = pltpu.roll(win_ref[0, bi, kv], WB - ts, axis=1)
            new_t = pltpu.roll(new_f.T, 128 - ts, axis=1)
            wout_ref[bi, kv, :, 0:WB - 128] = old[:, 0:WB - 128]
            wout_ref[bi, kv, :, WB - 128:WB] = jnp.where(new_lane >= 128 - ts, new_t, old[:, WB - 128:WB])
    hb_ref[...] = jnp.concatenate(out_rows, axis=0).astype(bf16)


def _nsa_sample(page_table, cache_t, cwin_t, qc, qr, sm, kvrow, winrow, cwk, cwv, win_stack, *, layer, row0, ts):
    nseq, n_pages = page_table.shape
    depth = cwin_t.shape[0]
    P = n_pages * PAGE
    WB = cwin_t.shape[4]
    n_chunk = P // CMP_STRIDE
    ovt = _sel_overlap_t(n_chunk, -(-(P + ts) // SEL_BLOCK))
    e = _block_expand(P + 128)
    r = np.arange(PAGE)
    perm = jnp.asarray(np.arange(PAGE)[None, :] == (CMP_STRIDE * (r % (PAGE // CMP_STRIDE)) + r // (PAGE // CMP_STRIDE))[:, None],
                       dtype=bf16)
    b0 = row0 // (GB * SEG)

    def page_map(i, pt, *, bi, p):
        return (layer, pt[i * GB + bi, p], 0, 0, 0)

    page_specs = [pl.BlockSpec((1, 1, 4, 128, PAGE), functools.partial(page_map, bi=bi, p=p))
                  for bi in range(GB) for p in range(n_pages)]
    row = lambda w: pl.BlockSpec((GB * SEG, w), lambda i, pt: (b0 + i, 0))
    const = lambda a: pl.BlockSpec(a.shape, lambda i, pt: (0,) * a.ndim)
    extra_in, extra_spec, win_shape = _stacked_out(win_stack, depth, (nseq, 2, 128, WB))
    args = (page_table, *([cache_t] * (GB * n_pages)), cwin_t, qc, qr, sm, kvrow, winrow, *cwk, *cwv, ovt, e, perm)
    grid_spec = pltpu.PrefetchScalarGridSpec(
        num_scalar_prefetch=1, grid=(nseq // GB,),
        in_specs=page_specs + [pl.BlockSpec((1, GB, 2, 128, WB), lambda i, pt: (layer, i, 0, 0, 0)),
                               row(N_WIDTH), row(N_WIDTH), row(128), row(512), row(256)]
        + [const(a) for a in (*cwk, *cwv, ovt, e, perm)] + extra_spec,
        out_specs=[pl.BlockSpec((GB * SEG, N_WIDTH), lambda i, pt: (i, 0)),
                   pl.BlockSpec((1, GB, 2, 128, WB), lambda i, pt: (layer, i, 0, 0, 0))])
    return pl.pallas_call(
        functools.partial(_nsa_sample_body, n_pages=n_pages, ts=ts), grid_spec=grid_spec,
        out_shape=[jax.ShapeDtypeStruct((nseq * SEG, N_WIDTH), bf16), win_shape],
        input_output_aliases={len(args): 1} if extra_in else {},
        compiler_params=pltpu.CompilerParams(dimension_semantics=("parallel",), vmem_limit_bytes=VMEM_LIMIT),
        name="nsa_sample",
    )(*args, *extra_in)


def kernel(x_prompt, x_sample, cache_kv_pages, page_table, cache_win, state_mlstm_C, state_mlstm_n, state_mlstm_m, state_mlstm_conv, norm1_g, w_in, b_gates, conv_w, conv_b, w_mq, w_mk, mlstm_norm_g, cmp_pe_k, cmp_pe_v, cmp_w1_k, cmp_w2_k, cmp_w1_v, cmp_w2_v, w_branch_a, w_branch_b, w_out, norm2_g, w_ffn_up, w_ffn_down, final_norm_g):
    Bp, Tp, _ = x_prompt.shape
    Bs, Ts, _ = x_sample.shape
    depth = w_in.shape[0]
    n_pool, page = cache_kv_pages.shape[1:3]
    n_pages = page_table.shape[1]
    past_len = n_pages * page
    wb_len = cache_win.shape[2]
    mp = Bp * Tp
    ms = Bs * SAMPLE_PAD
    assert page == PAGE and M_CONV - 1 <= Ts <= SAMPLE_PAD and Tp % MLSTM_CHUNK == 0 and Tp >= WINDOW + TQ
    assert mp % (SEG * SEQ_PER_STEP) == 0 and Bs % SEQ_PER_STEP == 0 and Bs % GB == 0

    pad_seq = lambda a: jnp.pad(a, ((0, 0), (0, SAMPLE_PAD - a.shape[1]), (0, 0)))
    xs = (x_prompt.reshape(mp, D_MODEL), pad_seq(x_sample).reshape(ms, D_MODEL))
    tabs = _rope_tables(Tp, past_len, ROW_TILE * (ms // ROW_TILE))
    w_all = _in_proj_weights(w_in)
    w_up, w_down = w_ffn_up.astype(bf16), w_ffn_down.astype(bf16)
    cache_t = jnp.transpose(cache_kv_pages, (0, 1, 3, 4, 5, 2)).reshape(depth, n_pool, 4, KV_W, PAGE)
    cwin_t = jnp.transpose(cache_win, (0, 1, 3, 4, 5, 2)).reshape(depth, Bs, 2, KV_W, wb_len)

    outs = {k: [] for k in ("p_win", "p_C", "p_n", "p_m", "p_conv", "s_kv", "s_n", "s_m", "s_conv")}
    c_stack = win_stack = kvt_stack = None
    y_parts = None
    for l in range(depth):
        w_t = jnp.concatenate([w_in[l][:, O_IG:O_IG + 8], w_in[l][:, O_NG:O_NG + 24]], axis=1).T.astype(bf16)
        xm, sm, smt, qc, qr, qct, qrt, kvrow, winrow, kk, vt, gates, kvt_stack = _in_proj(
            xs, norm1_g[l].reshape(1, D_MODEL), w_all, w_t, tabs, Tp, kvt_stack, layer=l, depth=depth)

        wq, wk = w_mq[l].astype(bf16), w_mk[l].astype(bf16)
        ha_p, c_p, n_p, m_p, cv_p = _mlstm_prompt(xm, sm, smt, conv_w[l], conv_b[l], wq, wk, b_gates[l], mlstm_norm_g[l],
                                                   nb=Bp, T=Tp, L=MLSTM_CHUNK)
        tail = jnp.pad(state_mlstm_conv[l], ((0, 0), (0, SEG - (M_CONV - 1)), (0, 0))).reshape(ms, M_WIDTH)
        n0 = jnp.pad(state_mlstm_n[l], ((0, 0), (0, 8 - M_HEADS), (0, 0)))
        m0 = jnp.broadcast_to(jnp.pad(state_mlstm_m[l], ((0, 0), (0, 128 - M_HEADS)))[:, None, :], (Bs, SEG, 128))
        ha_s, c_stack, n_s, mt_s = _mlstm_sample(
            xm, sm, smt, tail, state_mlstm_C, n0, m0.reshape(ms, 128), conv_w[l], conv_b[l], wq, wk, b_gates[l],
            mlstm_norm_g[l], c_stack, layer=l, depth=depth, row0=mp, nseq=Bs, n_valid=Ts)

        cwk = _compress_weights(cmp_pe_k[l], cmp_w1_k[l], cmp_w2_k[l])
        cwv = _compress_weights(cmp_pe_v[l], cmp_w1_v[l], cmp_w2_v[l])
        kc, vct = _compress_prompt(kvrow, cwk, cwv, nb=Bp, T=Tp)
        hbt_p = _nsa_prompt(qct, qrt, smt, kk, vt, kc, vct, nb=Bp, T=Tp)
        hb_s, win_stack = _nsa_sample(page_table, cache_t, cwin_t, qc, qr, sm, kvrow, winrow, cwk, cwv, win_stack,
                                      layer=l, row0=mp, ts=Ts)

        x_all, *y_parts = _mix_ffn(
            xs, (ha_p, ha_s), (hbt_p, hb_s.T), gates, w_branch_a[l].astype(bf16), w_branch_b[l].astype(bf16),
            w_out[l].astype(bf16), norm2_g[l].reshape(1, D_MODEL), w_up, w_down, final_norm_g.reshape(1, D_MODEL),
            layer=l, n_prompt_rows=mp, final=l == depth - 1)
        xs = (x_all,)

        sample_rows = lambda a: a[mp:].reshape(Bs, SAMPLE_PAD, a.shape[1])
        wp = min(WINDOW, Tp)
        outs["p_win"].append(winrow[:mp].reshape(Bp, Tp, 2 * KV_W)[:, -wp:].reshape(Bp, wp, 2, N_KV, HEAD_DIM))
        outs["p_C"].append(c_p)
        outs["p_n"].append(n_p[:, :M_HEADS])
        outs["p_m"].append(m_p[:, 0, :M_HEADS])
        outs["p_conv"].append(cv_p[:, 8 - (M_CONV - 1):])
        outs["s_kv"].append(sample_rows(kvrow)[:, :Ts].reshape(Bs, Ts, 4, N_KV, HEAD_DIM))
        outs["s_n"].append(n_s[:, :M_HEADS])
        outs["s_m"].append(mt_s.reshape(Bs, SEG, 128)[:, SEG - 1, :M_HEADS])
        outs["s_conv"].append(sample_rows(xm)[:, Ts - (M_CONV - 1):Ts, :M_WIDTH])

    s_win = jnp.transpose(win_stack.reshape(depth, Bs, 2, N_KV, HEAD_DIM, wb_len), (0, 1, 5, 2, 3, 4))
    y_prompt = y_parts[0].reshape(Bp, Tp, D_MODEL)
    y_sample = y_parts[1].reshape(Bs, SAMPLE_PAD, D_MODEL)[:, :Ts]
    st = lambda k: jnp.stack(outs[k])
    p_kv = jnp.transpose(kvt_stack.reshape(depth, Bp, 4, N_KV, HEAD_DIM, Tp), (0, 1, 5, 2, 3, 4))
    return (y_prompt, y_sample, p_kv, st("p_win"), st("p_C"), st("p_n"), st("p_m"), st("p_conv"),
            st("s_kv"), s_win, c_stack, st("s_n"), st("s_m"), st("s_conv"))
```

```python
import functools

import jax
import jax.numpy as jnp
import numpy as np
from jax import lax
from jax.experimental import pallas as pl
from jax.experimental.pallas import tpu as pltpu

f32 = jnp.float32
bf16 = jnp.bfloat16

D_MODEL = 1024
M_HEADS = 4
M_WIDTH = 512
M_HEAD_DIM = 128
M_CONV = 4
MLSTM_CHUNK = 256
HEAD_DIM = 64
N_WIDTH = 512
N_HEADS = 8
N_KV = 2
KV_W = 128
CMP_STRIDE = 16
CMP_LEN = 32
SEL_BLOCK = 64
SEL_TOP = 16
WINDOW = 512
ROT_DIM = 16
ROPE_THETA = 500000.0
D_FF = 2816
EPS = 1e-6
NEG = -1e30
TINY = 1e-30
FORCE_SCORE = 1e9

SAMPLE_PAD = 8
SEG = SAMPLE_PAD
SEQ_PER_STEP = 16
GB = 2
PAGE = 128
HD = M_HEAD_DIM
R = N_HEADS // N_KV
CMP_HID = 128
TQ = 128
TK = 1024
KV_SUB = 1
NBLK_PAD = 64
HIGHEST = lax.Precision.HIGHEST
NT = (((1,), (1,)), ((), ()))
LOG2E = 1.4426950408889634
ROW_TILE = 256
FF_CHUNK = 2816
VMEM_LIMIT = 56 * 1024 * 1024

C_XM, C_Q, C_KV, C_WIN, C_GATE, C_SM = 0, 1536, 2048, 2560, 2816, 4864
C_TOTAL = 4992
O_U, O_V, O_O, O_IG, O_FG, O_Q, O_KV, O_NG, O_GA, O_GB = 0, 512, 1024, 1536, 1540, 1544, 2056, 2824, 2848, 3872
IN_WIDTH = 4896


def _in_proj_weights(w):
    lead = w.shape[:-1]
    parts = [w[..., O_U:O_IG], w[..., O_Q:O_Q + N_WIDTH] * (HEAD_DIM ** -0.5), w[..., O_KV:O_KV + 6 * KV_W],
             w[..., O_GA:O_GA + 2 * D_MODEL], w[..., O_IG:O_IG + 8], w[..., O_NG:O_NG + 24],
             jnp.zeros(lead + (128 - 32,), f32)]
    out = jnp.concatenate(parts, axis=-1).astype(bf16)
    assert out.shape[-1] == C_TOTAL
    return out


def _rope_tables(seq, past_len, n_sample_rows):
    half = ROT_DIM // 2
    inv = ROPE_THETA ** (-jnp.arange(half, dtype=f32) / half)
    pos = jnp.concatenate([jnp.arange(seq), past_len + (jnp.arange(n_sample_rows) % SAMPLE_PAD)]).astype(f32)
    ang = pos[:, None] * inv[None, :]
    cos8, sin8 = jnp.cos(ang), jnp.sin(ang)
    n = pos.shape[0]
    one = jnp.ones((n, 64 - ROT_DIM), f32)
    zero = jnp.zeros((n, 64 - ROT_DIM), f32)
    z8 = jnp.zeros((n, half), f32)
    cos = jnp.concatenate([cos8, cos8, one], axis=1)
    sa = jnp.concatenate([-sin8, z8, zero], axis=1)
    sb = jnp.concatenate([z8, sin8, zero], axis=1)
    tile2 = lambda a: jnp.concatenate([a, a], axis=1)
    return tile2(cos), tile2(sa), tile2(sb)


def _rms(x, g):
    return x * lax.rsqrt(jnp.mean(x * x, axis=-1, keepdims=True) + EPS) * g


def _in_proj_body(*refs, n_x, n_sample_tiles):
    x_refs = refs[:n_x]
    g_ref, w_ref, wt_ref, cos_ref, sa_ref, sb_ref = refs[n_x:n_x + 6]
    (xm_ref, sm_ref, smt_ref, qc_ref, qr_ref, qct_ref, qrt_ref, kv_ref, win_ref, kk_ref, vt_ref, gate_ref,
     kvt_stack) = refs[-13:]
    x = x_refs[0][...]
    if n_x == 2:
        x = jnp.where(pl.program_id(0) < n_sample_tiles, x_refs[1][...], x)
    hb = _rms(x, g_ref[...]).astype(bf16)
    cos, sa, sb = cos_ref[...], sa_ref[...], sb_ref[...]

    def rope(v):
        return v * cos + pltpu.roll(v, 128 - ROT_DIM // 2, axis=1) * sa + pltpu.roll(v, ROT_DIM // 2, axis=1) * sb

    def proj(c0, n):
        return jnp.dot(hb, w_ref[0, :, c0:c0 + n], preferred_element_type=f32)

    xm_ref[...] = proj(C_XM, 1536)
    sm_ref[...] = proj(C_SM, 128)
    smt_ref[...] = lax.dot_general(wt_ref[...], hb, NT, preferred_element_type=f32)
    qf = proj(C_Q, N_WIDTH) * LOG2E
    for j in range(N_HEADS // 2):
        sl = slice(j * 128, (j + 1) * 128)
        q = qf[:, sl]
        qrot = rope(q)
        qc_ref[:, sl] = q.astype(bf16)
        qr_ref[:, sl] = qrot.astype(bf16)
        qct_ref[sl, :] = q.T.astype(bf16)
        qrt_ref[sl, :] = qrot.T.astype(bf16)
    kv = proj(C_KV, 512)
    ksel = rope(kv[:, 256:384])
    kv_ref[:, 0:256] = kv[:, 0:256]
    kv_ref[:, 256:384] = ksel
    kv_ref[:, 384:512] = kv[:, 384:512]
    kvt_stack[0, 0, 0:256, :] = kv[:, 0:256].T
    kvt_stack[0, 0, 256:384, :] = ksel.T
    kvt_stack[0, 0, 384:512, :] = kv[:, 384:512].T
    win = proj(C_WIN, 256)
    kwin = rope(win[:, 0:128])
    win_ref[:, 0:128] = kwin
    win_ref[:, 128:256] = win[:, 128:256]
    kk_ref[:, 0:128] = ksel.astype(bf16)
    kk_ref[:, 128:256] = kwin.astype(bf16)
    vt_ref[0:128, :] = kv[:, 384:512].T.astype(bf16)
    vt_ref[128:256, :] = win[:, 128:256].T.astype(bf16)
    gate_ref[...] = proj(C_GATE, 2048)


def _in_proj(xs, norm_g, w_all, w_t, tabs, seq, kvt_stack, *, layer, depth):
    m = sum(x.shape[0] for x in xs)
    n_prompt_tiles_per_seq = seq // ROW_TILE
    n_prompt_tiles = (m - (tabs[0].shape[0] - seq)) // ROW_TILE
    n_seq = n_prompt_tiles // n_prompt_tiles_per_seq
    n_tiles = m // ROW_TILE
    n_sample_tiles = n_tiles - n_prompt_tiles

    def tile(i):
        return (i + n_prompt_tiles) % n_tiles

    def kvt_map(i):
        j = jnp.maximum(i - n_sample_tiles, 0)
        return (layer, j // n_prompt_tiles_per_seq, 0, j % n_prompt_tiles_per_seq)

    def tab_map(i):
        t = tile(i)
        return (jnp.where(t < n_prompt_tiles, t % n_prompt_tiles_per_seq, n_prompt_tiles_per_seq + t - n_prompt_tiles), 0)

    row = lambda w: pl.BlockSpec((ROW_TILE, w), lambda i: (tile(i), 0))
    const = lambda s: pl.BlockSpec(s, lambda i: (0, 0), pipeline_mode=pl.Buffered(1))
    tab = pl.BlockSpec((ROW_TILE, 128), tab_map)
    widths = (1536, 128, -32, 512, 512, -512, -512, 512, 256, 256, -256, 2048)
    dts = (f32, f32, f32, bf16, bf16, bf16, bf16, f32, f32, bf16, bf16, f32)
    out_shape = [jax.ShapeDtypeStruct((-w, m) if w < 0 else (m, w), d) for w, d in zip(widths, dts)]
    out_specs = [pl.BlockSpec((-w, ROW_TILE), lambda i: (0, tile(i))) if w < 0 else row(w) for w in widths]
    extra_in, extra_spec, kvt_shape = _stacked_out(kvt_stack, depth, (n_seq, 4 * KV_W, seq))
    if len(xs) == 1:
        x_specs = [row(D_MODEL)]
    else:
        x_specs = [pl.BlockSpec((ROW_TILE, D_MODEL), lambda i: (jnp.maximum(i - n_sample_tiles, 0), 0)),
                   pl.BlockSpec((ROW_TILE, D_MODEL), lambda i: (jnp.minimum(i, n_sample_tiles - 1), 0))]
    args = (*xs, norm_g, w_all, w_t, *tabs)
    return pl.pallas_call(
        functools.partial(_in_proj_body, n_x=len(xs), n_sample_tiles=n_sample_tiles), grid=(n_tiles,),
        in_specs=x_specs + [const((1, D_MODEL)),
                            pl.BlockSpec((1, D_MODEL, C_TOTAL), lambda i: (layer, 0, 0), pipeline_mode=pl.Buffered(1)),
                            const((32, D_MODEL)), tab, tab, tab]
        + extra_spec,
        out_specs=out_specs + [pl.BlockSpec((1, 1, 4 * KV_W, ROW_TILE), kvt_map)],
        out_shape=out_shape + [kvt_shape], name="in_proj",
        input_output_aliases={len(args): len(out_shape)} if extra_in else {},
        compiler_params=pltpu.CompilerParams(dimension_semantics=("arbitrary",), vmem_limit_bytes=VMEM_LIMIT),
    )(*args, *extra_in)


def _mix_ffn_body(*refs, n_x, n_prompt_tiles, final):
    x_refs = refs[:n_x]
    hap_ref, has_ref, hbp_ref, hbs_ref, gate_ref, wa_ref, wb_ref, wo_ref, g2_ref, wup_ref, wdn_ref, gf_ref = (
        refs[n_x:n_x + 12])
    xo_ref, *y_refs = refs[n_x + 12:]
    in_prompt = pl.program_id(0) < n_prompt_tiles
    pick = lambda p_ref, s_ref: jnp.where(in_prompt, p_ref[...], s_ref[...])
    x = x_refs[0][...] if n_x == 1 else pick(*x_refs)
    a = jnp.dot(pick(hap_ref, has_ref).astype(bf16), wa_ref[...], preferred_element_type=f32)
    b = lax.dot_general(pick(hbp_ref, hbs_ref), wb_ref[...], (((0,), (0,)), ((), ())),
                        preferred_element_type=f32)
    merged = jax.nn.sigmoid(gate_ref[:, 0:D_MODEL]) * a + jax.nn.sigmoid(gate_ref[:, D_MODEL:2 * D_MODEL]) * b
    x1 = x + jnp.dot(merged.astype(bf16), wo_ref[...], preferred_element_type=f32)
    hn = _rms(x1, g2_ref[...]).astype(bf16)
    acc = x1
    for c in range(D_FF // FF_CHUNK):
        lo = c * FF_CHUNK
        g = jnp.dot(hn, wup_ref[0, :, lo:lo + FF_CHUNK], preferred_element_type=f32)
        u = jnp.dot(hn, wup_ref[0, :, D_FF + lo:D_FF + lo + FF_CHUNK], preferred_element_type=f32)
        act = (g * jax.nn.sigmoid(g) * u).astype(bf16)
        acc = acc + jnp.dot(act, wdn_ref[0, lo:lo + FF_CHUNK, :], preferred_element_type=f32)
    xo_ref[...] = acc
    if final:
        y = _rms(acc, gf_ref[...])
        yp_ref, ys_ref = y_refs

        @pl.when(in_prompt)
        def _():
            yp_ref[...] = y

        @pl.when(jnp.logical_not(in_prompt))
        def _():
            ys_ref[...] = y


def _mix_ffn(xs, h_a, h_bt, gates, wa, wb, wo, g2, wup, wdn, gf, *, layer, n_prompt_rows, final):
    m = gates.shape[0]
    npt = n_prompt_rows // ROW_TILE
    row = lambda w: pl.BlockSpec((ROW_TILE, w), lambda i: (i, 0))
    p_row = lambda w: pl.BlockSpec((ROW_TILE, w), lambda i: (jnp.minimum(i, npt - 1), 0))
    s_row = lambda w: pl.BlockSpec((ROW_TILE, w), lambda i: (jnp.maximum(i - npt, 0), 0))
    p_col = lambda h: pl.BlockSpec((h, ROW_TILE), lambda i: (0, jnp.minimum(i, npt - 1)))
    s_col = lambda h: pl.BlockSpec((h, ROW_TILE), lambda i: (0, jnp.maximum(i - npt, 0)))
    const = lambda s: pl.BlockSpec(s, lambda i: (0, 0), pipeline_mode=pl.Buffered(1))
    layer_w = lambda a: pl.BlockSpec((1,) + a.shape[1:], lambda i: (layer, 0, 0), pipeline_mode=pl.Buffered(1))
    out_specs = [row(D_MODEL)]
    out_shape = [jax.ShapeDtypeStruct((m, D_MODEL), f32)]
    if final:
        out_specs += [p_row(D_MODEL), s_row(D_MODEL)]
        out_shape += [jax.ShapeDtypeStruct((n_prompt_rows, D_MODEL), f32),
                      jax.ShapeDtypeStruct((m - n_prompt_rows, D_MODEL), f32)]
    x_specs = [row(D_MODEL)] if len(xs) == 1 else [p_row(D_MODEL), s_row(D_MODEL)]
    return pl.pallas_call(
        functools.partial(_mix_ffn_body, n_x=len(xs), n_prompt_tiles=npt, final=final), grid=(m // ROW_TILE,),
        in_specs=x_specs + [p_row(M_WIDTH), s_row(M_WIDTH), p_col(N_WIDTH), s_col(N_WIDTH), row(2 * D_MODEL),
                            const(wa.shape), const(wb.shape), const(wo.shape), const((1, D_MODEL)),
                            layer_w(wup), layer_w(wdn), const((1, D_MODEL))],
        out_specs=out_specs, out_shape=out_shape, name="mix_ffn",
        compiler_params=pltpu.CompilerParams(dimension_semantics=("arbitrary",), vmem_limit_bytes=VMEM_LIMIT),
    )(*xs, *h_a, *h_bt, gates, wa, wb, wo, g2, wup, wdn, gf)


def _log_sigmoid(x):
    return jnp.minimum(x, 0.0) - jnp.log1p(jnp.exp(-jnp.abs(x)))


def _seg_scan(x, axis, seg, op, fill):
    idx = lax.broadcasted_iota(jnp.int32, x.shape, axis) % seg
    d = 1
    while d < seg:
        x = op(x, jnp.where(idx >= d, pltpu.roll(x, d, axis=axis), fill))
        d *= 2
    return x


def _conv_silu(u, tail, conv_w, conv_b, seg):
    L = u.shape[0]
    row = lax.broadcasted_iota(jnp.int32, u.shape, 0) % seg
    y = conv_b + u * conv_w[3:4, :]
    for k in (1, 2, 3):
        prev = tail if k == 3 else pltpu.roll(tail, L - (3 - k), axis=0)
        sh = jnp.where(row < k, prev, pltpu.roll(u, k, axis=0))
        y = y + sh * conv_w[3 - k:4 - k, :]
    return y * jax.nn.sigmoid(y)


def _gate_columns(sm, smt, bg_row, bg_col, m_vec, seg, n_valid):
    G = sm + bg_row
    lf = _log_sigmoid(G)
    ipre = G
    if n_valid < seg:
        rowc = lax.broadcasted_iota(jnp.int32, G.shape, 0) % seg
        lf = jnp.where(rowc < n_valid, lf, 0.0)
        ipre = jnp.where(rowc < n_valid, G, NEG)
    F = _seg_scan(lf, 0, seg, jnp.add, 0.0)
    F_al = pltpu.roll(F, 128 - M_HEADS, axis=1)
    a_col = ipre - F_al
    g_col = jnp.maximum(m_vec, _seg_scan(a_col, 0, seg, jnp.maximum, NEG))
    mt_col = F_al + g_col
    Gr = smt + bg_col
    lfr = _log_sigmoid(Gr)
    ir = Gr
    if n_valid < seg:
        lanec = lax.broadcasted_iota(jnp.int32, Gr.shape, 1) % seg
        lfr = jnp.where(lanec < n_valid, lfr, 0.0)
        ir = jnp.where(lanec < n_valid, Gr, NEG)
    Fr = _seg_scan(lfr, 1, seg, jnp.add, 0.0)
    a_row = ir - pltpu.roll(Fr, M_HEADS, axis=0)
    return a_col, g_col, mt_col, a_row


def _head_norm_gate(hh, ng, o):
    return hh * lax.rsqrt(jnp.mean(hh * hh, axis=-1, keepdims=True) + EPS) * ng * jax.nn.sigmoid(o)


def _mlstm_prompt_body(xm_ref, sm_ref, smt_ref, cw_ref, cb_ref, wq_ref, wk_ref, bgr_ref, bgc_ref, ng_ref,
                       ha_ref, c_out, n_out, m_out, conv_out, c_scr, m_scr, tail_scr, *, L):
    c_idx = pl.program_id(1)

    @pl.when(c_idx == 0)
    def _():
        c_scr[...] = jnp.zeros_like(c_scr)
        m_scr[...] = jnp.zeros_like(m_scr)
        tail_scr[...] = jnp.zeros_like(tail_scr)

    u = xm_ref[:, 0:M_WIDTH]
    tail = jnp.concatenate([tail_scr[...], jnp.zeros((L - 8, M_WIDTH), f32)], axis=0)
    cact = _conv_silu(u, tail, cw_ref[...], cb_ref[...], L)
    tail_scr[...] = pltpu.roll(u[L - 8:L, :], 3, axis=0)
    conv_out[0] = u[L - 8:L, :]

    m_vec = m_scr[0:1, :]
    a_col, g_col, mt_col, a_row = _gate_columns(sm_ref[...], smt_ref[...], bgr_ref[...], bgc_ref[...], m_vec, L, L)
    inter_col = jnp.exp(m_vec - g_col)
    floor_col = jnp.exp(-mt_col)
    g_last = g_col[L - 1:L, :]
    w_col = jnp.exp(a_col - g_last)
    decay = jnp.exp(m_vec - g_last)
    m_new = mt_col[L - 1:L, :]
    ti = lax.broadcasted_iota(jnp.int32, (L, L), 0)
    si = lax.broadcasted_iota(jnp.int32, (L, L), 1)
    causal = si <= ti
    scale = HD ** -0.5
    ones_v = jnp.ones((L, HD), bf16)
    for h in range(M_HEADS):
        sl = slice(h * HD, (h + 1) * HD)
        ch = cact[:, sl].astype(bf16)
        qh = jnp.dot(ch, wq_ref[h], preferred_element_type=f32)
        kh = jnp.dot(ch, wk_ref[h], preferred_element_type=f32) * scale
        vh = jnp.concatenate([xm_ref[:, M_WIDTH + h * HD:M_WIDTH + (h + 1) * HD].astype(bf16), ones_v], axis=1)
        qb = qh.astype(bf16)
        D = jnp.exp(jnp.where(causal, a_row[h:h + 1, :] - g_col[:, h:h + 1], NEG))
        S = lax.dot_general(qb, kh.astype(bf16), NT, preferred_element_type=f32) * D
        C = c_scr[h]
        ic = inter_col[:, h:h + 1]
        both = jnp.dot(S.astype(bf16), vh, preferred_element_type=f32) + ic * jnp.dot(
            qb, C.astype(bf16), preferred_element_type=f32)
        hh = both[:, 0:HD] / jnp.maximum(jnp.abs(both[:, HD:2 * HD]), floor_col[:, h:h + 1])
        o = xm_ref[:, 2 * M_WIDTH + h * HD:2 * M_WIDTH + (h + 1) * HD]
        ha_ref[:, sl] = _head_norm_gate(hh, ng_ref[:, sl], o)
        kw = kh * w_col[:, h:h + 1]
        c_scr[h] = decay[:, h:h + 1] * C + lax.dot_general(kw.astype(bf16), vh, (((0,), (0,)), ((), ())),
                                                           preferred_element_type=f32)
    m_scr[0:1, :] = m_new

    @pl.when(c_idx == pl.num_programs(1) - 1)
    def _():
        c_out[0] = c_scr[:, :, 0:HD]
        for h in range(M_HEADS):
            n_out[0, h:h + 1, :] = c_scr[h][:, HD:2 * HD].T[0:1, :]
        n_out[0, M_HEADS:8, :] = jnp.zeros((8 - M_HEADS, HD), f32)
        m_out[0] = m_scr[...]


def _mlstm_prompt(xm, sm, smt, conv_w, conv_b, wq, wk, b_gates, norm_g, *, nb, T, L):
    nc = T // L
    bg_row = jnp.zeros((1, 128), f32).at[0, :8].set(b_gates)
    bg_col = b_gates.reshape(8, 1)
    const = lambda s: pl.BlockSpec(s, lambda b, c: (0,) * len(s))
    return pl.pallas_call(
        functools.partial(_mlstm_prompt_body, L=L), grid=(nb, nc),
        in_specs=[pl.BlockSpec((L, 3 * M_WIDTH), lambda b, c: (b * nc + c, 0)),
                  pl.BlockSpec((L, 128), lambda b, c: (b * nc + c, 0)),
                  pl.BlockSpec((8, L), lambda b, c: (0, b * nc + c)),
                  const((4, M_WIDTH)), const((1, M_WIDTH)), const((M_HEADS, HD, HD)), const((M_HEADS, HD, HD)),
                  const((1, 128)), const((8, 1)), const((1, M_WIDTH))],
        out_specs=[pl.BlockSpec((L, M_WIDTH), lambda b, c: (b * nc + c, 0)),
                   pl.BlockSpec((1, M_HEADS, HD, HD), lambda b, c: (b, 0, 0, 0)),
                   pl.BlockSpec((1, 8, HD), lambda b, c: (b, 0, 0)),
                   pl.BlockSpec((1, 8, 128), lambda b, c: (b, 0, 0)),
                   pl.BlockSpec((1, 8, M_WIDTH), lambda b, c: (b, 0, 0))],
        out_shape=[jax.ShapeDtypeStruct((nb * T, M_WIDTH), f32),
                   jax.ShapeDtypeStruct((nb, M_HEADS, HD, HD), f32),
                   jax.ShapeDtypeStruct((nb, 8, HD), f32),
                   jax.ShapeDtypeStruct((nb, 8, 128), f32),
                   jax.ShapeDtypeStruct((nb, 8, M_WIDTH), f32)],
        scratch_shapes=[pltpu.VMEM((M_HEADS, HD, 2 * HD), f32), pltpu.VMEM((8, 128), f32),
                        pltpu.VMEM((8, M_WIDTH), f32)],
        compiler_params=pltpu.CompilerParams(dimension_semantics=("parallel", "arbitrary")),
        name="mlstm_prompt",
    )(xm, sm, smt, conv_w, conv_b.reshape(1, M_WIDTH), wq, wk, bg_row, bg_col, norm_g.reshape(1, M_WIDTH))


def _mlstm_sample_body(xm_ref, sm_ref, smt_ref, tail_ref, c_in, n_in, m_in, cw_ref, cb_ref, wq_ref, wk_ref,
                       bgr_ref, bgc_ref, ng_ref, *out_refs, n_valid):
    ha_ref, c_stack, n_out, m_out = out_refs[-4:]
    c_out = c_stack.at[0]
    L = SEG * SEQ_PER_STEP
    u = xm_ref[:, 0:M_WIDTH]
    cact = _conv_silu(u, tail_ref[...], cw_ref[...], cb_ref[...], SEG)
    m_rows = m_in[...]
    a_col, g_col, mt_col, a_row = _gate_columns(sm_ref[...], smt_ref[...], bgr_ref[...], bgc_ref[...], m_rows, SEG,
                                                n_valid)
    inter_col = jnp.exp(m_rows - g_col)
    floor_col = jnp.exp(-mt_col)
    ti = lax.broadcasted_iota(jnp.int32, (L, L), 0)
    si = lax.broadcasted_iota(jnp.int32, (L, L), 1)
    mask = (si <= ti) & ((si // SEG) == (ti // SEG))
    scale = HD ** -0.5
    for h in range(M_HEADS):
        sl = slice(h * HD, (h + 1) * HD)
        ch = cact[:, sl].astype(bf16)
        qh = jnp.dot(ch, wq_ref[h], preferred_element_type=f32)
        kh = jnp.dot(ch, wk_ref[h], preferred_element_type=f32) * scale
        vh = xm_ref[:, M_WIDTH + h * HD:M_WIDTH + (h + 1) * HD].astype(bf16)
        qb = qh.astype(bf16)
        D = jnp.exp(jnp.where(mask, a_row[h:h + 1, :] - g_col[:, h:h + 1], NEG))
        S = lax.dot_general(qb, kh.astype(bf16), NT, preferred_element_type=f32) * D
        num = jnp.dot(S.astype(bf16), vh, preferred_element_type=f32)
        den = jnp.sum(S, axis=-1, keepdims=True)
        inter_num, inter_den = [], []
        for s in range(SEQ_PER_STEP):
            rs = slice(s * SEG, (s + 1) * SEG)
            C = c_in[0, s, h]
            nrow = n_in[s, h:h + 1, :]
            inter_num.append(jnp.dot(qb[rs], C.astype(bf16), preferred_element_type=f32))
            inter_den.append(jnp.sum(qh[rs] * nrow, axis=-1, keepdims=True))
            g_last = g_col[s * SEG + SEG - 1:s * SEG + SEG, h:h + 1]
            m_prev = m_rows[s * SEG:s * SEG + 1, h:h + 1]
            w = jnp.exp(a_col[rs, h:h + 1] - g_last)
            dh = jnp.exp(m_prev - g_last)
            kw = kh[rs] * w
            c_out[s, h] = dh * C + lax.dot_general(kw.astype(bf16), vh[rs], (((0,), (0,)), ((), ())),
                                                    preferred_element_type=f32)
            n_out[s, h:h + 1, :] = dh * nrow + jnp.sum(kw, axis=0, keepdims=True)
        ic = inter_col[:, h:h + 1]
        num = num + ic * jnp.concatenate(inter_num, axis=0)
        den = den + ic * jnp.concatenate(inter_den, axis=0)
        hh = num / jnp.maximum(jnp.abs(den), floor_col[:, h:h + 1])
        o = xm_ref[:, 2 * M_WIDTH + h * HD:2 * M_WIDTH + (h + 1) * HD]
        ha_ref[:, sl] = _head_norm_gate(hh, ng_ref[:, sl], o)
    for s in range(SEQ_PER_STEP):
        n_out[s, M_HEADS:8, :] = jnp.zeros((8 - M_HEADS, HD), f32)
    m_out[...] = mt_col


def _stacked_out(prev, depth, shape):
    extra_in = [] if prev is None else [prev]
    extra_spec = [] if prev is None else [pl.BlockSpec(memory_space=pl.ANY)]
    return extra_in, extra_spec, jax.ShapeDtypeStruct((depth,) + shape, f32)


def _mlstm_sample(xm, sm, smt, tail, c0, n0, m0, conv_w, conv_b, wq, wk, b_gates, norm_g, c_stack, *, layer, depth,
                  row0, nseq, n_valid):
    L = SEG * SEQ_PER_STEP
    b0 = row0 // L
    bg_row = jnp.zeros((1, 128), f32).at[0, :8].set(b_gates)
    bg_col = b_gates.reshape(8, 1)
    const = lambda s: pl.BlockSpec(s, lambda i: (0,) * len(s))
    extra_in, extra_spec, c_shape = _stacked_out(c_stack, depth, (nseq, M_HEADS, HD, HD))
    n_in = 14
    return pl.pallas_call(
        functools.partial(_mlstm_sample_body, n_valid=n_valid), grid=(nseq // SEQ_PER_STEP,),
        in_specs=[pl.BlockSpec((L, 3 * M_WIDTH), lambda i: (b0 + i, 0)),
                  pl.BlockSpec((L, 128), lambda i: (b0 + i, 0)),
                  pl.BlockSpec((8, L), lambda i: (0, b0 + i)),
                  pl.BlockSpec((L, M_WIDTH), lambda i: (i, 0)),
                  pl.BlockSpec((1, SEQ_PER_STEP, M_HEADS, HD, HD), lambda i: (layer, i, 0, 0, 0)),
                  pl.BlockSpec((SEQ_PER_STEP, 8, HD), lambda i: (i, 0, 0)),
                  pl.BlockSpec((L, 128), lambda i: (i, 0)),
                  const((4, M_WIDTH)), const((1, M_WIDTH)), const((M_HEADS, HD, HD)), const((M_HEADS, HD, HD)),
                  const((1, 128)), const((8, 1)), const((1, M_WIDTH))] + extra_spec,
        out_specs=[pl.BlockSpec((L, M_WIDTH), lambda i: (i, 0)),
                   pl.BlockSpec((1, SEQ_PER_STEP, M_HEADS, HD, HD), lambda i: (layer, i, 0, 0, 0)),
                   pl.BlockSpec((SEQ_PER_STEP, 8, HD), lambda i: (i, 0, 0)),
                   pl.BlockSpec((L, 128), lambda i: (i, 0))],
        out_shape=[jax.ShapeDtypeStruct((nseq * SEG, M_WIDTH), f32), c_shape,
                   jax.ShapeDtypeStruct((nseq, 8, HD), f32),
                   jax.ShapeDtypeStruct((nseq * SEG, 128), f32)],
        input_output_aliases={n_in: 1} if extra_in else {},
        compiler_params=pltpu.CompilerParams(dimension_semantics=("parallel",), vmem_limit_bytes=VMEM_LIMIT),
        name="mlstm_sample",
    )(xm, sm, smt, tail, c0, n0, m0, conv_w, conv_b.reshape(1, M_WIDTH), wq, wk, bg_row, bg_col,
      norm_g.reshape(1, M_WIDTH), *extra_in)


def _sel_overlap_t(n_chunk, n_blk):
    n_cmp = n_chunk - 1
    start = np.arange(n_cmp) * CMP_STRIDE
    bs = np.arange(n_blk) * SEL_BLOCK
    ov = np.minimum(start[:, None] + CMP_LEN, bs[None, :] + SEL_BLOCK) - np.maximum(start[:, None], bs[None, :])
    ov = np.clip(ov, 0, None) / CMP_LEN
    out = np.zeros((NBLK_PAD, n_chunk), np.float32)
    out[:n_blk, :n_cmp] = ov.T
    return jnp.asarray(out)


def _block_expand(n_keys):
    e = (np.arange(n_keys)[None, :] // SEL_BLOCK) == np.arange(128)[:, None]
    return jnp.asarray(e, dtype=bf16)


def _compress_weights(pe, w1, w2):
    W = jnp.zeros((16, 2, 64, 2, 2, CMP_HID), f32)
    for g in range(N_KV):
        W = W.at[:, g, :, g, 0, :].set(w1[:16])
        W = W.at[:, g, :, g, 1, :].set(w1[16:])
    W = W.reshape(2048, 512)
    peA = jnp.broadcast_to(pe[:16, None, :], (16, 2, 64)).reshape(1, 2048)
    peB = jnp.broadcast_to(pe[16:, None, :], (16, 2, 64)).reshape(1, 2048)
    PE = jnp.concatenate([peA, peB, jnp.zeros((14, 2048), f32)], axis=0)
    W2 = jnp.zeros((2, CMP_HID, 128), f32).at[0, :, 0:64].set(w2).at[1, :, 64:128].set(w2)
    return W.astype(bf16), PE.astype(bf16), W2.astype(bf16)


def _compress_rows(flat, w_ref, pe_ref, w2_ref):
    n = flat.shape[0]
    y = jnp.dot(jnp.concatenate([flat, pe_ref[...]], axis=0), w_ref[...], preferred_element_type=f32)
    c = y[n:n + 8]
    out = jnp.zeros((n, w2_ref.shape[2]), f32)
    for g in range(N_KV):
        a = y[0:n, (2 * g) * CMP_HID:(2 * g + 1) * CMP_HID] + c[0:1, (2 * g) * CMP_HID:(2 * g + 1) * CMP_HID]
        b = (y[0:n, (2 * g + 1) * CMP_HID:(2 * g + 2) * CMP_HID]
             + c[1:2, (2 * g + 1) * CMP_HID:(2 * g + 2) * CMP_HID])
        hid = a + pltpu.roll(b, n - 1, axis=0)
        out = out + jnp.dot(jax.nn.gelu(hid, approximate=True).astype(bf16), w2_ref[g], preferred_element_type=f32)
    return out


def _compress_prompt_body(xk_ref, xv_ref, wk_ref, pek_ref, w2k_ref, wv_ref, pev_ref, w2v_ref, kc_ref, vc_ref, *,
                          n_chunk):
    def flat(x_ref):
        return jnp.concatenate(
            [x_ref[pl.ds(l, n_chunk, stride=CMP_STRIDE), :].astype(bf16) for l in range(CMP_STRIDE)], axis=1)
    kc_ref[0] = _compress_rows(flat(xk_ref), wk_ref, pek_ref, w2k_ref).astype(bf16)
    vc_ref[0] = _compress_rows(flat(xv_ref), wv_ref, pev_ref, w2v_ref).T.astype(bf16)


def _compress_prompt(kvrow, cwk, cwv, *, nb, T):
    n_chunk = T // CMP_STRIDE
    const = lambda a: pl.BlockSpec(a.shape, lambda b: (0,) * a.ndim)
    return pl.pallas_call(
        functools.partial(_compress_prompt_body, n_chunk=n_chunk), grid=(nb,),
        in_specs=[pl.BlockSpec((T, 128), lambda b: (b, 0)), pl.BlockSpec((T, 128), lambda b: (b, 1))]
        + [const(a) for a in (*cwk, *cwv)],
        out_specs=[pl.BlockSpec((1, n_chunk, 128), lambda b: (b, 0, 0)),
                   pl.BlockSpec((1, 128, n_chunk), lambda b: (b, 0, 0))],
        out_shape=[jax.ShapeDtypeStruct((nb, n_chunk, 128), bf16), jax.ShapeDtypeStruct((nb, 128, n_chunk), bf16)],
        compiler_params=pltpu.CompilerParams(dimension_semantics=("parallel",), vmem_limit_bytes=VMEM_LIMIT),
        name="nsa_compress",
    )(kvrow, kvrow, *cwk, *cwv)


def _softmax_rows(s, mask):
    s = jnp.where(mask, s, NEG)
    e = jnp.exp2(s - jnp.max(s, axis=-1, keepdims=True))
    e = jnp.where(mask, e, 0.0)
    return e / jnp.maximum(jnp.sum(e, axis=-1, keepdims=True), TINY)


def _select_blocks(imp_t, qpos_row, n_blk):
    n = imp_t.shape[1]
    j = lax.broadcasted_iota(jnp.int32, imp_t.shape, 0)
    cur = qpos_row // SEL_BLOCK
    forced = (j == 0) | (j == cur) | (j == cur - 1)
    score = jnp.where(j <= cur, jnp.where(forced, FORCE_SCORE, imp_t), -1.0)
    n_tiles = -(-n_blk // 8)
    sub = lax.broadcasted_iota(jnp.int32, (8, n), 0)
    tiles = [score[8 * v:8 * v + 8] for v in range(n_tiles)]
    if n_blk % 8:
        tiles[-1] = jnp.where(sub < n_blk % 8, tiles[-1], -2.0)
    cnts = [jnp.zeros((8, n), f32)] * n_tiles
    for jp in range(n_blk):
        sj = tiles[jp // 8][jp % 8:jp % 8 + 1, :]
        for v in range(n_tiles):
            if v > jp // 8:
                beat = jnp.where(sj >= tiles[v], 1.0, 0.0)
            elif v < jp // 8:
                beat = jnp.where(sj > tiles[v], 1.0, 0.0)
            else:
                beat = jnp.where(sub > jp % 8, jnp.where(sj >= tiles[v], 1.0, 0.0), jnp.where(sj > tiles[v], 1.0, 0.0))
            cnts[v] = cnts[v] + beat
    keep = [jnp.where(c < float(min(SEL_TOP, n_blk)), 1.0, 0.0) for c in cnts]
    if n_blk % 8:
        keep[-1] = jnp.where(sub < n_blk % 8, keep[-1], 0.0)
    return jnp.concatenate(keep + [jnp.zeros((NBLK_PAD - 8 * n_tiles, n), f32)] * (NBLK_PAD > 8 * n_tiles), axis=0)


def _transpose_sel(sel_t):
    return jnp.concatenate([sel_t, jnp.zeros((128 - NBLK_PAD, sel_t.shape[1]), f32)], axis=0).T


def _softmax_cols(s, mask):
    s = jnp.where(mask, s, NEG)
    e = jnp.exp2(s - jnp.max(s, axis=0, keepdims=True))
    e = jnp.where(mask, e, 0.0)
    return e * (1.0 / jnp.maximum(jnp.sum(e, axis=0, keepdims=True), TINY))


def _lane_tile(x, n):
    return jnp.concatenate([x] * n, axis=1)


def _nsa_prompt_body(qct_ref, qrt_ref, gt_ref, kk_ref, vt_ref, kc_ref, vct_ref, ovt_ref, hbt_ref, sel_scr, *, T):
    i = pl.program_id(1)
    n_chunk = T // CMP_STRIDE
    n_blk = T // SEL_BLOCK
    q0 = i * TQ
    qpos_row = q0 + lax.broadcasted_iota(jnp.int32, (1, TQ), 1)
    gsig = jax.nn.sigmoid(gt_ref[...])
    n_kt = (q0 + TQ + TK - 1) // TK
    w0 = jnp.maximum(i - WINDOW // TQ, 0) * TQ
    WK = WINDOW + TQ
    blk_per_tile = TK // SEL_BLOCK
    sub_keys = TK // KV_SUB
    sub_blk = sub_keys // SEL_BLOCK
    ones_lhs = jnp.ones((16, sub_keys), bf16)
    zeros_q = jnp.zeros((HEAD_DIM, TQ), bf16)

    def group_queries(ref, g):
        cols = []
        for r in range(R):
            q = ref[(g * R + r) * HEAD_DIM:(g * R + r + 1) * HEAD_DIM, :]
            cols.append(jnp.concatenate([q, zeros_q] if g == 0 else [zeros_q, q], axis=0))
        return jnp.concatenate(cols, axis=1)

    qct_all = jnp.concatenate([group_queries(qct_ref, g) for g in range(N_KV)], axis=1)
    qrt_all = jnp.concatenate([group_queries(qrt_ref, g) for g in range(N_KV)], axis=1)
    s = jnp.dot(kc_ref[0], qct_all, preferred_element_type=f32)
    cmp_end = lax.broadcasted_iota(jnp.int32, (n_chunk, 1), 0) * CMP_STRIDE + (CMP_LEN - 1)
    p = _softmax_cols(s, _lane_tile(cmp_end <= qpos_row, N_KV * R))
    o_c_all = jnp.dot(vct_ref[0], p.astype(bf16), preferred_element_type=f32)
    for g in range(N_KV):
        base = g * R * TQ
        psum = (p[:, base:base + TQ] + p[:, base + TQ:base + 2 * TQ] + p[:, base + 2 * TQ:base + 3 * TQ]
                + p[:, base + 3 * TQ:base + 4 * TQ])
        imp_t = jnp.dot(ovt_ref[...], psum, preferred_element_type=f32, precision=HIGHEST)
        sel_scr[g] = jnp.where(_select_blocks(imp_t, qpos_row, n_blk) > 0.5, 0.0, NEG)

    def kv_step(kt, carry):
        b0 = pl.multiple_of(kt * blk_per_tile, blk_per_tile)
        sel_rows = [sel_scr[g, pl.ds(b0, blk_per_tile), :] for g in range(N_KV)]
        m_i, acc = carry
        for sub in range(KV_SUB):
            k0 = pl.multiple_of(kt * TK + sub * sub_keys, sub_keys)
            causal = (k0 + lax.broadcasted_iota(jnp.int32, (sub_keys, 1), 0)) <= qpos_row
            sc = jnp.dot(kk_ref[pl.ds(k0, sub_keys), 0:128], qrt_all, preferred_element_type=f32)
            biases = []
            for g in range(N_KV):
                rows = sel_rows[g][sub * sub_blk:(sub + 1) * sub_blk]
                bias = jnp.broadcast_to(rows[:, None, :], (sub_blk, SEL_BLOCK, TQ)).reshape(sub_keys, TQ)
                biases.append(_lane_tile(jnp.where(causal, bias, NEG), R))
            sc = sc + jnp.concatenate(biases, axis=1)
            m_new = jnp.maximum(m_i, jnp.max(sc, axis=0, keepdims=True))
            alpha = jnp.exp2(m_i - m_new)
            pp = jnp.exp2(sc - m_new).astype(bf16)
            v_ext = jnp.concatenate([vt_ref[0:128, pl.ds(k0, sub_keys)], ones_lhs], axis=0)
            m_i, acc = m_new, alpha * acc + jnp.dot(v_ext, pp, preferred_element_type=f32)
        return m_i, acc

    init = (jnp.full((1, N_KV * R * TQ), NEG, f32), jnp.zeros((128 + 16, N_KV * R * TQ), f32))
    _, acc_all = lax.fori_loop(0, n_kt, kv_step, init)

    o_s_all = acc_all[0:128] * (1.0 / acc_all[128:129])
    w0a = pl.multiple_of(w0, TQ)
    sw = jnp.dot(kk_ref[pl.ds(w0a, WK), 128:256], qrt_all, preferred_element_type=f32)
    dpos = qpos_row - (w0 + lax.broadcasted_iota(jnp.int32, (WK, 1), 0))
    biasw = jnp.where((dpos >= 0) & (dpos < WINDOW), 0.0, NEG)
    sw = sw + _lane_tile(biasw, N_KV * R)
    pw = jnp.exp2(sw - jnp.max(sw, axis=0, keepdims=True)).astype(bf16)
    ones_w = jnp.ones((16, WK), bf16)
    ow = jnp.dot(jnp.concatenate([vt_ref[128:256, pl.ds(w0a, WK)], ones_w], axis=0), pw, preferred_element_type=f32)
    o_w_all = ow[0:128] * (1.0 / ow[128:129])
    for h in range(N_HEADS):
        rows = slice((h // R) * HEAD_DIM, (h // R + 1) * HEAD_DIM)
        cs = slice(h * TQ, (h + 1) * TQ)
        c0 = 8 + 3 * h
        out = (gsig[c0:c0 + 1, :] * o_c_all[rows, cs] + gsig[c0 + 1:c0 + 2, :] * o_s_all[rows, cs]
               + gsig[c0 + 2:c0 + 3, :] * o_w_all[rows, cs])
        hbt_ref[h * 64:(h + 1) * 64, :] = out.astype(bf16)


def _nsa_prompt(qct, qrt, gt, kk, vt, kc, vct, *, nb, T):
    nq = T // TQ
    n_chunk = T // CMP_STRIDE
    ovt = _sel_overlap_t(n_chunk, T // SEL_BLOCK)
    col = lambda h: pl.BlockSpec((h, TQ), lambda b, i: (0, b * nq + i))
    const = lambda a: pl.BlockSpec(a.shape, lambda b, i: (0,) * a.ndim)
    return pl.pallas_call(
        functools.partial(_nsa_prompt_body, T=T), grid=(nb, nq),
        in_specs=[col(512), col(512), col(32),
                  pl.BlockSpec((T, 256), lambda b, i: (b, 0)),
                  pl.BlockSpec((256, T), lambda b, i: (0, b)),
                  pl.BlockSpec((1, n_chunk, 128), lambda b, i: (b, 0, 0)),
                  pl.BlockSpec((1, 128, n_chunk), lambda b, i: (b, 0, 0)),
                  const(ovt)],
        out_specs=col(512),
        out_shape=jax.ShapeDtypeStruct((512, nb * T), bf16),
        scratch_shapes=[pltpu.VMEM((N_KV, NBLK_PAD, TQ), f32)],
        compiler_params=pltpu.CompilerParams(dimension_semantics=("parallel", "arbitrary"),
                                             vmem_limit_bytes=VMEM_LIMIT),
        name="nsa_prompt",
    )(qct, qrt, gt, kk, vt, kc, vct, ovt)


def _nsa_sample_body(pt_ref, *refs, n_pages, ts):
    n_pg = GB * n_pages
    pages = refs[:n_pg]
    (win_ref, qc_ref, qr_ref, sm_ref, kvn_ref, wn_ref, wk_ref, pek_ref, w2k_ref, wv_ref, pev_ref, w2v_ref,
     ovt_ref, e_ref, perm_ref) = refs[n_pg:n_pg + 15]
    hb_ref, wout_stack = refs[-2:]
    wout_ref = wout_stack.at[0]
    P = n_pages * PAGE
    n_chunk = P // CMP_STRIDE
    n_blk = -(-(P + ts) // SEL_BLOCK)
    WB = win_ref.shape[4]
    chunks_per_page = PAGE // CMP_STRIDE

    qc_all = qc_ref[...].astype(f32)
    qr_all = qr_ref[...].astype(f32)
    gsig = jax.nn.sigmoid(sm_ref[...])
    rows = N_KV * R * SEG
    qpos_col = P + lax.broadcasted_iota(jnp.int32, (rows, 1), 0) % SEG
    qpos_row = P + lax.broadcasted_iota(jnp.int32, (1, 128), 1) % SEG
    new_lane = lax.broadcasted_iota(jnp.int32, (1, 128), 1)
    zpad = jnp.zeros((128 - SEG, 128), f32)

    def stack(q_all, bi):
        parts = []
        for h in range(N_HEADS):
            g = h // R
            q = q_all[bi * SEG:(bi + 1) * SEG, (h // 2) * 128:(h // 2 + 1) * 128]
            if h % 2 != g:
                q = pltpu.roll(q, HEAD_DIM, axis=1)
            parts.append(jnp.where(new_lane // HEAD_DIM == g, q, 0.0))
        return jnp.concatenate(parts, axis=0).astype(bf16)

    kwpos = jnp.concatenate([P - WB + lax.broadcasted_iota(jnp.int32, (1, WB), 1), P + new_lane], axis=1)
    validw = jnp.concatenate([jnp.full((1, WB), True), new_lane < SEG], axis=1)
    dpos = qpos_col - kwpos
    maskw = (dpos >= 0) & (dpos < WINDOW) & (kwpos >= 0) & validw
    pre = []
    for bi in range(GB):
        rs8 = slice(bi * SEG, (bi + 1) * SEG)
        ksel_t = jnp.concatenate([pages[bi * n_pages + pp][0, 0, 2].astype(bf16) for pp in range(n_pages)], axis=1)
        vsel_t = jnp.concatenate([pages[bi * n_pages + pp][0, 0, 3].astype(bf16) for pp in range(n_pages)], axis=1)
        knew = jnp.concatenate([kvn_ref[rs8, 256:384], zpad], axis=0).astype(bf16)
        vnew = jnp.concatenate([kvn_ref[rs8, 384:512], zpad], axis=0).astype(bf16)
        kwnew_f = jnp.concatenate([wn_ref[rs8, 0:128], zpad], axis=0)
        vwnew_f = jnp.concatenate([wn_ref[rs8, 128:256], zpad], axis=0)
        qr = stack(qr_all, bi)
        s_sel = jnp.concatenate([jnp.dot(qr, ksel_t, preferred_element_type=f32),
                                 lax.dot_general(qr, knew, NT, preferred_element_type=f32)], axis=1)
        s_w = jnp.concatenate([jnp.dot(qr, win_ref[0, bi, 0].astype(bf16), preferred_element_type=f32),
                               lax.dot_general(qr, kwnew_f.astype(bf16), NT, preferred_element_type=f32)], axis=1)
        p_w = _softmax_rows(s_w, maskw).astype(bf16)
        o_w = (lax.dot_general(p_w[:, 0:WB], win_ref[0, bi, 1].astype(bf16), NT, preferred_element_type=f32)
               + jnp.dot(p_w[:, WB:WB + 128], vwnew_f.astype(bf16), preferred_element_type=f32))
        pre.append((s_sel, o_w, vsel_t, vnew, kwnew_f, vwnew_f))

    xt_stack = jnp.concatenate([pages[j][0, 0, c].astype(bf16) for j in range(n_pg) for c in (0, 1)], axis=0)
    y_all = lax.dot_general(perm_ref[...], xt_stack, NT, preferred_element_type=f32)

    def flat(c):
        return jnp.concatenate(
            [jnp.concatenate([y_all[l * chunks_per_page:(l + 1) * chunks_per_page, (2 * j + c) * 128:(2 * j + c + 1) * 128]
                              for j in range(n_pg)], axis=0)
             for l in range(CMP_STRIDE)], axis=1).astype(bf16)

    kc_all = _compress_rows(flat(0), wk_ref, pek_ref, w2k_ref).astype(bf16)
    vc_all = _compress_rows(flat(1), wv_ref, pev_ref, w2v_ref).astype(bf16)

    cmp_end = lax.broadcasted_iota(jnp.int32, (1, n_chunk), 1) * CMP_STRIDE + (CMP_LEN - 1)
    o_cs, psums = [], []
    for bi in range(GB):
        s = lax.dot_general(stack(qc_all, bi), kc_all[bi * n_chunk:(bi + 1) * n_chunk], NT,
                            preferred_element_type=f32)
        p = _softmax_rows(s, cmp_end <= qpos_col)
        o_cs.append(jnp.dot(p.astype(bf16), vc_all[bi * n_chunk:(bi + 1) * n_chunk], preferred_element_type=f32))
        for g in range(N_KV):
            pg = p[g * R * SEG:(g + 1) * R * SEG]
            psums.append(pg[0:SEG] + pg[SEG:2 * SEG] + pg[2 * SEG:3 * SEG] + pg[3 * SEG:4 * SEG])
    psum_all = jnp.concatenate(psums + [jnp.zeros((128 - SEG * GB * N_KV, n_chunk), f32)], axis=0)
    imp_t = lax.dot_general(ovt_ref[...], psum_all, NT, preferred_element_type=f32, precision=HIGHEST)
    sel_all = _transpose_sel(_select_blocks(imp_t, qpos_row, n_blk)).astype(bf16)
    mk_all = jnp.dot(sel_all[0:GB * N_KV * SEG], e_ref[...], preferred_element_type=f32)

    out_rows = []
    for bi in range(GB):
        rs8 = slice(bi * SEG, (bi + 1) * SEG)
        s_sel, o_w, vsel_t, vnew, kwnew_f, vwnew_f = pre[bi]
        o_c = o_cs[bi]
        mk = jnp.concatenate(
            [mk_all[(bi * N_KV + g) * SEG:(bi * N_KV + g + 1) * SEG] for g in range(N_KV) for _ in range(R)], axis=0) > 0.5
        new_ok = (P + new_lane <= qpos_col) & (new_lane < SEG)
        p_s = _softmax_rows(s_sel, jnp.concatenate([mk[:, 0:P], mk[:, P:P + 128] & new_ok], axis=1)).astype(bf16)
        o_s = (lax.dot_general(p_s[:, 0:P], vsel_t, NT, preferred_element_type=f32)
               + jnp.dot(p_s[:, P:P + 128], vnew, preferred_element_type=f32))
        gs = gsig[rs8]
        heads = [None] * N_HEADS
        for h in range(N_HEADS):
            rr = slice(h * SEG, (h + 1) * SEG)
            c0 = 8 + 3 * h
            out = gs[:, c0:c0 + 1] * o_c[rr] + gs[:, c0 + 1:c0 + 2] * o_s[rr] + gs[:, c0 + 2:c0 + 3] * o_w[rr]
            if h % 2 != h // R:
                out = pltpu.roll(out, HEAD_DIM, axis=1)
            heads[h] = out
        out_rows.append(jnp.concatenate(
            [jnp.where(new_lane < HEAD_DIM, heads[2 * j], heads[2 * j + 1]) for j in range(N_HEADS // 2)], axis=1))
        for kv, new_f in ((0, kwnew_f), (1, vwnew_f)):
            old = pltpu.roll(win_ref[0, bi, kv], WB - ts, axis=1)
            new_t = pltpu.roll(new_f.T, 128 - ts, axis=1)
            wout_ref[bi, kv, :, 0:WB - 128] = old[:, 0:WB - 128]
            wout_ref[bi, kv, :, WB - 128:WB] = jnp.where(new_lane >= 128 - ts, new_t, old[:, WB - 128:WB])
    hb_ref[...] = jnp.concatenate(out_rows, axis=0).astype(bf16)


def _nsa_sample(page_table, cache_t, cwin_t, qc, qr, sm, kvrow, winrow, cwk, cwv, win_stack, *, layer, row0, ts):
    nseq, n_pages = page_table.shape
    depth = cwin_t.shape[0]
    P = n_pages * PAGE
    WB = cwin_t.shape[4]
    n_chunk = P // CMP_STRIDE
    ovt = _sel_overlap_t(n_chunk, -(-(P + ts) // SEL_BLOCK))
    e = _block_expand(P + 128)
    r = np.arange(PAGE)
    perm = jnp.asarray(np.arange(PAGE)[None, :] == (CMP_STRIDE * (r % (PAGE // CMP_STRIDE)) + r // (PAGE // CMP_STRIDE))[:, None],
                       dtype=bf16)
    b0 = row0 // (GB * SEG)

    def page_map(i, pt, *, bi, p):
        return (layer, pt[i * GB + bi, p], 0, 0, 0)

    page_specs = [pl.BlockSpec((1, 1, 4, 128, PAGE), functools.partial(page_map, bi=bi, p=p))
                  for bi in range(GB) for p in range(n_pages)]
    row = lambda w: pl.BlockSpec((GB * SEG, w), lambda i, pt: (b0 + i, 0))
    const = lambda a: pl.BlockSpec(a.shape, lambda i, pt: (0,) * a.ndim)
    extra_in, extra_spec, win_shape = _stacked_out(win_stack, depth, (nseq, 2, 128, WB))
    args = (page_table, *([cache_t] * (GB * n_pages)), cwin_t, qc, qr, sm, kvrow, winrow, *cwk, *cwv, ovt, e, perm)
    grid_spec = pltpu.PrefetchScalarGridSpec(
        num_scalar_prefetch=1, grid=(nseq // GB,),
        in_specs=page_specs + [pl.BlockSpec((1, GB, 2, 128, WB), lambda i, pt: (layer, i, 0, 0, 0)),
                               row(N_WIDTH), row(N_WIDTH), row(128), row(512), row(256)]
        + [const(a) for a in (*cwk, *cwv, ovt, e, perm)] + extra_spec,
        out_specs=[pl.BlockSpec((GB * SEG, N_WIDTH), lambda i, pt: (i, 0)),
                   pl.BlockSpec((1, GB, 2, 128, WB), lambda i, pt: (layer, i, 0, 0, 0))])
    return pl.pallas_call(
        functools.partial(_nsa_sample_body, n_pages=n_pages, ts=ts), grid_spec=grid_spec,
        out_shape=[jax.ShapeDtypeStruct((nseq * SEG, N_WIDTH), bf16), win_shape],
        input_output_aliases={len(args): 1} if extra_in else {},
        compiler_params=pltpu.CompilerParams(dimension_semantics=("parallel",), vmem_limit_bytes=VMEM_LIMIT),
        name="nsa_sample",
    )(*args, *extra_in)


def kernel(x_prompt, x_sample, cache_kv_pages, page_table, cache_win, state_mlstm_C, state_mlstm_n, state_mlstm_m, state_mlstm_conv, norm1_g, w_in, b_gates, conv_w, conv_b, w_mq, w_mk, mlstm_norm_g, cmp_pe_k, cmp_pe_v, cmp_w1_k, cmp_w2_k, cmp_w1_v, cmp_w2_v, w_branch_a, w_branch_b, w_out, norm2_g, w_ffn_up, w_ffn_down, final_norm_g):
    Bp, Tp, _ = x_prompt.shape
    Bs, Ts, _ = x_sample.shape
    depth = w_in.shape[0]
    n_pool, page = cache_kv_pages.shape[1:3]
    n_pages = page_table.shape[1]
    past_len = n_pages * page
    wb_len = cache_win.shape[2]
    mp = Bp * Tp
    ms = Bs * SAMPLE_PAD
    assert page == PAGE and M_CONV - 1 <= Ts <= SAMPLE_PAD and Tp % MLSTM_CHUNK == 0 and Tp >= WINDOW + TQ
    assert mp % (SEG * SEQ_PER_STEP) == 0 and Bs % SEQ_PER_STEP == 0 and Bs % GB == 0

    pad_seq = lambda a: jnp.pad(a, ((0, 0), (0, SAMPLE_PAD - a.shape[1]), (0, 0)))
    xs = (x_prompt.reshape(mp, D_MODEL), pad_seq(x_sample).reshape(ms, D_MODEL))
    tabs = _rope_tables(Tp, past_len, ROW_TILE * (ms // ROW_TILE))
    w_all = _in_proj_weights(w_in)
    w_up, w_down = w_ffn_up.astype(bf16), w_ffn_down.astype(bf16)
    cache_t = jnp.transpose(cache_kv_pages, (0, 1, 3, 4, 5, 2)).reshape(depth, n_pool, 4, KV_W, PAGE)
    cwin_t = jnp.transpose(cache_win, (0, 1, 3, 4, 5, 2)).reshape(depth, Bs, 2, KV_W, wb_len)

    outs = {k: [] for k in ("p_win", "p_C", "p_n", "p_m", "p_conv", "s_kv", "s_n", "s_m", "s_conv")}
    c_stack = win_stack = kvt_stack = None
    y_parts = None
    for l in range(depth):
        w_t = jnp.concatenate([w_in[l][:, O_IG:O_IG + 8], w_in[l][:, O_NG:O_NG + 24]], axis=1).T.astype(bf16)
        xm, sm, smt, qc, qr, qct, qrt, kvrow, winrow, kk, vt, gates, kvt_stack = _in_proj(
            xs, norm1_g[l].reshape(1, D_MODEL), w_all, w_t, tabs, Tp, kvt_stack, layer=l, depth=depth)

        wq, wk = w_mq[l].astype(bf16), w_mk[l].astype(bf16)
        ha_p, c_p, n_p, m_p, cv_p = _mlstm_prompt(xm, sm, smt, conv_w[l], conv_b[l], wq, wk, b_gates[l], mlstm_norm_g[l],
                                                   nb=Bp, T=Tp, L=MLSTM_CHUNK)
        tail = jnp.pad(state_mlstm_conv[l], ((0, 0), (0, SEG - (M_CONV - 1)), (0, 0))).reshape(ms, M_WIDTH)
        n0 = jnp.pad(state_mlstm_n[l], ((0, 0), (0, 8 - M_HEADS), (0, 0)))
        m0 = jnp.broadcast_to(jnp.pad(state_mlstm_m[l], ((0, 0), (0, 128 - M_HEADS)))[:, None, :], (Bs, SEG, 128))
        ha_s, c_stack, n_s, mt_s = _mlstm_sample(
            xm, sm, smt, tail, state_mlstm_C, n0, m0.reshape(ms, 128), conv_w[l], conv_b[l], wq, wk, b_gates[l],
            mlstm_norm_g[l], c_stack, layer=l, depth=depth, row0=mp, nseq=Bs, n_valid=Ts)

        cwk = _compress_weights(cmp_pe_k[l], cmp_w1_k[l], cmp_w2_k[l])
        cwv = _compress_weights(cmp_pe_v[l], cmp_w1_v[l], cmp_w2_v[l])
        kc, vct = _compress_prompt(kvrow, cwk, cwv, nb=Bp, T=Tp)
        hbt_p = _nsa_prompt(qct, qrt, smt, kk, vt, kc, vct, nb=Bp, T=Tp)
        hb_s, win_stack = _nsa_sample(page_table, cache_t, cwin_t, qc, qr, sm, kvrow, winrow, cwk, cwv, win_stack,
                                      layer=l, row0=mp, ts=Ts)

        x_all, *y_parts = _mix_ffn(
            xs, (ha_p, ha_s), (hbt_p, hb_s.T), gates, w_branch_a[l].astype(bf16), w_branch_b[l].astype(bf16),
            w_out[l].astype(bf16), norm2_g[l].reshape(1, D_MODEL), w_up, w_down, final_norm_g.reshape(1, D_MODEL),
            layer=l, n_prompt_rows=mp, final=l == depth - 1)
        xs = (x_all,)

        sample_rows = lambda a: a[mp:].reshape(Bs, SAMPLE_PAD, a.shape[1])
        wp = min(WINDOW, Tp)
        outs["p_win"].append(winrow[:mp].reshape(Bp, Tp, 2 * KV_W)[:, -wp:].reshape(Bp, wp, 2, N_KV, HEAD_DIM))
        outs["p_C"].append(c_p)
        outs["p_n"].append(n_p[:, :M_HEADS])
        outs["p_m"].append(m_p[:, 0, :M_HEADS])
        outs["p_conv"].append(cv_p[:, 8 - (M_CONV - 1):])
        outs["s_kv"].append(sample_rows(kvrow)[:, :Ts].reshape(Bs, Ts, 4, N_KV, HEAD_DIM))
        outs["s_n"].append(n_s[:, :M_HEADS])
        outs["s_m"].append(mt_s.reshape(Bs, SEG, 128)[:, SEG - 1, :M_HEADS])
        outs["s_conv"].append(sample_rows(xm)[:, Ts - (M_CONV - 1):Ts, :M_WIDTH])

    s_win = jnp.transpose(win_stack.reshape(depth, Bs, 2, N_KV, HEAD_DIM, wb_len), (0, 1, 5, 2, 3, 4))
    y_prompt = y_parts[0].reshape(Bp, Tp, D_MODEL)
    y_sample = y_parts[1].reshape(Bs, SAMPLE_PAD, D_MODEL)[:, :Ts]
    st = lambda k: jnp.stack(outs[k])
    p_kv = jnp.transpose(kvt_stack.reshape(depth, Bp, 4, N_KV, HEAD_DIM, Tp), (0, 1, 5, 2, 3, 4))
    return (y_prompt, y_sample, p_kv, st("p_win"), st("p_C"), st("p_n"), st("p_m"), st("p_conv"),
            st("s_kv"), s_win, c_stack, st("s_n"), st("s_m"), st("s_conv"))
```
